```python
import math
import jax, jax.numpy as jnp
from jax import lax
import numpy as np

D_MODEL = 1024
BATCH = 4
SEQ = 4096
DEPTH = 1

CTX_LEN = 256
GRID_W = 64
D_MIX = D_MODEL
D_LRU = D_MIX // 2
LRU_BLOCKS = 8
LRU_BLOCK_W = D_LRU // LRU_BLOCKS
CONV_W = 4
LRU_C = 8.0
GLA_HEADS = 4
D_GLA_V = D_MIX - D_LRU
D_GLA_K = D_GLA_V // 2
GLA_DK = D_GLA_K // GLA_HEADS
GLA_DV = D_GLA_V // GLA_HEADS
GATE_RANK = 16
GATE_TAU = 16.0
GLA_CHUNK = 64
N_EXPERTS = 32
TOP_K = 4
D_EXPERT = D_MODEL
SWIGLU_LIMIT = 7.0
SWIGLU_ALPHA = 1.702
MOE_BLOCK = 128
N_MOD = 6
DEEPNORM_ALPHA = (2.0 * DEPTH) ** 0.25
DEEPNORM_BETA = (8.0 * DEPTH) ** -0.25
LN_EPS = 1e-5
RMS_EPS = 1e-6
SPLIT_SIZES = (D_LRU, D_LRU, D_GLA_K, D_GLA_K, D_GLA_V, D_GLA_V, GATE_RANK)
D_IN = sum(SPLIT_SIZES)
SPLIT_POINTS = tuple(sum(SPLIT_SIZES[:i + 1]) for i in range(len(SPLIT_SIZES) - 1))

kernel_name = "hymba_rglru_gla_moe_dit_layer"

f32 = jnp.float32


def layer_norm(x, g, b):
    xf = x.astype(f32)
    mu = xf.mean(-1, keepdims=True)
    var = jnp.square(xf - mu).mean(-1, keepdims=True)
    return ((xf - mu) * lax.rsqrt(var + LN_EPS) * g.astype(f32) + b.astype(f32)).astype(x.dtype)


def ada_mod(cvec, w, b):
    m = jax.nn.silu(cvec) @ w + b
    return jnp.split(m[..., None, :], N_MOD, axis=-1)


def modulate(x, shift, scale):
    return x * (1.0 + scale) + shift


def centred_dwconv(u, w, b):
    pad_l = CONV_W // 2
    pad_r = CONV_W - 1 - pad_l
    L = u.shape[-2]
    up = jnp.pad(u, [(0, 0)] * (u.ndim - 2) + [(pad_l, pad_r), (0, 0)])
    out = b
    for tap in range(CONV_W):
        out = out + up[..., tap:tap + L, :] * w[tap]
    return out


def to_col_major(t, rows):
    B_, L, C = t.shape
    return t.reshape(B_, rows, GRID_W, C).transpose(0, 2, 1, 3).reshape(B_, L, C)


def from_col_major(t, rows):
    B_, L, C = t.shape
    return t.reshape(B_, GRID_W, rows, C).transpose(0, 2, 1, 3).reshape(B_, L, C)


def _linear_combine(left, right):
    a_l, b_l = left
    a_r, b_r = right
    return a_l * a_r, a_r * b_l + b_r


def rglru_scan(u, h0, w_a, b_a, w_x, b_x, lam):
    B_, L, _ = u.shape
    ub = u.reshape(B_, L, LRU_BLOCKS, LRU_BLOCK_W)
    r = jax.nn.sigmoid(jnp.einsum('blnc,ncd->blnd', ub, w_a).reshape(B_, L, D_LRU) + b_a)
    i = jax.nn.sigmoid(jnp.einsum('blnc,ncd->blnd', ub, w_x).reshape(B_, L, D_LRU) + b_x)
    log_a = (-LRU_C * jax.nn.softplus(-lam.astype(f32))) * r.astype(f32)
    a = jnp.exp(log_a)
    b = jnp.sqrt(-jnp.expm1(2.0 * log_a)) * (i * u).astype(f32)
    b = b.at[:, 0].add(a[:, 0] * h0)
    _, h = lax.associative_scan(_linear_combine, (a, b), axis=1)
    return h


def gla_chunked(q, k, v, log_alpha, s0):
    B_, H, L, _ = q.shape
    n = L // GLA_CHUNK

    def chunks(t):
        return t.astype(f32).reshape(B_, H, n, GLA_CHUNK, t.shape[-1])

    qc, kc, vc, lac = chunks(q), chunks(k), chunks(v), chunks(log_alpha)
    bcum = jnp.cumsum(lac, axis=-2)
    btot = bcum[..., -1:, :]
    q_dec = qc * jnp.exp(bcum)
    k_dec = kc * jnp.exp(-bcum)
    k_end = kc * jnp.exp(btot - bcum)
    mask = jnp.tril(jnp.ones((GLA_CHUNK, GLA_CHUNK), dtype=bool))
    scores = jnp.where(mask, jnp.einsum('bhnik,bhnjk->bhnij', q_dec, k_dec), 0.0)
    o_intra = jnp.einsum('bhnij,bhnjv->bhniv', scores, vc)
    chunk_kv = jnp.einsum('bhnjk,bhnjv->bhnkv', k_end, vc)
    decay_tot = jnp.exp(btot[..., 0, :])

    def step(s, inp):
        dec, kv = inp
        return dec[..., None] * s + kv, s

    s_final, s_starts = lax.scan(step, s0, (jnp.moveaxis(decay_tot, 2, 0), jnp.moveaxis(chunk_kv, 2, 0)))
    s_starts = jnp.moveaxis(s_starts, 0, 2)
    o_inter = jnp.einsum('bhnik,bhnkv->bhniv', q_dec, s_starts)
    o = (o_intra + o_inter).reshape(B_, H, L, GLA_DV)
    return o, s_final


def mix_stream(u, rows, lru_init, gla_init, w_in, conv_w, conv_b, lru_wa, lru_ba, lru_wx, lru_bx,
               lru_lam, gla_wa, gla_ba):
    B_, L, _ = u.shape
    proj = u @ w_in
    x_lru, gate_lru, q, k, v, g, a_low = jnp.split(proj, SPLIT_POINTS, axis=-1)
    if rows is None:
        x_lru = centred_dwconv(x_lru, conv_w, conv_b)
    else:
        x_lru = centred_dwconv(x_lru.reshape(B_, rows, GRID_W, D_LRU), conv_w, conv_b).reshape(B_, L, D_LRU)
        q, k, v, a_low = (to_col_major(t, rows) for t in (q, k, v, a_low))

    def heads(t):
        return t.reshape(B_, L, GLA_HEADS, -1).transpose(0, 2, 1, 3)

    qh = heads(q) * (GLA_DK ** -0.5)
    kh = heads(k)
    vh = heads(v)
    lru_states, gla_states, h_dirs, o_dirs = [], [], [], []
    for d in range(2):
        xs = x_lru if d == 0 else jnp.flip(x_lru, 1)
        h = rglru_scan(xs, lru_init[d], lru_wa[d], lru_ba[d], lru_wx[d], lru_bx[d], lru_lam[d])
        lru_states.append(h[:, -1])
        h_dirs.append(h if d == 0 else jnp.flip(h, 1))
        la = heads(jax.nn.log_sigmoid((a_low @ gla_wa[d] + gla_ba[d]).astype(f32)) / GATE_TAU)
        seqs = (qh, kh, vh, la) if d == 0 else tuple(jnp.flip(t, 2) for t in (qh, kh, vh, la))
        o, s = gla_chunked(*seqs, gla_init[d])
        gla_states.append(s)
        o_dirs.append(o if d == 0 else jnp.flip(o, 2))
    h_lru = h_dirs[0] + h_dirs[1]
    o_gla = (o_dirs[0] + o_dirs[1]).transpose(0, 2, 1, 3).reshape(B_, L, D_GLA_V)
    if rows is not None:
        o_gla = from_col_major(o_gla, rows)
    return (h_lru, gate_lru, o_gla, g), jnp.stack(lru_states), jnp.stack(gla_states)


def merge_heads(h_lru, gate_lru, o_gla, g, gla_norm_g, w_out):
    y_lru = h_lru * jax.nn.gelu(gate_lru.astype(f32))
    B_, L, _ = o_gla.shape
    oh = o_gla.reshape(B_, L, GLA_HEADS, GLA_DV)
    oh = oh * lax.rsqrt(jnp.mean(jnp.square(oh), -1, keepdims=True) + RMS_EPS)
    y_gla = oh.reshape(B_, L, D_GLA_V) * gla_norm_g.astype(f32) * jax.nn.silu(g.astype(f32))
    y = jnp.concatenate([y_lru, y_gla], axis=-1).astype(w_out.dtype)
    return y @ w_out


def moe_ffn(h, w_router, b_router, w_gate, b_gate, w_up, b_up, w_down, b_down):
    B_, L, D = h.shape
    xt = h.reshape(-1, D)
    T = xt.shape[0]
    logits = (xt @ w_router + b_router).astype(f32)
    top_v, top_i = lax.top_k(logits, TOP_K)
    top_w = jax.nn.softmax(top_v, axis=-1)
    n_assign = T * TOP_K
    expert_flat = top_i.reshape(-1)
    token_flat = jnp.arange(n_assign, dtype=jnp.int32) // TOP_K
    w_flat = top_w.reshape(-1)
    order = jnp.argsort(expert_flat)
    e_sorted = expert_flat[order]
    counts = jnp.bincount(expert_flat, length=N_EXPERTS)
    starts = jnp.cumsum(counts) - counts
    padded = (counts + MOE_BLOCK - 1) // MOE_BLOCK * MOE_BLOCK
    pad_ends = jnp.cumsum(padded)
    pad_starts = pad_ends - padded
    rank = jnp.arange(n_assign, dtype=jnp.int32) - starts[e_sorted]
    dest = pad_starts[e_sorted] + rank
    n_blocks = -(-(n_assign + N_EXPERTS * (MOE_BLOCK - 1)) // MOE_BLOCK)
    n_rows = n_blocks * MOE_BLOCK
    row_token = jnp.zeros((n_rows,), jnp.int32).at[dest].set(token_flat[order])
    row_weight = jnp.zeros((n_rows,), f32).at[dest].set(w_flat[order])
    block_expert = jnp.minimum(
        jnp.searchsorted(pad_ends, jnp.arange(n_blocks, dtype=jnp.int32) * MOE_BLOCK, side='right'),
        N_EXPERTS - 1)

    def expert_block(args):
        tok, e = args
        xb = xt[tok]
        gate = jnp.minimum(xb @ w_gate[e] + b_gate[e], SWIGLU_LIMIT)
        up = jnp.clip(xb @ w_up[e] + b_up[e], -SWIGLU_LIMIT, SWIGLU_LIMIT)
        act = (up + 1.0) * gate * jax.nn.sigmoid(SWIGLU_ALPHA * gate)
        return act @ w_down[e] + b_down[e]

    y_rows = lax.map(expert_block, (row_token.reshape(n_blocks, MOE_BLOCK), block_expert))
    y_rows = y_rows.reshape(n_rows, D) * row_weight[:, None].astype(y_rows.dtype)
    y = jnp.zeros_like(xt).at[row_token].add(y_rows.astype(xt.dtype))
    return y.reshape(B_, L, D)


def setup_inputs(seed: int = 0) -> dict:
    key = jax.random.key(seed)
    ks = jax.random.split(key, 32)
    nrm = lambda k, shape, s: jax.random.normal(k, shape, f32) * s
    a_init = jax.random.uniform(ks[13], (DEPTH, 2, D_LRU), f32, 0.9, 0.999)
    root = a_init ** (1.0 / LRU_C)
    lru_lam = jnp.log(root) - jnp.log1p(-root)
    return {
        "x": nrm(ks[0], (BATCH, SEQ, D_MODEL), 1.0),
        "c": nrm(ks[1], (BATCH, D_MODEL), 1.0),
        "ctx": nrm(ks[2], (BATCH, CTX_LEN, D_MODEL), 1.0),
        "c_ctx": nrm(ks[3], (D_MODEL,), 1.0),
        "w_ada": nrm(ks[4], (DEPTH, D_MODEL, N_MOD * D_MODEL), 0.5 * D_MODEL ** -0.5),
        "b_ada": nrm(ks[5], (DEPTH, N_MOD * D_MODEL), 0.02),
        "w_in": nrm(ks[6], (DEPTH, D_MODEL, D_IN), D_MODEL ** -0.5),
        "conv_w": nrm(ks[7], (DEPTH, CONV_W, D_LRU), CONV_W ** -0.5),
        "conv_b": nrm(ks[8], (DEPTH, D_LRU), 0.02),
        "lru_wa": nrm(ks[9], (DEPTH, 2, LRU_BLOCKS, LRU_BLOCK_W, LRU_BLOCK_W), LRU_BLOCK_W ** -0.5),
        "lru_ba": nrm(ks[10], (DEPTH, 2, D_LRU), 0.02),
        "lru_wx": nrm(ks[11], (DEPTH, 2, LRU_BLOCKS, LRU_BLOCK_W, LRU_BLOCK_W), LRU_BLOCK_W ** -0.5),
        "lru_bx": nrm(ks[12], (DEPTH, 2, D_LRU), 0.02),
        "lru_lam": lru_lam,
        "gla_wa": nrm(ks[14], (DEPTH, 2, GATE_RANK, D_GLA_K), GATE_RANK ** -0.5),
        "gla_ba": nrm(ks[15], (DEPTH, 2, D_GLA_K), 0.1),
        "gla_norm_g": 1.0 + nrm(ks[16], (DEPTH, D_GLA_V), 0.02),
        "w_out": nrm(ks[17], (DEPTH, D_MIX, D_MODEL), DEEPNORM_BETA * D_MIX ** -0.5),
        "ln1_g": 1.0 + nrm(ks[18], (DEPTH, D_MODEL), 0.02),
        "ln1_b": nrm(ks[19], (DEPTH, D_MODEL), 0.02),
        "w_router": nrm(ks[20], (DEPTH, D_MODEL, N_EXPERTS), D_MODEL ** -0.5),
        "b_router": nrm(ks[21], (DEPTH, N_EXPERTS), 0.01),
        "w_gate": nrm(ks[22], (DEPTH, N_EXPERTS, D_MODEL, D_EXPERT), D_MODEL ** -0.5),
        "b_gate": nrm(ks[23], (DEPTH, N_EXPERTS, D_EXPERT), 0.02),
        "w_up": nrm(ks[24], (DEPTH, N_EXPERTS, D_MODEL, D_EXPERT), D_MODEL ** -0.5),
        "b_up": nrm(ks[25], (DEPTH, N_EXPERTS, D_EXPERT), 0.02),
        "w_down": nrm(ks[26], (DEPTH, N_EXPERTS, D_EXPERT, D_MODEL), DEEPNORM_BETA * D_EXPERT ** -0.5),
        "b_down": nrm(ks[27], (DEPTH, N_EXPERTS, D_MODEL), 0.02),
        "ln2_g": 1.0 + nrm(ks[28], (DEPTH, D_MODEL), 0.02),
        "ln2_b": nrm(ks[29], (DEPTH, D_MODEL), 0.02),
    }


def reference(x, c, ctx, c_ctx, w_ada, b_ada, w_in, conv_w, conv_b, lru_wa, lru_ba, lru_wx, lru_bx,
              lru_lam, gla_wa, gla_ba, gla_norm_g, w_out, ln1_g, ln1_b, w_router, b_router, w_gate,
              b_gate, w_up, b_up, w_down, b_down, ln2_g, ln2_b):
    B_ = x.shape[0]
    rows = x.shape[1] // GRID_W
    zero_lru = jnp.zeros((2, B_, D_LRU), f32)
    zero_gla = jnp.zeros((2, B_, GLA_HEADS, GLA_DK, GLA_DV), f32)
    for l in range(DEPTH):
        sh1, sc1, g1, sh2, sc2, g2 = ada_mod(c, w_ada[l], b_ada[l])
        sh1c, sc1c, g1c, sh2c, sc2c, g2c = ada_mod(c_ctx[None], w_ada[l], b_ada[l])
        mix_params = (w_in[l], conv_w[l], conv_b[l], lru_wa[l], lru_ba[l], lru_wx[l], lru_bx[l],
                      lru_lam[l], gla_wa[l], gla_ba[l])
        raw_ctx, lru_ctx_state, gla_ctx_state = mix_stream(
            modulate(ctx, sh1c, sc1c), None, zero_lru, zero_gla, *mix_params)
        raw_lat, _, _ = mix_stream(
            modulate(x, sh1, sc1), rows, lru_ctx_state, gla_ctx_state, *mix_params)
        y = merge_heads(*raw_lat, gla_norm_g[l], w_out[l])
        x = layer_norm(DEEPNORM_ALPHA * x + g1 * y, ln1_g[l], ln1_b[l])
        moe_params = (w_router[l], b_router[l], w_gate[l], b_gate[l], w_up[l], b_up[l], w_down[l], b_down[l])
        x = layer_norm(DEEPNORM_ALPHA * x + g2 * moe_ffn(modulate(x, sh2, sc2), *moe_params),
                       ln2_g[l], ln2_b[l])
        if l < DEPTH - 1:
            y_c = merge_heads(*raw_ctx, gla_norm_g[l], w_out[l])
            ctx = layer_norm(DEEPNORM_ALPHA * ctx + g1c * y_c, ln1_g[l], ln1_b[l])
            ctx = layer_norm(DEEPNORM_ALPHA * ctx + g2c * moe_ffn(modulate(ctx, sh2c, sc2c), *moe_params),
                             ln2_g[l], ln2_b[l])
    return x
```

```python
import jax, jax.numpy as jnp
from jax import lax
from jax.experimental import pallas as pl

D_MODEL = 1024
DEPTH = 1
GRID_W = 64
D_MIX = D_MODEL
D_LRU = D_MIX // 2
LRU_BLOCKS = 8
LRU_BLOCK_W = D_LRU // LRU_BLOCKS
CONV_W = 4
LRU_C = 8.0
GLA_HEADS = 4
D_GLA_V = D_MIX - D_LRU
D_GLA_K = D_GLA_V // 2
GLA_DK = D_GLA_K // GLA_HEADS
GLA_DV = D_GLA_V // GLA_HEADS
GATE_RANK = 16
GATE_TAU = 16.0
GLA_CHUNK = 64
N_EXPERTS = 32
TOP_K = 4
SWIGLU_LIMIT = 7.0
SWIGLU_ALPHA = 1.702
MOE_BLOCK = 128
N_MOD = 6
DEEPNORM_ALPHA = (2.0 * DEPTH) ** 0.25
LN_EPS = 1e-5
RMS_EPS = 1e-6
SPLIT_SIZES = (D_LRU, D_LRU, D_GLA_K, D_GLA_K, D_GLA_V, D_GLA_V, GATE_RANK)
SPLIT_POINTS = tuple(sum(SPLIT_SIZES[:i + 1]) for i in range(len(SPLIT_SIZES) - 1))

f32 = jnp.float32


def layer_norm(x, g, b):
    mu = x.mean(-1, keepdims=True)
    var = jnp.square(x - mu).mean(-1, keepdims=True)
    return (x - mu) * lax.rsqrt(var + LN_EPS) * g + b


def ada_mod(cvec, w, b):
    m = jax.nn.silu(cvec) @ w + b
    return jnp.split(m[..., None, :], N_MOD, axis=-1)


def modulate(x, shift, scale):
    return x * (1.0 + scale) + shift


def centred_dwconv(u, w, b):
    pad_l = CONV_W // 2
    pad_r = CONV_W - 1 - pad_l
    L = u.shape[-2]
    up = jnp.pad(u, [(0, 0)] * (u.ndim - 2) + [(pad_l, pad_r), (0, 0)])
    out = b
    for tap in range(CONV_W):
        out = out + up[..., tap:tap + L, :] * w[tap]
    return out


def to_col_major(t, rows):
    B_, L, C = t.shape
    return t.reshape(B_, rows, GRID_W, C).transpose(0, 2, 1, 3).reshape(B_, L, C)


def from_col_major(t, rows):
    B_, L, C = t.shape
    return t.reshape(B_, GRID_W, rows, C).transpose(0, 2, 1, 3).reshape(B_, L, C)


def _linear_combine(left, right):
    a_l, b_l = left
    a_r, b_r = right
    return a_l * a_r, a_r * b_l + b_r


def rglru_scan(u, h0, w_a, b_a, w_x, b_x, lam):
    B_, L, _ = u.shape
    ub = u.reshape(B_, L, LRU_BLOCKS, LRU_BLOCK_W)
    r = jax.nn.sigmoid(jnp.einsum('blnc,ncd->blnd', ub, w_a).reshape(B_, L, D_LRU) + b_a)
    i = jax.nn.sigmoid(jnp.einsum('blnc,ncd->blnd', ub, w_x).reshape(B_, L, D_LRU) + b_x)
    log_a = (-LRU_C * jax.nn.softplus(-lam)) * r
    a = jnp.exp(log_a)
    b = jnp.sqrt(-jnp.expm1(2.0 * log_a)) * (i * u)
    b = b.at[:, 0].add(a[:, 0] * h0)
    _, h = lax.associative_scan(_linear_combine, (a, b), axis=1)
    return h


def gla_chunked(q, k, v, log_alpha, s0):
    B_, H, L, _ = q.shape
    n = L // GLA_CHUNK

    def chunks(t):
        return t.reshape(B_, H, n, GLA_CHUNK, t.shape[-1])

    qc, kc, vc, lac = chunks(q), chunks(k), chunks(v), chunks(log_alpha)
    bcum = jnp.cumsum(lac, axis=-2)
    btot = bcum[..., -1:, :]
    q_dec = qc * jnp.exp(bcum)
    k_dec = kc * jnp.exp(-bcum)
    k_end = kc * jnp.exp(btot - bcum)
    mask = jnp.tril(jnp.ones((GLA_CHUNK, GLA_CHUNK), dtype=bool))
    scores = jnp.where(mask, jnp.einsum('bhnik,bhnjk->bhnij', q_dec, k_dec), 0.0)
    o_intra = jnp.einsum('bhnij,bhnjv->bhniv', scores, vc)
    chunk_kv = jnp.einsum('bhnjk,bhnjv->bhnkv', k_end, vc)
    decay_tot = jnp.exp(btot[..., 0, :])

    def step(s, inp):
        dec, kv = inp
        return dec[..., None] * s + kv, s

    s_final, s_starts = lax.scan(step, s0, (jnp.moveaxis(decay_tot, 2, 0), jnp.moveaxis(chunk_kv, 2, 0)))
    s_starts = jnp.moveaxis(s_starts, 0, 2)
    o_inter = jnp.einsum('bhnik,bhnkv->bhniv', q_dec, s_starts)
    o = (o_intra + o_inter).reshape(B_, H, L, GLA_DV)
    return o, s_final


def mix_stream(u, rows, lru_init, gla_init, w_in, conv_w, conv_b, lru_wa, lru_ba, lru_wx, lru_bx,
               lru_lam, gla_wa, gla_ba):
    B_, L, _ = u.shape
    proj = u @ w_in
    x_lru, gate_lru, q, k, v, g, a_low = jnp.split(proj, SPLIT_POINTS, axis=-1)
    if rows is None:
        x_lru = centred_dwconv(x_lru, conv_w, conv_b)
    else:
        x_lru = centred_dwconv(x_lru.reshape(B_, rows, GRID_W, D_LRU), conv_w, conv_b).reshape(B_, L, D_LRU)
        q, k, v, a_low = (to_col_major(t, rows) for t in (q, k, v, a_low))

    def heads(t):
        return t.reshape(B_, L, GLA_HEADS, -1).transpose(0, 2, 1, 3)

    qh = heads(q) * (GLA_DK ** -0.5)
    kh = heads(k)
    vh = heads(v)
    lru_states, gla_states, h_dirs, o_dirs = [], [], [], []
    for d in range(2):
        xs = x_lru if d == 0 else jnp.flip(x_lru, 1)
        h = rglru_scan(xs, lru_init[d], lru_wa[d], lru_ba[d], lru_wx[d], lru_bx[d], lru_lam[d])
        lru_states.append(h[:, -1])
        h_dirs.append(h if d == 0 else jnp.flip(h, 1))
        la = heads(jax.nn.log_sigmoid(a_low @ gla_wa[d] + gla_ba[d]) / GATE_TAU)
        seqs = (qh, kh, vh, la) if d == 0 else tuple(jnp.flip(t, 2) for t in (qh, kh, vh, la))
        o, s = gla_chunked(*seqs, gla_init[d])
        gla_states.append(s)
        o_dirs.append(o if d == 0 else jnp.flip(o, 2))
    h_lru = h_dirs[0] + h_dirs[1]
    o_gla = (o_dirs[0] + o_dirs[1]).transpose(0, 2, 1, 3).reshape(B_, L, D_GLA_V)
    if rows is not None:
        o_gla = from_col_major(o_gla, rows)
    return (h_lru, gate_lru, o_gla, g), jnp.stack(lru_states), jnp.stack(gla_states)


def merge_heads(h_lru, gate_lru, o_gla, g, gla_norm_g, w_out):
    y_lru = h_lru * jax.nn.gelu(gate_lru)
    B_, L, _ = o_gla.shape
    oh = o_gla.reshape(B_, L, GLA_HEADS, GLA_DV)
    oh = oh * lax.rsqrt(jnp.mean(jnp.square(oh), -1, keepdims=True) + RMS_EPS)
    y_gla = oh.reshape(B_, L, D_GLA_V) * gla_norm_g * jax.nn.silu(g)
    y = jnp.concatenate([y_lru, y_gla], axis=-1)
    return y @ w_out


def moe_ffn(h, w_router, b_router, w_gate, b_gate, w_up, b_up, w_down, b_down):
    B_, L, D = h.shape
    xt = h.reshape(-1, D)
    T = xt.shape[0]
    logits = xt @ w_router + b_router
    top_v, top_i = lax.top_k(logits, TOP_K)
    top_w = jax.nn.softmax(top_v, axis=-1)
    n_assign = T * TOP_K
    expert_flat = top_i.reshape(-1)
    token_flat = jnp.arange(n_assign, dtype=jnp.int32) // TOP_K
    w_flat = top_w.reshape(-1)
    order = jnp.argsort(expert_flat)
    e_sorted = expert_flat[order]
    counts = jnp.bincount(expert_flat, length=N_EXPERTS)
    starts = jnp.cumsum(counts) - counts
    padded = (counts + MOE_BLOCK - 1) // MOE_BLOCK * MOE_BLOCK
    pad_ends = jnp.cumsum(padded)
    pad_starts = pad_ends - padded
    rank = jnp.arange(n_assign, dtype=jnp.int32) - starts[e_sorted]
    dest = pad_starts[e_sorted] + rank
    n_blocks = -(-(n_assign + N_EXPERTS * (MOE_BLOCK - 1)) // MOE_BLOCK)
    n_rows = n_blocks * MOE_BLOCK
    row_token = jnp.zeros((n_rows,), jnp.int32).at[dest].set(token_flat[order])
    row_weight = jnp.zeros((n_rows,), f32).at[dest].set(w_flat[order])
    block_expert = jnp.minimum(
        jnp.searchsorted(pad_ends, jnp.arange(n_blocks, dtype=jnp.int32) * MOE_BLOCK, side='right'),
        N_EXPERTS - 1)

    def expert_block(args):
        tok, e = args
        xb = xt[tok]
        gate = jnp.minimum(xb @ w_gate[e] + b_gate[e], SWIGLU_LIMIT)
        up = jnp.clip(xb @ w_up[e] + b_up[e], -SWIGLU_LIMIT, SWIGLU_LIMIT)
        act = (up + 1.0) * gate * jax.nn.sigmoid(SWIGLU_ALPHA * gate)
        return act @ w_down[e] + b_down[e]

    y_rows = lax.map(expert_block, (row_token.reshape(n_blocks, MOE_BLOCK), block_expert))
    y_rows = y_rows.reshape(n_rows, D) * row_weight[:, None]
    y = jnp.zeros_like(xt).at[row_token].add(y_rows)
    return y.reshape(B_, L, D)


def _ln_kernel(x_ref, g_ref, b_ref, o_ref):
    xf = x_ref[...]
    mu = xf.mean(-1, keepdims=True)
    var = jnp.square(xf - mu).mean(-1, keepdims=True)
    o_ref[...] = (xf - mu) * lax.rsqrt(var + LN_EPS) * g_ref[...] + b_ref[...]


def _ln_pallas(x, g, b):
    B_, L, D = x.shape
    tm = 512
    out = pl.pallas_call(
        _ln_kernel,
        grid=(B_ * L // tm,),
        in_specs=[pl.BlockSpec((tm, D), lambda i: (i, 0)),
                  pl.BlockSpec((1, D), lambda i: (0, 0)),
                  pl.BlockSpec((1, D), lambda i: (0, 0))],
        out_specs=pl.BlockSpec((tm, D), lambda i: (i, 0)),
        out_shape=jax.ShapeDtypeStruct((B_ * L, D), x.dtype),
    )(x.reshape(B_ * L, D), g.reshape(1, D), b.reshape(1, D))
    return out.reshape(B_, L, D)


def kernel(x, c, ctx, c_ctx, w_ada, b_ada, w_in, conv_w, conv_b, lru_wa, lru_ba, lru_wx, lru_bx,
           lru_lam, gla_wa, gla_ba, gla_norm_g, w_out, ln1_g, ln1_b, w_router, b_router, w_gate,
           b_gate, w_up, b_up, w_down, b_down, ln2_g, ln2_b):
    B_ = x.shape[0]
    rows = x.shape[1] // GRID_W
    zero_lru = jnp.zeros((2, B_, D_LRU), f32)
    zero_gla = jnp.zeros((2, B_, GLA_HEADS, GLA_DK, GLA_DV), f32)
    l = 0
    sh1, sc1, g1, sh2, sc2, g2 = ada_mod(c, w_ada[l], b_ada[l])
    sh1c, sc1c, _, _, _, _ = ada_mod(c_ctx[None], w_ada[l], b_ada[l])
    mix_params = (w_in[l], conv_w[l], conv_b[l], lru_wa[l], lru_ba[l], lru_wx[l], lru_bx[l],
                  lru_lam[l], gla_wa[l], gla_ba[l])
    _, lru_ctx_state, gla_ctx_state = mix_stream(
        modulate(ctx, sh1c, sc1c), None, zero_lru, zero_gla, *mix_params)
    raw_lat, _, _ = mix_stream(
        modulate(x, sh1, sc1), rows, lru_ctx_state, gla_ctx_state, *mix_params)
    y = merge_heads(*raw_lat, gla_norm_g[l], w_out[l])
    x = layer_norm(DEEPNORM_ALPHA * x + g1 * y, ln1_g[l], ln1_b[l])
    moe_params = (w_router[l], b_router[l], w_gate[l], b_gate[l], w_up[l], b_up[l], w_down[l], b_down[l])
    x = _ln_pallas(DEEPNORM_ALPHA * x + g2 * moe_ffn(modulate(x, sh2, sc2), *moe_params),
                   ln2_g[l], ln2_b[l])
    return x
```

```python
import functools

import jax
import jax.numpy as jnp
from jax import lax
from jax.experimental import pallas as pl
from jax.experimental.pallas import tpu as pltpu
from jax.experimental.pallas import tpu_sc as plsc

D_MODEL = 1024
DEPTH = 1
GRID_W = 64
D_LRU = 512
LRU_BLOCKS = 8
CONV_W = 4
LRU_C = 8.0
GLA_HEADS = 4
D_GLA_V = 512
D_GLA_K = 256
GLA_DK = 64
GLA_DV = 128
GATE_RANK = 16
GATE_TAU = 16.0
GLA_CHUNK = 64
N_EXPERTS = 32
TOP_K = 4
SWIGLU_LIMIT = 7.0
SWIGLU_ALPHA = 1.702
N_MOD = 6
DEEPNORM_ALPHA = (2.0 * DEPTH) ** 0.25
LN_EPS = 1e-5
RMS_EPS = 1e-6

LANES = 128
A_LOW_PAD = LANES
D_PG = 2 * D_GLA_K + 2 * D_GLA_V + A_LOW_PAD
D_PROJ = 2 * D_LRU + D_PG
TOKEN_TILE = 512
LRU_TILE = 256
MOE_TILE = 256
SC_CHUNK = 32
VMEM_LIMIT = 48 * 1024 * 1024

f32 = jnp.float32
bf16 = jnp.bfloat16


def _cparams(sem):
    return pltpu.CompilerParams(dimension_semantics=sem, vmem_limit_bytes=VMEM_LIMIT)


def _ada_kernel(c_ref, w_ref, b_ref, o_ref):
    s = c_ref[...]
    s = s * jax.nn.sigmoid(s)
    o_ref[...] = jnp.dot(s.astype(bf16), w_ref[...].astype(bf16), preferred_element_type=f32) + b_ref[...]


def _ada_mod(cpad, w, b):
    n = w.shape[1]
    tn = 1024
    return pl.pallas_call(
        _ada_kernel,
        grid=(n // tn,),
        in_specs=[pl.BlockSpec((8, D_MODEL), lambda j: (0, 0)),
                  pl.BlockSpec((D_MODEL, tn), lambda j: (0, j)),
                  pl.BlockSpec((1, tn), lambda j: (0, j))],
        out_specs=pl.BlockSpec((8, tn), lambda j: (0, j)),
        out_shape=jax.ShapeDtypeStruct((8, n), f32),
        compiler_params=_cparams(("arbitrary",)),
        name="ada_mod",
    )(cpad, w, b.reshape(1, n))


def _inproj_kernel(x_ref, mod_ref, w_ref, xl_ref, gl_ref, pg_ref):
    sh = mod_ref[0, :, 0:D_MODEL]
    sc = mod_ref[0, :, D_MODEL:2 * D_MODEL]
    u = (x_ref[...] * (1.0 + sc) + sh).astype(bf16)
    xl_ref[...] = jnp.dot(u, w_ref[:, 0:D_LRU], preferred_element_type=f32)
    gl_ref[...] = jnp.dot(u, w_ref[:, D_LRU:2 * D_LRU], preferred_element_type=f32).astype(bf16)
    pg_ref[...] = jnp.dot(u, w_ref[:, 2 * D_LRU:D_PROJ], preferred_element_type=f32).astype(bf16)


def _inproj(xt, mod3, w_cat, batch_of_tile):
    T = xt.shape[0]
    tm = TOKEN_TILE
    return pl.pallas_call(
        _inproj_kernel,
        grid=(T // tm,),
        in_specs=[pl.BlockSpec((tm, D_MODEL), lambda i: (i, 0)),
                  pl.BlockSpec((1, 1, 2 * D_MODEL), lambda i: (batch_of_tile(i), 0, 0)),
                  pl.BlockSpec((D_MODEL, D_PROJ), lambda i: (0, 0))],
        out_specs=[pl.BlockSpec((tm, D_LRU), lambda i: (i, 0)),
                   pl.BlockSpec((tm, D_LRU), lambda i: (i, 0)),
                   pl.BlockSpec((tm, D_PG), lambda i: (i, 0))],
        out_shape=[jax.ShapeDtypeStruct((T, D_LRU), f32),
                   jax.ShapeDtypeStruct((T, D_LRU), bf16),
                   jax.ShapeDtypeStruct((T, D_PG), bf16)],
        compiler_params=_cparams(("arbitrary",)),
        name="inproj",
    )(xt, mod3, w_cat)


def _gelu_tanh(x):
    return 0.5 * x * (1.0 + jnp.tanh(0.7978845608028654 * (x + 0.044715 * (x * x * x))))


def _lru_kernel(*refs, row_w, reverse, merge):
    if merge:
        (xl_ref, cw_ref, cb_ref, wg_ref, bg_ref, lam_ref, h0_ref, hb_ref, gl_ref,
         out_ref, hfin_ref, carry) = refs
    else:
        xl_ref, cw_ref, cb_ref, wg_ref, bg_ref, lam_ref, h0_ref, out_ref, hfin_ref, carry = refs
    t = pl.program_id(1)

    @pl.when(t == 0)
    def _():
        carry[...] = h0_ref[0]

    x = xl_ref[...]
    tt, ch = x.shape
    row = lax.broadcasted_iota(jnp.int32, (tt, ch), 0)
    col = row & (row_w - 1)

    def shifted(v, s):
        r = pltpu.roll(v, s % tt, 0)
        ok = (col >= s) if s > 0 else (col < row_w + s)
        return jnp.where(ok, r, 0.0)

    cw = cw_ref[...]
    u = (cb_ref[...] + shifted(x, 2) * cw[0:1] + shifted(x, 1) * cw[1:2]
         + x * cw[2:3] + shifted(x, -1) * cw[3:4])
    g = jnp.dot(u.astype(bf16), wg_ref[...], preferred_element_type=f32) + bg_ref[...]
    r_gate = jax.nn.sigmoid(g[:, 0:ch])
    i_gate = jax.nn.sigmoid(g[:, ch:2 * ch])
    lam = lam_ref[...]
    softplus_neg = jnp.maximum(-lam, 0.0) + jnp.log(1.0 + jnp.exp(-jnp.abs(lam)))
    log_a = (-LRU_C * softplus_neg) * r_gate
    a = jnp.exp(log_a)
    t2 = 2.0 * log_a
    series = -t2 * (1.0 + t2 * (0.5 + t2 * (1.0 / 6.0 + t2 * (1.0 / 24.0 + t2 * (1.0 / 120.0)))))
    one_minus_a2 = jnp.where(t2 > -0.03, series, 1.0 - jnp.exp(t2))
    b = jnp.sqrt(one_minus_a2) * (i_gate * u)

    d = 1
    while d < tt:
        if reverse:
            a_s = pltpu.roll(a, tt - d, 0)
            b_s = pltpu.roll(b, tt - d, 0)
            ok = row < tt - d
        else:
            a_s = pltpu.roll(a, d, 0)
            b_s = pltpu.roll(b, d, 0)
            ok = row >= d
        b = jnp.where(ok, a * b_s + b, b)
        a = jnp.where(ok, a * a_s, a)
        d *= 2
    h = a * carry[...] + b
    last = h[0:1] if reverse else h[tt - 1:tt]
    carry[...] = last
    hfin_ref[0] = last
    if merge:
        out_ref[...] = ((h + hb_ref[...]) * _gelu_tanh(gl_ref[...].astype(f32))).astype(out_ref.dtype)
    else:
        out_ref[...] = h


def _lru_pass(xl, conv_w, conv_b, wg, bg, lam, h0, *, n_batch, row_w, tile, reverse, hb=None, gl=None):
    T = xl.shape[0]
    nt = T // n_batch // tile
    merge = hb is not None

    def tok(b, t):
        return (b * nt + (nt - 1 - t if reverse else t), 0)

    const = lambda b, t: (0, 0)
    in_specs = [pl.BlockSpec((tile, D_LRU), tok),
                pl.BlockSpec((CONV_W, D_LRU), const),
                pl.BlockSpec((1, D_LRU), const),
                pl.BlockSpec((D_LRU, 2 * D_LRU), const),
                pl.BlockSpec((1, 2 * D_LRU), const),
                pl.BlockSpec((1, D_LRU), const),
                pl.BlockSpec((1, 1, D_LRU), lambda b, t: (b, 0, 0))]
    args = [xl, conv_w, conv_b, wg, bg, lam, h0]
    if merge:
        in_specs += [pl.BlockSpec((tile, D_LRU), tok), pl.BlockSpec((tile, D_LRU), tok)]
        args += [hb, gl]
    return pl.pallas_call(
        functools.partial(_lru_kernel, row_w=row_w, reverse=reverse, merge=merge),
        grid=(n_batch, nt),
        in_specs=in_specs,
        out_specs=[pl.BlockSpec((tile, D_LRU), tok),
                   pl.BlockSpec((1, 1, D_LRU), lambda b, t: (b, 0, 0))],
        out_shape=[jax.ShapeDtypeStruct((T, D_LRU), bf16 if merge else f32),
                   jax.ShapeDtypeStruct((n_batch, 1, D_LRU), f32)],
        scratch_shapes=[pltpu.VMEM((1, D_LRU), f32)],
        compiler_params=_cparams(("arbitrary", "arbitrary")),
        name="lru_merge" if merge else "lru_scan",
    )(*args)


def _gla_kernel(*refs, reverse, merge):
    if merge:
        pg_ref, wa_ref, ba_ref, s0_ref, ob_ref, gn_ref, out_ref, sfin_ref, state = refs
    else:
        pg_ref, wa_ref, ba_ref, s0_ref, out_ref, sfin_ref, state = refs
    n = pl.program_id(0)

    @pl.when(n == 0)
    def _():
        state[...] = s0_ref[...]

    n_batch = pg_ref.shape[0]
    ck = GLA_CHUNK
    ii = lax.broadcasted_iota(jnp.int32, (ck, ck), 0)
    jj = lax.broadcasted_iota(jnp.int32, (ck, ck), 1)
    tri = (jj >= ii) if reverse else (jj <= ii)
    trif = tri.astype(f32)
    head_of_lane = lax.broadcasted_iota(jnp.int32, (1, D_GLA_K), 1) // GLA_DK
    nt_dims = (((1,), (1,)), ((), ()))
    tn_dims = (((0,), (0,)), ((), ()))
    o_q, o_k, o_v, o_g, o_a = 0, D_GLA_K, 2 * D_GLA_K, 2 * D_GLA_K + D_GLA_V, 2 * D_GLA_K + 2 * D_GLA_V
    for bi in range(n_batch):
        q = pg_ref[bi, :, o_q:o_k].astype(f32) * (GLA_DK ** -0.5)
        k = pg_ref[bi, :, o_k:o_v].astype(f32)
        v = pg_ref[bi, :, o_v:o_g]
        a_low = pg_ref[bi, :, o_a:o_a + A_LOW_PAD]
        z = jnp.dot(a_low, wa_ref[...], preferred_element_type=f32) + ba_ref[...]
        log_alpha = (jnp.minimum(z, 0.0) - jnp.log(1.0 + jnp.exp(-jnp.abs(z)))) * (1.0 / GATE_TAU)
        bcum = jnp.dot(trif, log_alpha, preferred_element_type=f32, precision=lax.Precision.HIGHEST)
        btot = bcum[0:1] if reverse else bcum[ck - 1:ck]
        q_dec = q * jnp.exp(bcum)
        k_dec = (k * jnp.exp(-bcum)).astype(bf16)
        k_end = (k * jnp.exp(btot - bcum)).astype(bf16)
        s_t = state[bi]
        s_tb = s_t.astype(bf16)
        kv_t = jnp.zeros_like(s_t)
        outs = []
        for hd in range(GLA_HEADS):
            in_head = head_of_lane == hd
            qh = jnp.where(in_head, q_dec, 0.0).astype(bf16)
            scores = lax.dot_general(qh, k_dec, nt_dims, preferred_element_type=f32)
            scores = jnp.where(tri, scores, 0.0).astype(bf16)
            vh = v[:, hd * GLA_DV:(hd + 1) * GLA_DV]
            o_h = (jnp.dot(scores, vh, preferred_element_type=f32)
                   + lax.dot_general(qh, s_tb, nt_dims, preferred_element_type=f32))
            outs.append(o_h)
            p = lax.dot_general(vh, k_end, tn_dims, preferred_element_type=f32)
            kv_t = jnp.where(in_head, p, kv_t)
        state[bi] = s_t * jnp.exp(btot) + kv_t
        if merge:
            g = pg_ref[bi, :, o_g:o_a].astype(f32)
            normed = []
            for hd in range(GLA_HEADS):
                oh = outs[hd] + ob_ref[bi, :, hd * GLA_DV:(hd + 1) * GLA_DV]
                ms = jnp.mean(oh * oh, axis=-1, keepdims=True)
                normed.append(oh * lax.rsqrt(ms + RMS_EPS))
            y = jnp.concatenate(normed, axis=-1) * gn_ref[...] * (g * jax.nn.sigmoid(g))
            out_ref[bi] = y.astype(out_ref.dtype)
        else:
            out_ref[bi] = jnp.concatenate(outs, axis=-1)

    @pl.when(n == pl.num_programs(0) - 1)
    def _():
        sfin_ref[...] = state[...]


def _gla_pass(pg_view, chunk_spec, out_view_shape, out_chunk_spec, n_chunks, wa, ba, s0, *, reverse,
              ob_view=None, gn=None):
    n_batch = s0.shape[0]
    merge = ob_view is not None
    order = (lambda n: n_chunks - 1 - n) if reverse else (lambda n: n)
    const2 = lambda n: (0, 0)
    const3 = lambda n: (0, 0, 0)
    in_specs = [chunk_spec(D_PG, order),
                pl.BlockSpec((A_LOW_PAD, D_GLA_K), const2),
                pl.BlockSpec((1, D_GLA_K), const2),
                pl.BlockSpec((n_batch, GLA_DV, D_GLA_K), const3)]
    args = [pg_view, wa, ba, s0]
    if merge:
        in_specs += [out_chunk_spec(D_GLA_V, order), pl.BlockSpec((1, D_GLA_V), const2)]
        args += [ob_view, gn]
    return pl.pallas_call(
        functools.partial(_gla_kernel, reverse=reverse, merge=merge),
        grid=(n_chunks,),
        in_specs=in_specs,
        out_specs=[out_chunk_spec(D_GLA_V, order),
                   pl.BlockSpec((n_batch, GLA_DV, D_GLA_K), const3)],
        out_shape=[jax.ShapeDtypeStruct(out_view_shape, bf16 if merge else f32),
                   jax.ShapeDtypeStruct((n_batch, GLA_DV, D_GLA_K), f32)],
        scratch_shapes=[pltpu.VMEM((n_batch, GLA_DV, D_GLA_K), f32)],
        compiler_params=_cparams(("arbitrary",)),
        name="gla_merge" if merge else "gla_scan",
    )(*args)


def _outproj_router_kernel(yl_ref, yg_ref, x_ref, mod_ref, wo_ref, lg_ref, lb_ref, wr_ref, br_ref,
                           x1_ref, h_ref, ids_ref, wts_ref, rank_ref, cnt_ref, running):
    i = pl.program_id(0)

    @pl.when(i == 0)
    def _():
        running[...] = jnp.zeros_like(running)

    d = D_MODEL
    y = (jnp.dot(yl_ref[...], wo_ref[0:D_LRU, :], preferred_element_type=f32)
         + jnp.dot(yg_ref[...], wo_ref[D_LRU:2 * D_LRU, :], preferred_element_type=f32))
    g1 = mod_ref[0, :, 2 * d:3 * d]
    sh2 = mod_ref[0, :, 3 * d:4 * d]
    sc2 = mod_ref[0, :, 4 * d:5 * d]
    z = DEEPNORM_ALPHA * x_ref[...] + g1 * y
    mu = jnp.mean(z, axis=-1, keepdims=True)
    zc = z - mu
    var = jnp.mean(zc * zc, axis=-1, keepdims=True)
    x1 = zc * lax.rsqrt(var + LN_EPS) * lg_ref[...] + lb_ref[...]
    x1_ref[...] = x1
    hmod = x1 * (1.0 + sc2) + sh2
    h_ref[...] = hmod

    logits = jnp.dot(hmod, wr_ref[...], preferred_element_type=f32,
                     precision=lax.Precision.HIGHEST) + br_ref[...]
    tm = logits.shape[0]
    lane = lax.broadcasted_iota(jnp.int32, (tm, LANES), 1)
    neg_inf = jnp.float32(-jnp.inf)
    live = jnp.where(lane < N_EXPERTS, logits, neg_inf)
    sel = jnp.zeros((tm, LANES), dtype=jnp.bool_)
    ids, vals = [], []
    for _ in range(TOP_K):
        m = jnp.max(live, axis=-1, keepdims=True)
        j = jnp.min(jnp.where(live == m, lane, LANES), axis=-1, keepdims=True)
        pick = lane == j
        sel = sel | pick
        live = jnp.where(pick, neg_inf, live)
        ids.append(j)
        vals.append(m)
    exps = [jnp.exp(vk - vals[0]) for vk in vals]
    denom = exps[0] + exps[1] + exps[2] + exps[3]

    selb = sel.astype(bf16)
    ri = lax.broadcasted_iota(jnp.int32, (tm, tm), 0)
    ci = lax.broadcasted_iota(jnp.int32, (tm, tm), 1)
    earlier = (ci < ri).astype(bf16)
    rank_dense = running[...] + jnp.dot(earlier, selb, preferred_element_type=f32)
    running[...] = running[...] + jnp.sum(sel.astype(f32), axis=0, keepdims=True)
    cnt_ref[...] = running[...]

    ids_out = jnp.zeros((tm, LANES), jnp.int32)
    wts_out = jnp.zeros((tm, LANES), f32)
    rank_out = jnp.zeros((tm, LANES), jnp.int32)
    for kk in range(TOP_K):
        rk = jnp.sum(jnp.where(lane == ids[kk], rank_dense, 0.0), axis=-1, keepdims=True)
        here = lane == kk
        ids_out = jnp.where(here, ids[kk], ids_out)
        wts_out = jnp.where(here, exps[kk] / denom, wts_out)
        rank_out = jnp.where(here, rk.astype(jnp.int32), rank_out)
    ids_ref[...] = ids_out
    wts_ref[...] = wts_out
    rank_ref[...] = rank_out


def _outproj_router(yl, yg, xt, mod3, wo, lg, lb, wr, br, tokens_per_batch):
    T = xt.shape[0]
    tm = TOKEN_TILE
    tiles_per_batch = tokens_per_batch // tm
    tokrow = lambda i: (i, 0)
    const = lambda i: (0, 0)
    return pl.pallas_call(
        _outproj_router_kernel,
        grid=(T // tm,),
        in_specs=[pl.BlockSpec((tm, D_LRU), tokrow),
                  pl.BlockSpec((tm, D_GLA_V), tokrow),
                  pl.BlockSpec((tm, D_MODEL), tokrow),
                  pl.BlockSpec((1, 1, N_MOD * D_MODEL), lambda i: (i // tiles_per_batch, 0, 0)),
                  pl.BlockSpec((D_MODEL, D_MODEL), const),
                  pl.BlockSpec((1, D_MODEL), const),
                  pl.BlockSpec((1, D_MODEL), const),
                  pl.BlockSpec((D_MODEL, LANES), const),
                  pl.BlockSpec((1, LANES), const)],
        out_specs=[pl.BlockSpec((tm, D_MODEL), tokrow),
                   pl.BlockSpec((tm, D_MODEL), tokrow),
                   pl.BlockSpec((tm, LANES), tokrow),
                   pl.BlockSpec((tm, LANES), tokrow),
                   pl.BlockSpec((tm, LANES), tokrow),
                   pl.BlockSpec((1, LANES), const)],
        out_shape=[jax.ShapeDtypeStruct((T, D_MODEL), f32),
                   jax.ShapeDtypeStruct((T, D_MODEL), f32),
                   jax.ShapeDtypeStruct((T, LANES), jnp.int32),
                   jax.ShapeDtypeStruct((T, LANES), f32),
                   jax.ShapeDtypeStruct((T, LANES), jnp.int32),
                   jax.ShapeDtypeStruct((1, LANES), f32)],
        scratch_shapes=[pltpu.VMEM((1, LANES), f32)],
        compiler_params=_cparams(("arbitrary",)),
        name="outproj_router",
    )(yl, yg, xt, mod3, wo, lg, lb, wr, br)


def _sc_workers():
    info = plsc.get_sparse_core_info()
    return info.num_cores, info.num_subcores


def _sc_dispatch(rows, dest_flat, n_out):
    T, D = rows.shape
    nc, ns = _sc_workers()
    per_w = T // (nc * ns)
    n_chunks = per_w // SC_CHUNK
    mesh = plsc.VectorSubcoreMesh(core_axis_name="c", subcore_axis_name="s")

    @functools.partial(
        pl.kernel, mesh=mesh,
        out_type=jax.ShapeDtypeStruct((n_out, D), rows.dtype),
        scratch_types=[pltpu.VMEM((SC_CHUNK,), jnp.int32),
                       pltpu.VMEM((SC_CHUNK, D), rows.dtype),
                       pltpu.SemaphoreType.DMA],
    )
    def k(rows_hbm, dest_hbm, out_hbm, idx_v, rows_v, sem):
        wid = lax.axis_index("s") * nc + lax.axis_index("c")
        base = wid * per_w

        @pl.loop(0, n_chunks)
        def _(j):
            off = base + j * SC_CHUNK
            pltpu.sync_copy(rows_hbm.at[pl.ds(off, SC_CHUNK)], rows_v)
            for kk in range(TOP_K):
                pltpu.sync_copy(dest_hbm.at[pl.ds(kk * T + off, SC_CHUNK)], idx_v)
                pltpu.async_copy(rows_v, out_hbm.at[idx_v], sem).wait()

    return k(rows, dest_flat)


def _sc_gather(table, idx):
    _, D = table.shape
    N = idx.shape[0]
    nc, ns = _sc_workers()
    per_w = N // (nc * ns)
    n_chunks = per_w // SC_CHUNK
    mesh = plsc.VectorSubcoreMesh(core_axis_name="c", subcore_axis_name="s")

    @functools.partial(
        pl.kernel, mesh=mesh,
        out_type=jax.ShapeDtypeStruct((N, D), table.dtype),
        scratch_types=[pltpu.VMEM((SC_CHUNK,), jnp.int32),
                       pltpu.VMEM((SC_CHUNK, D), table.dtype),
                       pltpu.SemaphoreType.DMA],
    )
    def k(table_hbm, idx_hbm, out_hbm, idx_v, rows_v, sem):
        wid = lax.axis_index("s") * nc + lax.axis_index("c")
        base = wid * per_w

        @pl.loop(0, n_chunks)
        def _(j):
            off = base + j * SC_CHUNK
            pltpu.sync_copy(idx_hbm.at[pl.ds(off, SC_CHUNK)], idx_v)
            pltpu.async_copy(table_hbm.at[idx_v], rows_v, sem).wait()
            pltpu.sync_copy(rows_v, out_hbm.at[pl.ds(off, SC_CHUNK)])

    return k(table, idx)


def _ffn_kernel(te_ref, tv_ref, xs_ref, wg_ref, bg_ref, wu_ref, bu_ref, wd_ref, bd_ref, eo_ref,
                wg_b, wu_b, wd_b):
    i = pl.program_id(0)
    prev = te_ref[jnp.maximum(i - 1, 0)]

    @pl.when((i == 0) | (te_ref[i] != prev))
    def _():
        wg_b[...] = wg_ref[...].astype(bf16)
        wu_b[...] = wu_ref[...].astype(bf16)
        wd_b[...] = wd_ref[...].astype(bf16)

    valid = tv_ref[i]

    @pl.when(valid > 0)
    def _():
        bm = xs_ref.shape[0]
        row = lax.broadcasted_iota(jnp.int32, (bm, 1), 0)
        x = jnp.where(row < valid, xs_ref[...], 0.0).astype(bf16)
        gate = jnp.minimum(jnp.dot(x, wg_b[...], preferred_element_type=f32) + bg_ref[...], SWIGLU_LIMIT)
        up = jnp.clip(jnp.dot(x, wu_b[...], preferred_element_type=f32) + bu_ref[...],
                      -SWIGLU_LIMIT, SWIGLU_LIMIT)
        act = (up + 1.0) * gate * jax.nn.sigmoid(SWIGLU_ALPHA * gate)
        eo_ref[...] = jnp.dot(act.astype(bf16), wd_b[...], preferred_element_type=f32) + bd_ref[...]


def _expert_ffn(tile_expert, tile_valid, xs, w_gate, b_gate, w_up, b_up, w_down, b_down):
    n_rows, d = xs.shape
    bm = MOE_TILE
    d_e = w_gate.shape[-1]
    wspec = lambda k_, n_: pl.BlockSpec((None, k_, n_), lambda i, te, tv: (te[i], 0, 0))
    grid_spec = pltpu.PrefetchScalarGridSpec(
        num_scalar_prefetch=2,
        grid=(n_rows // bm,),
        in_specs=[pl.BlockSpec((bm, d), lambda i, te, tv: (i, 0)),
                  wspec(d, d_e), wspec(1, d_e), wspec(d, d_e), wspec(1, d_e), wspec(d_e, d), wspec(1, d)],
        out_specs=pl.BlockSpec((bm, d), lambda i, te, tv: (i, 0)),
        scratch_shapes=[pltpu.VMEM((d, d_e), bf16), pltpu.VMEM((d, d_e), bf16), pltpu.VMEM((d_e, d), bf16)],
    )
    return pl.pallas_call(
        _ffn_kernel,
        grid_spec=grid_spec,
        out_shape=jax.ShapeDtypeStruct((n_rows, d), f32),
        compiler_params=_cparams(("arbitrary",)),
        name="expert_ffn",
    )(tile_expert, tile_valid, xs, w_gate, b_gate.reshape(N_EXPERTS, 1, d_e), w_up,
      b_up.reshape(N_EXPERTS, 1, d_e), w_down, b_down.reshape(N_EXPERTS, 1, d))


def _combine_kernel(eg_ref, wts_ref, x1_ref, mod_ref, lg_ref, lb_ref, o_ref):
    w = wts_ref[...]
    y = eg_ref[0] * w[:, 0:1]
    for kk in range(1, TOP_K):
        y = y + eg_ref[kk] * w[:, kk:kk + 1]
    z = DEEPNORM_ALPHA * x1_ref[...] + mod_ref[0] * y
    mu = jnp.mean(z, axis=-1, keepdims=True)
    zc = z - mu
    var = jnp.mean(zc * zc, axis=-1, keepdims=True)
    o_ref[...] = zc * lax.rsqrt(var + LN_EPS) * lg_ref[...] + lb_ref[...]


def _combine_ln(eg, wts, x1, mod3, lg, lb, tokens_per_batch):
    T = x1.shape[0]
    tm = 256
    tiles_per_batch = tokens_per_batch // tm
    const = lambda i: (0, 0)
    return pl.pallas_call(
        _combine_kernel,
        grid=(T // tm,),
        in_specs=[pl.BlockSpec((TOP_K, tm, D_MODEL), lambda i: (0, i, 0)),
                  pl.BlockSpec((tm, LANES), lambda i: (i, 0)),
                  pl.BlockSpec((tm, D_MODEL), lambda i: (i, 0)),
                  pl.BlockSpec((1, 1, D_MODEL), lambda i: (i // tiles_per_batch, 0, N_MOD - 1)),
                  pl.BlockSpec((1, D_MODEL), const),
                  pl.BlockSpec((1, D_MODEL), const)],
        out_specs=pl.BlockSpec((tm, D_MODEL), lambda i: (i, 0)),
        out_shape=jax.ShapeDtypeStruct((T, D_MODEL), f32),
        compiler_params=_cparams(("arbitrary",)),
        name="combine_ln",
    )(eg, wts, x1, mod3, lg, lb)


def _block_diag(w):
    n, c, d = w.shape
    eye = jnp.eye(n, dtype=w.dtype)
    return jnp.einsum('ncd,nm->ncmd', w, eye).reshape(n * c, n * d)


def kernel(x, c, ctx, c_ctx, w_ada, b_ada, w_in, conv_w, conv_b, lru_wa, lru_ba, lru_wx, lru_bx,
           lru_lam, gla_wa, gla_ba, gla_norm_g, w_out, ln1_g, ln1_b, w_router, b_router, w_gate,
           b_gate, w_up, b_up, w_down, b_down, ln2_g, ln2_b):
    B, L, D = x.shape
    Lc = ctx.shape[1]
    T = B * L
    rows = L // GRID_W
    l = 0

    cpad = jnp.zeros((8, D), f32).at[0:B].set(c).at[B].set(c_ctx)
    w_cat = jnp.pad(w_in[l], ((0, 0), (0, D_PROJ - w_in.shape[-1]))).astype(bf16)
    wg = [jnp.concatenate([_block_diag(lru_wa[l, d]), _block_diag(lru_wx[l, d])], axis=1).astype(bf16)
          for d in range(2)]
    bg = [jnp.concatenate([lru_ba[l, d], lru_bx[l, d]])[None] for d in range(2)]
    lam = [lru_lam[l, d][None] for d in range(2)]
    wa = [jnp.pad(gla_wa[l, d], ((0, A_LOW_PAD - GATE_RANK), (0, 0))).astype(bf16) for d in range(2)]
    ba = [gla_ba[l, d][None] for d in range(2)]
    cw, cb = conv_w[l], conv_b[l][None]
    wr = jnp.pad(w_router[l], ((0, 0), (0, LANES - N_EXPERTS)))
    br = jnp.pad(b_router[l], (0, LANES - N_EXPERTS))[None]

    mod3 = _ada_mod(cpad, w_ada[l], b_ada[l]).reshape(8, 1, N_MOD * D)

    tiles_per_batch = L // TOKEN_TILE
    xl_c, _, pg_c = _inproj(ctx.reshape(B * Lc, D), mod3, w_cat, lambda i: B)
    zero_h = jnp.zeros((B, 1, D_LRU), f32)
    zero_s = jnp.zeros((B, GLA_DV, D_GLA_K), f32)
    n_cc = Lc // GLA_CHUNK
    pg_c4 = pg_c.reshape(B, n_cc, GLA_CHUNK, D_PG)
    ctx_chunk = lambda width, order: pl.BlockSpec((B, None, GLA_CHUNK, width), lambda n: (0, order(n), 0, 0))
    h_ctx, s_ctx = [], []
    for d in range(2):
        _, hf = _lru_pass(xl_c, cw, cb, wg[d], bg[d], lam[d], zero_h, n_batch=B, row_w=Lc, tile=Lc,
                          reverse=bool(d))
        _, sf = _gla_pass(pg_c4, ctx_chunk, (B, n_cc, GLA_CHUNK, D_GLA_V), ctx_chunk, n_cc, wa[d], ba[d],
                          zero_s, reverse=bool(d))
        h_ctx.append(hf)
        s_ctx.append(sf)

    xt = x.reshape(T, D)
    xl, gl, pg = _inproj(xt, mod3, w_cat, lambda i: i // tiles_per_batch)
    hb, _ = _lru_pass(xl, cw, cb, wg[1], bg[1], lam[1], h_ctx[1], n_batch=B, row_w=GRID_W, tile=LRU_TILE,
                      reverse=True)
    y_lru, _ = _lru_pass(xl, cw, cb, wg[0], bg[0], lam[0], h_ctx[0], n_batch=B, row_w=GRID_W,
                         tile=LRU_TILE, reverse=False, hb=hb, gl=gl)
    pg_v = pg.reshape(B, rows, GRID_W * D_PG)
    col_chunk = lambda width, order: pl.BlockSpec((B, rows, width), lambda n: (0, 0, order(n)))
    o_shape = (B, rows, GRID_W * D_GLA_V)
    ob, _ = _gla_pass(pg_v, col_chunk, o_shape, col_chunk, GRID_W, wa[1], ba[1], s_ctx[1], reverse=True)
    y_gla, _ = _gla_pass(pg_v, col_chunk, o_shape, col_chunk, GRID_W, wa[0], ba[0], s_ctx[0], reverse=False,
                         ob_view=ob, gn=gla_norm_g[l][None])
    y_gla = y_gla.reshape(T, D_GLA_V)

    x1, h, ids, wts, rank, cnt = _outproj_router(
        y_lru, y_gla, xt, mod3, w_out[l].astype(bf16), ln1_g[l][None], ln1_b[l][None], wr, br, L)

    bm = MOE_TILE
    n_tiles = T * TOP_K // bm + N_EXPERTS
    counts = cnt[0, :N_EXPERTS].astype(jnp.int32)
    padded = (counts + bm - 1) // bm * bm
    ends = jnp.cumsum(padded)
    base = ends - padded
    ids4 = ids[:, :TOP_K]
    dest = (base[ids4] + rank[:, :TOP_K]).T.reshape(-1)
    tile_start = jnp.arange(n_tiles, dtype=jnp.int32) * bm
    tile_expert = jnp.minimum(jnp.sum(tile_start[:, None] >= ends[None, :], axis=1), N_EXPERTS - 1)
    tile_expert = tile_expert.astype(jnp.int32)
    tile_valid = jnp.clip(counts[tile_expert] - (tile_start - base[tile_expert]), 0, bm).astype(jnp.int32)

    xs = _sc_dispatch(h, dest, n_tiles * bm)
    eo = _expert_ffn(tile_expert, tile_valid, xs, w_gate[l], b_gate[l], w_up[l], b_up[l], w_down[l],
                     b_down[l])
    eg = _sc_gather(eo, dest).reshape(TOP_K, T, D)
    out = _combine_ln(eg, wts, x1, mod3, ln2_g[l][None], ln2_b[l][None], L)
    return out.reshape(B, L, D)
```

```python
import functools

import jax
import jax.numpy as jnp
from jax import lax
from jax.experimental import pallas as pl
from jax.experimental.pallas import tpu as pltpu
from jax.experimental.pallas import tpu_sc as plsc

D_MODEL = 1024
DEPTH = 1
GRID_W = 64
D_LRU = 512
LRU_BLOCKS = 8
CONV_W = 4
LRU_C = 8.0
GLA_HEADS = 4
D_GLA_V = 512
D_GLA_K = 256
GLA_DK = 64
GLA_DV = 128
GATE_RANK = 16
GATE_TAU = 16.0
GLA_CHUNK = 64
N_EXPERTS = 32
TOP_K = 4
SWIGLU_LIMIT = 7.0
SWIGLU_ALPHA = 1.702
N_MOD = 6
DEEPNORM_ALPHA = (2.0 * DEPTH) ** 0.25
LN_EPS = 1e-5
RMS_EPS = 1e-6

LANES = 128
A_LOW_PAD = LANES
D_PG = 2 * D_GLA_K + 2 * D_GLA_V + A_LOW_PAD
D_PROJ = 2 * D_LRU + D_PG
TOKEN_TILE = 512
LRU_TILE = 256
MOE_TILE = 512
MOE_SUBTILE = 256
SC_CHUNK = 64
VMEM_LIMIT = 48 * 1024 * 1024

f32 = jnp.float32
bf16 = jnp.bfloat16


def _cparams(sem):
    return pltpu.CompilerParams(dimension_semantics=sem, vmem_limit_bytes=VMEM_LIMIT)


def _ada_kernel(c_ref, w_ref, b_ref, o_ref):
    s = c_ref[...]
    s = s * jax.nn.sigmoid(s)
    o_ref[...] = jnp.dot(s.astype(bf16), w_ref[...].astype(bf16), preferred_element_type=f32) + b_ref[...]


def _ada_mod(cpad, w, b):
    n = w.shape[1]
    tn = 1024
    return pl.pallas_call(
        _ada_kernel,
        grid=(n // tn,),
        in_specs=[pl.BlockSpec((8, D_MODEL), lambda j: (0, 0)),
                  pl.BlockSpec((D_MODEL, tn), lambda j: (0, j)),
                  pl.BlockSpec((1, tn), lambda j: (0, j))],
        out_specs=pl.BlockSpec((8, tn), lambda j: (0, j)),
        out_shape=jax.ShapeDtypeStruct((8, n), f32),
        compiler_params=_cparams(("arbitrary",)),
        name="ada_mod",
    )(cpad, w, b.reshape(1, n))


def _inproj_kernel(x_ref, mod_ref, w_ref, *out_refs, parts, col_major):
    sh = mod_ref[0, :, 0:D_MODEL]
    sc = mod_ref[0, :, D_MODEL:2 * D_MODEL]
    if col_major:
        x = jnp.concatenate([x_ref[0, :, j, :] for j in range(x_ref.shape[2])], axis=0)
    else:
        x = x_ref[...]
    u = (x * (1.0 + sc) + sh).astype(bf16)
    for (lo, hi), o_ref in zip(parts, out_refs):
        p = jnp.dot(u, w_ref[:, lo:hi], preferred_element_type=f32).astype(o_ref.dtype)
        o_ref[...] = p.reshape(o_ref.shape)


def _inproj_rows(xt, mod3, w_cat, batch_of_tile, parts, dtypes):
    T = xt.shape[0]
    tm = TOKEN_TILE
    return pl.pallas_call(
        functools.partial(_inproj_kernel, parts=parts, col_major=False),
        grid=(T // tm,),
        in_specs=[pl.BlockSpec((tm, D_MODEL), lambda i: (i, 0)),
                  pl.BlockSpec((1, 1, 2 * D_MODEL), lambda i: (batch_of_tile(i), 0, 0)),
                  pl.BlockSpec((D_MODEL, D_PROJ), lambda i: (0, 0))],
        out_specs=[pl.BlockSpec((tm, hi - lo), lambda i: (i, 0)) for lo, hi in parts],
        out_shape=[jax.ShapeDtypeStruct((T, hi - lo), dt) for (lo, hi), dt in zip(parts, dtypes)],
        compiler_params=_cparams(("arbitrary",)),
        name="inproj_rows",
    )(xt, mod3, w_cat)


def _inproj_cols(x4, mod3, w_cat, part):
    n_batch, rows, cols, _ = x4.shape
    lo, hi = part
    cb = TOKEN_TILE // rows
    return pl.pallas_call(
        functools.partial(_inproj_kernel, parts=(part,), col_major=True),
        grid=(n_batch, cols // cb),
        in_specs=[pl.BlockSpec((1, rows, cb, D_MODEL), lambda b, n: (b, 0, n, 0)),
                  pl.BlockSpec((1, 1, 2 * D_MODEL), lambda b, n: (b, 0, 0)),
                  pl.BlockSpec((D_MODEL, D_PROJ), lambda b, n: (0, 0))],
        out_specs=[pl.BlockSpec((1, cb, rows, hi - lo), lambda b, n: (b, n, 0, 0))],
        out_shape=[jax.ShapeDtypeStruct((n_batch, cols, rows, hi - lo), bf16)],
        compiler_params=_cparams(("arbitrary", "arbitrary")),
        name="inproj_cols",
    )(x4, mod3, w_cat)[0]


def _gelu_tanh(x):
    return 0.5 * x * (1.0 + jnp.tanh(0.7978845608028654 * (x + 0.044715 * (x * x * x))))


def _lru_kernel(*refs, row_w, reverse, merge):
    if merge:
        (xl_ref, cw_ref, cb_ref, wg_ref, bg_ref, lam_ref, h0_ref, hb_ref, gl_ref,
         out_ref, hfin_ref, carry) = refs
    else:
        xl_ref, cw_ref, cb_ref, wg_ref, bg_ref, lam_ref, h0_ref, out_ref, hfin_ref, carry = refs
    t = pl.program_id(1)

    @pl.when(t == 0)
    def _():
        carry[...] = h0_ref[0]

    x = xl_ref[...]
    tt, ch = x.shape
    row = lax.broadcasted_iota(jnp.int32, (tt, ch), 0)
    col = row & (row_w - 1)

    def shifted(v, s):
        r = pltpu.roll(v, s % tt, 0)
        ok = (col >= s) if s > 0 else (col < row_w + s)
        return jnp.where(ok, r, 0.0)

    cw = cw_ref[...]
    u = (cb_ref[...] + shifted(x, 2) * cw[0:1] + shifted(x, 1) * cw[1:2]
         + x * cw[2:3] + shifted(x, -1) * cw[3:4])
    g = jnp.dot(u.astype(bf16), wg_ref[...], preferred_element_type=f32) + bg_ref[...]
    r_gate = jax.nn.sigmoid(g[:, 0:ch])
    i_gate = jax.nn.sigmoid(g[:, ch:2 * ch])
    lam = lam_ref[...]
    softplus_neg = jnp.maximum(-lam, 0.0) + jnp.log(1.0 + jnp.exp(-jnp.abs(lam)))
    log_a = (-LRU_C * softplus_neg) * r_gate
    a = jnp.exp(log_a)
    t2 = 2.0 * log_a
    series = -t2 * (1.0 + t2 * (0.5 + t2 * (1.0 / 6.0 + t2 * (1.0 / 24.0 + t2 * (1.0 / 120.0)))))
    one_minus_a2 = jnp.where(t2 > -0.03, series, 1.0 - jnp.exp(t2))
    b = jnp.sqrt(one_minus_a2) * (i_gate * u)

    d = 1
    while d < tt:
        if reverse:
            a_s = pltpu.roll(a, tt - d, 0)
            b_s = pltpu.roll(b, tt - d, 0)
            ok = row < tt - d
        else:
            a_s = pltpu.roll(a, d, 0)
            b_s = pltpu.roll(b, d, 0)
            ok = row >= d
        b = jnp.where(ok, a * b_s + b, b)
        a = jnp.where(ok, a * a_s, a)
        d *= 2
    h = a * carry[...] + b
    last = h[0:1] if reverse else h[tt - 1:tt]
    carry[...] = last
    hfin_ref[0] = last
    if merge:
        out_ref[...] = ((h + hb_ref[...]) * _gelu_tanh(gl_ref[...].astype(f32))).astype(out_ref.dtype)
    else:
        out_ref[...] = h


def _lru_pass(xl, conv_w, conv_b, wg, bg, lam, h0, *, n_batch, row_w, tile, reverse, hb=None, gl=None):
    T = xl.shape[0]
    nt = T // n_batch // tile
    merge = hb is not None

    def tok(b, t):
        return (b * nt + (nt - 1 - t if reverse else t), 0)

    const = lambda b, t: (0, 0)
    in_specs = [pl.BlockSpec((tile, D_LRU), tok),
                pl.BlockSpec((CONV_W, D_LRU), const),
                pl.BlockSpec((1, D_LRU), const),
                pl.BlockSpec((D_LRU, 2 * D_LRU), const),
                pl.BlockSpec((1, 2 * D_LRU), const),
                pl.BlockSpec((1, D_LRU), const),
                pl.BlockSpec((1, 1, D_LRU), lambda b, t: (b, 0, 0))]
    args = [xl, conv_w, conv_b, wg, bg, lam, h0]
    if merge:
        in_specs += [pl.BlockSpec((tile, D_LRU), tok), pl.BlockSpec((tile, D_LRU), tok)]
        args += [hb, gl]
    return pl.pallas_call(
        functools.partial(_lru_kernel, row_w=row_w, reverse=reverse, merge=merge),
        grid=(n_batch, nt),
        in_specs=in_specs,
        out_specs=[pl.BlockSpec((tile, D_LRU), tok),
                   pl.BlockSpec((1, 1, D_LRU), lambda b, t: (b, 0, 0))],
        out_shape=[jax.ShapeDtypeStruct((T, D_LRU), bf16 if merge else f32),
                   jax.ShapeDtypeStruct((n_batch, 1, D_LRU), f32)],
        scratch_shapes=[pltpu.VMEM((1, D_LRU), f32)],
        compiler_params=_cparams(("arbitrary", "arbitrary")),
        name="lru_merge" if merge else "lru_scan",
    )(*args)


def _gla_kernel(*refs, reverse, merge):
    if merge:
        pg_ref, wa_ref, ba_ref, s0_ref, ob_ref, gn_ref, out_ref, sfin_ref, state = refs
    else:
        pg_ref, wa_ref, ba_ref, s0_ref, out_ref, sfin_ref, state = refs
    n = pl.program_id(0)

    @pl.when(n == 0)
    def _():
        state[...] = s0_ref[...]

    n_batch = pg_ref.shape[0]
    ck = GLA_CHUNK
    ii = lax.broadcasted_iota(jnp.int32, (ck, ck), 0)
    jj = lax.broadcasted_iota(jnp.int32, (ck, ck), 1)
    tri = (jj >= ii) if reverse else (jj <= ii)
    trif = tri.astype(f32)
    head_of_lane = lax.broadcasted_iota(jnp.int32, (1, D_GLA_K), 1) // GLA_DK
    nt_dims = (((1,), (1,)), ((), ()))
    tn_dims = (((0,), (0,)), ((), ()))
    o_q, o_k, o_v, o_g, o_a = 0, D_GLA_K, 2 * D_GLA_K, 2 * D_GLA_K + D_GLA_V, 2 * D_GLA_K + 2 * D_GLA_V
    for bi in range(n_batch):
        q = pg_ref[bi, :, o_q:o_k].astype(f32) * (GLA_DK ** -0.5)
        k = pg_ref[bi, :, o_k:o_v].astype(f32)
        v = pg_ref[bi, :, o_v:o_g]
        a_low = pg_ref[bi, :, o_a:o_a + A_LOW_PAD]
        z = jnp.dot(a_low, wa_ref[...], preferred_element_type=f32) + ba_ref[...]
        log_alpha = (jnp.minimum(z, 0.0) - jnp.log(1.0 + jnp.exp(-jnp.abs(z)))) * (1.0 / GATE_TAU)
        bcum = jnp.dot(trif, log_alpha, preferred_element_type=f32, precision=lax.Precision.HIGHEST)
        btot = bcum[0:1] if reverse else bcum[ck - 1:ck]
        q_dec = q * jnp.exp(bcum)
        k_dec = (k * jnp.exp(-bcum)).astype(bf16)
        k_end = (k * jnp.exp(btot - bcum)).astype(bf16)
        s_t = state[bi]
        s_tb = s_t.astype(bf16)
        kv_t = jnp.zeros_like(s_t)
        outs = []
        for hd in range(GLA_HEADS):
            in_head = head_of_lane == hd
            qh = jnp.where(in_head, q_dec, 0.0).astype(bf16)
            scores = lax.dot_general(qh, k_dec, nt_dims, preferred_element_type=f32)
            scores = jnp.where(tri, scores, 0.0).astype(bf16)
            vh = v[:, hd * GLA_DV:(hd + 1) * GLA_DV]
            o_h = (jnp.dot(scores, vh, preferred_element_type=f32)
                   + lax.dot_general(qh, s_tb, nt_dims, preferred_element_type=f32))
            outs.append(o_h)
            p = lax.dot_general(vh, k_end, tn_dims, preferred_element_type=f32)
            kv_t = jnp.where(in_head, p, kv_t)
        state[bi] = s_t * jnp.exp(btot) + kv_t
        if merge:
            g = pg_ref[bi, :, o_g:o_a].astype(f32)
            normed = []
            for hd in range(GLA_HEADS):
                oh = outs[hd] + ob_ref[bi, :, hd * GLA_DV:(hd + 1) * GLA_DV]
                ms = jnp.mean(oh * oh, axis=-1, keepdims=True)
                normed.append(oh * lax.rsqrt(ms + RMS_EPS))
            y = jnp.concatenate(normed, axis=-1) * gn_ref[...] * (g * jax.nn.sigmoid(g))
            out_ref[bi, :, n % out_ref.shape[2], :] = y
        else:
            out_ref[bi] = jnp.concatenate(outs, axis=-1)

    @pl.when(n == pl.num_programs(0) - 1)
    def _():
        sfin_ref[...] = state[...]


def _gla_pass(pg, wa, ba, s0, *, reverse, ob=None, gn=None):
    n_batch, n_chunks = pg.shape[0], pg.shape[1]
    merge = ob is not None
    order = (lambda n: n_chunks - 1 - n) if reverse else (lambda n: n)
    const2 = lambda n: (0, 0)
    const3 = lambda n: (0, 0, 0)
    chunk = lambda width: pl.BlockSpec((n_batch, None, GLA_CHUNK, width), lambda n: (0, order(n), 0, 0))
    in_specs = [chunk(D_PG),
                pl.BlockSpec((A_LOW_PAD, D_GLA_K), const2),
                pl.BlockSpec((1, D_GLA_K), const2),
                pl.BlockSpec((n_batch, GLA_DV, D_GLA_K), const3)]
    args = [pg, wa, ba, s0]
    if merge:
        in_specs += [chunk(D_GLA_V), pl.BlockSpec((1, D_GLA_V), const2)]
        args += [ob, gn]
        cols_per_block = 8
        out_spec = pl.BlockSpec((n_batch, GLA_CHUNK, cols_per_block, D_GLA_V),
                                lambda n: (0, 0, n // cols_per_block, 0))
        out_shape = (n_batch, GLA_CHUNK, n_chunks, D_GLA_V)
    else:
        out_spec = chunk(D_GLA_V)
        out_shape = (n_batch, n_chunks, GLA_CHUNK, D_GLA_V)
    return pl.pallas_call(
        functools.partial(_gla_kernel, reverse=reverse, merge=merge),
        grid=(n_chunks,),
        in_specs=in_specs,
        out_specs=[out_spec,
                   pl.BlockSpec((n_batch, GLA_DV, D_GLA_K), const3)],
        out_shape=[jax.ShapeDtypeStruct(out_shape, f32),
                   jax.ShapeDtypeStruct((n_batch, GLA_DV, D_GLA_K), f32)],
        scratch_shapes=[pltpu.VMEM((n_batch, GLA_DV, D_GLA_K), f32)],
        compiler_params=_cparams(("arbitrary",)),
        name="gla_merge" if merge else "gla_scan",
    )(*args)


def _pack_rows(v):
    n = v.shape[1] // 2
    w = lax.bitcast_convert_type(v.astype(bf16).astype(f32), jnp.int32)
    return (w[:, :n] & jnp.int32(-65536)) | lax.shift_right_logical(w[:, n:], 16)


def _unpack_rows(w):
    hi = lax.bitcast_convert_type(w & jnp.int32(-65536), f32)
    lo = lax.bitcast_convert_type(lax.shift_left(w, 16), f32)
    return hi, lo


def _outproj_router_kernel(yl_ref, yg_ref, x_ref, mod_ref, wo_ref, lg_ref, lb_ref, wrh_ref, wrl_ref, br_ref,
                           x1_ref, hp_ref, ids_ref, rank_ref, wts_ref, cnt_ref, running):
    i = pl.program_id(0)

    @pl.when(i == 0)
    def _():
        running[...] = jnp.zeros_like(running)

    d = D_MODEL
    y = (jnp.dot(yl_ref[...], wo_ref[0:D_LRU, :], preferred_element_type=f32)
         + jnp.dot(yg_ref[...].astype(bf16), wo_ref[D_LRU:2 * D_LRU, :], preferred_element_type=f32))
    g1 = mod_ref[0, :, 2 * d:3 * d]
    sh2 = mod_ref[0, :, 3 * d:4 * d]
    sc2 = mod_ref[0, :, 4 * d:5 * d]
    z = DEEPNORM_ALPHA * x_ref[...] + g1 * y
    mu = jnp.mean(z, axis=-1, keepdims=True)
    zc = z - mu
    var = jnp.mean(zc * zc, axis=-1, keepdims=True)
    x1 = zc * lax.rsqrt(var + LN_EPS) * lg_ref[...] + lb_ref[...]
    x1_ref[...] = x1
    hmod = x1 * (1.0 + sc2) + sh2
    hp_ref[...] = _pack_rows(hmod)

    nt_dims = (((1,), (1,)), ((), ()))
    h_hi = hmod.astype(bf16)
    h_lo = (hmod - h_hi.astype(f32)).astype(bf16)
    logits = (lax.dot_general(wrh_ref[...], h_hi, nt_dims, preferred_element_type=f32)
              + lax.dot_general(wrh_ref[...], h_lo, nt_dims, preferred_element_type=f32)
              + lax.dot_general(wrl_ref[...], h_hi, nt_dims, preferred_element_type=f32)) + br_ref[...]
    ne, tm = logits.shape
    expert = lax.broadcasted_iota(jnp.int32, (ne, tm), 0).astype(f32)
    neg_inf = jnp.float32(-jnp.inf)
    live = logits
    sel = jnp.zeros((ne, tm), f32)
    ids, vals = [], []
    for _ in range(TOP_K):
        m = jnp.max(live, axis=0, keepdims=True)
        j = jnp.min(jnp.where(live == m, expert, float(ne)), axis=0, keepdims=True)
        pick = expert == j
        sel = jnp.where(pick, 1.0, sel)
        live = jnp.where(pick, neg_inf, live)
        ids.append(j)
        vals.append(m)
    exps = [jnp.exp(vk - vals[0]) for vk in vals]
    denom = exps[0] + exps[1] + exps[2] + exps[3]

    ri = lax.broadcasted_iota(jnp.int32, (tm, tm), 0)
    ci = lax.broadcasted_iota(jnp.int32, (tm, tm), 1)
    earlier = (ri < ci).astype(bf16)
    rank_dense = running[...] + jnp.dot(sel.astype(bf16), earlier, preferred_element_type=f32)
    running[...] = running[...] + jnp.sum(sel, axis=1, keepdims=True)
    cnt_ref[...] = running[...]

    slot = lax.broadcasted_iota(jnp.int32, (8, tm), 0)
    ids_out = jnp.zeros((8, tm), f32)
    rank_out = jnp.zeros((8, tm), f32)
    wts_out = jnp.zeros((8, tm), f32)
    for kk in range(TOP_K):
        rk = jnp.sum(jnp.where(expert == ids[kk], rank_dense, 0.0), axis=0, keepdims=True)
        here = slot == kk
        ids_out = jnp.where(here, ids[kk], ids_out)
        rank_out = jnp.where(here, rk, rank_out)
        wts_out = jnp.where(here, exps[kk] / denom, wts_out)
    ids_ref[...] = ids_out.astype(jnp.int32)
    rank_ref[...] = rank_out.astype(jnp.int32)
    wts_ref[...] = jnp.concatenate([wts_out, jnp.zeros((LANES - 8, tm), f32)], axis=0).T


def _outproj_router(yl, yg, xt, mod3, wo, lg, lb, wr_hi, wr_lo, br, tokens_per_batch):
    T = xt.shape[0]
    tm = TOKEN_TILE
    tiles_per_batch = tokens_per_batch // tm
    tokrow = lambda i: (i, 0)
    const = lambda i: (0, 0)
    return pl.pallas_call(
        _outproj_router_kernel,
        grid=(T // tm,),
        in_specs=[pl.BlockSpec((tm, D_LRU), tokrow),
                  pl.BlockSpec((tm, D_GLA_V), tokrow),
                  pl.BlockSpec((tm, D_MODEL), tokrow),
                  pl.BlockSpec((1, 1, N_MOD * D_MODEL), lambda i: (i // tiles_per_batch, 0, 0)),
                  pl.BlockSpec((D_MODEL, D_MODEL), const),
                  pl.BlockSpec((1, D_MODEL), const),
                  pl.BlockSpec((1, D_MODEL), const),
                  pl.BlockSpec((N_EXPERTS, D_MODEL), const),
                  pl.BlockSpec((N_EXPERTS, D_MODEL), const),
                  pl.BlockSpec((N_EXPERTS, 1), const)],
        out_specs=[pl.BlockSpec((tm, D_MODEL), tokrow),
                   pl.BlockSpec((tm, D_MODEL // 2), tokrow),
                   pl.BlockSpec((8, tm), lambda i: (0, i)),
                   pl.BlockSpec((8, tm), lambda i: (0, i)),
                   pl.BlockSpec((tm, LANES), tokrow),
                   pl.BlockSpec((N_EXPERTS, 1), const)],
        out_shape=[jax.ShapeDtypeStruct((T, D_MODEL), f32),
                   jax.ShapeDtypeStruct((T, D_MODEL // 2), jnp.int32),
                   jax.ShapeDtypeStruct((8, T), jnp.int32),
                   jax.ShapeDtypeStruct((8, T), jnp.int32),
                   jax.ShapeDtypeStruct((T, LANES), f32),
                   jax.ShapeDtypeStruct((N_EXPERTS, 1), f32)],
        scratch_shapes=[pltpu.VMEM((N_EXPERTS, 1), f32)],
        compiler_params=_cparams(("arbitrary",)),
        name="outproj_router",
    )(yl, yg, xt, mod3, wo, lg, lb, wr_hi, wr_lo, br)


def _route_kernel(ids_ref, rank_ref, cnt_ref, dest_ref, tiles_ref, *, bm):
    ne = cnt_ref.shape[0]
    cnt = cnt_ref[...]
    padded = jnp.floor((cnt + (bm - 1.0)) * (1.0 / bm)) * bm
    sub = lax.broadcasted_iota(jnp.int32, (ne, LANES), 0)
    lane = lax.broadcasted_iota(jnp.int32, (ne, LANES), 1)
    padded_row = jnp.sum(jnp.where(sub == lane, padded, 0.0), axis=0, keepdims=True)
    base = jnp.sum(jnp.where(lane < sub, padded_row, 0.0), axis=1, keepdims=True)
    ends = base + padded

    tc = ids_ref.shape[1]
    expert = lax.broadcasted_iota(jnp.int32, (ne, tc), 0)
    slot = lax.broadcasted_iota(jnp.int32, (8, tc), 0)
    ids = ids_ref[...]
    group_base = jnp.zeros((8, tc), f32)
    for kk in range(TOP_K):
        b_k = jnp.sum(jnp.where(expert == ids[kk:kk + 1], base, 0.0), axis=0, keepdims=True)
        group_base = jnp.where(slot == kk, b_k, group_base)
    dest_ref[...] = group_base.astype(jnp.int32) + rank_ref[...]

    nl = tiles_ref.shape[1]
    start = lax.broadcasted_iota(jnp.int32, (ne, nl), 1).astype(f32) * bm
    te = jnp.minimum(jnp.sum(jnp.where(start >= ends, 1.0, 0.0), axis=0, keepdims=True), ne - 1.0)
    at_te = lax.broadcasted_iota(jnp.int32, (ne, nl), 0).astype(f32) == te
    cnt_te = jnp.sum(jnp.where(at_te, cnt, 0.0), axis=0, keepdims=True)
    base_te = jnp.sum(jnp.where(at_te, base, 0.0), axis=0, keepdims=True)
    valid = jnp.clip(cnt_te - (start[0:1] - base_te), 0.0, float(bm))
    srow = lax.broadcasted_iota(jnp.int32, (8, nl), 0)
    tiles_ref[...] = jnp.where(srow == 0, te, jnp.where(srow == 1, valid, 0.0)).astype(jnp.int32)


def _route(ids, rank, cnt, bm, n_tiles):
    T = ids.shape[1]
    tc = 2048
    nl = -(-n_tiles // LANES) * LANES
    tok = lambda i: (0, i)
    const = lambda i: (0, 0)
    return pl.pallas_call(
        functools.partial(_route_kernel, bm=bm),
        grid=(T // tc,),
        in_specs=[pl.BlockSpec((8, tc), tok), pl.BlockSpec((8, tc), tok),
                  pl.BlockSpec((N_EXPERTS, 1), const)],
        out_specs=[pl.BlockSpec((8, tc), tok), pl.BlockSpec((8, nl), const)],
        out_shape=[jax.ShapeDtypeStruct((8, T), jnp.int32), jax.ShapeDtypeStruct((8, nl), jnp.int32)],
        compiler_params=_cparams(("arbitrary",)),
        name="route",
    )(ids, rank, cnt)


def _sc_workers():
    info = plsc.get_sparse_core_info()
    return info.num_cores, info.num_subcores


def _sc_dispatch(rows, dest_flat, n_out):
    T, D = rows.shape
    nc, ns = _sc_workers()
    per_w = T // (nc * ns)
    n_chunks = per_w // SC_CHUNK
    mesh = plsc.VectorSubcoreMesh(core_axis_name="c", subcore_axis_name="s")

    @functools.partial(
        pl.kernel, mesh=mesh,
        out_type=jax.ShapeDtypeStruct((n_out, D), rows.dtype),
        scratch_types=([pltpu.VMEM((SC_CHUNK,), jnp.int32)] * TOP_K
                       + [pltpu.VMEM((SC_CHUNK, D), rows.dtype)]
                       + [pltpu.SemaphoreType.DMA] * TOP_K),
    )
    def k(rows_hbm, dest_hbm, out_hbm, *scratch):
        idx_v = scratch[:TOP_K]
        rows_v = scratch[TOP_K]
        sems = scratch[TOP_K + 1:]
        wid = lax.axis_index("s") * nc + lax.axis_index("c")
        base = wid * per_w

        @pl.loop(0, n_chunks)
        def _(j):
            off = base + j * SC_CHUNK
            pltpu.sync_copy(rows_hbm.at[pl.ds(off, SC_CHUNK)], rows_v)
            for kk in range(TOP_K):
                pltpu.sync_copy(dest_hbm.at[pl.ds(kk * T + off, SC_CHUNK)], idx_v[kk])
            copies = [pltpu.async_copy(rows_v, out_hbm.at[idx_v[kk]], sems[kk]) for kk in range(TOP_K)]
            for cp in copies:
                cp.wait()

    return k(rows, dest_flat)


def _sc_gather(table, idx):
    _, D = table.shape
    N = idx.shape[0]
    nc, ns = _sc_workers()
    per_w = N // (nc * ns)
    n_chunks = per_w // SC_CHUNK
    assert n_chunks % 2 == 0
    mesh = plsc.VectorSubcoreMesh(core_axis_name="c", subcore_axis_name="s")

    @functools.partial(
        pl.kernel, mesh=mesh,
        out_type=jax.ShapeDtypeStruct((N, D), table.dtype),
        scratch_types=([pltpu.VMEM((SC_CHUNK,), jnp.int32)] * 2
                       + [pltpu.VMEM((SC_CHUNK, D), table.dtype)] * 2
                       + [pltpu.SemaphoreType.DMA] * 2),
    )
    def k(table_hbm, idx_hbm, out_hbm, idx0, idx1, buf0, buf1, sem0, sem1):
        idxs, bufs, sems = (idx0, idx1), (buf0, buf1), (sem0, sem1)
        wid = lax.axis_index("s") * nc + lax.axis_index("c")
        base = wid * per_w

        def gather(b):
            return pltpu.make_async_copy(table_hbm.at[idxs[b]], bufs[b], sems[b])

        def start(chunk, b):
            pltpu.sync_copy(idx_hbm.at[pl.ds(base + chunk * SC_CHUNK, SC_CHUNK)], idxs[b])
            gather(b).start()

        def finish(chunk, b):
            gather(b).wait()
            pltpu.sync_copy(bufs[b], out_hbm.at[pl.ds(base + chunk * SC_CHUNK, SC_CHUNK)])

        start(0, 0)

        @pl.loop(0, n_chunks, step=2)
        def _(j):
            start(j + 1, 1)
            finish(j, 0)

            @pl.when(j + 2 < n_chunks)
            def _():
                start(j + 2, 0)

            finish(j + 1, 1)

    return k(table, idx)


def _ffn_kernel(te_ref, tv_ref, xs_ref, wg_ref, bg_ref, wu_ref, bu_ref, wd_ref, bd_ref, eo_ref,
                wg_b, wu_b, wd_b):
    i = pl.program_id(0)
    prev = te_ref[jnp.maximum(i - 1, 0)]

    @pl.when((i == 0) | (te_ref[i] != prev))
    def _():
        wg_b[...] = wg_ref[...].astype(bf16)
        wu_b[...] = wu_ref[...].astype(bf16)
        wd_b[...] = wd_ref[...].astype(bf16)

    valid = tv_ref[i]
    for part in range(xs_ref.shape[0] // MOE_SUBTILE):
        r0 = part * MOE_SUBTILE

        @pl.when(valid > r0)
        def _(r0=r0):
            row = lax.broadcasted_iota(jnp.int32, (MOE_SUBTILE, 1), 0) + r0
            xw = jnp.where(row < valid, xs_ref[r0:r0 + MOE_SUBTILE, :], 0)
            x_hi, x_lo = _unpack_rows(xw)
            x = jnp.concatenate([x_hi, x_lo], axis=1).astype(bf16)
            gate = jnp.minimum(jnp.dot(x, wg_b[...], preferred_element_type=f32) + bg_ref[...], SWIGLU_LIMIT)
            up = jnp.clip(jnp.dot(x, wu_b[...], preferred_element_type=f32) + bu_ref[...],
                          -SWIGLU_LIMIT, SWIGLU_LIMIT)
            act = (up + 1.0) * gate * jax.nn.sigmoid(SWIGLU_ALPHA * gate)
            out = jnp.dot(act.astype(bf16), wd_b[...], preferred_element_type=f32) + bd_ref[...]
            eo_ref[r0:r0 + MOE_SUBTILE, :] = _pack_rows(out)


def _expert_ffn(tile_expert, tile_valid, xs, w_gate, b_gate, w_up, b_up, w_down, b_down):
    n_rows, dp = xs.shape
    d = 2 * dp
    bm = MOE_TILE
    d_e = w_gate.shape[-1]
    wspec = lambda k_, n_: pl.BlockSpec((None, k_, n_), lambda i, te, tv: (te[i], 0, 0))
    grid_spec = pltpu.PrefetchScalarGridSpec(
        num_scalar_prefetch=2,
        grid=(n_rows // bm,),
        in_specs=[pl.BlockSpec((bm, dp), lambda i, te, tv: (i, 0)),
                  wspec(d, d_e), wspec(1, d_e), wspec(d, d_e), wspec(1, d_e), wspec(d_e, d), wspec(1, d)],
        out_specs=pl.BlockSpec((bm, dp), lambda i, te, tv: (i, 0)),
        scratch_shapes=[pltpu.VMEM((d, d_e), bf16), pltpu.VMEM((d, d_e), bf16), pltpu.VMEM((d_e, d), bf16)],
    )
    return pl.pallas_call(
        _ffn_kernel,
        grid_spec=grid_spec,
        out_shape=jax.ShapeDtypeStruct((n_rows, dp), jnp.int32),
        compiler_params=_cparams(("arbitrary",)),
        name="expert_ffn",
    )(tile_expert, tile_valid, xs, w_gate, b_gate.reshape(N_EXPERTS, 1, d_e), w_up,
      b_up.reshape(N_EXPERTS, 1, d_e), w_down, b_down.reshape(N_EXPERTS, 1, d))


def _combine_kernel(eg_ref, wts_ref, x1_ref, mod_ref, lg_ref, lb_ref, o_ref):
    w = wts_ref[...]
    y_hi, y_lo = _unpack_rows(eg_ref[0])
    y_hi, y_lo = y_hi * w[:, 0:1], y_lo * w[:, 0:1]
    for kk in range(1, TOP_K):
        e_hi, e_lo = _unpack_rows(eg_ref[kk])
        y_hi = y_hi + e_hi * w[:, kk:kk + 1]
        y_lo = y_lo + e_lo * w[:, kk:kk + 1]
    y = jnp.concatenate([y_hi, y_lo], axis=1)
    z = DEEPNORM_ALPHA * x1_ref[...] + mod_ref[0] * y
    mu = jnp.mean(z, axis=-1, keepdims=True)
    zc = z - mu
    var = jnp.mean(zc * zc, axis=-1, keepdims=True)
    o_ref[...] = zc * lax.rsqrt(var + LN_EPS) * lg_ref[...] + lb_ref[...]


def _combine_ln(eg, wts, x1, mod3, lg, lb, tokens_per_batch):
    T = x1.shape[0]
    tm = TOKEN_TILE
    tiles_per_batch = tokens_per_batch // tm
    const = lambda i: (0, 0)
    return pl.pallas_call(
        _combine_kernel,
        grid=(T // tm,),
        in_specs=[pl.BlockSpec((TOP_K, tm, D_MODEL // 2), lambda i: (0, i, 0)),
                  pl.BlockSpec((tm, LANES), lambda i: (i, 0)),
                  pl.BlockSpec((tm, D_MODEL), lambda i: (i, 0)),
                  pl.BlockSpec((1, 1, D_MODEL), lambda i: (i // tiles_per_batch, 0, N_MOD - 1)),
                  pl.BlockSpec((1, D_MODEL), const),
                  pl.BlockSpec((1, D_MODEL), const)],
        out_specs=pl.BlockSpec((tm, D_MODEL), lambda i: (i, 0)),
        out_shape=jax.ShapeDtypeStruct((T, D_MODEL), f32),
        compiler_params=_cparams(("arbitrary",)),
        name="combine_ln",
    )(eg, wts, x1, mod3, lg, lb)


def _block_diag(w):
    n, c, d = w.shape
    eye = jnp.eye(n, dtype=w.dtype)
    return jnp.einsum('ncd,nm->ncmd', w, eye).reshape(n * c, n * d)


def kernel(x, c, ctx, c_ctx, w_ada, b_ada, w_in, conv_w, conv_b, lru_wa, lru_ba, lru_wx, lru_bx,
           lru_lam, gla_wa, gla_ba, gla_norm_g, w_out, ln1_g, ln1_b, w_router, b_router, w_gate,
           b_gate, w_up, b_up, w_down, b_down, ln2_g, ln2_b):
    B, L, D = x.shape
    Lc = ctx.shape[1]
    T = B * L
    rows = L // GRID_W
    l = 0

    cpad = jnp.zeros((8, D), f32).at[0:B].set(c).at[B].set(c_ctx)
    w_cat = jnp.pad(w_in[l], ((0, 0), (0, D_PROJ - w_in.shape[-1]))).astype(bf16)
    wg = [jnp.concatenate([_block_diag(lru_wa[l, d]), _block_diag(lru_wx[l, d])], axis=1).astype(bf16)
          for d in range(2)]
    bg = [jnp.concatenate([lru_ba[l, d], lru_bx[l, d]])[None] for d in range(2)]
    lam = [lru_lam[l, d][None] for d in range(2)]
    wa = [jnp.pad(gla_wa[l, d], ((0, A_LOW_PAD - GATE_RANK), (0, 0))).astype(bf16) for d in range(2)]
    ba = [gla_ba[l, d][None] for d in range(2)]
    cw, cb = conv_w[l], conv_b[l][None]
    wr_t = w_router[l].T
    wr_hi = wr_t.astype(bf16)
    wr_lo = (wr_t - wr_hi.astype(f32)).astype(bf16)
    br = b_router[l][:, None]
    c_xl, c_gl, c_pg = (0, D_LRU), (D_LRU, 2 * D_LRU), (2 * D_LRU, D_PROJ)

    mod3 = _ada_mod(cpad, w_ada[l], b_ada[l]).reshape(8, 1, N_MOD * D)

    tiles_per_batch = L // TOKEN_TILE
    xl_c, pg_c = _inproj_rows(ctx.reshape(B * Lc, D), mod3, w_cat, lambda i: B, (c_xl, c_pg), (f32, bf16))
    zero_h = jnp.zeros((B, 1, D_LRU), f32)
    zero_s = jnp.zeros((B, GLA_DV, D_GLA_K), f32)
    pg_c = pg_c.reshape(B, Lc // GLA_CHUNK, GLA_CHUNK, D_PG)
    h_ctx, s_ctx = [], []
    for d in range(2):
        _, hf = _lru_pass(xl_c, cw, cb, wg[d], bg[d], lam[d], zero_h, n_batch=B, row_w=Lc, tile=Lc,
                          reverse=bool(d))
        _, sf = _gla_pass(pg_c, wa[d], ba[d], zero_s, reverse=bool(d))
        h_ctx.append(hf)
        s_ctx.append(sf)

    xt = x.reshape(T, D)
    xl, gl = _inproj_rows(xt, mod3, w_cat, lambda i: i // tiles_per_batch, (c_xl, c_gl), (f32, bf16))
    hb, _ = _lru_pass(xl, cw, cb, wg[1], bg[1], lam[1], h_ctx[1], n_batch=B, row_w=GRID_W, tile=LRU_TILE,
                      reverse=True)
    y_lru, _ = _lru_pass(xl, cw, cb, wg[0], bg[0], lam[0], h_ctx[0], n_batch=B, row_w=GRID_W,
                         tile=LRU_TILE, reverse=False, hb=hb, gl=gl)
    pg = _inproj_cols(x.reshape(B, rows, GRID_W, D), mod3, w_cat, c_pg)
    ob, _ = _gla_pass(pg, wa[1], ba[1], s_ctx[1], reverse=True)
    y_gla, _ = _gla_pass(pg, wa[0], ba[0], s_ctx[0], reverse=False, ob=ob, gn=gla_norm_g[l][None])
    y_gla = y_gla.reshape(T, D_GLA_V)

    x1, hp, ids, rank, wts, cnt = _outproj_router(
        y_lru, y_gla, xt, mod3, w_out[l].astype(bf16), ln1_g[l][None], ln1_b[l][None], wr_hi, wr_lo, br, L)

    n_tiles = T * TOP_K // MOE_TILE + N_EXPERTS
    dest, tiles = _route(ids, rank, cnt, MOE_TILE, n_tiles)
    xs = _sc_dispatch(hp, dest.reshape(-1), n_tiles * MOE_TILE)
    eo = _expert_ffn(tiles[0, :n_tiles], tiles[1, :n_tiles], xs, w_gate[l], b_gate[l], w_up[l], b_up[l],
                     w_down[l], b_down[l])
    eg = _sc_gather(eo, dest[:TOP_K].reshape(-1)).reshape(TOP_K, T, D // 2)
    out = _combine_ln(eg, wts, x1, mod3, ln2_g[l][None], ln2_b[l][None], L)
    return out.reshape(B, L, D)
```

```python
import functools

import jax
import jax.numpy as jnp
from jax import lax
from jax.experimental import pallas as pl
from jax.experimental.pallas import tpu as pltpu
from jax.experimental.pallas import tpu_sc as plsc

D_MODEL = 1024
DEPTH = 1
GRID_W = 64
D_LRU = 512
LRU_BLOCKS = 8
CONV_W = 4
LRU_C = 8.0
GLA_HEADS = 4
D_GLA_V = 512
D_GLA_K = 256
GLA_DK = 64
GLA_DV = 128
GATE_RANK = 16
GATE_TAU = 16.0
GLA_CHUNK = 64
N_EXPERTS = 32
TOP_K = 4
SWIGLU_LIMIT = 7.0
SWIGLU_ALPHA = 1.702
N_MOD = 6
DEEPNORM_ALPHA = (2.0 * DEPTH) ** 0.25
LN_EPS = 1e-5
RMS_EPS = 1e-6

LANES = 128
A_LOW_PAD = LANES
D_PG = 2 * D_GLA_K + 2 * D_GLA_V + A_LOW_PAD
D_PROJ = 2 * D_LRU + D_PG
TOKEN_TILE = 512
LRU_TILE = 256
MOE_TILE = 512
MOE_SUBTILE = 256
SC_CHUNK = 64
VMEM_LIMIT = 48 * 1024 * 1024

f32 = jnp.float32
bf16 = jnp.bfloat16


def _cparams(sem):
    return pltpu.CompilerParams(dimension_semantics=sem, vmem_limit_bytes=VMEM_LIMIT)


def _ada_kernel(c_ref, w_ref, b_ref, o_ref):
    s = c_ref[...]
    s = s * jax.nn.sigmoid(s)
    o_ref[...] = jnp.dot(s.astype(bf16), w_ref[...].astype(bf16), preferred_element_type=f32) + b_ref[...]


def _ada_mod(cpad, w, b):
    n = w.shape[1]
    tn = 1024
    return pl.pallas_call(
        _ada_kernel,
        grid=(n // tn,),
        in_specs=[pl.BlockSpec((8, D_MODEL), lambda j: (0, 0)),
                  pl.BlockSpec((D_MODEL, tn), lambda j: (0, j)),
                  pl.BlockSpec((1, tn), lambda j: (0, j))],
        out_specs=pl.BlockSpec((8, tn), lambda j: (0, j)),
        out_shape=jax.ShapeDtypeStruct((8, n), f32),
        compiler_params=_cparams(("arbitrary",)),
        name="ada_mod",
    )(cpad, w, b.reshape(1, n))


def _inproj_kernel(x_ref, mod_ref, w_ref, *out_refs, parts, col_major):
    sh = mod_ref[0, :, 0:D_MODEL]
    sc = mod_ref[0, :, D_MODEL:2 * D_MODEL]
    if col_major:
        x = jnp.concatenate([x_ref[0, :, j, :] for j in range(x_ref.shape[2])], axis=0)
    else:
        x = x_ref[...]
    u = (x * (1.0 + sc) + sh).astype(bf16)
    for (lo, hi), o_ref in zip(parts, out_refs):
        p = jnp.dot(u, w_ref[:, lo:hi], preferred_element_type=f32).astype(o_ref.dtype)
        o_ref[...] = p.reshape(o_ref.shape)


def _inproj_rows(xt, mod3, w_cat, batch_of_tile, parts, dtypes):
    T = xt.shape[0]
    tm = TOKEN_TILE
    return pl.pallas_call(
        functools.partial(_inproj_kernel, parts=parts, col_major=False),
        grid=(T // tm,),
        in_specs=[pl.BlockSpec((tm, D_MODEL), lambda i: (i, 0)),
                  pl.BlockSpec((1, 1, 2 * D_MODEL), lambda i: (batch_of_tile(i), 0, 0)),
                  pl.BlockSpec((D_MODEL, D_PROJ), lambda i: (0, 0))],
        out_specs=[pl.BlockSpec((tm, hi - lo), lambda i: (i, 0)) for lo, hi in parts],
        out_shape=[jax.ShapeDtypeStruct((T, hi - lo), dt) for (lo, hi), dt in zip(parts, dtypes)],
        compiler_params=_cparams(("arbitrary",)),
        name="inproj_rows",
    )(xt, mod3, w_cat)


def _inproj_cols(x4, mod3, w_cat, part):
    n_batch, rows, cols, _ = x4.shape
    lo, hi = part
    cb = TOKEN_TILE // rows
    return pl.pallas_call(
        functools.partial(_inproj_kernel, parts=(part,), col_major=True),
        grid=(n_batch, cols // cb),
        in_specs=[pl.BlockSpec((1, rows, cb, D_MODEL), lambda b, n: (b, 0, n, 0)),
                  pl.BlockSpec((1, 1, 2 * D_MODEL), lambda b, n: (b, 0, 0)),
                  pl.BlockSpec((D_MODEL, D_PROJ), lambda b, n: (0, 0))],
        out_specs=[pl.BlockSpec((1, cb, rows, hi - lo), lambda b, n: (b, n, 0, 0))],
        out_shape=[jax.ShapeDtypeStruct((n_batch, cols, rows, hi - lo), bf16)],
        compiler_params=_cparams(("arbitrary", "arbitrary")),
        name="inproj_cols",
    )(x4, mod3, w_cat)[0]


def _gelu_tanh(x):
    return 0.5 * x * (1.0 + jnp.tanh(0.7978845608028654 * (x + 0.044715 * (x * x * x))))


def _lru_kernel(*refs, row_w, reverse, merge):
    if merge:
        (xl_ref, cw_ref, cb_ref, wg_ref, bg_ref, lam_ref, h0_ref, hb_ref, gl_ref,
         out_ref, hfin_ref, carry) = refs
    else:
        xl_ref, cw_ref, cb_ref, wg_ref, bg_ref, lam_ref, h0_ref, out_ref, hfin_ref, carry = refs
    t = pl.program_id(1)

    @pl.when(t == 0)
    def _():
        carry[...] = h0_ref[0]

    x = xl_ref[...]
    tt, ch = x.shape
    row = lax.broadcasted_iota(jnp.int32, (tt, ch), 0)
    col = row & (row_w - 1)

    def shifted(v, s):
        r = pltpu.roll(v, s % tt, 0)
        ok = (col >= s) if s > 0 else (col < row_w + s)
        return jnp.where(ok, r, 0.0)

    cw = cw_ref[...]
    u = (cb_ref[...] + shifted(x, 2) * cw[0:1] + shifted(x, 1) * cw[1:2]
         + x * cw[2:3] + shifted(x, -1) * cw[3:4])
    g = jnp.dot(u.astype(bf16), wg_ref[...], preferred_element_type=f32) + bg_ref[...]
    r_gate = jax.nn.sigmoid(g[:, 0:ch])
    i_gate = jax.nn.sigmoid(g[:, ch:2 * ch])
    lam = lam_ref[...]
    softplus_neg = jnp.maximum(-lam, 0.0) + jnp.log(1.0 + jnp.exp(-jnp.abs(lam)))
    log_a = (-LRU_C * softplus_neg) * r_gate
    a = jnp.exp(log_a)
    t2 = 2.0 * log_a
    series = -t2 * (1.0 + t2 * (0.5 + t2 * (1.0 / 6.0 + t2 * (1.0 / 24.0 + t2 * (1.0 / 120.0)))))
    one_minus_a2 = jnp.where(t2 > -0.03, series, 1.0 - jnp.exp(t2))
    b = jnp.sqrt(one_minus_a2) * (i_gate * u)

    d = 1
    while d < tt:
        if reverse:
            a_s = pltpu.roll(a, tt - d, 0)
            b_s = pltpu.roll(b, tt - d, 0)
            ok = row < tt - d
        else:
            a_s = pltpu.roll(a, d, 0)
            b_s = pltpu.roll(b, d, 0)
            ok = row >= d
        b = jnp.where(ok, a * b_s + b, b)
        a = jnp.where(ok, a * a_s, a)
        d *= 2
    h = a * carry[...] + b
    last = h[0:1] if reverse else h[tt - 1:tt]
    carry[...] = last
    hfin_ref[0] = last
    if merge:
        out_ref[...] = ((h + hb_ref[...]) * _gelu_tanh(gl_ref[...].astype(f32))).astype(out_ref.dtype)
    else:
        out_ref[...] = h


def _lru_pass(xl, conv_w, conv_b, wg, bg, lam, h0, *, n_batch, row_w, tile, reverse, hb=None, gl=None):
    T = xl.shape[0]
    nt = T // n_batch // tile
    merge = hb is not None

    def tok(b, t):
        return (b * nt + (nt - 1 - t if reverse else t), 0)

    const = lambda b, t: (0, 0)
    in_specs = [pl.BlockSpec((tile, D_LRU), tok),
                pl.BlockSpec((CONV_W, D_LRU), const),
                pl.BlockSpec((1, D_LRU), const),
                pl.BlockSpec((D_LRU, 2 * D_LRU), const),
                pl.BlockSpec((1, 2 * D_LRU), const),
                pl.BlockSpec((1, D_LRU), const),
                pl.BlockSpec((1, 1, D_LRU), lambda b, t: (b, 0, 0))]
    args = [xl, conv_w, conv_b, wg, bg, lam, h0]
    if merge:
        in_specs += [pl.BlockSpec((tile, D_LRU), tok), pl.BlockSpec((tile, D_LRU), tok)]
        args += [hb, gl]
    return pl.pallas_call(
        functools.partial(_lru_kernel, row_w=row_w, reverse=reverse, merge=merge),
        grid=(n_batch, nt),
        in_specs=in_specs,
        out_specs=[pl.BlockSpec((tile, D_LRU), tok),
                   pl.BlockSpec((1, 1, D_LRU), lambda b, t: (b, 0, 0))],
        out_shape=[jax.ShapeDtypeStruct((T, D_LRU), bf16 if merge else f32),
                   jax.ShapeDtypeStruct((n_batch, 1, D_LRU), f32)],
        scratch_shapes=[pltpu.VMEM((1, D_LRU), f32)],
        compiler_params=_cparams(("arbitrary", "arbitrary")),
        name="lru_merge" if merge else "lru_scan",
    )(*args)


def _gla_kernel(*refs, reverse, merge):
    if merge:
        pg_ref, wa_ref, ba_ref, s0_ref, ob_ref, gn_ref, out_ref, sfin_ref, state = refs
    else:
        pg_ref, wa_ref, ba_ref, s0_ref, out_ref, sfin_ref, state = refs
    n = pl.program_id(0)

    @pl.when(n == 0)
    def _():
        state[...] = s0_ref[...]

    n_batch = pg_ref.shape[0]
    ck = GLA_CHUNK
    nh = GLA_HEADS
    ii = lax.broadcasted_iota(jnp.int32, (nh * ck, ck), 0) & (ck - 1)
    jj = lax.broadcasted_iota(jnp.int32, (nh * ck, ck), 1)
    seen = (jj >= ii) if reverse else (jj <= ii)
    head_of_lane = lax.broadcasted_iota(jnp.int32, (1, D_GLA_K), 1) // GLA_DK
    nt_dims = (((1,), (1,)), ((), ()))
    tn_dims = (((0,), (0,)), ((), ()))
    o_q, o_k, o_v, o_g, o_a = 0, D_GLA_K, 2 * D_GLA_K, 2 * D_GLA_K + D_GLA_V, 2 * D_GLA_K + 2 * D_GLA_V

    rows_all = n_batch * ck
    a_low = jnp.concatenate([pg_ref[bi, :, o_a:o_a + A_LOW_PAD] for bi in range(n_batch)], axis=0)
    z = jnp.dot(a_low, wa_ref[...], preferred_element_type=f32) + ba_ref[...]
    bcum_all = (jnp.minimum(z, 0.0) - jnp.log(1.0 + jnp.exp(-jnp.abs(z)))) * (1.0 / GATE_TAU)
    step = lax.broadcasted_iota(jnp.int32, (rows_all, D_GLA_K), 0) & (ck - 1)
    d = 1
    while d < ck:
        if reverse:
            bcum_all = bcum_all + jnp.where(step < ck - d, pltpu.roll(bcum_all, rows_all - d, 0), 0.0)
        else:
            bcum_all = bcum_all + jnp.where(step >= d, pltpu.roll(bcum_all, d, 0), 0.0)
        d *= 2

    for bi in range(n_batch):
        q = pg_ref[bi, :, o_q:o_k].astype(f32) * (GLA_DK ** -0.5)
        k = pg_ref[bi, :, o_k:o_v].astype(f32)
        v = pg_ref[bi, :, o_v:o_g]
        bcum = bcum_all[bi * ck:(bi + 1) * ck]
        btot = bcum[0:1] if reverse else bcum[ck - 1:ck]
        q_dec = q * jnp.exp(bcum)
        k_dec = (k * jnp.exp(-bcum)).astype(bf16)
        k_end = (k * jnp.exp(btot - bcum)).astype(bf16)
        s_t = state[bi]
        q_blk = jnp.concatenate([jnp.where(head_of_lane == hd, q_dec, 0.0) for hd in range(nh)],
                                axis=0).astype(bf16)
        scores = lax.dot_general(q_blk, k_dec, nt_dims, preferred_element_type=f32)
        scores = jnp.where(seen, scores, 0.0).astype(bf16)
        intra = jnp.dot(scores, v, preferred_element_type=f32)
        inter = lax.dot_general(q_blk, s_t.astype(bf16), nt_dims, preferred_element_type=f32)
        outs = [intra[hd * ck:(hd + 1) * ck, hd * GLA_DV:(hd + 1) * GLA_DV] + inter[hd * ck:(hd + 1) * ck]
                for hd in range(nh)]
        kv_full = lax.dot_general(v, k_end, tn_dims, preferred_element_type=f32)
        kv_t = jnp.zeros_like(s_t)
        for hd in range(nh):
            kv_t = jnp.where(head_of_lane == hd, kv_full[hd * GLA_DV:(hd + 1) * GLA_DV], kv_t)
        state[bi] = s_t * jnp.exp(btot) + kv_t
        if merge:
            g = pg_ref[bi, :, o_g:o_a].astype(f32)
            normed = []
            for hd in range(GLA_HEADS):
                oh = outs[hd] + ob_ref[bi, :, hd * GLA_DV:(hd + 1) * GLA_DV]
                ms = jnp.mean(oh * oh, axis=-1, keepdims=True)
                normed.append(oh * lax.rsqrt(ms + RMS_EPS))
            y = jnp.concatenate(normed, axis=-1) * gn_ref[...] * (g * jax.nn.sigmoid(g))
            out_ref[bi, :, n % out_ref.shape[2], :] = y
        else:
            out_ref[bi] = jnp.concatenate(outs, axis=-1)

    @pl.when(n == pl.num_programs(0) - 1)
    def _():
        sfin_ref[...] = state[...]


def _gla_pass(pg, wa, ba, s0, *, reverse, ob=None, gn=None):
    n_batch, n_chunks = pg.shape[0], pg.shape[1]
    merge = ob is not None
    order = (lambda n: n_chunks - 1 - n) if reverse else (lambda n: n)
    const2 = lambda n: (0, 0)
    const3 = lambda n: (0, 0, 0)
    chunk = lambda width: pl.BlockSpec((n_batch, None, GLA_CHUNK, width), lambda n: (0, order(n), 0, 0))
    in_specs = [chunk(D_PG),
                pl.BlockSpec((A_LOW_PAD, D_GLA_K), const2),
                pl.BlockSpec((1, D_GLA_K), const2),
                pl.BlockSpec((n_batch, GLA_DV, D_GLA_K), const3)]
    args = [pg, wa, ba, s0]
    if merge:
        in_specs += [chunk(D_GLA_V), pl.BlockSpec((1, D_GLA_V), const2)]
        args += [ob, gn]
        cols_per_block = 8
        out_spec = pl.BlockSpec((n_batch, GLA_CHUNK, cols_per_block, D_GLA_V),
                                lambda n: (0, 0, n // cols_per_block, 0))
        out_shape = (n_batch, GLA_CHUNK, n_chunks, D_GLA_V)
    else:
        out_spec = chunk(D_GLA_V)
        out_shape = (n_batch, n_chunks, GLA_CHUNK, D_GLA_V)
    return pl.pallas_call(
        functools.partial(_gla_kernel, reverse=reverse, merge=merge),
        grid=(n_chunks,),
        in_specs=in_specs,
        out_specs=[out_spec,
                   pl.BlockSpec((n_batch, GLA_DV, D_GLA_K), const3)],
        out_shape=[jax.ShapeDtypeStruct(out_shape, f32),
                   jax.ShapeDtypeStruct((n_batch, GLA_DV, D_GLA_K), f32)],
        scratch_shapes=[pltpu.VMEM((n_batch, GLA_DV, D_GLA_K), f32)],
        compiler_params=_cparams(("arbitrary",)),
        name="gla_merge" if merge else "gla_scan",
    )(*args)


def _pack_rows(v):
    n = v.shape[1] // 2
    w = lax.bitcast_convert_type(v.astype(bf16).astype(f32), jnp.int32)
    return (w[:, :n] & jnp.int32(-65536)) | lax.shift_right_logical(w[:, n:], 16)


def _unpack_rows(w):
    hi = lax.bitcast_convert_type(w & jnp.int32(-65536), f32)
    lo = lax.bitcast_convert_type(lax.shift_left(w, 16), f32)
    return hi, lo


def _outproj_router_kernel(yl_ref, yg_ref, x_ref, mod_ref, wo_ref, lg_ref, lb_ref, wrh_ref, wrl_ref, br_ref,
                           x1_ref, hp_ref, ids_ref, rank_ref, wts_ref, cnt_ref, running):
    i = pl.program_id(0)

    @pl.when(i == 0)
    def _():
        running[...] = jnp.zeros_like(running)

    d = D_MODEL
    y = (jnp.dot(yl_ref[...], wo_ref[0:D_LRU, :], preferred_element_type=f32)
         + jnp.dot(yg_ref[...].astype(bf16), wo_ref[D_LRU:2 * D_LRU, :], preferred_element_type=f32))
    g1 = mod_ref[0, :, 2 * d:3 * d]
    sh2 = mod_ref[0, :, 3 * d:4 * d]
    sc2 = mod_ref[0, :, 4 * d:5 * d]
    z = DEEPNORM_ALPHA * x_ref[...] + g1 * y
    mu = jnp.mean(z, axis=-1, keepdims=True)
    zc = z - mu
    var = jnp.mean(zc * zc, axis=-1, keepdims=True)
    x1 = zc * lax.rsqrt(var + LN_EPS) * lg_ref[...] + lb_ref[...]
    x1_ref[...] = x1
    hmod = x1 * (1.0 + sc2) + sh2
    hp_ref[...] = _pack_rows(hmod)

    nt_dims = (((1,), (1,)), ((), ()))
    h_hi = hmod.astype(bf16)
    h_lo = (hmod - h_hi.astype(f32)).astype(bf16)
    logits = (lax.dot_general(wrh_ref[...], h_hi, nt_dims, preferred_element_type=f32)
              + lax.dot_general(wrh_ref[...], h_lo, nt_dims, preferred_element_type=f32)
              + lax.dot_general(wrl_ref[...], h_hi, nt_dims, preferred_element_type=f32)) + br_ref[...]
    ne, tm = logits.shape
    expert = lax.broadcasted_iota(jnp.int32, (ne, tm), 0).astype(f32)
    neg_inf = jnp.float32(-jnp.inf)
    live = logits
    sel = jnp.zeros((ne, tm), f32)
    ids, vals = [], []
    for _ in range(TOP_K):
        m = jnp.max(live, axis=0, keepdims=True)
        j = jnp.min(jnp.where(live == m, expert, float(ne)), axis=0, keepdims=True)
        pick = expert == j
        sel = jnp.where(pick, 1.0, sel)
        live = jnp.where(pick, neg_inf, live)
        ids.append(j)
        vals.append(m)
    exps = [jnp.exp(vk - vals[0]) for vk in vals]
    denom = exps[0] + exps[1] + exps[2] + exps[3]

    ri = lax.broadcasted_iota(jnp.int32, (tm, tm), 0)
    ci = lax.broadcasted_iota(jnp.int32, (tm, tm), 1)
    earlier = (ri < ci).astype(bf16)
    rank_dense = running[...] + jnp.dot(sel.astype(bf16), earlier, preferred_element_type=f32)
    running[...] = running[...] + jnp.sum(sel, axis=1, keepdims=True)
    cnt_ref[...] = running[...]

    slot = lax.broadcasted_iota(jnp.int32, (8, tm), 0)
    ids_out = jnp.zeros((8, tm), f32)
    rank_out = jnp.zeros((8, tm), f32)
    wts_out = jnp.zeros((8, tm), f32)
    for kk in range(TOP_K):
        rk = jnp.sum(jnp.where(expert == ids[kk], rank_dense, 0.0), axis=0, keepdims=True)
        here = slot == kk
        ids_out = jnp.where(here, ids[kk], ids_out)
        rank_out = jnp.where(here, rk, rank_out)
        wts_out = jnp.where(here, exps[kk] / denom, wts_out)
    ids_ref[...] = ids_out.astype(jnp.int32)
    rank_ref[...] = rank_out.astype(jnp.int32)
    wts_ref[...] = jnp.concatenate([wts_out, jnp.zeros((LANES - 8, tm), f32)], axis=0).T


def _outproj_router(yl, yg, xt, mod3, wo, lg, lb, wr_hi, wr_lo, br, tokens_per_batch):
    T = xt.shape[0]
    tm = TOKEN_TILE
    tiles_per_batch = tokens_per_batch // tm
    tokrow = lambda i: (i, 0)
    const = lambda i: (0, 0)
    return pl.pallas_call(
        _outproj_router_kernel,
        grid=(T // tm,),
        in_specs=[pl.BlockSpec((tm, D_LRU), tokrow),
                  pl.BlockSpec((tm, D_GLA_V), tokrow),
                  pl.BlockSpec((tm, D_MODEL), tokrow),
                  pl.BlockSpec((1, 1, N_MOD * D_MODEL), lambda i: (i // tiles_per_batch, 0, 0)),
                  pl.BlockSpec((D_MODEL, D_MODEL), const),
                  pl.BlockSpec((1, D_MODEL), const),
                  pl.BlockSpec((1, D_MODEL), const),
                  pl.BlockSpec((N_EXPERTS, D_MODEL), const),
                  pl.BlockSpec((N_EXPERTS, D_MODEL), const),
                  pl.BlockSpec((N_EXPERTS, 1), const)],
        out_specs=[pl.BlockSpec((tm, D_MODEL), tokrow),
                   pl.BlockSpec((tm, D_MODEL // 2), tokrow),
                   pl.BlockSpec((8, tm), lambda i: (0, i)),
                   pl.BlockSpec((8, tm), lambda i: (0, i)),
                   pl.BlockSpec((tm, LANES), tokrow),
                   pl.BlockSpec((N_EXPERTS, 1), const)],
        out_shape=[jax.ShapeDtypeStruct((T, D_MODEL), f32),
                   jax.ShapeDtypeStruct((T, D_MODEL // 2), jnp.int32),
                   jax.ShapeDtypeStruct((8, T), jnp.int32),
                   jax.ShapeDtypeStruct((8, T), jnp.int32),
                   jax.ShapeDtypeStruct((T, LANES), f32),
                   jax.ShapeDtypeStruct((N_EXPERTS, 1), f32)],
        scratch_shapes=[pltpu.VMEM((N_EXPERTS, 1), f32)],
        compiler_params=_cparams(("arbitrary",)),
        name="outproj_router",
    )(yl, yg, xt, mod3, wo, lg, lb, wr_hi, wr_lo, br)


def _route_kernel(ids_ref, rank_ref, cnt_ref, dest_ref, tiles_ref, *, bm):
    ne = cnt_ref.shape[0]
    cnt = cnt_ref[...]
    padded = jnp.floor((cnt + (bm - 1.0)) * (1.0 / bm)) * bm
    sub = lax.broadcasted_iota(jnp.int32, (ne, LANES), 0)
    lane = lax.broadcasted_iota(jnp.int32, (ne, LANES), 1)
    padded_row = jnp.sum(jnp.where(sub == lane, padded, 0.0), axis=0, keepdims=True)
    base = jnp.sum(jnp.where(lane < sub, padded_row, 0.0), axis=1, keepdims=True)
    ends = base + padded

    tc = ids_ref.shape[1]
    expert = lax.broadcasted_iota(jnp.int32, (ne, tc), 0)
    slot = lax.broadcasted_iota(jnp.int32, (8, tc), 0)
    ids = ids_ref[...]
    group_base = jnp.zeros((8, tc), f32)
    for kk in range(TOP_K):
        b_k = jnp.sum(jnp.where(expert == ids[kk:kk + 1], base, 0.0), axis=0, keepdims=True)
        group_base = jnp.where(slot == kk, b_k, group_base)
    dest_ref[...] = group_base.astype(jnp.int32) + rank_ref[...]

    nl = tiles_ref.shape[1]
    start = lax.broadcasted_iota(jnp.int32, (ne, nl), 1).astype(f32) * bm
    te = jnp.minimum(jnp.sum(jnp.where(start >= ends, 1.0, 0.0), axis=0, keepdims=True), ne - 1.0)
    at_te = lax.broadcasted_iota(jnp.int32, (ne, nl), 0).astype(f32) == te
    cnt_te = jnp.sum(jnp.where(at_te, cnt, 0.0), axis=0, keepdims=True)
    base_te = jnp.sum(jnp.where(at_te, base, 0.0), axis=0, keepdims=True)
    valid = jnp.clip(cnt_te - (start[0:1] - base_te), 0.0, float(bm))
    next_group = jnp.sum(jnp.where(at_te, ends, 0.0), axis=0, keepdims=True) * (1.0 / bm)
    srow = lax.broadcasted_iota(jnp.int32, (8, nl), 0)
    table = jnp.where(srow == 0, te, jnp.where(srow == 1, valid, jnp.where(srow == 2, next_group, 0.0)))
    tiles_ref[...] = table.astype(jnp.int32)


def _route(ids, rank, cnt, bm, n_tiles):
    T = ids.shape[1]
    tc = 2048
    nl = -(-n_tiles // LANES) * LANES
    tok = lambda i: (0, i)
    const = lambda i: (0, 0)
    return pl.pallas_call(
        functools.partial(_route_kernel, bm=bm),
        grid=(T // tc,),
        in_specs=[pl.BlockSpec((8, tc), tok), pl.BlockSpec((8, tc), tok),
                  pl.BlockSpec((N_EXPERTS, 1), const)],
        out_specs=[pl.BlockSpec((8, tc), tok), pl.BlockSpec((8, nl), const)],
        out_shape=[jax.ShapeDtypeStruct((8, T), jnp.int32), jax.ShapeDtypeStruct((8, nl), jnp.int32)],
        compiler_params=_cparams(("arbitrary",)),
        name="route",
    )(ids, rank, cnt)


def _sc_workers():
    info = plsc.get_sparse_core_info()
    return info.num_cores, info.num_subcores


def _sc_dispatch(rows, dest_flat, n_out):
    T, D = rows.shape
    nc, ns = _sc_workers()
    per_w = T // (nc * ns)
    n_chunks = per_w // SC_CHUNK
    mesh = plsc.VectorSubcoreMesh(core_axis_name="c", subcore_axis_name="s")

    @functools.partial(
        pl.kernel, mesh=mesh,
        out_type=jax.ShapeDtypeStruct((n_out, D), rows.dtype),
        scratch_types=([pltpu.VMEM((SC_CHUNK,), jnp.int32)] * TOP_K
                       + [pltpu.VMEM((SC_CHUNK, D), rows.dtype)]
                       + [pltpu.SemaphoreType.DMA] * TOP_K),
    )
    def k(rows_hbm, dest_hbm, out_hbm, *scratch):
        idx_v = scratch[:TOP_K]
        rows_v = scratch[TOP_K]
        sems = scratch[TOP_K + 1:]
        wid = lax.axis_index("s") * nc + lax.axis_index("c")
        base = wid * per_w

        @pl.loop(0, n_chunks)
        def _(j):
            off = base + j * SC_CHUNK
            pltpu.sync_copy(rows_hbm.at[pl.ds(off, SC_CHUNK)], rows_v)
            for kk in range(TOP_K):
                pltpu.sync_copy(dest_hbm.at[pl.ds(kk * T + off, SC_CHUNK)], idx_v[kk])
            copies = [pltpu.async_copy(rows_v, out_hbm.at[idx_v[kk]], sems[kk]) for kk in range(TOP_K)]
            for cp in copies:
                cp.wait()

    return k(rows, dest_flat)


def _sc_gather(table, idx):
    _, D = table.shape
    N = idx.shape[0]
    nc, ns = _sc_workers()
    per_w = N // (nc * ns)
    n_chunks = per_w // SC_CHUNK
    assert n_chunks % 2 == 0
    mesh = plsc.VectorSubcoreMesh(core_axis_name="c", subcore_axis_name="s")

    @functools.partial(
        pl.kernel, mesh=mesh,
        out_type=jax.ShapeDtypeStruct((N, D), table.dtype),
        scratch_types=([pltpu.VMEM((SC_CHUNK,), jnp.int32)] * 2
                       + [pltpu.VMEM((SC_CHUNK, D), table.dtype)] * 2
                       + [pltpu.SemaphoreType.DMA] * 2),
    )
    def k(table_hbm, idx_hbm, out_hbm, idx0, idx1, buf0, buf1, sem0, sem1):
        idxs, bufs, sems = (idx0, idx1), (buf0, buf1), (sem0, sem1)
        wid = lax.axis_index("s") * nc + lax.axis_index("c")
        base = wid * per_w

        def gather(b):
            return pltpu.make_async_copy(table_hbm.at[idxs[b]], bufs[b], sems[b])

        def start(chunk, b):
            pltpu.sync_copy(idx_hbm.at[pl.ds(base + chunk * SC_CHUNK, SC_CHUNK)], idxs[b])
            gather(b).start()

        def finish(chunk, b):
            gather(b).wait()
            pltpu.sync_copy(bufs[b], out_hbm.at[pl.ds(base + chunk * SC_CHUNK, SC_CHUNK)])

        start(0, 0)

        @pl.loop(0, n_chunks, step=2)
        def _(j):
            start(j + 1, 1)
            finish(j, 0)

            @pl.when(j + 2 < n_chunks)
            def _():
                start(j + 2, 0)

            finish(j + 1, 1)

    return k(table, idx)


def _ffn_kernel(te_ref, tv_ref, tn_ref, xs_ref, wg_hbm, bg_ref, wu_hbm, bu_ref, wd_hbm, bd_ref, eo_ref,
                w_stage, wg_b, wu_b, wd_b, sems):
    i = pl.program_id(0)
    n_tiles = pl.num_programs(0)
    w_hbm = (wg_hbm, wu_hbm, wd_hbm)
    w_b = (wg_b, wu_b, wd_b)

    def fetch(e):
        return [pltpu.make_async_copy(w_hbm[m].at[e], w_stage.at[m], sems.at[m]) for m in range(3)]

    @pl.when(i == 0)
    def _():
        for cp in fetch(te_ref[0]):
            cp.start()

    prev = te_ref[jnp.maximum(i - 1, 0)]

    @pl.when((i == 0) | (te_ref[i] != prev))
    def _():
        for m, cp in enumerate(fetch(te_ref[i])):
            cp.wait()
            w_b[m][...] = w_stage[m].astype(bf16)
        nxt = tn_ref[i]
        e_nxt = te_ref[jnp.minimum(nxt, n_tiles - 1)]

        @pl.when((nxt > i) & (nxt < n_tiles) & (e_nxt != te_ref[i]))
        def _():
            for cp in fetch(e_nxt):
                cp.start()

    valid = tv_ref[i]
    for part in range(xs_ref.shape[0] // MOE_SUBTILE):
        r0 = part * MOE_SUBTILE

        @pl.when(valid > r0)
        def _(r0=r0):
            row = lax.broadcasted_iota(jnp.int32, (MOE_SUBTILE, 1), 0) + r0
            xw = jnp.where(row < valid, xs_ref[r0:r0 + MOE_SUBTILE, :], 0)
            x_hi, x_lo = _unpack_rows(xw)
            x = jnp.concatenate([x_hi, x_lo], axis=1).astype(bf16)
            gate = jnp.minimum(jnp.dot(x, wg_b[...], preferred_element_type=f32) + bg_ref[...], SWIGLU_LIMIT)
            up = jnp.clip(jnp.dot(x, wu_b[...], preferred_element_type=f32) + bu_ref[...],
                          -SWIGLU_LIMIT, SWIGLU_LIMIT)
            act = (up + 1.0) * gate * jax.nn.sigmoid(SWIGLU_ALPHA * gate)
            out = jnp.dot(act.astype(bf16), wd_b[...], preferred_element_type=f32) + bd_ref[...]
            eo_ref[r0:r0 + MOE_SUBTILE, :] = _pack_rows(out)

        @pl.when(valid <= r0)
        def _(r0=r0):
            eo_ref[r0:r0 + MOE_SUBTILE, :] = jnp.zeros((MOE_SUBTILE, eo_ref.shape[1]), eo_ref.dtype)


def _expert_ffn(tile_expert, tile_valid, tile_next, xs, w_gate, b_gate, w_up, b_up, w_down, b_down):
    n_rows, dp = xs.shape
    d = 2 * dp
    bm = MOE_TILE
    d_e = w_gate.shape[-1]
    assert d == d_e
    bspec = lambda n_: pl.BlockSpec((None, 1, n_), lambda i, te, tv, tn: (te[i], 0, 0))
    hbm = pl.BlockSpec(memory_space=pl.ANY)
    grid_spec = pltpu.PrefetchScalarGridSpec(
        num_scalar_prefetch=3,
        grid=(n_rows // bm,),
        in_specs=[pl.BlockSpec((bm, dp), lambda i, te, tv, tn: (i, 0)),
                  hbm, bspec(d_e), hbm, bspec(d_e), hbm, bspec(d)],
        out_specs=pl.BlockSpec((bm, dp), lambda i, te, tv, tn: (i, 0)),
        scratch_shapes=[pltpu.VMEM((3, d, d_e), f32),
                        pltpu.VMEM((d, d_e), bf16), pltpu.VMEM((d, d_e), bf16), pltpu.VMEM((d_e, d), bf16),
                        pltpu.SemaphoreType.DMA((3,))],
    )
    return pl.pallas_call(
        _ffn_kernel,
        grid_spec=grid_spec,
        out_shape=jax.ShapeDtypeStruct((n_rows, dp), jnp.int32),
        compiler_params=_cparams(("arbitrary",)),
        name="expert_ffn",
    )(tile_expert, tile_valid, tile_next, xs, w_gate, b_gate.reshape(N_EXPERTS, 1, d_e), w_up,
      b_up.reshape(N_EXPERTS, 1, d_e), w_down, b_down.reshape(N_EXPERTS, 1, d))


def _combine_kernel(eg_ref, wts_ref, x1_ref, mod_ref, lg_ref, lb_ref, o_ref):
    w = wts_ref[...]
    y_hi, y_lo = _unpack_rows(eg_ref[0])
    y_hi, y_lo = y_hi * w[:, 0:1], y_lo * w[:, 0:1]
    for kk in range(1, TOP_K):
        e_hi, e_lo = _unpack_rows(eg_ref[kk])
        y_hi = y_hi + e_hi * w[:, kk:kk + 1]
        y_lo = y_lo + e_lo * w[:, kk:kk + 1]
    y = jnp.concatenate([y_hi, y_lo], axis=1)
    z = DEEPNORM_ALPHA * x1_ref[...] + mod_ref[0] * y
    mu = jnp.mean(z, axis=-1, keepdims=True)
    zc = z - mu
    var = jnp.mean(zc * zc, axis=-1, keepdims=True)
    o_ref[...] = zc * lax.rsqrt(var + LN_EPS) * lg_ref[...] + lb_ref[...]


def _combine_ln(eg, wts, x1, mod3, lg, lb, tokens_per_batch):
    T = x1.shape[0]
    tm = TOKEN_TILE
    tiles_per_batch = tokens_per_batch // tm
    const = lambda i: (0, 0)
    return pl.pallas_call(
        _combine_kernel,
        grid=(T // tm,),
        in_specs=[pl.BlockSpec((TOP_K, tm, D_MODEL // 2), lambda i: (0, i, 0)),
                  pl.BlockSpec((tm, LANES), lambda i: (i, 0)),
                  pl.BlockSpec((tm, D_MODEL), lambda i: (i, 0)),
                  pl.BlockSpec((1, 1, D_MODEL), lambda i: (i // tiles_per_batch, 0, N_MOD - 1)),
                  pl.BlockSpec((1, D_MODEL), const),
                  pl.BlockSpec((1, D_MODEL), const)],
        out_specs=pl.BlockSpec((tm, D_MODEL), lambda i: (i, 0)),
        out_shape=jax.ShapeDtypeStruct((T, D_MODEL), f32),
        compiler_params=_cparams(("arbitrary",)),
        name="combine_ln",
    )(eg, wts, x1, mod3, lg, lb)


def _block_diag(w):
    n, c, d = w.shape
    eye = jnp.eye(n, dtype=w.dtype)
    return jnp.einsum('ncd,nm->ncmd', w, eye).reshape(n * c, n * d)


def kernel(x, c, ctx, c_ctx, w_ada, b_ada, w_in, conv_w, conv_b, lru_wa, lru_ba, lru_wx, lru_bx,
           lru_lam, gla_wa, gla_ba, gla_norm_g, w_out, ln1_g, ln1_b, w_router, b_router, w_gate,
           b_gate, w_up, b_up, w_down, b_down, ln2_g, ln2_b):
    B, L, D = x.shape
    Lc = ctx.shape[1]
    T = B * L
    rows = L // GRID_W
    l = 0

    cpad = jnp.zeros((8, D), f32).at[0:B].set(c).at[B].set(c_ctx)
    w_cat = jnp.pad(w_in[l], ((0, 0), (0, D_PROJ - w_in.shape[-1]))).astype(bf16)
    wg = [jnp.concatenate([_block_diag(lru_wa[l, d]), _block_diag(lru_wx[l, d])], axis=1).astype(bf16)
          for d in range(2)]
    bg = [jnp.concatenate([lru_ba[l, d], lru_bx[l, d]])[None] for d in range(2)]
    lam = [lru_lam[l, d][None] for d in range(2)]
    wa = [jnp.pad(gla_wa[l, d], ((0, A_LOW_PAD - GATE_RANK), (0, 0))).astype(bf16) for d in range(2)]
    ba = [gla_ba[l, d][None] for d in range(2)]
    cw, cb = conv_w[l], conv_b[l][None]
    wr_t = w_router[l].T
    wr_hi = wr_t.astype(bf16)
    wr_lo = (wr_t - wr_hi.astype(f32)).astype(bf16)
    br = b_router[l][:, None]
    c_xl, c_gl, c_pg = (0, D_LRU), (D_LRU, 2 * D_LRU), (2 * D_LRU, D_PROJ)

    mod3 = _ada_mod(cpad, w_ada[l], b_ada[l]).reshape(8, 1, N_MOD * D)

    tiles_per_batch = L // TOKEN_TILE
    xl_c, pg_c = _inproj_rows(ctx.reshape(B * Lc, D), mod3, w_cat, lambda i: B, (c_xl, c_pg), (f32, bf16))
    zero_h = jnp.zeros((B, 1, D_LRU), f32)
    zero_s = jnp.zeros((B, GLA_DV, D_GLA_K), f32)
    pg_c = pg_c.reshape(B, Lc // GLA_CHUNK, GLA_CHUNK, D_PG)
    h_ctx, s_ctx = [], []
    for d in range(2):
        _, hf = _lru_pass(xl_c, cw, cb, wg[d], bg[d], lam[d], zero_h, n_batch=B, row_w=Lc, tile=Lc,
                          reverse=bool(d))
        _, sf = _gla_pass(pg_c, wa[d], ba[d], zero_s, reverse=bool(d))
        h_ctx.append(hf)
        s_ctx.append(sf)

    xt = x.reshape(T, D)
    xl, gl = _inproj_rows(xt, mod3, w_cat, lambda i: i // tiles_per_batch, (c_xl, c_gl), (f32, bf16))
    hb, _ = _lru_pass(xl, cw, cb, wg[1], bg[1], lam[1], h_ctx[1], n_batch=B, row_w=GRID_W, tile=LRU_TILE,
                      reverse=True)
    y_lru, _ = _lru_pass(xl, cw, cb, wg[0], bg[0], lam[0], h_ctx[0], n_batch=B, row_w=GRID_W,
                         tile=LRU_TILE, reverse=False, hb=hb, gl=gl)
    pg = _inproj_cols(x.reshape(B, rows, GRID_W, D), mod3, w_cat, c_pg)
    ob, _ = _gla_pass(pg, wa[1], ba[1], s_ctx[1], reverse=True)
    y_gla, _ = _gla_pass(pg, wa[0], ba[0], s_ctx[0], reverse=False, ob=ob, gn=gla_norm_g[l][None])
    y_gla = y_gla.reshape(T, D_GLA_V)

    x1, hp, ids, rank, wts, cnt = _outproj_router(
        y_lru, y_gla, xt, mod3, w_out[l].astype(bf16), ln1_g[l][None], ln1_b[l][None], wr_hi, wr_lo, br, L)

    n_tiles = T * TOP_K // MOE_TILE + N_EXPERTS
    dest, tiles = _route(ids, rank, cnt, MOE_TILE, n_tiles)
    xs = _sc_dispatch(hp, dest.reshape(-1), n_tiles * MOE_TILE)
    eo = _expert_ffn(tiles[0, :n_tiles], tiles[1, :n_tiles], tiles[2, :n_tiles], xs, w_gate[l], b_gate[l],
                     w_up[l], b_up[l], w_down[l], b_down[l])
    eg = _sc_gather(eo, dest[:TOP_K].reshape(-1)).reshape(TOP_K, T, D // 2)
    out = _combine_ln(eg, wts, x1, mod3, ln2_g[l][None], ln2_b[l][None], L)
    return out.reshape(B, L, D)
```

```python
import functools

import jax
import jax.numpy as jnp
from jax import lax
from jax.experimental import pallas as pl
from jax.experimental.pallas import tpu as pltpu
from jax.experimental.pallas import tpu_sc as plsc

D_MODEL = 1024
DEPTH = 1
GRID_W = 64
D_LRU = 512
LRU_BLOCKS = 8
CONV_W = 4
LRU_C = 8.0
GLA_HEADS = 4
D_GLA_V = 512
D_GLA_K = 256
GLA_DK = 64
GLA_DV = 128
GATE_RANK = 16
GATE_TAU = 16.0
GLA_CHUNK = 64
N_EXPERTS = 32
TOP_K = 4
SWIGLU_LIMIT = 7.0
SWIGLU_ALPHA = 1.702
N_MOD = 6
DEEPNORM_ALPHA = (2.0 * DEPTH) ** 0.25
LN_EPS = 1e-5
RMS_EPS = 1e-6

LANES = 128
A_LOW_PAD = LANES
D_PG = 2 * D_GLA_K + 2 * D_GLA_V + A_LOW_PAD
D_PROJ = 2 * D_LRU + D_PG
TOKEN_TILE = 512
LRU_TILE = 512
MOE_TILE = 1024
MOE_PASS = 512
MOE_SUBTILE = 256
SC_CHUNK = 64
VMEM_LIMIT = 48 * 1024 * 1024

f32 = jnp.float32
bf16 = jnp.bfloat16


def _cparams(sem):
    return pltpu.CompilerParams(dimension_semantics=sem, vmem_limit_bytes=VMEM_LIMIT)


def _ada_kernel(c_ref, w_ref, b_ref, o_ref):
    s = c_ref[...]
    s = s * jax.nn.sigmoid(s)
    o_ref[...] = jnp.dot(s.astype(bf16), w_ref[...].astype(bf16), preferred_element_type=f32) + b_ref[...]


def _ada_mod(cpad, w, b):
    n = w.shape[1]
    tn = 1024
    return pl.pallas_call(
        _ada_kernel,
        grid=(n // tn,),
        in_specs=[pl.BlockSpec((8, D_MODEL), lambda j: (0, 0)),
                  pl.BlockSpec((D_MODEL, tn), lambda j: (0, j)),
                  pl.BlockSpec((1, tn), lambda j: (0, j))],
        out_specs=pl.BlockSpec((8, tn), lambda j: (0, j)),
        out_shape=jax.ShapeDtypeStruct((8, n), f32),
        compiler_params=_cparams(("arbitrary",)),
        name="ada_mod",
    )(cpad, w, b.reshape(1, n))


def _inproj_kernel(x_ref, mod_ref, w_ref, *out_refs, parts, col_major):
    sh = mod_ref[0, :, 0:D_MODEL]
    sc = mod_ref[0, :, D_MODEL:2 * D_MODEL]
    if col_major:
        x = jnp.concatenate([x_ref[0, :, j, :] for j in range(x_ref.shape[2])], axis=0)
    else:
        x = x_ref[...]
    u = (x * (1.0 + sc) + sh).astype(bf16)
    for (lo, hi), o_ref in zip(parts, out_refs):
        p = jnp.dot(u, w_ref[:, lo:hi], preferred_element_type=f32).astype(o_ref.dtype)
        o_ref[...] = p.reshape(o_ref.shape)


def _inproj_rows(xt, mod3, w_cat, batch_of_tile, parts, dtypes):
    T = xt.shape[0]
    tm = TOKEN_TILE
    return pl.pallas_call(
        functools.partial(_inproj_kernel, parts=parts, col_major=False),
        grid=(T // tm,),
        in_specs=[pl.BlockSpec((tm, D_MODEL), lambda i: (i, 0)),
                  pl.BlockSpec((1, 1, 2 * D_MODEL), lambda i: (batch_of_tile(i), 0, 0)),
                  pl.BlockSpec((D_MODEL, D_PROJ), lambda i: (0, 0))],
        out_specs=[pl.BlockSpec((tm, hi - lo), lambda i: (i, 0)) for lo, hi in parts],
        out_shape=[jax.ShapeDtypeStruct((T, hi - lo), dt) for (lo, hi), dt in zip(parts, dtypes)],
        compiler_params=_cparams(("arbitrary",)),
        name="inproj_rows",
    )(xt, mod3, w_cat)


def _inproj_cols(x4, mod3, w_cat, part):
    n_batch, rows, cols, _ = x4.shape
    lo, hi = part
    cb = TOKEN_TILE // rows
    return pl.pallas_call(
        functools.partial(_inproj_kernel, parts=(part,), col_major=True),
        grid=(n_batch, cols // cb),
        in_specs=[pl.BlockSpec((1, rows, cb, D_MODEL), lambda b, n: (b, 0, n, 0)),
                  pl.BlockSpec((1, 1, 2 * D_MODEL), lambda b, n: (b, 0, 0)),
                  pl.BlockSpec((D_MODEL, D_PROJ), lambda b, n: (0, 0))],
        out_specs=[pl.BlockSpec((1, cb, rows, hi - lo), lambda b, n: (b, n, 0, 0))],
        out_shape=[jax.ShapeDtypeStruct((n_batch, cols, rows, hi - lo), bf16)],
        compiler_params=_cparams(("arbitrary", "arbitrary")),
        name="inproj_cols",
    )(x4, mod3, w_cat)[0]


def _gelu_tanh(x):
    return 0.5 * x * (1.0 + jnp.tanh(0.7978845608028654 * (x + 0.044715 * (x * x * x))))


def _lru_kernel(*refs, row_w, reverse, merge):
    if merge:
        (xl_ref, cw_ref, cb_ref, wg_ref, bg_ref, lam_ref, h0_ref, hb_ref, gl_ref,
         out_ref, hfin_ref, carry, h_nat) = refs
    else:
        xl_ref, cw_ref, cb_ref, wg_ref, bg_ref, lam_ref, h0_ref, out_ref, hfin_ref, carry = refs
    t = pl.program_id(1)

    @pl.when(t == 0)
    def _():
        carry[...] = h0_ref[0]

    n_seg, seg, ch = xl_ref.shape
    blk = lambda v, j: v[j * n_seg:(j + 1) * n_seg]
    x = jnp.concatenate([xl_ref[:, j, :] for j in range(seg)], axis=0)
    tt = n_seg * seg

    segs_per_row = row_w // seg
    s_idx = lax.broadcasted_iota(jnp.int32, (n_seg, ch), 0)
    has_prev = (s_idx % segs_per_row) != 0
    has_next = (s_idx % segs_per_row) != segs_per_row - 1
    from_prev = lambda v: jnp.where(has_prev, pltpu.roll(v, 1, 0), 0.0)
    from_next = lambda v: jnp.where(has_next, pltpu.roll(v, n_seg - 1, 0), 0.0)
    x_m1 = jnp.concatenate([from_prev(blk(x, seg - 1)), x[:tt - n_seg]], axis=0)
    x_m2 = jnp.concatenate([from_prev(blk(x, seg - 2)), from_prev(blk(x, seg - 1)), x[:tt - 2 * n_seg]], axis=0)
    x_p1 = jnp.concatenate([x[n_seg:], from_next(blk(x, 0))], axis=0)
    cw = cw_ref[...]
    u = cb_ref[...] + x_m2 * cw[0:1] + x_m1 * cw[1:2] + x * cw[2:3] + x_p1 * cw[3:4]
    g = jnp.dot(u.astype(bf16), wg_ref[...], preferred_element_type=f32) + bg_ref[...]
    r_gate = jax.nn.sigmoid(g[:, 0:ch])
    i_gate = jax.nn.sigmoid(g[:, ch:2 * ch])
    lam = lam_ref[...]
    softplus_neg = jnp.maximum(-lam, 0.0) + jnp.log(1.0 + jnp.exp(-jnp.abs(lam)))
    log_a = (-LRU_C * softplus_neg) * r_gate
    a = jnp.exp(log_a)
    t2 = 2.0 * log_a
    series = -t2 * (1.0 + t2 * (0.5 + t2 * (1.0 / 6.0 + t2 * (1.0 / 24.0))))
    one_minus_a2 = jnp.where(t2 > -0.03, series, 1.0 - a * a)
    b = jnp.sqrt(one_minus_a2) * (i_gate * u)

    order = range(seg - 1, -1, -1) if reverse else range(seg)
    h_loc, a_cum = [None] * seg, [None] * seg
    h_run = a_run = None
    for j in order:
        a_j, b_j = blk(a, j), blk(b, j)
        h_run = b_j if h_run is None else a_j * h_run + b_j
        a_run = a_j if a_run is None else a_j * a_run
        h_loc[j], a_cum[j] = h_run, a_run
    seg_order = range(n_seg - 1, -1, -1) if reverse else range(n_seg)
    entering = [None] * n_seg
    state = carry[...]
    for s in seg_order:
        entering[s] = state
        state = h_run[s:s + 1] + a_run[s:s + 1] * state
    carry[...] = state
    hfin_ref[0] = state
    enter = jnp.concatenate(entering, axis=0)
    for j in range(seg):
        h_j = h_loc[j] + a_cum[j] * enter
        if merge:
            h_nat[:, j, :] = h_j
        else:
            out_ref[:, j, :] = h_j
    if merge:
        h = h_nat[...].reshape(tt, ch)
        out_ref[...] = ((h + hb_ref[...]) * _gelu_tanh(gl_ref[...].astype(f32))).astype(out_ref.dtype)


def _lru_pass(xl, conv_w, conv_b, wg, bg, lam, h0, *, n_batch, row_w, tile, reverse, hb=None, gl=None):
    T = xl.shape[0]
    nt = T // n_batch // tile
    merge = hb is not None
    n_seg = 8
    seg = tile // n_seg
    assert row_w % seg == 0 and seg >= 2

    def tok(b, t):
        return (b * nt + (nt - 1 - t if reverse else t), 0)

    seg_spec = pl.BlockSpec((n_seg, seg, D_LRU), lambda b, t: tok(b, t) + (0,))
    const = lambda b, t: (0, 0)
    in_specs = [seg_spec,
                pl.BlockSpec((CONV_W, D_LRU), const),
                pl.BlockSpec((1, D_LRU), const),
                pl.BlockSpec((D_LRU, 2 * D_LRU), const),
                pl.BlockSpec((1, 2 * D_LRU), const),
                pl.BlockSpec((1, D_LRU), const),
                pl.BlockSpec((1, 1, D_LRU), lambda b, t: (b, 0, 0))]
    args = [xl.reshape(T // seg, seg, D_LRU), conv_w, conv_b, wg, bg, lam, h0]
    scratch = [pltpu.VMEM((1, D_LRU), f32)]
    if merge:
        in_specs += [pl.BlockSpec((tile, D_LRU), tok), pl.BlockSpec((tile, D_LRU), tok)]
        args += [hb, gl]
        scratch += [pltpu.VMEM((n_seg, seg, D_LRU), f32)]
        out_spec, out_shape = pl.BlockSpec((tile, D_LRU), tok), jax.ShapeDtypeStruct((T, D_LRU), bf16)
    else:
        out_spec, out_shape = seg_spec, jax.ShapeDtypeStruct((T // seg, seg, D_LRU), f32)
    out, h_fin = pl.pallas_call(
        functools.partial(_lru_kernel, row_w=row_w, reverse=reverse, merge=merge),
        grid=(n_batch, nt),
        in_specs=in_specs,
        out_specs=[out_spec, pl.BlockSpec((1, 1, D_LRU), lambda b, t: (b, 0, 0))],
        out_shape=[out_shape, jax.ShapeDtypeStruct((n_batch, 1, D_LRU), f32)],
        scratch_shapes=scratch,
        compiler_params=_cparams(("arbitrary", "arbitrary")),
        name="lru_merge" if merge else "lru_scan",
    )(*args)
    return out.reshape(T, D_LRU), h_fin


def _gla_kernel(*refs, reverse, merge):
    if merge:
        pg_ref, wa_ref, ba_ref, s0_ref, ob_ref, gn_ref, out_ref, sfin_ref, state = refs
    else:
        pg_ref, wa_ref, ba_ref, s0_ref, out_ref, sfin_ref, state = refs
    n = pl.program_id(0)

    @pl.when(n == 0)
    def _():
        state[...] = s0_ref[...]

    n_batch = pg_ref.shape[0]
    ck = GLA_CHUNK
    nh = GLA_HEADS
    ii = lax.broadcasted_iota(jnp.int32, (nh * ck, ck), 0) & (ck - 1)
    jj = lax.broadcasted_iota(jnp.int32, (nh * ck, ck), 1)
    seen = (jj >= ii) if reverse else (jj <= ii)
    head_of_lane = lax.broadcasted_iota(jnp.int32, (1, D_GLA_K), 1) // GLA_DK
    nt_dims = (((1,), (1,)), ((), ()))
    tn_dims = (((0,), (0,)), ((), ()))
    o_q, o_k, o_v, o_g, o_a = 0, D_GLA_K, 2 * D_GLA_K, 2 * D_GLA_K + D_GLA_V, 2 * D_GLA_K + 2 * D_GLA_V

    rows_all = n_batch * ck
    a_low = jnp.concatenate([pg_ref[bi, :, o_a:o_a + A_LOW_PAD] for bi in range(n_batch)], axis=0)
    z = jnp.dot(a_low, wa_ref[...], preferred_element_type=f32) + ba_ref[...]
    bcum_all = (jnp.minimum(z, 0.0) - jnp.log(1.0 + jnp.exp(-jnp.abs(z)))) * (1.0 / GATE_TAU)
    step = lax.broadcasted_iota(jnp.int32, (rows_all, D_GLA_K), 0) & (ck - 1)
    d = 1
    while d < ck:
        if reverse:
            bcum_all = bcum_all + jnp.where(step < ck - d, pltpu.roll(bcum_all, rows_all - d, 0), 0.0)
        else:
            bcum_all = bcum_all + jnp.where(step >= d, pltpu.roll(bcum_all, d, 0), 0.0)
        d *= 2

    for bi in range(n_batch):
        q = pg_ref[bi, :, o_q:o_k].astype(f32) * (GLA_DK ** -0.5)
        k = pg_ref[bi, :, o_k:o_v].astype(f32)
        v = pg_ref[bi, :, o_v:o_g]
        bcum = bcum_all[bi * ck:(bi + 1) * ck]
        btot = bcum[0:1] if reverse else bcum[ck - 1:ck]
        q_dec = q * jnp.exp(bcum)
        k_dec = (k * jnp.exp(-bcum)).astype(bf16)
        k_end = (k * jnp.exp(btot - bcum)).astype(bf16)
        s_t = state[bi]
        q_blk = jnp.concatenate([jnp.where(head_of_lane == hd, q_dec, 0.0) for hd in range(nh)],
                                axis=0).astype(bf16)
        scores = lax.dot_general(q_blk, k_dec, nt_dims, preferred_element_type=f32)
        scores = jnp.where(seen, scores, 0.0).astype(bf16)
        intra = jnp.dot(scores, v, preferred_element_type=f32)
        inter = lax.dot_general(q_blk, s_t.astype(bf16), nt_dims, preferred_element_type=f32)
        outs = [intra[hd * ck:(hd + 1) * ck, hd * GLA_DV:(hd + 1) * GLA_DV] + inter[hd * ck:(hd + 1) * ck]
                for hd in range(nh)]
        kv_full = lax.dot_general(v, k_end, tn_dims, preferred_element_type=f32)
        kv_t = jnp.zeros_like(s_t)
        for hd in range(nh):
            kv_t = jnp.where(head_of_lane == hd, kv_full[hd * GLA_DV:(hd + 1) * GLA_DV], kv_t)
        state[bi] = s_t * jnp.exp(btot) + kv_t
        if merge:
            g = pg_ref[bi, :, o_g:o_a].astype(f32)
            normed = []
            for hd in range(GLA_HEADS):
                oh = outs[hd] + ob_ref[bi, :, hd * GLA_DV:(hd + 1) * GLA_DV]
                ms = jnp.mean(oh * oh, axis=-1, keepdims=True)
                normed.append(oh * lax.rsqrt(ms + RMS_EPS))
            y = jnp.concatenate(normed, axis=-1) * gn_ref[...] * (g * jax.nn.sigmoid(g))
            out_ref[bi, :, n % out_ref.shape[2], :] = y
        else:
            out_ref[bi] = jnp.concatenate(outs, axis=-1)

    @pl.when(n == pl.num_programs(0) - 1)
    def _():
        sfin_ref[...] = state[...]


def _gla_pass(pg, wa, ba, s0, *, reverse, ob=None, gn=None):
    n_batch, n_chunks = pg.shape[0], pg.shape[1]
    merge = ob is not None
    order = (lambda n: n_chunks - 1 - n) if reverse else (lambda n: n)
    const2 = lambda n: (0, 0)
    const3 = lambda n: (0, 0, 0)
    chunk = lambda width: pl.BlockSpec((n_batch, None, GLA_CHUNK, width), lambda n: (0, order(n), 0, 0))
    in_specs = [chunk(D_PG),
                pl.BlockSpec((A_LOW_PAD, D_GLA_K), const2),
                pl.BlockSpec((1, D_GLA_K), const2),
                pl.BlockSpec((n_batch, GLA_DV, D_GLA_K), const3)]
    args = [pg, wa, ba, s0]
    if merge:
        in_specs += [chunk(D_GLA_V), pl.BlockSpec((1, D_GLA_V), const2)]
        args += [ob, gn]
        cols_per_block = 8
        out_spec = pl.BlockSpec((n_batch, GLA_CHUNK, cols_per_block, D_GLA_V),
                                lambda n: (0, 0, n // cols_per_block, 0))
        out_shape = (n_batch, GLA_CHUNK, n_chunks, D_GLA_V)
    else:
        out_spec = chunk(D_GLA_V)
        out_shape = (n_batch, n_chunks, GLA_CHUNK, D_GLA_V)
    return pl.pallas_call(
        functools.partial(_gla_kernel, reverse=reverse, merge=merge),
        grid=(n_chunks,),
        in_specs=in_specs,
        out_specs=[out_spec,
                   pl.BlockSpec((n_batch, GLA_DV, D_GLA_K), const3)],
        out_shape=[jax.ShapeDtypeStruct(out_shape, f32),
                   jax.ShapeDtypeStruct((n_batch, GLA_DV, D_GLA_K), f32)],
        scratch_shapes=[pltpu.VMEM((n_batch, GLA_DV, D_GLA_K), f32)],
        compiler_params=_cparams(("arbitrary",)),
        name="gla_merge" if merge else "gla_scan",
    )(*args)


def _pack_rows(v):
    n = v.shape[1] // 2
    w = lax.bitcast_convert_type(v.astype(bf16).astype(f32), jnp.int32)
    return (w[:, :n] & jnp.int32(-65536)) | lax.shift_right_logical(w[:, n:], 16)


def _unpack_rows(w):
    hi = lax.bitcast_convert_type(w & jnp.int32(-65536), f32)
    lo = lax.bitcast_convert_type(lax.shift_left(w, 16), f32)
    return hi, lo


def _outproj_router_kernel(yl_ref, yg_ref, x_ref, mod_ref, wo_ref, lg_ref, lb_ref, wrh_ref, wrl_ref, br_ref,
                           x1_ref, hp_ref, ids_ref, rank_ref, wts_ref, cnt_ref, running):
    i = pl.program_id(0)

    @pl.when(i == 0)
    def _():
        running[...] = jnp.zeros_like(running)

    d = D_MODEL
    y = (jnp.dot(yl_ref[...], wo_ref[0:D_LRU, :], preferred_element_type=f32)
         + jnp.dot(yg_ref[...].astype(bf16), wo_ref[D_LRU:2 * D_LRU, :], preferred_element_type=f32))
    g1 = mod_ref[0, :, 2 * d:3 * d]
    sh2 = mod_ref[0, :, 3 * d:4 * d]
    sc2 = mod_ref[0, :, 4 * d:5 * d]
    z = DEEPNORM_ALPHA * x_ref[...] + g1 * y
    mu = jnp.mean(z, axis=-1, keepdims=True)
    zc = z - mu
    var = jnp.mean(zc * zc, axis=-1, keepdims=True)
    x1 = zc * lax.rsqrt(var + LN_EPS) * lg_ref[...] + lb_ref[...]
    x1_ref[...] = x1
    hmod = x1 * (1.0 + sc2) + sh2
    hp_ref[...] = _pack_rows(hmod)

    nt_dims = (((1,), (1,)), ((), ()))
    h_hi = hmod.astype(bf16)
    h_lo = (hmod - h_hi.astype(f32)).astype(bf16)
    logits = (lax.dot_general(wrh_ref[...], h_hi, nt_dims, preferred_element_type=f32)
              + lax.dot_general(wrh_ref[...], h_lo, nt_dims, preferred_element_type=f32)
              + lax.dot_general(wrl_ref[...], h_hi, nt_dims, preferred_element_type=f32)) + br_ref[...]
    ne, tm = logits.shape
    expert = lax.broadcasted_iota(jnp.int32, (ne, tm), 0).astype(f32)
    neg_inf = jnp.float32(-jnp.inf)
    live = logits
    sel = jnp.zeros((ne, tm), f32)
    ids, vals = [], []
    for _ in range(TOP_K):
        m = jnp.max(live, axis=0, keepdims=True)
        j = jnp.min(jnp.where(live == m, expert, float(ne)), axis=0, keepdims=True)
        pick = expert == j
        sel = jnp.where(pick, 1.0, sel)
        live = jnp.where(pick, neg_inf, live)
        ids.append(j)
        vals.append(m)
    exps = [jnp.exp(vk - vals[0]) for vk in vals]
    denom = exps[0] + exps[1] + exps[2] + exps[3]

    ri = lax.broadcasted_iota(jnp.int32, (tm, tm), 0)
    ci = lax.broadcasted_iota(jnp.int32, (tm, tm), 1)
    earlier = (ri < ci).astype(bf16)
    rank_dense = running[...] + jnp.dot(sel.astype(bf16), earlier, preferred_element_type=f32)
    running[...] = running[...] + jnp.sum(sel, axis=1, keepdims=True)
    cnt_ref[...] = running[...]

    slot = lax.broadcasted_iota(jnp.int32, (8, tm), 0)
    ids_out = jnp.zeros((8, tm), f32)
    rank_out = jnp.zeros((8, tm), f32)
    wts_out = jnp.zeros((8, tm), f32)
    for kk in range(TOP_K):
        rk = jnp.sum(jnp.where(expert == ids[kk], rank_dense, 0.0), axis=0, keepdims=True)
        here = slot == kk
        ids_out = jnp.where(here, ids[kk], ids_out)
        rank_out = jnp.where(here, rk, rank_out)
        wts_out = jnp.where(here, exps[kk] / denom, wts_out)
    ids_ref[...] = ids_out.astype(jnp.int32)
    rank_ref[...] = rank_out.astype(jnp.int32)
    wts_ref[...] = jnp.concatenate([wts_out, jnp.zeros((LANES - 8, tm), f32)], axis=0).T


def _outproj_router(yl, yg, xt, mod3, wo, lg, lb, wr_hi, wr_lo, br, tokens_per_batch):
    T = xt.shape[0]
    tm = TOKEN_TILE
    tiles_per_batch = tokens_per_batch // tm
    tokrow = lambda i: (i, 0)
    const = lambda i: (0, 0)
    return pl.pallas_call(
        _outproj_router_kernel,
        grid=(T // tm,),
        in_specs=[pl.BlockSpec((tm, D_LRU), tokrow),
                  pl.BlockSpec((tm, D_GLA_V), tokrow),
                  pl.BlockSpec((tm, D_MODEL), tokrow),
                  pl.BlockSpec((1, 1, N_MOD * D_MODEL), lambda i: (i // tiles_per_batch, 0, 0)),
                  pl.BlockSpec((D_MODEL, D_MODEL), const),
                  pl.BlockSpec((1, D_MODEL), const),
                  pl.BlockSpec((1, D_MODEL), const),
                  pl.BlockSpec((N_EXPERTS, D_MODEL), const),
                  pl.BlockSpec((N_EXPERTS, D_MODEL), const),
                  pl.BlockSpec((N_EXPERTS, 1), const)],
        out_specs=[pl.BlockSpec((tm, D_MODEL), tokrow),
                   pl.BlockSpec((tm, D_MODEL // 2), tokrow),
                   pl.BlockSpec((8, tm), lambda i: (0, i)),
                   pl.BlockSpec((8, tm), lambda i: (0, i)),
                   pl.BlockSpec((tm, LANES), tokrow),
                   pl.BlockSpec((N_EXPERTS, 1), const)],
        out_shape=[jax.ShapeDtypeStruct((T, D_MODEL), f32),
                   jax.ShapeDtypeStruct((T, D_MODEL // 2), jnp.int32),
                   jax.ShapeDtypeStruct((8, T), jnp.int32),
                   jax.ShapeDtypeStruct((8, T), jnp.int32),
                   jax.ShapeDtypeStruct((T, LANES), f32),
                   jax.ShapeDtypeStruct((N_EXPERTS, 1), f32)],
        scratch_shapes=[pltpu.VMEM((N_EXPERTS, 1), f32)],
        compiler_params=_cparams(("arbitrary",)),
        name="outproj_router",
    )(yl, yg, xt, mod3, wo, lg, lb, wr_hi, wr_lo, br)


def _route_kernel(ids_ref, rank_ref, cnt_ref, dest_ref, tiles_ref, *, bm):
    ne = cnt_ref.shape[0]
    cnt = cnt_ref[...]
    padded = jnp.floor((cnt + (bm - 1.0)) * (1.0 / bm)) * bm
    sub = lax.broadcasted_iota(jnp.int32, (ne, LANES), 0)
    lane = lax.broadcasted_iota(jnp.int32, (ne, LANES), 1)
    padded_row = jnp.sum(jnp.where(sub == lane, padded, 0.0), axis=0, keepdims=True)
    base = jnp.sum(jnp.where(lane < sub, padded_row, 0.0), axis=1, keepdims=True)
    ends = base + padded

    tc = ids_ref.shape[1]
    expert = lax.broadcasted_iota(jnp.int32, (ne, tc), 0)
    slot = lax.broadcasted_iota(jnp.int32, (8, tc), 0)
    ids = ids_ref[...]
    group_base = jnp.zeros((8, tc), f32)
    for kk in range(TOP_K):
        b_k = jnp.sum(jnp.where(expert == ids[kk:kk + 1], base, 0.0), axis=0, keepdims=True)
        group_base = jnp.where(slot == kk, b_k, group_base)
    dest_ref[...] = group_base.astype(jnp.int32) + rank_ref[...]

    nl = tiles_ref.shape[1]
    start = lax.broadcasted_iota(jnp.int32, (ne, nl), 1).astype(f32) * bm
    te = jnp.minimum(jnp.sum(jnp.where(start >= ends, 1.0, 0.0), axis=0, keepdims=True), ne - 1.0)
    at_te = lax.broadcasted_iota(jnp.int32, (ne, nl), 0).astype(f32) == te
    cnt_te = jnp.sum(jnp.where(at_te, cnt, 0.0), axis=0, keepdims=True)
    base_te = jnp.sum(jnp.where(at_te, base, 0.0), axis=0, keepdims=True)
    valid = jnp.clip(cnt_te - (start[0:1] - base_te), 0.0, float(bm))
    next_group = jnp.sum(jnp.where(at_te, ends, 0.0), axis=0, keepdims=True) * (1.0 / bm)
    srow = lax.broadcasted_iota(jnp.int32, (8, nl), 0)
    table = jnp.where(srow == 0, te, jnp.where(srow == 1, valid, jnp.where(srow == 2, next_group, 0.0)))
    tiles_ref[...] = table.astype(jnp.int32)


def _route(ids, rank, cnt, bm, n_tiles):
    T = ids.shape[1]
    tc = 2048
    nl = -(-n_tiles // LANES) * LANES
    tok = lambda i: (0, i)
    const = lambda i: (0, 0)
    return pl.pallas_call(
        functools.partial(_route_kernel, bm=bm),
        grid=(T // tc,),
        in_specs=[pl.BlockSpec((8, tc), tok), pl.BlockSpec((8, tc), tok),
                  pl.BlockSpec((N_EXPERTS, 1), const)],
        out_specs=[pl.BlockSpec((8, tc), tok), pl.BlockSpec((8, nl), const)],
        out_shape=[jax.ShapeDtypeStruct((8, T), jnp.int32), jax.ShapeDtypeStruct((8, nl), jnp.int32)],
        compiler_params=_cparams(("arbitrary",)),
        name="route",
    )(ids, rank, cnt)


def _sc_workers():
    info = plsc.get_sparse_core_info()
    return info.num_cores, info.num_subcores


def _sc_dispatch(rows, dest_flat, n_out):
    T, D = rows.shape
    nc, ns = _sc_workers()
    per_w = T // (nc * ns)
    n_chunks = per_w // SC_CHUNK
    mesh = plsc.VectorSubcoreMesh(core_axis_name="c", subcore_axis_name="s")

    @functools.partial(
        pl.kernel, mesh=mesh,
        out_type=jax.ShapeDtypeStruct((n_out, D), rows.dtype),
        scratch_types=([pltpu.VMEM((SC_CHUNK,), jnp.int32)] * TOP_K
                       + [pltpu.VMEM((SC_CHUNK, D), rows.dtype)]
                       + [pltpu.SemaphoreType.DMA] * TOP_K),
    )
    def k(rows_hbm, dest_hbm, out_hbm, *scratch):
        idx_v = scratch[:TOP_K]
        rows_v = scratch[TOP_K]
        sems = scratch[TOP_K + 1:]
        wid = lax.axis_index("s") * nc + lax.axis_index("c")
        base = wid * per_w

        @pl.loop(0, n_chunks)
        def _(j):
            off = base + j * SC_CHUNK
            pltpu.sync_copy(rows_hbm.at[pl.ds(off, SC_CHUNK)], rows_v)
            for kk in range(TOP_K):
                pltpu.sync_copy(dest_hbm.at[pl.ds(kk * T + off, SC_CHUNK)], idx_v[kk])
            copies = [pltpu.async_copy(rows_v, out_hbm.at[idx_v[kk]], sems[kk]) for kk in range(TOP_K)]
            for cp in copies:
                cp.wait()

    return k(rows, dest_flat)


def _sc_gather(table, idx):
    _, D = table.shape
    N = idx.shape[0]
    nc, ns = _sc_workers()
    per_w = N // (nc * ns)
    n_chunks = per_w // SC_CHUNK
    assert n_chunks % 2 == 0
    mesh = plsc.VectorSubcoreMesh(core_axis_name="c", subcore_axis_name="s")

    @functools.partial(
        pl.kernel, mesh=mesh,
        out_type=jax.ShapeDtypeStruct((N, D), table.dtype),
        scratch_types=([pltpu.VMEM((SC_CHUNK,), jnp.int32)] * 2
                       + [pltpu.VMEM((SC_CHUNK, D), table.dtype)] * 2
                       + [pltpu.SemaphoreType.DMA] * 2),
    )
    def k(table_hbm, idx_hbm, out_hbm, idx0, idx1, buf0, buf1, sem0, sem1):
        idxs, bufs, sems = (idx0, idx1), (buf0, buf1), (sem0, sem1)
        wid = lax.axis_index("s") * nc + lax.axis_index("c")
        base = wid * per_w

        def gather(b):
            return pltpu.make_async_copy(table_hbm.at[idxs[b]], bufs[b], sems[b])

        def start(chunk, b):
            pltpu.sync_copy(idx_hbm.at[pl.ds(base + chunk * SC_CHUNK, SC_CHUNK)], idxs[b])
            gather(b).start()

        def finish(chunk, b):
            gather(b).wait()
            pltpu.sync_copy(bufs[b], out_hbm.at[pl.ds(base + chunk * SC_CHUNK, SC_CHUNK)])

        start(0, 0)

        @pl.loop(0, n_chunks, step=2)
        def _(j):
            start(j + 1, 1)
            finish(j, 0)

            @pl.when(j + 2 < n_chunks)
            def _():
                start(j + 2, 0)

            finish(j + 1, 1)

    return k(table, idx)


def _ffn_kernel(te_ref, tv_ref, tn_ref, xs_ref, wg_hbm, bg_ref, wu_hbm, bu_ref, wd_hbm, bd_ref, eo_ref,
                w_stage, wg_b, wu_b, wd_b, sems):
    i = pl.program_id(0)
    n_tiles = pl.num_programs(0)
    w_hbm = (wg_hbm, wu_hbm, wd_hbm)
    w_b = (wg_b, wu_b, wd_b)

    def fetch(e):
        return [pltpu.make_async_copy(w_hbm[m].at[e], w_stage.at[m], sems.at[m]) for m in range(3)]

    @pl.when(i == 0)
    def _():
        for cp in fetch(te_ref[0]):
            cp.start()

    prev = te_ref[jnp.maximum(i - 1, 0)]

    @pl.when((i == 0) | (te_ref[i] != prev))
    def _():
        for m, cp in enumerate(fetch(te_ref[i])):
            cp.wait()
            w_b[m][...] = w_stage[m].astype(bf16)
        nxt = tn_ref[i]
        e_nxt = te_ref[jnp.minimum(nxt, n_tiles - 1)]

        @pl.when((nxt > i) & (nxt < n_tiles) & (e_nxt != te_ref[i]))
        def _():
            for cp in fetch(e_nxt):
                cp.start()

    valid = tv_ref[i]

    def ffn_rows(r0, m):
        row = lax.broadcasted_iota(jnp.int32, (m, 1), 0) + r0
        xw = jnp.where(row < valid, xs_ref[r0:r0 + m, :], 0)
        x_hi, x_lo = _unpack_rows(xw)
        x = jnp.concatenate([x_hi, x_lo], axis=1).astype(bf16)
        gate = jnp.minimum(jnp.dot(x, wg_b[...], preferred_element_type=f32) + bg_ref[...], SWIGLU_LIMIT)
        up = jnp.clip(jnp.dot(x, wu_b[...], preferred_element_type=f32) + bu_ref[...],
                      -SWIGLU_LIMIT, SWIGLU_LIMIT)
        act = (up + 1.0) * gate * jax.nn.sigmoid(SWIGLU_ALPHA * gate)
        out = jnp.dot(act.astype(bf16), wd_b[...], preferred_element_type=f32) + bd_ref[...]
        eo_ref[r0:r0 + m, :] = _pack_rows(out)

    def zero_rows(r0, m):
        eo_ref[r0:r0 + m, :] = jnp.zeros((m, eo_ref.shape[1]), eo_ref.dtype)

    for r0 in range(0, xs_ref.shape[0], MOE_PASS):
        @pl.when(valid > r0 + MOE_SUBTILE)
        def _(r0=r0):
            ffn_rows(r0, MOE_PASS)

        @pl.when((valid > r0) & (valid <= r0 + MOE_SUBTILE))
        def _(r0=r0):
            ffn_rows(r0, MOE_SUBTILE)
            zero_rows(r0 + MOE_SUBTILE, MOE_PASS - MOE_SUBTILE)

        @pl.when(valid <= r0)
        def _(r0=r0):
            zero_rows(r0, MOE_PASS)


def _expert_ffn(tile_expert, tile_valid, tile_next, xs, w_gate, b_gate, w_up, b_up, w_down, b_down):
    n_rows, dp = xs.shape
    d = 2 * dp
    bm = MOE_TILE
    d_e = w_gate.shape[-1]
    assert d == d_e
    bspec = lambda n_: pl.BlockSpec((None, 1, n_), lambda i, te, tv, tn: (te[i], 0, 0))
    hbm = pl.BlockSpec(memory_space=pl.ANY)
    grid_spec = pltpu.PrefetchScalarGridSpec(
        num_scalar_prefetch=3,
        grid=(n_rows // bm,),
        in_specs=[pl.BlockSpec((bm, dp), lambda i, te, tv, tn: (i, 0)),
                  hbm, bspec(d_e), hbm, bspec(d_e), hbm, bspec(d)],
        out_specs=pl.BlockSpec((bm, dp), lambda i, te, tv, tn: (i, 0)),
        scratch_shapes=[pltpu.VMEM((3, d, d_e), f32),
                        pltpu.VMEM((d, d_e), bf16), pltpu.VMEM((d, d_e), bf16), pltpu.VMEM((d_e, d), bf16),
                        pltpu.SemaphoreType.DMA((3,))],
    )
    return pl.pallas_call(
        _ffn_kernel,
        grid_spec=grid_spec,
        out_shape=jax.ShapeDtypeStruct((n_rows, dp), jnp.int32),
        compiler_params=_cparams(("arbitrary",)),
        name="expert_ffn",
    )(tile_expert, tile_valid, tile_next, xs, w_gate, b_gate.reshape(N_EXPERTS, 1, d_e), w_up,
      b_up.reshape(N_EXPERTS, 1, d_e), w_down, b_down.reshape(N_EXPERTS, 1, d))


def _combine_kernel(eg_ref, wts_ref, x1_ref, mod_ref, lg_ref, lb_ref, o_ref):
    w = wts_ref[...]
    y_hi, y_lo = _unpack_rows(eg_ref[0])
    y_hi, y_lo = y_hi * w[:, 0:1], y_lo * w[:, 0:1]
    for kk in range(1, TOP_K):
        e_hi, e_lo = _unpack_rows(eg_ref[kk])
        y_hi = y_hi + e_hi * w[:, kk:kk + 1]
        y_lo = y_lo + e_lo * w[:, kk:kk + 1]
    y = jnp.concatenate([y_hi, y_lo], axis=1)
    z = DEEPNORM_ALPHA * x1_ref[...] + mod_ref[0] * y
    mu = jnp.mean(z, axis=-1, keepdims=True)
    zc = z - mu
    var = jnp.mean(zc * zc, axis=-1, keepdims=True)
    o_ref[...] = zc * lax.rsqrt(var + LN_EPS) * lg_ref[...] + lb_ref[...]


def _combine_ln(eg, wts, x1, mod3, lg, lb, tokens_per_batch):
    T = x1.shape[0]
    tm = TOKEN_TILE
    tiles_per_batch = tokens_per_batch // tm
    const = lambda i: (0, 0)
    return pl.pallas_call(
        _combine_kernel,
        grid=(T // tm,),
        in_specs=[pl.BlockSpec((TOP_K, tm, D_MODEL // 2), lambda i: (0, i, 0)),
                  pl.BlockSpec((tm, LANES), lambda i: (i, 0)),
                  pl.BlockSpec((tm, D_MODEL), lambda i: (i, 0)),
                  pl.BlockSpec((1, 1, D_MODEL), lambda i: (i // tiles_per_batch, 0, N_MOD - 1)),
                  pl.BlockSpec((1, D_MODEL), const),
                  pl.BlockSpec((1, D_MODEL), const)],
        out_specs=pl.BlockSpec((tm, D_MODEL), lambda i: (i, 0)),
        out_shape=jax.ShapeDtypeStruct((T, D_MODEL), f32),
        compiler_params=_cparams(("arbitrary",)),
        name="combine_ln",
    )(eg, wts, x1, mod3, lg, lb)


def _block_diag(w):
    n, c, d = w.shape
    eye = jnp.eye(n, dtype=w.dtype)
    return jnp.einsum('ncd,nm->ncmd', w, eye).reshape(n * c, n * d)


def kernel(x, c, ctx, c_ctx, w_ada, b_ada, w_in, conv_w, conv_b, lru_wa, lru_ba, lru_wx, lru_bx,
           lru_lam, gla_wa, gla_ba, gla_norm_g, w_out, ln1_g, ln1_b, w_router, b_router, w_gate,
           b_gate, w_up, b_up, w_down, b_down, ln2_g, ln2_b):
    B, L, D = x.shape
    Lc = ctx.shape[1]
    T = B * L
    rows = L // GRID_W
    l = 0

    cpad = jnp.zeros((8, D), f32).at[0:B].set(c).at[B].set(c_ctx)
    w_cat = jnp.pad(w_in[l], ((0, 0), (0, D_PROJ - w_in.shape[-1]))).astype(bf16)
    wg = [jnp.concatenate([_block_diag(lru_wa[l, d]), _block_diag(lru_wx[l, d])], axis=1).astype(bf16)
          for d in range(2)]
    bg = [jnp.concatenate([lru_ba[l, d], lru_bx[l, d]])[None] for d in range(2)]
    lam = [lru_lam[l, d][None] for d in range(2)]
    wa = [jnp.pad(gla_wa[l, d], ((0, A_LOW_PAD - GATE_RANK), (0, 0))).astype(bf16) for d in range(2)]
    ba = [gla_ba[l, d][None] for d in range(2)]
    cw, cb = conv_w[l], conv_b[l][None]
    wr_t = w_router[l].T
    wr_hi = wr_t.astype(bf16)
    wr_lo = (wr_t - wr_hi.astype(f32)).astype(bf16)
    br = b_router[l][:, None]
    c_xl, c_gl, c_pg = (0, D_LRU), (D_LRU, 2 * D_LRU), (2 * D_LRU, D_PROJ)

    mod3 = _ada_mod(cpad, w_ada[l], b_ada[l]).reshape(8, 1, N_MOD * D)

    tiles_per_batch = L // TOKEN_TILE
    xl_c, pg_c = _inproj_rows(ctx.reshape(B * Lc, D), mod3, w_cat, lambda i: B, (c_xl, c_pg), (f32, bf16))
    zero_h = jnp.zeros((B, 1, D_LRU), f32)
    zero_s = jnp.zeros((B, GLA_DV, D_GLA_K), f32)
    pg_c = pg_c.reshape(B, Lc // GLA_CHUNK, GLA_CHUNK, D_PG)
    h_ctx, s_ctx = [], []
    for d in range(2):
        _, hf = _lru_pass(xl_c, cw, cb, wg[d], bg[d], lam[d], zero_h, n_batch=B, row_w=Lc, tile=Lc,
                          reverse=bool(d))
        _, sf = _gla_pass(pg_c, wa[d], ba[d], zero_s, reverse=bool(d))
        h_ctx.append(hf)
        s_ctx.append(sf)

    xt = x.reshape(T, D)
    xl, gl = _inproj_rows(xt, mod3, w_cat, lambda i: i // tiles_per_batch, (c_xl, c_gl), (f32, bf16))
    hb, _ = _lru_pass(xl, cw, cb, wg[1], bg[1], lam[1], h_ctx[1], n_batch=B, row_w=GRID_W, tile=LRU_TILE,
                      reverse=True)
    y_lru, _ = _lru_pass(xl, cw, cb, wg[0], bg[0], lam[0], h_ctx[0], n_batch=B, row_w=GRID_W,
                         tile=LRU_TILE, reverse=False, hb=hb, gl=gl)
    pg = _inproj_cols(x.reshape(B, rows, GRID_W, D), mod3, w_cat, c_pg)
    ob, _ = _gla_pass(pg, wa[1], ba[1], s_ctx[1], reverse=True)
    y_gla, _ = _gla_pass(pg, wa[0], ba[0], s_ctx[0], reverse=False, ob=ob, gn=gla_norm_g[l][None])
    y_gla = y_gla.reshape(T, D_GLA_V)

    x1, hp, ids, rank, wts, cnt = _outproj_router(
        y_lru, y_gla, xt, mod3, w_out[l].astype(bf16), ln1_g[l][None], ln1_b[l][None], wr_hi, wr_lo, br, L)

    n_tiles = T * TOP_K // MOE_TILE + N_EXPERTS
    dest, tiles = _route(ids, rank, cnt, MOE_TILE, n_tiles)
    xs = _sc_dispatch(hp, dest.reshape(-1), n_tiles * MOE_TILE)
    eo = _expert_ffn(tiles[0, :n_tiles], tiles[1, :n_tiles], tiles[2, :n_tiles], xs, w_gate[l], b_gate[l],
                     w_up[l], b_up[l], w_down[l], b_down[l])
    eg = _sc_gather(eo, dest[:TOP_K].reshape(-1)).reshape(TOP_K, T, D // 2)
    out = _combine_ln(eg, wts, x1, mod3, ln2_g[l][None], ln2_b[l][None], L)
    return out.reshape(B, L, D)
```

```python
import functools

import jax
import jax.numpy as jnp
from jax import lax
from jax.experimental import pallas as pl
from jax.experimental.pallas import tpu as pltpu
from jax.experimental.pallas import tpu_sc as plsc

D_MODEL = 1024
DEPTH = 1
GRID_W = 64
D_LRU = 512
LRU_BLOCKS = 8
CONV_W = 4
LRU_C = 8.0
GLA_HEADS = 4
D_GLA_V = 512
D_GLA_K = 256
GLA_DK = 64
GLA_DV = 128
GATE_RANK = 16
GATE_TAU = 16.0
GLA_CHUNK = 64
N_EXPERTS = 32
TOP_K = 4
SWIGLU_LIMIT = 7.0
SWIGLU_ALPHA = 1.702
N_MOD = 6
DEEPNORM_ALPHA = (2.0 * DEPTH) ** 0.25
LN_EPS = 1e-5
RMS_EPS = 1e-6

LANES = 128
A_LOW_PAD = LANES
D_PG = 2 * D_GLA_K + 2 * D_GLA_V + A_LOW_PAD
D_PROJ = 2 * D_LRU + D_PG
TOKEN_TILE = 512
LRU_TILE = 512
MOE_TILE = 1024
MOE_PASS = 512
MOE_SUBTILE = 256
SC_CHUNK = 64
VMEM_LIMIT = 48 * 1024 * 1024

f32 = jnp.float32
bf16 = jnp.bfloat16


def _cparams(sem):
    return pltpu.CompilerParams(dimension_semantics=sem, vmem_limit_bytes=VMEM_LIMIT)


def _ada_kernel(c_ref, w_ref, b_ref, o_ref):
    s = c_ref[...]
    s = s * jax.nn.sigmoid(s)
    o_ref[...] = jnp.dot(s.astype(bf16), w_ref[...].astype(bf16), preferred_element_type=f32) + b_ref[...]


def _ada_mod(cpad, w, b):
    n = w.shape[1]
    tn = 1024
    return pl.pallas_call(
        _ada_kernel,
        grid=(n // tn,),
        in_specs=[pl.BlockSpec((8, D_MODEL), lambda j: (0, 0)),
                  pl.BlockSpec((D_MODEL, tn), lambda j: (0, j)),
                  pl.BlockSpec((1, tn), lambda j: (0, j))],
        out_specs=pl.BlockSpec((8, tn), lambda j: (0, j)),
        out_shape=jax.ShapeDtypeStruct((8, n), f32),
        compiler_params=_cparams(("arbitrary",)),
        name="ada_mod",
    )(cpad, w, b.reshape(1, n))


def _inproj_kernel(x_ref, mod_ref, w_ref, *out_refs, parts, col_major):
    sh = mod_ref[0, :, 0:D_MODEL]
    sc = mod_ref[0, :, D_MODEL:2 * D_MODEL]
    if col_major:
        x = jnp.concatenate([x_ref[0, :, j, :] for j in range(x_ref.shape[2])], axis=0)
    else:
        x = x_ref[...]
    u = (x * (1.0 + sc) + sh).astype(bf16)
    for (lo, hi), o_ref in zip(parts, out_refs):
        p = jnp.dot(u, w_ref[:, lo:hi], preferred_element_type=f32).astype(o_ref.dtype)
        o_ref[...] = p.reshape(o_ref.shape)


def _inproj_rows(xt, mod3, w_cat, batch_of_tile, parts, dtypes):
    T = xt.shape[0]
    tm = TOKEN_TILE
    return pl.pallas_call(
        functools.partial(_inproj_kernel, parts=parts, col_major=False),
        grid=(T // tm,),
        in_specs=[pl.BlockSpec((tm, D_MODEL), lambda i: (i, 0)),
                  pl.BlockSpec((1, 1, 2 * D_MODEL), lambda i: (batch_of_tile(i), 0, 0)),
                  pl.BlockSpec((D_MODEL, D_PROJ), lambda i: (0, 0))],
        out_specs=[pl.BlockSpec((tm, hi - lo), lambda i: (i, 0)) for lo, hi in parts],
        out_shape=[jax.ShapeDtypeStruct((T, hi - lo), dt) for (lo, hi), dt in zip(parts, dtypes)],
        compiler_params=_cparams(("arbitrary",)),
        name="inproj_rows",
    )(xt, mod3, w_cat)


def _inproj_cols(x4, mod3, w_cat, part):
    n_batch, rows, cols, _ = x4.shape
    lo, hi = part
    cb = TOKEN_TILE // rows
    return pl.pallas_call(
        functools.partial(_inproj_kernel, parts=(part,), col_major=True),
        grid=(n_batch, cols // cb),
        in_specs=[pl.BlockSpec((1, rows, cb, D_MODEL), lambda b, n: (b, 0, n, 0)),
                  pl.BlockSpec((1, 1, 2 * D_MODEL), lambda b, n: (b, 0, 0)),
                  pl.BlockSpec((D_MODEL, D_PROJ), lambda b, n: (0, 0))],
        out_specs=[pl.BlockSpec((1, cb, rows, hi - lo), lambda b, n: (b, n, 0, 0))],
        out_shape=[jax.ShapeDtypeStruct((n_batch, cols, rows, hi - lo), bf16)],
        compiler_params=_cparams(("arbitrary", "arbitrary")),
        name="inproj_cols",
    )(x4, mod3, w_cat)[0]


def _gelu_tanh(x):
    return 0.5 * x * (1.0 + jnp.tanh(0.7978845608028654 * (x + 0.044715 * (x * x * x))))


def _lru_kernel(*refs, row_w, reverse, merge):
    if merge:
        (xl_ref, cw_ref, cb_ref, wg_ref, bg_ref, lam_ref, h0_ref, hb_ref, gl_ref,
         out_ref, hfin_ref, carry, h_nat) = refs
    else:
        xl_ref, cw_ref, cb_ref, wg_ref, bg_ref, lam_ref, h0_ref, out_ref, hfin_ref, carry = refs
    t = pl.program_id(1)

    @pl.when(t == 0)
    def _():
        carry[...] = h0_ref[0]

    n_seg, seg, ch = xl_ref.shape
    blk = lambda v, j: v[j * n_seg:(j + 1) * n_seg]
    x = jnp.concatenate([xl_ref[:, j, :] for j in range(seg)], axis=0)
    tt = n_seg * seg

    segs_per_row = row_w // seg
    s_idx = lax.broadcasted_iota(jnp.int32, (n_seg, ch), 0)
    has_prev = (s_idx % segs_per_row) != 0
    has_next = (s_idx % segs_per_row) != segs_per_row - 1
    from_prev = lambda v: jnp.where(has_prev, pltpu.roll(v, 1, 0), 0.0)
    from_next = lambda v: jnp.where(has_next, pltpu.roll(v, n_seg - 1, 0), 0.0)
    x_m1 = jnp.concatenate([from_prev(blk(x, seg - 1)), x[:tt - n_seg]], axis=0)
    x_m2 = jnp.concatenate([from_prev(blk(x, seg - 2)), from_prev(blk(x, seg - 1)), x[:tt - 2 * n_seg]], axis=0)
    x_p1 = jnp.concatenate([x[n_seg:], from_next(blk(x, 0))], axis=0)
    cw = cw_ref[...]
    u = cb_ref[...] + x_m2 * cw[0:1] + x_m1 * cw[1:2] + x * cw[2:3] + x_p1 * cw[3:4]
    g = jnp.dot(u.astype(bf16), wg_ref[...], preferred_element_type=f32) + bg_ref[...]
    r_gate = jax.nn.sigmoid(g[:, 0:ch])
    i_gate = jax.nn.sigmoid(g[:, ch:2 * ch])
    lam = lam_ref[...]
    softplus_neg = jnp.maximum(-lam, 0.0) + jnp.log(1.0 + jnp.exp(-jnp.abs(lam)))
    log_a = (-LRU_C * softplus_neg) * r_gate
    a = jnp.exp(log_a)
    t2 = 2.0 * log_a
    series = -t2 * (1.0 + t2 * (0.5 + t2 * (1.0 / 6.0 + t2 * (1.0 / 24.0))))
    one_minus_a2 = jnp.where(t2 > -0.03, series, 1.0 - a * a)
    b = jnp.sqrt(one_minus_a2) * (i_gate * u)

    half = seg // 2
    h_loc, a_cum = [None] * seg, [None] * seg
    h_run, a_run = [None, None], [None, None]
    for i in range(half):
        for p in range(2):
            j = p * half + (half - 1 - i if reverse else i)
            a_j, b_j = blk(a, j), blk(b, j)
            h_run[p] = b_j if i == 0 else a_j * h_run[p] + b_j
            a_run[p] = a_j if i == 0 else a_j * a_run[p]
            h_loc[j], a_cum[j] = h_run[p], a_run[p]
    pieces = [(s, p) for s in range(n_seg) for p in range(2)]
    if reverse:
        pieces.reverse()
    entering = [[None] * n_seg, [None] * n_seg]
    state = carry[...]
    for s, p in pieces:
        entering[p][s] = state
        state = h_run[p][s:s + 1] + a_run[p][s:s + 1] * state
    carry[...] = state
    hfin_ref[0] = state
    enter = [jnp.concatenate(e, axis=0) for e in entering]
    for j in range(seg):
        h_j = h_loc[j] + a_cum[j] * enter[j // half]
        if merge:
            h_nat[:, j, :] = h_j
        else:
            out_ref[:, j, :] = h_j
    if merge:
        h = h_nat[...].reshape(tt, ch)
        out_ref[...] = ((h + hb_ref[...]) * _gelu_tanh(gl_ref[...].astype(f32))).astype(out_ref.dtype)


def _lru_pass(xl, conv_w, conv_b, wg, bg, lam, h0, *, n_batch, row_w, tile, reverse, hb=None, gl=None):
    T = xl.shape[0]
    nt = T // n_batch // tile
    merge = hb is not None
    n_seg = 8
    seg = tile // n_seg
    assert row_w % seg == 0 and seg >= 2

    def tok(b, t):
        return (b * nt + (nt - 1 - t if reverse else t), 0)

    seg_spec = pl.BlockSpec((n_seg, seg, D_LRU), lambda b, t: tok(b, t) + (0,))
    const = lambda b, t: (0, 0)
    in_specs = [seg_spec,
                pl.BlockSpec((CONV_W, D_LRU), const),
                pl.BlockSpec((1, D_LRU), const),
                pl.BlockSpec((D_LRU, 2 * D_LRU), const),
                pl.BlockSpec((1, 2 * D_LRU), const),
                pl.BlockSpec((1, D_LRU), const),
                pl.BlockSpec((1, 1, D_LRU), lambda b, t: (b, 0, 0))]
    args = [xl.reshape(T // seg, seg, D_LRU), conv_w, conv_b, wg, bg, lam, h0]
    scratch = [pltpu.VMEM((1, D_LRU), f32)]
    if merge:
        in_specs += [pl.BlockSpec((tile, D_LRU), tok), pl.BlockSpec((tile, D_LRU), tok)]
        args += [hb, gl]
        scratch += [pltpu.VMEM((n_seg, seg, D_LRU), f32)]
        out_spec, out_shape = pl.BlockSpec((tile, D_LRU), tok), jax.ShapeDtypeStruct((T, D_LRU), bf16)
    else:
        out_spec, out_shape = seg_spec, jax.ShapeDtypeStruct((T // seg, seg, D_LRU), f32)
    out, h_fin = pl.pallas_call(
        functools.partial(_lru_kernel, row_w=row_w, reverse=reverse, merge=merge),
        grid=(n_batch, nt),
        in_specs=in_specs,
        out_specs=[out_spec, pl.BlockSpec((1, 1, D_LRU), lambda b, t: (b, 0, 0))],
        out_shape=[out_shape, jax.ShapeDtypeStruct((n_batch, 1, D_LRU), f32)],
        scratch_shapes=scratch,
        compiler_params=_cparams(("arbitrary", "arbitrary")),
        name="lru_merge" if merge else "lru_scan",
    )(*args)
    return out.reshape(T, D_LRU), h_fin


def _gla_kernel(*refs, reverse, merge):
    if merge:
        pg_ref, wa_ref, ba_ref, s0_ref, ob_ref, gn_ref, out_ref, sfin_ref, state = refs
    else:
        pg_ref, wa_ref, ba_ref, s0_ref, out_ref, sfin_ref, state = refs
    n = pl.program_id(0)

    @pl.when(n == 0)
    def _():
        state[...] = s0_ref[...]

    n_batch = pg_ref.shape[0]
    ck = GLA_CHUNK
    nh = GLA_HEADS
    ii = lax.broadcasted_iota(jnp.int32, (nh * ck, ck), 0) & (ck - 1)
    jj = lax.broadcasted_iota(jnp.int32, (nh * ck, ck), 1)
    seen = (jj >= ii) if reverse else (jj <= ii)
    head_of_lane = lax.broadcasted_iota(jnp.int32, (1, D_GLA_K), 1) // GLA_DK
    nt_dims = (((1,), (1,)), ((), ()))
    tn_dims = (((0,), (0,)), ((), ()))
    o_q, o_k, o_v, o_g, o_a = 0, D_GLA_K, 2 * D_GLA_K, 2 * D_GLA_K + D_GLA_V, 2 * D_GLA_K + 2 * D_GLA_V

    rows_all = n_batch * ck
    a_low = jnp.concatenate([pg_ref[bi, :, o_a:o_a + A_LOW_PAD] for bi in range(n_batch)], axis=0)
    z = jnp.dot(a_low, wa_ref[...], preferred_element_type=f32) + ba_ref[...]
    bcum_all = (jnp.minimum(z, 0.0) - jnp.log(1.0 + jnp.exp(-jnp.abs(z)))) * (1.0 / GATE_TAU)
    step = lax.broadcasted_iota(jnp.int32, (rows_all, D_GLA_K), 0) & (ck - 1)
    d = 1
    while d < ck:
        if reverse:
            bcum_all = bcum_all + jnp.where(step < ck - d, pltpu.roll(bcum_all, rows_all - d, 0), 0.0)
        else:
            bcum_all = bcum_all + jnp.where(step >= d, pltpu.roll(bcum_all, d, 0), 0.0)
        d *= 2

    for bi in range(n_batch):
        q = pg_ref[bi, :, o_q:o_k].astype(f32) * (GLA_DK ** -0.5)
        k = pg_ref[bi, :, o_k:o_v].astype(f32)
        v = pg_ref[bi, :, o_v:o_g]
        bcum = bcum_all[bi * ck:(bi + 1) * ck]
        btot = bcum[0:1] if reverse else bcum[ck - 1:ck]
        q_dec = q * jnp.exp(bcum)
        k_dec = (k * jnp.exp(-bcum)).astype(bf16)
        k_end = (k * jnp.exp(btot - bcum)).astype(bf16)
        s_t = state[bi]
        q_blk = jnp.concatenate([jnp.where(head_of_lane == hd, q_dec, 0.0) for hd in range(nh)],
                                axis=0).astype(bf16)
        scores = lax.dot_general(q_blk, k_dec, nt_dims, preferred_element_type=f32)
        scores = jnp.where(seen, scores, 0.0).astype(bf16)
        intra = jnp.dot(scores, v, preferred_element_type=f32)
        inter = lax.dot_general(q_blk, s_t.astype(bf16), nt_dims, preferred_element_type=f32)
        outs = [intra[hd * ck:(hd + 1) * ck, hd * GLA_DV:(hd + 1) * GLA_DV] + inter[hd * ck:(hd + 1) * ck]
                for hd in range(nh)]
        kv_full = lax.dot_general(v, k_end, tn_dims, preferred_element_type=f32)
        kv_t = jnp.zeros_like(s_t)
        for hd in range(nh):
            kv_t = jnp.where(head_of_lane == hd, kv_full[hd * GLA_DV:(hd + 1) * GLA_DV], kv_t)
        state[bi] = s_t * jnp.exp(btot) + kv_t
        if merge:
            g = pg_ref[bi, :, o_g:o_a].astype(f32)
            normed = []
            for hd in range(GLA_HEADS):
                oh = outs[hd] + ob_ref[bi, :, hd * GLA_DV:(hd + 1) * GLA_DV]
                ms = jnp.mean(oh * oh, axis=-1, keepdims=True)
                normed.append(oh * lax.rsqrt(ms + RMS_EPS))
            y = jnp.concatenate(normed, axis=-1) * gn_ref[...] * (g * jax.nn.sigmoid(g))
            out_ref[bi, :, n % out_ref.shape[2], :] = y
        else:
            out_ref[bi] = jnp.concatenate(outs, axis=-1)

    @pl.when(n == pl.num_programs(0) - 1)
    def _():
        sfin_ref[...] = state[...]


def _gla_pass(pg, wa, ba, s0, *, reverse, ob=None, gn=None):
    n_batch, n_chunks = pg.shape[0], pg.shape[1]
    merge = ob is not None
    order = (lambda n: n_chunks - 1 - n) if reverse else (lambda n: n)
    const2 = lambda n: (0, 0)
    const3 = lambda n: (0, 0, 0)
    chunk = lambda width: pl.BlockSpec((n_batch, None, GLA_CHUNK, width), lambda n: (0, order(n), 0, 0))
    in_specs = [chunk(D_PG),
                pl.BlockSpec((A_LOW_PAD, D_GLA_K), const2),
                pl.BlockSpec((1, D_GLA_K), const2),
                pl.BlockSpec((n_batch, GLA_DV, D_GLA_K), const3)]
    args = [pg, wa, ba, s0]
    if merge:
        in_specs += [chunk(D_GLA_V), pl.BlockSpec((1, D_GLA_V), const2)]
        args += [ob, gn]
        cols_per_block = 8
        out_spec = pl.BlockSpec((n_batch, GLA_CHUNK, cols_per_block, D_GLA_V),
                                lambda n: (0, 0, n // cols_per_block, 0))
        out_shape = (n_batch, GLA_CHUNK, n_chunks, D_GLA_V)
    else:
        out_spec = chunk(D_GLA_V)
        out_shape = (n_batch, n_chunks, GLA_CHUNK, D_GLA_V)
    return pl.pallas_call(
        functools.partial(_gla_kernel, reverse=reverse, merge=merge),
        grid=(n_chunks,),
        in_specs=in_specs,
        out_specs=[out_spec,
                   pl.BlockSpec((n_batch, GLA_DV, D_GLA_K), const3)],
        out_shape=[jax.ShapeDtypeStruct(out_shape, f32),
                   jax.ShapeDtypeStruct((n_batch, GLA_DV, D_GLA_K), f32)],
        scratch_shapes=[pltpu.VMEM((n_batch, GLA_DV, D_GLA_K), f32)],
        compiler_params=_cparams(("arbitrary",)),
        name="gla_merge" if merge else "gla_scan",
    )(*args)


def _pack_rows(v):
    n = v.shape[1] // 2
    w = lax.bitcast_convert_type(v.astype(bf16).astype(f32), jnp.int32)
    return (w[:, :n] & jnp.int32(-65536)) | lax.shift_right_logical(w[:, n:], 16)


def _unpack_rows(w):
    hi = lax.bitcast_convert_type(w & jnp.int32(-65536), f32)
    lo = lax.bitcast_convert_type(lax.shift_left(w, 16), f32)
    return hi, lo


def _outproj_router_kernel(yl_ref, yg_ref, x_ref, mod_ref, wo_ref, lg_ref, lb_ref, wrh_ref, wrl_ref, br_ref,
                           x1_ref, hp_ref, ids_ref, rank_ref, wts_ref, cnt_ref, running):
    i = pl.program_id(0)

    @pl.when(i == 0)
    def _():
        running[...] = jnp.zeros_like(running)

    d = D_MODEL
    g1 = mod_ref[0, :, 2 * d:3 * d]
    sh2 = mod_ref[0, :, 3 * d:4 * d]
    sc2 = mod_ref[0, :, 4 * d:5 * d]
    y = (jnp.dot(yl_ref[...], wo_ref[0:D_LRU, :], preferred_element_type=f32)
         + jnp.dot(yg_ref[...].astype(bf16), wo_ref[D_LRU:2 * D_LRU, :], preferred_element_type=f32))
    z = DEEPNORM_ALPHA * x_ref[...] + g1 * y
    mu = jnp.mean(z, axis=-1, keepdims=True)
    zc = z - mu
    var = jnp.mean(zc * zc, axis=-1, keepdims=True)
    x1 = zc * lax.rsqrt(var + LN_EPS) * lg_ref[...] + lb_ref[...]
    x1_ref[...] = x1
    hmod = x1 * (1.0 + sc2) + sh2
    hp_ref[...] = _pack_rows(hmod)

    nt_dims = (((1,), (1,)), ((), ()))
    h_hi = hmod.astype(bf16)
    h_lo = (hmod - h_hi.astype(f32)).astype(bf16)
    logits = (lax.dot_general(wrh_ref[...], h_hi, nt_dims, preferred_element_type=f32)
              + lax.dot_general(wrh_ref[...], h_lo, nt_dims, preferred_element_type=f32)
              + lax.dot_general(wrl_ref[...], h_hi, nt_dims, preferred_element_type=f32)) + br_ref[...]
    ne, tm = logits.shape
    expert = lax.broadcasted_iota(jnp.int32, (ne, tm), 0).astype(f32)
    neg_inf = jnp.float32(-jnp.inf)
    live = logits
    sel = jnp.zeros((ne, tm), f32)
    ids, vals = [], []
    for _ in range(TOP_K):
        m = jnp.max(live, axis=0, keepdims=True)
        j = jnp.min(jnp.where(live == m, expert, float(ne)), axis=0, keepdims=True)
        pick = expert == j
        sel = jnp.where(pick, 1.0, sel)
        live = jnp.where(pick, neg_inf, live)
        ids.append(j)
        vals.append(m)
    exps = [jnp.exp(vk - vals[0]) for vk in vals]
    denom = exps[0] + exps[1] + exps[2] + exps[3]

    ri = lax.broadcasted_iota(jnp.int32, (tm, tm), 0)
    ci = lax.broadcasted_iota(jnp.int32, (tm, tm), 1)
    earlier = (ri < ci).astype(bf16)
    rank_dense = running[...] + jnp.dot(sel.astype(bf16), earlier, preferred_element_type=f32)
    running[...] = running[...] + jnp.sum(sel, axis=1, keepdims=True)
    cnt_ref[...] = running[...]

    slot = lax.broadcasted_iota(jnp.int32, (8, tm), 0)
    ids_out = jnp.zeros((8, tm), f32)
    rank_out = jnp.zeros((8, tm), f32)
    wts_out = jnp.zeros((8, tm), f32)
    for kk in range(TOP_K):
        rk = jnp.sum(jnp.where(expert == ids[kk], rank_dense, 0.0), axis=0, keepdims=True)
        here = slot == kk
        ids_out = jnp.where(here, ids[kk], ids_out)
        rank_out = jnp.where(here, rk, rank_out)
        wts_out = jnp.where(here, exps[kk] / denom, wts_out)
    ids_ref[...] = ids_out.astype(jnp.int32)
    rank_ref[...] = rank_out.astype(jnp.int32)
    wts_ref[...] = jnp.concatenate([wts_out, jnp.zeros((LANES - 8, tm), f32)], axis=0).T


def _outproj_router(yl, yg, xt, mod3, wo, lg, lb, wr_hi, wr_lo, br, tokens_per_batch):
    T = xt.shape[0]
    tm = TOKEN_TILE
    tiles_per_batch = tokens_per_batch // tm
    tokrow = lambda i: (i, 0)
    const = lambda i: (0, 0)
    return pl.pallas_call(
        _outproj_router_kernel,
        grid=(T // tm,),
        in_specs=[pl.BlockSpec((tm, D_LRU), tokrow),
                  pl.BlockSpec((tm, D_GLA_V), tokrow),
                  pl.BlockSpec((tm, D_MODEL), tokrow),
                  pl.BlockSpec((1, 1, N_MOD * D_MODEL), lambda i: (i // tiles_per_batch, 0, 0)),
                  pl.BlockSpec((D_MODEL, D_MODEL), const),
                  pl.BlockSpec((1, D_MODEL), const),
                  pl.BlockSpec((1, D_MODEL), const),
                  pl.BlockSpec((N_EXPERTS, D_MODEL), const),
                  pl.BlockSpec((N_EXPERTS, D_MODEL), const),
                  pl.BlockSpec((N_EXPERTS, 1), const)],
        out_specs=[pl.BlockSpec((tm, D_MODEL), tokrow),
                   pl.BlockSpec((tm, D_MODEL // 2), tokrow),
                   pl.BlockSpec((8, tm), lambda i: (0, i)),
                   pl.BlockSpec((8, tm), lambda i: (0, i)),
                   pl.BlockSpec((tm, LANES), tokrow),
                   pl.BlockSpec((N_EXPERTS, 1), const)],
        out_shape=[jax.ShapeDtypeStruct((T, D_MODEL), f32),
                   jax.ShapeDtypeStruct((T, D_MODEL // 2), jnp.int32),
                   jax.ShapeDtypeStruct((8, T), jnp.int32),
                   jax.ShapeDtypeStruct((8, T), jnp.int32),
                   jax.ShapeDtypeStruct((T, LANES), f32),
                   jax.ShapeDtypeStruct((N_EXPERTS, 1), f32)],
        scratch_shapes=[pltpu.VMEM((N_EXPERTS, 1), f32)],
        compiler_params=_cparams(("arbitrary",)),
        name="outproj_router",
    )(yl, yg, xt, mod3, wo, lg, lb, wr_hi, wr_lo, br)


def _route_kernel(ids_ref, rank_ref, cnt_ref, dest_ref, tiles_ref, *, bm):
    ne = cnt_ref.shape[0]
    cnt = cnt_ref[...]
    padded = jnp.floor((cnt + (bm - 1.0)) * (1.0 / bm)) * bm
    sub = lax.broadcasted_iota(jnp.int32, (ne, LANES), 0)
    lane = lax.broadcasted_iota(jnp.int32, (ne, LANES), 1)
    padded_row = jnp.sum(jnp.where(sub == lane, padded, 0.0), axis=0, keepdims=True)
    base = jnp.sum(jnp.where(lane < sub, padded_row, 0.0), axis=1, keepdims=True)
    ends = base + padded

    tc = ids_ref.shape[1]
    expert = lax.broadcasted_iota(jnp.int32, (ne, tc), 0)
    slot = lax.broadcasted_iota(jnp.int32, (8, tc), 0)
    ids = ids_ref[...]
    group_base = jnp.zeros((8, tc), f32)
    for kk in range(TOP_K):
        b_k = jnp.sum(jnp.where(expert == ids[kk:kk + 1], base, 0.0), axis=0, keepdims=True)
        group_base = jnp.where(slot == kk, b_k, group_base)
    dest_ref[...] = group_base.astype(jnp.int32) + rank_ref[...]

    nl = tiles_ref.shape[1]
    start = lax.broadcasted_iota(jnp.int32, (ne, nl), 1).astype(f32) * bm
    te = jnp.minimum(jnp.sum(jnp.where(start >= ends, 1.0, 0.0), axis=0, keepdims=True), ne - 1.0)
    at_te = lax.broadcasted_iota(jnp.int32, (ne, nl), 0).astype(f32) == te
    cnt_te = jnp.sum(jnp.where(at_te, cnt, 0.0), axis=0, keepdims=True)
    base_te = jnp.sum(jnp.where(at_te, base, 0.0), axis=0, keepdims=True)
    valid = jnp.clip(cnt_te - (start[0:1] - base_te), 0.0, float(bm))
    next_group = jnp.sum(jnp.where(at_te, ends, 0.0), axis=0, keepdims=True) * (1.0 / bm)
    srow = lax.broadcasted_iota(jnp.int32, (8, nl), 0)
    table = jnp.where(srow == 0, te, jnp.where(srow == 1, valid, jnp.where(srow == 2, next_group, 0.0)))
    tiles_ref[...] = table.astype(jnp.int32)


def _route(ids, rank, cnt, bm, n_tiles):
    T = ids.shape[1]
    tc = 2048
    nl = -(-n_tiles // LANES) * LANES
    tok = lambda i: (0, i)
    const = lambda i: (0, 0)
    return pl.pallas_call(
        functools.partial(_route_kernel, bm=bm),
        grid=(T // tc,),
        in_specs=[pl.BlockSpec((8, tc), tok), pl.BlockSpec((8, tc), tok),
                  pl.BlockSpec((N_EXPERTS, 1), const)],
        out_specs=[pl.BlockSpec((8, tc), tok), pl.BlockSpec((8, nl), const)],
        out_shape=[jax.ShapeDtypeStruct((8, T), jnp.int32), jax.ShapeDtypeStruct((8, nl), jnp.int32)],
        compiler_params=_cparams(("arbitrary",)),
        name="route",
    )(ids, rank, cnt)


def _sc_workers():
    info = plsc.get_sparse_core_info()
    return info.num_cores, info.num_subcores


def _sc_dispatch(rows, dest_flat, n_out):
    T, D = rows.shape
    nc, ns = _sc_workers()
    per_w = T // (nc * ns)
    n_chunks = per_w // SC_CHUNK
    mesh = plsc.VectorSubcoreMesh(core_axis_name="c", subcore_axis_name="s")

    @functools.partial(
        pl.kernel, mesh=mesh,
        out_type=jax.ShapeDtypeStruct((n_out, D), rows.dtype),
        scratch_types=([pltpu.VMEM((SC_CHUNK,), jnp.int32)] * TOP_K
                       + [pltpu.VMEM((SC_CHUNK, D), rows.dtype)]
                       + [pltpu.SemaphoreType.DMA] * TOP_K),
    )
    def k(rows_hbm, dest_hbm, out_hbm, *scratch):
        idx_v = scratch[:TOP_K]
        rows_v = scratch[TOP_K]
        sems = scratch[TOP_K + 1:]
        wid = lax.axis_index("s") * nc + lax.axis_index("c")
        base = wid * per_w

        @pl.loop(0, n_chunks)
        def _(j):
            off = base + j * SC_CHUNK
            pltpu.sync_copy(rows_hbm.at[pl.ds(off, SC_CHUNK)], rows_v)
            for kk in range(TOP_K):
                pltpu.sync_copy(dest_hbm.at[pl.ds(kk * T + off, SC_CHUNK)], idx_v[kk])
            copies = [pltpu.async_copy(rows_v, out_hbm.at[idx_v[kk]], sems[kk]) for kk in range(TOP_K)]
            for cp in copies:
                cp.wait()

    return k(rows, dest_flat)


def _sc_gather(table, idx):
    _, D = table.shape
    N = idx.shape[0]
    nc, ns = _sc_workers()
    per_w = N // (nc * ns)
    n_chunks = per_w // SC_CHUNK
    assert n_chunks % 2 == 0
    mesh = plsc.VectorSubcoreMesh(core_axis_name="c", subcore_axis_name="s")

    @functools.partial(
        pl.kernel, mesh=mesh,
        out_type=jax.ShapeDtypeStruct((N, D), table.dtype),
        scratch_types=([pltpu.VMEM((SC_CHUNK,), jnp.int32)] * 2
                       + [pltpu.VMEM((SC_CHUNK, D), table.dtype)] * 2
                       + [pltpu.SemaphoreType.DMA] * 2),
    )
    def k(table_hbm, idx_hbm, out_hbm, idx0, idx1, buf0, buf1, sem0, sem1):
        idxs, bufs, sems = (idx0, idx1), (buf0, buf1), (sem0, sem1)
        wid = lax.axis_index("s") * nc + lax.axis_index("c")
        base = wid * per_w

        def gather(b):
            return pltpu.make_async_copy(table_hbm.at[idxs[b]], bufs[b], sems[b])

        def start(chunk, b):
            pltpu.sync_copy(idx_hbm.at[pl.ds(base + chunk * SC_CHUNK, SC_CHUNK)], idxs[b])
            gather(b).start()

        def finish(chunk, b):
            gather(b).wait()
            pltpu.sync_copy(bufs[b], out_hbm.at[pl.ds(base + chunk * SC_CHUNK, SC_CHUNK)])

        start(0, 0)

        @pl.loop(0, n_chunks, step=2)
        def _(j):
            start(j + 1, 1)
            finish(j, 0)

            @pl.when(j + 2 < n_chunks)
            def _():
                start(j + 2, 0)

            finish(j + 1, 1)

    return k(table, idx)


def _ffn_kernel(te_ref, tv_ref, tn_ref, xs_ref, wg_hbm, bg_ref, wu_hbm, bu_ref, wd_hbm, bd_ref, eo_ref,
                w_stage, wg_b, wu_b, wd_b, sems):
    i = pl.program_id(0)
    n_tiles = pl.num_programs(0)
    w_hbm = (wg_hbm, wu_hbm, wd_hbm)
    w_b = (wg_b, wu_b, wd_b)

    def fetch(e):
        return [pltpu.make_async_copy(w_hbm[m].at[e], w_stage.at[m], sems.at[m]) for m in range(3)]

    @pl.when(i == 0)
    def _():
        for cp in fetch(te_ref[0]):
            cp.start()

    prev = te_ref[jnp.maximum(i - 1, 0)]

    @pl.when((i == 0) | (te_ref[i] != prev))
    def _():
        for m, cp in enumerate(fetch(te_ref[i])):
            cp.wait()
            w_b[m][...] = w_stage[m].astype(bf16)
        nxt = tn_ref[i]
        e_nxt = te_ref[jnp.minimum(nxt, n_tiles - 1)]

        @pl.when((nxt > i) & (nxt < n_tiles) & (e_nxt != te_ref[i]))
        def _():
            for cp in fetch(e_nxt):
                cp.start()

    valid = tv_ref[i]

    def ffn_rows(r0, m):
        row = lax.broadcasted_iota(jnp.int32, (m, 1), 0) + r0
        xw = jnp.where(row < valid, xs_ref[r0:r0 + m, :], 0)
        x_hi, x_lo = _unpack_rows(xw)
        x = jnp.concatenate([x_hi, x_lo], axis=1).astype(bf16)
        gate = jnp.minimum(jnp.dot(x, wg_b[...], preferred_element_type=f32) + bg_ref[...], SWIGLU_LIMIT)
        up = jnp.clip(jnp.dot(x, wu_b[...], preferred_element_type=f32) + bu_ref[...],
                      -SWIGLU_LIMIT, SWIGLU_LIMIT)
        act = (up + 1.0) * gate * jax.nn.sigmoid(SWIGLU_ALPHA * gate)
        out = jnp.dot(act.astype(bf16), wd_b[...], preferred_element_type=f32) + bd_ref[...]
        eo_ref[r0:r0 + m, :] = _pack_rows(out)

    def zero_rows(r0, m):
        eo_ref[r0:r0 + m, :] = jnp.zeros((m, eo_ref.shape[1]), eo_ref.dtype)

    for r0 in range(0, xs_ref.shape[0], MOE_PASS):
        @pl.when(valid > r0 + MOE_SUBTILE)
        def _(r0=r0):
            ffn_rows(r0, MOE_PASS)

        @pl.when((valid > r0) & (valid <= r0 + MOE_SUBTILE))
        def _(r0=r0):
            ffn_rows(r0, MOE_SUBTILE)
            zero_rows(r0 + MOE_SUBTILE, MOE_PASS - MOE_SUBTILE)

        @pl.when(valid <= r0)
        def _(r0=r0):
            zero_rows(r0, MOE_PASS)


def _expert_ffn(tile_expert, tile_valid, tile_next, xs, w_gate, b_gate, w_up, b_up, w_down, b_down):
    n_rows, dp = xs.shape
    d = 2 * dp
    bm = MOE_TILE
    d_e = w_gate.shape[-1]
    assert d == d_e
    bspec = lambda n_: pl.BlockSpec((None, 1, n_), lambda i, te, tv, tn: (te[i], 0, 0))
    hbm = pl.BlockSpec(memory_space=pl.ANY)
    grid_spec = pltpu.PrefetchScalarGridSpec(
        num_scalar_prefetch=3,
        grid=(n_rows // bm,),
        in_specs=[pl.BlockSpec((bm, dp), lambda i, te, tv, tn: (i, 0)),
                  hbm, bspec(d_e), hbm, bspec(d_e), hbm, bspec(d)],
        out_specs=pl.BlockSpec((bm, dp), lambda i, te, tv, tn: (i, 0)),
        scratch_shapes=[pltpu.VMEM((3, d, d_e), f32),
                        pltpu.VMEM((d, d_e), bf16), pltpu.VMEM((d, d_e), bf16), pltpu.VMEM((d_e, d), bf16),
                        pltpu.SemaphoreType.DMA((3,))],
    )
    return pl.pallas_call(
        _ffn_kernel,
        grid_spec=grid_spec,
        out_shape=jax.ShapeDtypeStruct((n_rows, dp), jnp.int32),
        compiler_params=_cparams(("arbitrary",)),
        name="expert_ffn",
    )(tile_expert, tile_valid, tile_next, xs, w_gate, b_gate.reshape(N_EXPERTS, 1, d_e), w_up,
      b_up.reshape(N_EXPERTS, 1, d_e), w_down, b_down.reshape(N_EXPERTS, 1, d))


def _combine_kernel(eg_ref, wts_ref, x1_ref, mod_ref, lg_ref, lb_ref, *rest):
    o_ref = rest[-1]
    w = wts_ref[...]
    y_hi, y_lo = _unpack_rows(eg_ref[0])
    y_hi, y_lo = y_hi * w[:, 0:1], y_lo * w[:, 0:1]
    for kk in range(1, TOP_K):
        e_hi, e_lo = _unpack_rows(eg_ref[kk])
        y_hi = y_hi + e_hi * w[:, kk:kk + 1]
        y_lo = y_lo + e_lo * w[:, kk:kk + 1]
    y = jnp.concatenate([y_hi, y_lo], axis=1)
    z = DEEPNORM_ALPHA * x1_ref[...] + mod_ref[0] * y
    mu = jnp.mean(z, axis=-1, keepdims=True)
    zc = z - mu
    var = jnp.mean(zc * zc, axis=-1, keepdims=True)
    o_ref[...] = zc * lax.rsqrt(var + LN_EPS) * lg_ref[...] + lb_ref[...]


def _combine_ln(eg, wts, x1, mod3, lg, lb, tokens_per_batch, first_token, out_so_far=None):
    T = x1.shape[0]
    tm = TOKEN_TILE
    tiles_per_batch = tokens_per_batch // tm
    t0 = first_token // tm
    const = lambda i: (0, 0)
    in_specs = [pl.BlockSpec((TOP_K, tm, D_MODEL // 2), lambda i: (0, i, 0)),
                pl.BlockSpec((tm, LANES), lambda i: (i + t0, 0)),
                pl.BlockSpec((tm, D_MODEL), lambda i: (i + t0, 0)),
                pl.BlockSpec((1, 1, D_MODEL), lambda i: ((i + t0) // tiles_per_batch, 0, N_MOD - 1)),
                pl.BlockSpec((1, D_MODEL), const),
                pl.BlockSpec((1, D_MODEL), const)]
    args = [eg, wts, x1, mod3, lg, lb]
    aliases = {}
    if out_so_far is not None:
        in_specs.append(pl.BlockSpec(memory_space=pl.ANY))
        args.append(out_so_far)
        aliases = {len(args) - 1: 0}
    return pl.pallas_call(
        _combine_kernel,
        grid=(eg.shape[1] // tm,),
        in_specs=in_specs,
        out_specs=pl.BlockSpec((tm, D_MODEL), lambda i: (i + t0, 0)),
        out_shape=jax.ShapeDtypeStruct((T, D_MODEL), f32),
        input_output_aliases=aliases,
        compiler_params=_cparams(("arbitrary",)),
        name="combine_ln",
    )(*args)


def _block_diag(w):
    n, c, d = w.shape
    eye = jnp.eye(n, dtype=w.dtype)
    return jnp.einsum('ncd,nm->ncmd', w, eye).reshape(n * c, n * d)


def kernel(x, c, ctx, c_ctx, w_ada, b_ada, w_in, conv_w, conv_b, lru_wa, lru_ba, lru_wx, lru_bx,
           lru_lam, gla_wa, gla_ba, gla_norm_g, w_out, ln1_g, ln1_b, w_router, b_router, w_gate,
           b_gate, w_up, b_up, w_down, b_down, ln2_g, ln2_b):
    B, L, D = x.shape
    Lc = ctx.shape[1]
    T = B * L
    rows = L // GRID_W
    l = 0

    cpad = jnp.zeros((8, D), f32).at[0:B].set(c).at[B].set(c_ctx)
    w_cat = jnp.pad(w_in[l], ((0, 0), (0, D_PROJ - w_in.shape[-1]))).astype(bf16)
    wg = [jnp.concatenate([_block_diag(lru_wa[l, d]), _block_diag(lru_wx[l, d])], axis=1).astype(bf16)
          for d in range(2)]
    bg = [jnp.concatenate([lru_ba[l, d], lru_bx[l, d]])[None] for d in range(2)]
    lam = [lru_lam[l, d][None] for d in range(2)]
    wa = [jnp.pad(gla_wa[l, d], ((0, A_LOW_PAD - GATE_RANK), (0, 0))).astype(bf16) for d in range(2)]
    ba = [gla_ba[l, d][None] for d in range(2)]
    cw, cb = conv_w[l], conv_b[l][None]
    wr_t = w_router[l].T
    wr_hi = wr_t.astype(bf16)
    wr_lo = (wr_t - wr_hi.astype(f32)).astype(bf16)
    br = b_router[l][:, None]
    c_xl, c_gl, c_pg = (0, D_LRU), (D_LRU, 2 * D_LRU), (2 * D_LRU, D_PROJ)

    mod3 = _ada_mod(cpad, w_ada[l], b_ada[l]).reshape(8, 1, N_MOD * D)

    tiles_per_batch = L // TOKEN_TILE
    xl_c, pg_c = _inproj_rows(ctx.reshape(B * Lc, D), mod3, w_cat, lambda i: B, (c_xl, c_pg), (f32, bf16))
    zero_h = jnp.zeros((B, 1, D_LRU), f32)
    zero_s = jnp.zeros((B, GLA_DV, D_GLA_K), f32)
    pg_c = pg_c.reshape(B, Lc // GLA_CHUNK, GLA_CHUNK, D_PG)
    h_ctx, s_ctx = [], []
    for d in range(2):
        _, hf = _lru_pass(xl_c, cw, cb, wg[d], bg[d], lam[d], zero_h, n_batch=B, row_w=Lc, tile=Lc,
                          reverse=bool(d))
        _, sf = _gla_pass(pg_c, wa[d], ba[d], zero_s, reverse=bool(d))
        h_ctx.append(hf)
        s_ctx.append(sf)

    xt = x.reshape(T, D)
    xl, gl = _inproj_rows(xt, mod3, w_cat, lambda i: i // tiles_per_batch, (c_xl, c_gl), (f32, bf16))
    hb, _ = _lru_pass(xl, cw, cb, wg[1], bg[1], lam[1], h_ctx[1], n_batch=B, row_w=GRID_W, tile=LRU_TILE,
                      reverse=True)
    y_lru, _ = _lru_pass(xl, cw, cb, wg[0], bg[0], lam[0], h_ctx[0], n_batch=B, row_w=GRID_W,
                         tile=LRU_TILE, reverse=False, hb=hb, gl=gl)
    pg = _inproj_cols(x.reshape(B, rows, GRID_W, D), mod3, w_cat, c_pg)
    ob, _ = _gla_pass(pg, wa[1], ba[1], s_ctx[1], reverse=True)
    y_gla, _ = _gla_pass(pg, wa[0], ba[0], s_ctx[0], reverse=False, ob=ob, gn=gla_norm_g[l][None])
    y_gla = y_gla.reshape(T, D_GLA_V)

    x1, hp, ids, rank, wts, cnt = _outproj_router(
        y_lru, y_gla, xt, mod3, w_out[l].astype(bf16), ln1_g[l][None], ln1_b[l][None], wr_hi, wr_lo, br, L)

    n_tiles = T * TOP_K // MOE_TILE + N_EXPERTS
    dest, tiles = _route(ids, rank, cnt, MOE_TILE, n_tiles)
    xs = _sc_dispatch(hp, dest.reshape(-1), n_tiles * MOE_TILE)
    eo = _expert_ffn(tiles[0, :n_tiles], tiles[1, :n_tiles], tiles[2, :n_tiles], xs, w_gate[l], b_gate[l],
                     w_up[l], b_up[l], w_down[l], b_down[l])
    out = None
    n_parts = 2
    tp = T // n_parts
    for part in range(n_parts):
        idx = dest[:TOP_K, part * tp:(part + 1) * tp].reshape(-1)
        eg = _sc_gather(eo, idx).reshape(TOP_K, tp, D // 2)
        out = _combine_ln(eg, wts, x1, mod3, ln2_g[l][None], ln2_b[l][None], L, part * tp, out)
    return out.reshape(B, L, D)
```

```python
import functools

import jax
import jax.numpy as jnp
from jax import lax
from jax.experimental import pallas as pl
from jax.experimental.pallas import tpu as pltpu
from jax.experimental.pallas import tpu_sc as plsc

D_MODEL = 1024
DEPTH = 1
GRID_W = 64
D_LRU = 512
LRU_BLOCKS = 8
CONV_W = 4
LRU_C = 8.0
GLA_HEADS = 4
D_GLA_V = 512
D_GLA_K = 256
GLA_DK = 64
GLA_DV = 128
GATE_RANK = 16
GATE_TAU = 16.0
GLA_CHUNK = 64
N_EXPERTS = 32
TOP_K = 4
SWIGLU_LIMIT = 7.0
SWIGLU_ALPHA = 1.702
N_MOD = 6
DEEPNORM_ALPHA = (2.0 * DEPTH) ** 0.25
LN_EPS = 1e-5
RMS_EPS = 1e-6

LANES = 128
A_LOW_PAD = LANES
D_PG = 2 * D_GLA_K + 2 * D_GLA_V + A_LOW_PAD
D_PROJ = 2 * D_LRU + D_PG
TOKEN_TILE = 512
LRU_TILE = 512
MOE_TILE = 1024
MOE_PASS = 512
MOE_PASS_SIZES = (128, 256, MOE_PASS)
SC_CHUNK = 64
VMEM_LIMIT = 48 * 1024 * 1024

f32 = jnp.float32
bf16 = jnp.bfloat16


def _cparams(sem):
    return pltpu.CompilerParams(dimension_semantics=sem, vmem_limit_bytes=VMEM_LIMIT)


def _ada_kernel(c_ref, w_ref, b_ref, o_ref):
    s = c_ref[...]
    s = s * jax.nn.sigmoid(s)
    o_ref[...] = jnp.dot(s.astype(bf16), w_ref[...].astype(bf16), preferred_element_type=f32) + b_ref[...]


def _ada_mod(cpad, w, b):
    n = w.shape[1]
    tn = 1024
    return pl.pallas_call(
        _ada_kernel,
        grid=(n // tn,),
        in_specs=[pl.BlockSpec((8, D_MODEL), lambda j: (0, 0)),
                  pl.BlockSpec((D_MODEL, tn), lambda j: (0, j)),
                  pl.BlockSpec((1, tn), lambda j: (0, j))],
        out_specs=pl.BlockSpec((8, tn), lambda j: (0, j)),
        out_shape=jax.ShapeDtypeStruct((8, n), f32),
        compiler_params=_cparams(("arbitrary",)),
        name="ada_mod",
    )(cpad, w, b.reshape(1, n))


def _inproj_kernel(x_ref, mod_ref, w_ref, *out_refs, parts, col_major):
    sh = mod_ref[0, :, 0:D_MODEL]
    sc = mod_ref[0, :, D_MODEL:2 * D_MODEL]
    if col_major:
        x = jnp.concatenate([x_ref[0, :, j, :] for j in range(x_ref.shape[2])], axis=0)
    else:
        x = x_ref[...]
    u = (x * (1.0 + sc) + sh).astype(bf16)
    for (lo, hi), o_ref in zip(parts, out_refs):
        p = jnp.dot(u, w_ref[:, lo:hi], preferred_element_type=f32).astype(o_ref.dtype)
        o_ref[...] = p.reshape(o_ref.shape)


def _inproj_rows(xt, mod3, w_cat, batch_of_tile, parts, dtypes):
    T = xt.shape[0]
    tm = TOKEN_TILE
    return pl.pallas_call(
        functools.partial(_inproj_kernel, parts=parts, col_major=False),
        grid=(T // tm,),
        in_specs=[pl.BlockSpec((tm, D_MODEL), lambda i: (i, 0)),
                  pl.BlockSpec((1, 1, 2 * D_MODEL), lambda i: (batch_of_tile(i), 0, 0)),
                  pl.BlockSpec((D_MODEL, D_PROJ), lambda i: (0, 0))],
        out_specs=[pl.BlockSpec((tm, hi - lo), lambda i: (i, 0)) for lo, hi in parts],
        out_shape=[jax.ShapeDtypeStruct((T, hi - lo), dt) for (lo, hi), dt in zip(parts, dtypes)],
        compiler_params=_cparams(("arbitrary",)),
        name="inproj_rows",
    )(xt, mod3, w_cat)


def _inproj_cols(x4, mod3, w_cat, part):
    n_batch, rows, cols, _ = x4.shape
    lo, hi = part
    cb = TOKEN_TILE // rows
    return pl.pallas_call(
        functools.partial(_inproj_kernel, parts=(part,), col_major=True),
        grid=(n_batch, cols // cb),
        in_specs=[pl.BlockSpec((1, rows, cb, D_MODEL), lambda b, n: (b, 0, n, 0)),
                  pl.BlockSpec((1, 1, 2 * D_MODEL), lambda b, n: (b, 0, 0)),
                  pl.BlockSpec((D_MODEL, D_PROJ), lambda b, n: (0, 0))],
        out_specs=[pl.BlockSpec((1, cb, rows, hi - lo), lambda b, n: (b, n, 0, 0))],
        out_shape=[jax.ShapeDtypeStruct((n_batch, cols, rows, hi - lo), bf16)],
        compiler_params=_cparams(("arbitrary", "arbitrary")),
        name="inproj_cols",
    )(x4, mod3, w_cat)[0]


def _gelu_tanh(x):
    return 0.5 * x * (1.0 + jnp.tanh(0.7978845608028654 * (x + 0.044715 * (x * x * x))))


def _lru_kernel(*refs, row_w, reverse, merge):
    if merge:
        (xl_ref, cw_ref, cb_ref, wg_ref, bg_ref, lam_ref, h0_ref, hb_ref, gl_ref,
         out_ref, hfin_ref, carry, h_nat) = refs
    else:
        xl_ref, cw_ref, cb_ref, wg_ref, bg_ref, lam_ref, h0_ref, out_ref, hfin_ref, carry = refs
    t = pl.program_id(1)

    @pl.when(t == 0)
    def _():
        carry[...] = h0_ref[0]

    n_seg, seg, ch = xl_ref.shape
    blk = lambda v, j: v[j * n_seg:(j + 1) * n_seg]
    x = jnp.concatenate([xl_ref[:, j, :] for j in range(seg)], axis=0)
    tt = n_seg * seg

    segs_per_row = row_w // seg
    s_idx = lax.broadcasted_iota(jnp.int32, (n_seg, ch), 0)
    has_prev = (s_idx % segs_per_row) != 0
    has_next = (s_idx % segs_per_row) != segs_per_row - 1
    from_prev = lambda v: jnp.where(has_prev, pltpu.roll(v, 1, 0), 0.0)
    from_next = lambda v: jnp.where(has_next, pltpu.roll(v, n_seg - 1, 0), 0.0)
    x_m1 = jnp.concatenate([from_prev(blk(x, seg - 1)), x[:tt - n_seg]], axis=0)
    x_m2 = jnp.concatenate([from_prev(blk(x, seg - 2)), from_prev(blk(x, seg - 1)), x[:tt - 2 * n_seg]], axis=0)
    x_p1 = jnp.concatenate([x[n_seg:], from_next(blk(x, 0))], axis=0)
    cw = cw_ref[...]
    u = cb_ref[...] + x_m2 * cw[0:1] + x_m1 * cw[1:2] + x * cw[2:3] + x_p1 * cw[3:4]
    g = jnp.dot(u.astype(bf16), wg_ref[...], preferred_element_type=f32) + bg_ref[...]
    r_gate = jax.nn.sigmoid(g[:, 0:ch])
    i_gate = jax.nn.sigmoid(g[:, ch:2 * ch])
    lam = lam_ref[...]
    softplus_neg = jnp.maximum(-lam, 0.0) + jnp.log(1.0 + jnp.exp(-jnp.abs(lam)))
    log_a = (-LRU_C * softplus_neg) * r_gate
    a = jnp.exp(log_a)
    t2 = 2.0 * log_a
    series = -t2 * (1.0 + t2 * (0.5 + t2 * (1.0 / 6.0 + t2 * (1.0 / 24.0))))
    one_minus_a2 = jnp.where(t2 > -0.03, series, 1.0 - a * a)
    b = jnp.sqrt(one_minus_a2) * (i_gate * u)

    half = seg // 2
    h_loc, a_cum = [None] * seg, [None] * seg
    h_run, a_run = [None, None], [None, None]
    for i in range(half):
        for p in range(2):
            j = p * half + (half - 1 - i if reverse else i)
            a_j, b_j = blk(a, j), blk(b, j)
            h_run[p] = b_j if i == 0 else a_j * h_run[p] + b_j
            a_run[p] = a_j if i == 0 else a_j * a_run[p]
            h_loc[j], a_cum[j] = h_run[p], a_run[p]
    pieces = [(s, p) for s in range(n_seg) for p in range(2)]
    if reverse:
        pieces.reverse()
    entering = [[None] * n_seg, [None] * n_seg]
    state = carry[...]
    for s, p in pieces:
        entering[p][s] = state
        state = h_run[p][s:s + 1] + a_run[p][s:s + 1] * state
    carry[...] = state
    hfin_ref[0] = state
    enter = [jnp.concatenate(e, axis=0) for e in entering]
    h_rows = [h_loc[j] + a_cum[j] * enter[j // half] for j in range(seg)]
    if merge:
        for j in range(seg):
            h_nat[:, j, :] = h_rows[j] + blk(hb_ref, j)
        h = h_nat[...].reshape(tt, ch)
        out_ref[...] = (h * _gelu_tanh(gl_ref[...].astype(f32))).astype(out_ref.dtype)
    else:
        out_ref[...] = jnp.concatenate(h_rows, axis=0)


def _lru_pass(xl, conv_w, conv_b, wg, bg, lam, h0, *, n_batch, row_w, tile, reverse, hb=None, gl=None):
    T = xl.shape[0]
    nt = T // n_batch // tile
    merge = hb is not None
    n_seg = 8
    seg = tile // n_seg
    assert row_w % seg == 0 and seg >= 2

    def tok(b, t):
        return (b * nt + (nt - 1 - t if reverse else t), 0)

    seg_spec = pl.BlockSpec((n_seg, seg, D_LRU), lambda b, t: tok(b, t) + (0,))
    const = lambda b, t: (0, 0)
    in_specs = [seg_spec,
                pl.BlockSpec((CONV_W, D_LRU), const),
                pl.BlockSpec((1, D_LRU), const),
                pl.BlockSpec((D_LRU, 2 * D_LRU), const),
                pl.BlockSpec((1, 2 * D_LRU), const),
                pl.BlockSpec((1, D_LRU), const),
                pl.BlockSpec((1, 1, D_LRU), lambda b, t: (b, 0, 0))]
    args = [xl.reshape(T // seg, seg, D_LRU), conv_w, conv_b, wg, bg, lam, h0]
    scratch = [pltpu.VMEM((1, D_LRU), f32)]
    if merge:
        in_specs += [pl.BlockSpec((tile, D_LRU), tok), pl.BlockSpec((tile, D_LRU), tok)]
        args += [hb, gl]
        scratch += [pltpu.VMEM((n_seg, seg, D_LRU), f32)]
        out_spec, out_shape = pl.BlockSpec((tile, D_LRU), tok), jax.ShapeDtypeStruct((T, D_LRU), bf16)
    else:
        out_spec, out_shape = pl.BlockSpec((tile, D_LRU), tok), jax.ShapeDtypeStruct((T, D_LRU), f32)
    out, h_fin = pl.pallas_call(
        functools.partial(_lru_kernel, row_w=row_w, reverse=reverse, merge=merge),
        grid=(n_batch, nt),
        in_specs=in_specs,
        out_specs=[out_spec, pl.BlockSpec((1, 1, D_LRU), lambda b, t: (b, 0, 0))],
        out_shape=[out_shape, jax.ShapeDtypeStruct((n_batch, 1, D_LRU), f32)],
        scratch_shapes=scratch,
        compiler_params=_cparams(("arbitrary", "arbitrary")),
        name="lru_merge" if merge else "lru_scan",
    )(*args)
    return out.reshape(T, D_LRU), h_fin


def _gla_kernel(*refs, reverse, merge):
    if merge:
        pg_ref, wa_ref, ba_ref, s0_ref, ob_ref, gn_ref, out_ref, sfin_ref, state = refs
    else:
        pg_ref, wa_ref, ba_ref, s0_ref, out_ref, sfin_ref, state = refs
    n = pl.program_id(0)

    @pl.when(n == 0)
    def _():
        state[...] = s0_ref[...]

    n_batch = pg_ref.shape[0]
    ck = GLA_CHUNK
    nh = GLA_HEADS
    ii = lax.broadcasted_iota(jnp.int32, (nh * ck, ck), 0) & (ck - 1)
    jj = lax.broadcasted_iota(jnp.int32, (nh * ck, ck), 1)
    seen = (jj >= ii) if reverse else (jj <= ii)
    head_of_lane = lax.broadcasted_iota(jnp.int32, (1, D_GLA_K), 1) // GLA_DK
    nt_dims = (((1,), (1,)), ((), ()))
    tn_dims = (((0,), (0,)), ((), ()))
    o_q, o_k, o_v, o_g, o_a = 0, D_GLA_K, 2 * D_GLA_K, 2 * D_GLA_K + D_GLA_V, 2 * D_GLA_K + 2 * D_GLA_V

    rows_all = n_batch * ck
    a_low = jnp.concatenate([pg_ref[bi, :, o_a:o_a + A_LOW_PAD] for bi in range(n_batch)], axis=0)
    z = jnp.dot(a_low, wa_ref[...], preferred_element_type=f32) + ba_ref[...]
    bcum_all = (jnp.minimum(z, 0.0) - jnp.log(1.0 + jnp.exp(-jnp.abs(z)))) * (1.0 / GATE_TAU)
    step = lax.broadcasted_iota(jnp.int32, (rows_all, D_GLA_K), 0) & (ck - 1)
    d = 1
    while d < ck:
        if reverse:
            bcum_all = bcum_all + jnp.where(step < ck - d, pltpu.roll(bcum_all, rows_all - d, 0), 0.0)
        else:
            bcum_all = bcum_all + jnp.where(step >= d, pltpu.roll(bcum_all, d, 0), 0.0)
        d *= 2

    for bi in range(n_batch):
        q = pg_ref[bi, :, o_q:o_k].astype(f32) * (GLA_DK ** -0.5)
        k = pg_ref[bi, :, o_k:o_v].astype(f32)
        v = pg_ref[bi, :, o_v:o_g]
        bcum = bcum_all[bi * ck:(bi + 1) * ck]
        btot = bcum[0:1] if reverse else bcum[ck - 1:ck]
        q_dec = q * jnp.exp(bcum)
        k_dec = (k * jnp.exp(-bcum)).astype(bf16)
        k_end = (k * jnp.exp(btot - bcum)).astype(bf16)
        s_t = state[bi]
        q_blk = jnp.concatenate([jnp.where(head_of_lane == hd, q_dec, 0.0) for hd in range(nh)],
                                axis=0).astype(bf16)
        scores = lax.dot_general(q_blk, k_dec, nt_dims, preferred_element_type=f32)
        scores = jnp.where(seen, scores, 0.0).astype(bf16)
        intra = jnp.dot(scores, v, preferred_element_type=f32)
        inter = lax.dot_general(q_blk, s_t.astype(bf16), nt_dims, preferred_element_type=f32)
        outs = [intra[hd * ck:(hd + 1) * ck, hd * GLA_DV:(hd + 1) * GLA_DV] + inter[hd * ck:(hd + 1) * ck]
                for hd in range(nh)]
        kv_full = lax.dot_general(v, k_end, tn_dims, preferred_element_type=f32)
        kv_t = jnp.zeros_like(s_t)
        for hd in range(nh):
            kv_t = jnp.where(head_of_lane == hd, kv_full[hd * GLA_DV:(hd + 1) * GLA_DV], kv_t)
        state[bi] = s_t * jnp.exp(btot) + kv_t
        if merge:
            g = pg_ref[bi, :, o_g:o_a].astype(f32)
            normed = []
            for hd in range(GLA_HEADS):
                oh = outs[hd] + ob_ref[bi, :, hd * GLA_DV:(hd + 1) * GLA_DV]
                ms = jnp.mean(oh * oh, axis=-1, keepdims=True)
                normed.append(oh * lax.rsqrt(ms + RMS_EPS))
            y = jnp.concatenate(normed, axis=-1) * gn_ref[...] * (g * jax.nn.sigmoid(g))
            out_ref[bi] = y
        else:
            out_ref[bi] = jnp.concatenate(outs, axis=-1)

    @pl.when(n == pl.num_programs(0) - 1)
    def _():
        sfin_ref[...] = state[...]


def _gla_pass(pg, wa, ba, s0, *, reverse, ob=None, gn=None):
    n_batch, n_chunks = pg.shape[0], pg.shape[1]
    merge = ob is not None
    order = (lambda n: n_chunks - 1 - n) if reverse else (lambda n: n)
    const2 = lambda n: (0, 0)
    const3 = lambda n: (0, 0, 0)
    chunk = lambda width: pl.BlockSpec((n_batch, None, GLA_CHUNK, width), lambda n: (0, order(n), 0, 0))
    in_specs = [chunk(D_PG),
                pl.BlockSpec((A_LOW_PAD, D_GLA_K), const2),
                pl.BlockSpec((1, D_GLA_K), const2),
                pl.BlockSpec((n_batch, GLA_DV, D_GLA_K), const3)]
    args = [pg, wa, ba, s0]
    if merge:
        in_specs += [chunk(D_GLA_V), pl.BlockSpec((1, D_GLA_V), const2)]
        args += [ob, gn]
    out_spec = chunk(D_GLA_V)
    out_shape = (n_batch, n_chunks, GLA_CHUNK, D_GLA_V)
    return pl.pallas_call(
        functools.partial(_gla_kernel, reverse=reverse, merge=merge),
        grid=(n_chunks,),
        in_specs=in_specs,
        out_specs=[out_spec,
                   pl.BlockSpec((n_batch, GLA_DV, D_GLA_K), const3)],
        out_shape=[jax.ShapeDtypeStruct(out_shape, f32),
                   jax.ShapeDtypeStruct((n_batch, GLA_DV, D_GLA_K), f32)],
        scratch_shapes=[pltpu.VMEM((n_batch, GLA_DV, D_GLA_K), f32)],
        compiler_params=_cparams(("arbitrary",)),
        name="gla_merge" if merge else "gla_scan",
    )(*args)


def _pack_rows(v):
    n = v.shape[1] // 2
    w = lax.bitcast_convert_type(v.astype(bf16).astype(f32), jnp.int32)
    return (w[:, :n] & jnp.int32(-65536)) | lax.shift_right_logical(w[:, n:], 16)


def _unpack_rows(w):
    hi = lax.bitcast_convert_type(w & jnp.int32(-65536), f32)
    lo = lax.bitcast_convert_type(lax.shift_left(w, 16), f32)
    return hi, lo


def _outproj_router_kernel(yl_ref, yg_ref, x_ref, mod_ref, wo_ref, lg_ref, lb_ref, wrh_ref, wrl_ref, br_ref,
                           x1_ref, hp_ref, ids_ref, rank_ref, wts_ref, cnt_ref, running):
    i = pl.program_id(0)

    @pl.when(i == 0)
    def _():
        running[...] = jnp.zeros_like(running)

    d = D_MODEL
    g1 = mod_ref[0, :, 2 * d:3 * d]
    sh2 = mod_ref[0, :, 3 * d:4 * d]
    sc2 = mod_ref[0, :, 4 * d:5 * d]
    yg = jnp.concatenate([yg_ref[0, :, r, :] for r in range(yg_ref.shape[2])], axis=0).astype(bf16)
    y = (jnp.dot(yl_ref[...], wo_ref[0:D_LRU, :], preferred_element_type=f32)
         + jnp.dot(yg, wo_ref[D_LRU:2 * D_LRU, :], preferred_element_type=f32))
    z = DEEPNORM_ALPHA * x_ref[...] + g1 * y
    mu = jnp.mean(z, axis=-1, keepdims=True)
    zc = z - mu
    var = jnp.mean(zc * zc, axis=-1, keepdims=True)
    x1 = zc * lax.rsqrt(var + LN_EPS) * lg_ref[...] + lb_ref[...]
    x1_ref[...] = x1
    hmod = x1 * (1.0 + sc2) + sh2
    hp_ref[...] = _pack_rows(hmod)

    nt_dims = (((1,), (1,)), ((), ()))
    h_hi = hmod.astype(bf16)
    h_lo = (hmod - h_hi.astype(f32)).astype(bf16)
    logits = (lax.dot_general(wrh_ref[...], h_hi, nt_dims, preferred_element_type=f32)
              + lax.dot_general(wrh_ref[...], h_lo, nt_dims, preferred_element_type=f32)
              + lax.dot_general(wrl_ref[...], h_hi, nt_dims, preferred_element_type=f32)) + br_ref[...]
    ne, tm = logits.shape
    expert = lax.broadcasted_iota(jnp.int32, (ne, tm), 0).astype(f32)
    neg_inf = jnp.float32(-jnp.inf)
    live = logits
    sel = jnp.zeros((ne, tm), f32)
    ids, vals = [], []
    for _ in range(TOP_K):
        m = jnp.max(live, axis=0, keepdims=True)
        j = jnp.min(jnp.where(live == m, expert, float(ne)), axis=0, keepdims=True)
        pick = expert == j
        sel = jnp.where(pick, 1.0, sel)
        live = jnp.where(pick, neg_inf, live)
        ids.append(j)
        vals.append(m)
    exps = [jnp.exp(vk - vals[0]) for vk in vals]
    denom = exps[0] + exps[1] + exps[2] + exps[3]

    ri = lax.broadcasted_iota(jnp.int32, (tm, tm), 0)
    ci = lax.broadcasted_iota(jnp.int32, (tm, tm), 1)
    earlier = (ri < ci).astype(bf16)
    rank_dense = running[...] + jnp.dot(sel.astype(bf16), earlier, preferred_element_type=f32)
    running[...] = running[...] + jnp.sum(sel, axis=1, keepdims=True)
    cnt_ref[...] = running[...]

    slot = lax.broadcasted_iota(jnp.int32, (8, tm), 0)
    ids_out = jnp.zeros((8, tm), f32)
    rank_out = jnp.zeros((8, tm), f32)
    wts_out = jnp.zeros((8, tm), f32)
    for kk in range(TOP_K):
        rk = jnp.sum(jnp.where(expert == ids[kk], rank_dense, 0.0), axis=0, keepdims=True)
        here = slot == kk
        ids_out = jnp.where(here, ids[kk], ids_out)
        rank_out = jnp.where(here, rk, rank_out)
        wts_out = jnp.where(here, exps[kk] / denom, wts_out)
    ids_ref[...] = ids_out.astype(jnp.int32)
    rank_ref[...] = rank_out.astype(jnp.int32)
    wts_ref[...] = jnp.concatenate([wts_out, jnp.zeros((LANES - 8, tm), f32)], axis=0).T


def _outproj_router(yl, yg, xt, mod3, wo, lg, lb, wr_hi, wr_lo, br, tokens_per_batch):
    T = xt.shape[0]
    tm = TOKEN_TILE
    tiles_per_batch = tokens_per_batch // tm
    tokrow = lambda i: (i, 0)
    const = lambda i: (0, 0)
    cols, rows_per_tile = yg.shape[1], tm // yg.shape[1]
    return pl.pallas_call(
        _outproj_router_kernel,
        grid=(T // tm,),
        in_specs=[pl.BlockSpec((tm, D_LRU), tokrow),
                  pl.BlockSpec((1, cols, rows_per_tile, D_GLA_V),
                               lambda i: (i // tiles_per_batch, 0, i % tiles_per_batch, 0)),
                  pl.BlockSpec((tm, D_MODEL), tokrow),
                  pl.BlockSpec((1, 1, N_MOD * D_MODEL), lambda i: (i // tiles_per_batch, 0, 0)),
                  pl.BlockSpec((D_MODEL, D_MODEL), const),
                  pl.BlockSpec((1, D_MODEL), const),
                  pl.BlockSpec((1, D_MODEL), const),
                  pl.BlockSpec((N_EXPERTS, D_MODEL), const),
                  pl.BlockSpec((N_EXPERTS, D_MODEL), const),
                  pl.BlockSpec((N_EXPERTS, 1), const)],
        out_specs=[pl.BlockSpec((tm, D_MODEL), tokrow),
                   pl.BlockSpec((tm, D_MODEL // 2), tokrow),
                   pl.BlockSpec((8, tm), lambda i: (0, i)),
                   pl.BlockSpec((8, tm), lambda i: (0, i)),
                   pl.BlockSpec((tm, LANES), tokrow),
                   pl.BlockSpec((N_EXPERTS, 1), const)],
        out_shape=[jax.ShapeDtypeStruct((T, D_MODEL), f32),
                   jax.ShapeDtypeStruct((T, D_MODEL // 2), jnp.int32),
                   jax.ShapeDtypeStruct((8, T), jnp.int32),
                   jax.ShapeDtypeStruct((8, T), jnp.int32),
                   jax.ShapeDtypeStruct((T, LANES), f32),
                   jax.ShapeDtypeStruct((N_EXPERTS, 1), f32)],
        scratch_shapes=[pltpu.VMEM((N_EXPERTS, 1), f32)],
        compiler_params=_cparams(("arbitrary",)),
        name="outproj_router",
    )(yl, yg, xt, mod3, wo, lg, lb, wr_hi, wr_lo, br)


def _route_kernel(ids_ref, rank_ref, cnt_ref, dest_ref, tiles_ref, *, bm):
    ne = cnt_ref.shape[0]
    cnt = cnt_ref[...]
    padded = jnp.floor((cnt + (bm - 1.0)) * (1.0 / bm)) * bm
    sub = lax.broadcasted_iota(jnp.int32, (ne, LANES), 0)
    lane = lax.broadcasted_iota(jnp.int32, (ne, LANES), 1)
    padded_row = jnp.sum(jnp.where(sub == lane, padded, 0.0), axis=0, keepdims=True)
    base = jnp.sum(jnp.where(lane < sub, padded_row, 0.0), axis=1, keepdims=True)
    ends = base + padded

    tc = ids_ref.shape[1]
    expert = lax.broadcasted_iota(jnp.int32, (ne, tc), 0)
    slot = lax.broadcasted_iota(jnp.int32, (8, tc), 0)
    ids = ids_ref[...]
    group_base = jnp.zeros((8, tc), f32)
    for kk in range(TOP_K):
        b_k = jnp.sum(jnp.where(expert == ids[kk:kk + 1], base, 0.0), axis=0, keepdims=True)
        group_base = jnp.where(slot == kk, b_k, group_base)
    dest_ref[...] = group_base.astype(jnp.int32) + rank_ref[...]

    nl = tiles_ref.shape[1]
    start = lax.broadcasted_iota(jnp.int32, (ne, nl), 1).astype(f32) * bm
    te = jnp.minimum(jnp.sum(jnp.where(start >= ends, 1.0, 0.0), axis=0, keepdims=True), ne - 1.0)
    at_te = lax.broadcasted_iota(jnp.int32, (ne, nl), 0).astype(f32) == te
    cnt_te = jnp.sum(jnp.where(at_te, cnt, 0.0), axis=0, keepdims=True)
    base_te = jnp.sum(jnp.where(at_te, base, 0.0), axis=0, keepdims=True)
    valid = jnp.clip(cnt_te - (start[0:1] - base_te), 0.0, float(bm))
    next_group = jnp.sum(jnp.where(at_te, ends, 0.0), axis=0, keepdims=True) * (1.0 / bm)
    last_used = jnp.sum(padded, axis=0, keepdims=True) * (1.0 / bm) - 1.0
    srow = lax.broadcasted_iota(jnp.int32, (8, nl), 0)
    table = jnp.where(srow == 0, te, jnp.where(srow == 1, valid, jnp.where(srow == 2, next_group,
                                                                           jnp.where(srow == 3, last_used, 0.0))))
    tiles_ref[...] = table.astype(jnp.int32)


def _route(ids, rank, cnt, bm, n_tiles):
    T = ids.shape[1]
    tc = 2048
    nl = -(-n_tiles // LANES) * LANES
    tok = lambda i: (0, i)
    const = lambda i: (0, 0)
    return pl.pallas_call(
        functools.partial(_route_kernel, bm=bm),
        grid=(T // tc,),
        in_specs=[pl.BlockSpec((8, tc), tok), pl.BlockSpec((8, tc), tok),
                  pl.BlockSpec((N_EXPERTS, 1), const)],
        out_specs=[pl.BlockSpec((8, tc), tok), pl.BlockSpec((8, nl), const)],
        out_shape=[jax.ShapeDtypeStruct((8, T), jnp.int32), jax.ShapeDtypeStruct((8, nl), jnp.int32)],
        compiler_params=_cparams(("arbitrary",)),
        name="route",
    )(ids, rank, cnt)


def _sc_workers():
    info = plsc.get_sparse_core_info()
    return info.num_cores, info.num_subcores


def _sc_dispatch(rows, dest_flat, n_out):
    T, D = rows.shape
    nc, ns = _sc_workers()
    per_w = T // (nc * ns)
    n_chunks = per_w // SC_CHUNK
    mesh = plsc.VectorSubcoreMesh(core_axis_name="c", subcore_axis_name="s")

    @functools.partial(
        pl.kernel, mesh=mesh,
        out_type=jax.ShapeDtypeStruct((n_out, D), rows.dtype),
        scratch_types=([pltpu.VMEM((SC_CHUNK,), jnp.int32)] * TOP_K
                       + [pltpu.VMEM((SC_CHUNK, D), rows.dtype)]
                       + [pltpu.SemaphoreType.DMA] * TOP_K),
    )
    def k(rows_hbm, dest_hbm, out_hbm, *scratch):
        idx_v = scratch[:TOP_K]
        rows_v = scratch[TOP_K]
        sems = scratch[TOP_K + 1:]
        wid = lax.axis_index("s") * nc + lax.axis_index("c")
        base = wid * per_w

        @pl.loop(0, n_chunks)
        def _(j):
            off = base + j * SC_CHUNK
            pltpu.sync_copy(rows_hbm.at[pl.ds(off, SC_CHUNK)], rows_v)
            for kk in range(TOP_K):
                pltpu.sync_copy(dest_hbm.at[pl.ds(kk * T + off, SC_CHUNK)], idx_v[kk])
            copies = [pltpu.async_copy(rows_v, out_hbm.at[idx_v[kk]], sems[kk]) for kk in range(TOP_K)]
            for cp in copies:
                cp.wait()

    return k(rows, dest_flat)


def _sc_gather(table, idx):
    _, D = table.shape
    N = idx.shape[0]
    nc, ns = _sc_workers()
    per_w = N // (nc * ns)
    n_chunks = per_w // SC_CHUNK
    assert n_chunks % 2 == 0
    mesh = plsc.VectorSubcoreMesh(core_axis_name="c", subcore_axis_name="s")

    @functools.partial(
        pl.kernel, mesh=mesh,
        out_type=jax.ShapeDtypeStruct((N, D), table.dtype),
        scratch_types=([pltpu.VMEM((SC_CHUNK,), jnp.int32)] * 2
                       + [pltpu.VMEM((SC_CHUNK, D), table.dtype)] * 2
                       + [pltpu.SemaphoreType.DMA] * 2),
    )
    def k(table_hbm, idx_hbm, out_hbm, idx0, idx1, buf0, buf1, sem0, sem1):
        idxs, bufs, sems = (idx0, idx1), (buf0, buf1), (sem0, sem1)
        wid = lax.axis_index("s") * nc + lax.axis_index("c")
        base = wid * per_w

        def gather(b):
            return pltpu.make_async_copy(table_hbm.at[idxs[b]], bufs[b], sems[b])

        def start(chunk, b):
            pltpu.sync_copy(idx_hbm.at[pl.ds(base + chunk * SC_CHUNK, SC_CHUNK)], idxs[b])
            gather(b).start()

        def finish(chunk, b):
            gather(b).wait()
            pltpu.sync_copy(bufs[b], out_hbm.at[pl.ds(base + chunk * SC_CHUNK, SC_CHUNK)])

        start(0, 0)

        @pl.loop(0, n_chunks, step=2)
        def _(j):
            start(j + 1, 1)
            finish(j, 0)

            @pl.when(j + 2 < n_chunks)
            def _():
                start(j + 2, 0)

            finish(j + 1, 1)

    return k(table, idx)


def _ffn_kernel(te_ref, tv_ref, tn_ref, tl_ref, xs_ref, wg_hbm, bg_ref, wu_hbm, bu_ref, wd_hbm, bd_ref, eo_ref,
                w_stage, wg_b, wu_b, wd_b, sems):
    i = pl.program_id(0)
    n_tiles = pl.num_programs(0)
    w_hbm = (wg_hbm, wu_hbm, wd_hbm)
    w_b = (wg_b, wu_b, wd_b)

    def fetch(e):
        return [pltpu.make_async_copy(w_hbm[m].at[e], w_stage.at[m], sems.at[m]) for m in range(3)]

    @pl.when(i == 0)
    def _():
        for cp in fetch(te_ref[0]):
            cp.start()

    prev = te_ref[jnp.maximum(i - 1, 0)]

    @pl.when((i == 0) | (te_ref[i] != prev))
    def _():
        for m, cp in enumerate(fetch(te_ref[i])):
            cp.wait()
            w_b[m][...] = w_stage[m].astype(bf16)
        nxt = tn_ref[i]
        e_nxt = te_ref[jnp.minimum(nxt, n_tiles - 1)]

        @pl.when((nxt > i) & (nxt < n_tiles) & (e_nxt != te_ref[i]))
        def _():
            for cp in fetch(e_nxt):
                cp.start()

    valid = tv_ref[i]

    def ffn_rows(r0, m):
        row = lax.broadcasted_iota(jnp.int32, (m, 1), 0) + r0
        xw = jnp.where(row < valid, xs_ref[r0:r0 + m, :], 0)
        x_hi, x_lo = _unpack_rows(xw)
        x = jnp.concatenate([x_hi, x_lo], axis=1).astype(bf16)
        gate = jnp.minimum(jnp.dot(x, wg_b[...], preferred_element_type=f32) + bg_ref[...], SWIGLU_LIMIT)
        up = jnp.clip(jnp.dot(x, wu_b[...], preferred_element_type=f32) + bu_ref[...],
                      -SWIGLU_LIMIT, SWIGLU_LIMIT)
        act = (up + 1.0) * gate * jax.nn.sigmoid(SWIGLU_ALPHA * gate)
        out = jnp.dot(act.astype(bf16), wd_b[...], preferred_element_type=f32) + bd_ref[...]
        eo_ref[r0:r0 + m, :] = _pack_rows(out)

    def zero_rows(r0, m):
        eo_ref[r0:r0 + m, :] = jnp.zeros((m, eo_ref.shape[1]), eo_ref.dtype)

    for r0 in range(0, xs_ref.shape[0], MOE_PASS):
        lo = 0
        for m in MOE_PASS_SIZES:
            @pl.when((valid > r0 + lo) & ((valid <= r0 + m) | (m == MOE_PASS)))
            def _(r0=r0, m=m):
                ffn_rows(r0, m)
                if m < MOE_PASS:
                    zero_rows(r0 + m, MOE_PASS - m)
            lo = m

        @pl.when(valid <= r0)
        def _(r0=r0):
            zero_rows(r0, MOE_PASS)


def _expert_ffn(tile_expert, tile_valid, tile_next, last_used, xs, w_gate, b_gate, w_up, b_up, w_down, b_down):
    n_rows, dp = xs.shape
    d = 2 * dp
    bm = MOE_TILE
    d_e = w_gate.shape[-1]
    assert d == d_e
    bspec = lambda n_: pl.BlockSpec((None, 1, n_), lambda i, te, tv, tn, tl: (te[i], 0, 0))
    hbm = pl.BlockSpec(memory_space=pl.ANY)
    grid_spec = pltpu.PrefetchScalarGridSpec(
        num_scalar_prefetch=4,
        grid=(n_rows // bm,),
        in_specs=[pl.BlockSpec((bm, dp), lambda i, te, tv, tn, tl: (jnp.minimum(i, tl[0]), 0)),
                  hbm, bspec(d_e), hbm, bspec(d_e), hbm, bspec(d)],
        out_specs=pl.BlockSpec((bm, dp), lambda i, te, tv, tn, tl: (i, 0)),
        scratch_shapes=[pltpu.VMEM((3, d, d_e), f32),
                        pltpu.VMEM((d, d_e), bf16), pltpu.VMEM((d, d_e), bf16), pltpu.VMEM((d_e, d), bf16),
                        pltpu.SemaphoreType.DMA((3,))],
    )
    return pl.pallas_call(
        _ffn_kernel,
        grid_spec=grid_spec,
        out_shape=jax.ShapeDtypeStruct((n_rows, dp), jnp.int32),
        compiler_params=_cparams(("arbitrary",)),
        name="expert_ffn",
    )(tile_expert, tile_valid, tile_next, last_used, xs, w_gate, b_gate.reshape(N_EXPERTS, 1, d_e), w_up,
      b_up.reshape(N_EXPERTS, 1, d_e), w_down, b_down.reshape(N_EXPERTS, 1, d))


def _combine_kernel(eg_ref, wts_ref, x1_ref, mod_ref, lg_ref, lb_ref, o_ref):
    w = wts_ref[...]
    y_hi, y_lo = _unpack_rows(eg_ref[0])
    y_hi, y_lo = y_hi * w[:, 0:1], y_lo * w[:, 0:1]
    for kk in range(1, TOP_K):
        e_hi, e_lo = _unpack_rows(eg_ref[kk])
        y_hi = y_hi + e_hi * w[:, kk:kk + 1]
        y_lo = y_lo + e_lo * w[:, kk:kk + 1]
    y = jnp.concatenate([y_hi, y_lo], axis=1)
    z = DEEPNORM_ALPHA * x1_ref[...] + mod_ref[0] * y
    mu = jnp.mean(z, axis=-1, keepdims=True)
    zc = z - mu
    var = jnp.mean(zc * zc, axis=-1, keepdims=True)
    o_ref[...] = zc * lax.rsqrt(var + LN_EPS) * lg_ref[...] + lb_ref[...]


def _combine_ln(eg, wts, x1, mod3, lg, lb, tokens_per_batch):
    T = x1.shape[0]
    tm = TOKEN_TILE
    tiles_per_batch = tokens_per_batch // tm
    const = lambda i: (0, 0)
    return pl.pallas_call(
        _combine_kernel,
        grid=(T // tm,),
        in_specs=[pl.BlockSpec((TOP_K, tm, D_MODEL // 2), lambda i: (0, i, 0)),
                  pl.BlockSpec((tm, LANES), lambda i: (i, 0)),
                  pl.BlockSpec((tm, D_MODEL), lambda i: (i, 0)),
                  pl.BlockSpec((1, 1, D_MODEL), lambda i: (i // tiles_per_batch, 0, N_MOD - 1)),
                  pl.BlockSpec((1, D_MODEL), const),
                  pl.BlockSpec((1, D_MODEL), const)],
        out_specs=pl.BlockSpec((tm, D_MODEL), lambda i: (i, 0)),
        out_shape=jax.ShapeDtypeStruct((T, D_MODEL), f32),
        compiler_params=_cparams(("arbitrary",)),
        name="combine_ln",
    )(eg, wts, x1, mod3, lg, lb)


def _block_diag(w):
    n, c, d = w.shape
    eye = jnp.eye(n, dtype=w.dtype)
    return jnp.einsum('ncd,nm->ncmd', w, eye).reshape(n * c, n * d)


def kernel(x, c, ctx, c_ctx, w_ada, b_ada, w_in, conv_w, conv_b, lru_wa, lru_ba, lru_wx, lru_bx,
           lru_lam, gla_wa, gla_ba, gla_norm_g, w_out, ln1_g, ln1_b, w_router, b_router, w_gate,
           b_gate, w_up, b_up, w_down, b_down, ln2_g, ln2_b):
    B, L, D = x.shape
    Lc = ctx.shape[1]
    T = B * L
    rows = L // GRID_W
    l = 0

    cpad = jnp.zeros((8, D), f32).at[0:B].set(c).at[B].set(c_ctx)
    w_cat = jnp.pad(w_in[l], ((0, 0), (0, D_PROJ - w_in.shape[-1]))).astype(bf16)
    wg = [jnp.concatenate([_block_diag(lru_wa[l, d]), _block_diag(lru_wx[l, d])], axis=1).astype(bf16)
          for d in range(2)]
    bg = [jnp.concatenate([lru_ba[l, d], lru_bx[l, d]])[None] for d in range(2)]
    lam = [lru_lam[l, d][None] for d in range(2)]
    wa = [jnp.pad(gla_wa[l, d], ((0, A_LOW_PAD - GATE_RANK), (0, 0))).astype(bf16) for d in range(2)]
    ba = [gla_ba[l, d][None] for d in range(2)]
    cw, cb = conv_w[l], conv_b[l][None]
    wr_t = w_router[l].T
    wr_hi = wr_t.astype(bf16)
    wr_lo = (wr_t - wr_hi.astype(f32)).astype(bf16)
    br = b_router[l][:, None]
    c_xl, c_gl, c_pg = (0, D_LRU), (D_LRU, 2 * D_LRU), (2 * D_LRU, D_PROJ)

    mod3 = _ada_mod(cpad, w_ada[l], b_ada[l]).reshape(8, 1, N_MOD * D)

    tiles_per_batch = L // TOKEN_TILE
    xl_c, pg_c = _inproj_rows(ctx.reshape(B * Lc, D), mod3, w_cat, lambda i: B, (c_xl, c_pg), (f32, bf16))
    zero_h = jnp.zeros((B, 1, D_LRU), f32)
    zero_s = jnp.zeros((B, GLA_DV, D_GLA_K), f32)
    pg_c = pg_c.reshape(B, Lc // GLA_CHUNK, GLA_CHUNK, D_PG)
    h_ctx, s_ctx = [], []
    for d in range(2):
        _, hf = _lru_pass(xl_c, cw, cb, wg[d], bg[d], lam[d], zero_h, n_batch=B, row_w=Lc, tile=Lc,
                          reverse=bool(d))
        _, sf = _gla_pass(pg_c, wa[d], ba[d], zero_s, reverse=bool(d))
        h_ctx.append(hf)
        s_ctx.append(sf)

    xt = x.reshape(T, D)
    xl, gl = _inproj_rows(xt, mod3, w_cat, lambda i: i // tiles_per_batch, (c_xl, c_gl), (f32, bf16))
    hb, _ = _lru_pass(xl, cw, cb, wg[1], bg[1], lam[1], h_ctx[1], n_batch=B, row_w=GRID_W, tile=LRU_TILE,
                      reverse=True)
    y_lru, _ = _lru_pass(xl, cw, cb, wg[0], bg[0], lam[0], h_ctx[0], n_batch=B, row_w=GRID_W,
                         tile=LRU_TILE, reverse=False, hb=hb, gl=gl)
    pg = _inproj_cols(x.reshape(B, rows, GRID_W, D), mod3, w_cat, c_pg)
    ob, _ = _gla_pass(pg, wa[1], ba[1], s_ctx[1], reverse=True)
    y_gla, _ = _gla_pass(pg, wa[0], ba[0], s_ctx[0], reverse=False, ob=ob, gn=gla_norm_g[l][None])

    x1, hp, ids, rank, wts, cnt = _outproj_router(
        y_lru, y_gla, xt, mod3, w_out[l].astype(bf16), ln1_g[l][None], ln1_b[l][None], wr_hi, wr_lo, br, L)

    n_tiles = T * TOP_K // MOE_TILE + N_EXPERTS
    dest, tiles = _route(ids, rank, cnt, MOE_TILE, n_tiles)
    xs = _sc_dispatch(hp, dest.reshape(-1), n_tiles * MOE_TILE)
    eo = _expert_ffn(tiles[0, :n_tiles], tiles[1, :n_tiles], tiles[2, :n_tiles], tiles[3, :1], xs, w_gate[l],
                     b_gate[l], w_up[l], b_up[l], w_down[l], b_down[l])
    eg = _sc_gather(eo, dest[:TOP_K].reshape(-1)).reshape(TOP_K, T, D // 2)
    out = _combine_ln(eg, wts, x1, mod3, ln2_g[l][None], ln2_b[l][None], L)
    return out.reshape(B, L, D)
```

```python
import functools

import jax
import jax.numpy as jnp
from jax import lax
from jax.experimental import pallas as pl
from jax.experimental.pallas import tpu as pltpu
from jax.experimental.pallas import tpu_sc as plsc

D_MODEL = 1024
DEPTH = 1
GRID_W = 64
D_LRU = 512
LRU_BLOCKS = 8
CONV_W = 4
LRU_C = 8.0
GLA_HEADS = 4
D_GLA_V = 512
D_GLA_K = 256
GLA_DK = 64
GLA_DV = 128
GATE_RANK = 16
GATE_TAU = 16.0
GLA_CHUNK = 64
N_EXPERTS = 32
TOP_K = 4
SWIGLU_LIMIT = 7.0
SWIGLU_ALPHA = 1.702
N_MOD = 6
DEEPNORM_ALPHA = (2.0 * DEPTH) ** 0.25
LN_EPS = 1e-5
RMS_EPS = 1e-6

LANES = 128
A_LOW_PAD = LANES
D_PG = 2 * D_GLA_K + 2 * D_GLA_V + A_LOW_PAD
D_PROJ = 2 * D_LRU + D_PG
TOKEN_TILE = 512
LRU_TILE = 512
MOE_TILE = 1024
MOE_PASS = 512
MOE_PASS_SIZES = (128, 256, MOE_PASS)
SC_CHUNK = 64
VMEM_LIMIT = 48 * 1024 * 1024

f32 = jnp.float32
bf16 = jnp.bfloat16


def _cparams(sem):
    return pltpu.CompilerParams(dimension_semantics=sem, vmem_limit_bytes=VMEM_LIMIT)


def _ada_kernel(c_ref, w_ref, b_ref, o_ref):
    s = c_ref[...]
    s = s * jax.nn.sigmoid(s)
    o_ref[...] = jnp.dot(s.astype(bf16), w_ref[...].astype(bf16), preferred_element_type=f32) + b_ref[...]


def _ada_mod(cpad, w, b):
    n = w.shape[1]
    tn = 1024
    return pl.pallas_call(
        _ada_kernel,
        grid=(n // tn,),
        in_specs=[pl.BlockSpec((8, D_MODEL), lambda j: (0, 0)),
                  pl.BlockSpec((D_MODEL, tn), lambda j: (0, j)),
                  pl.BlockSpec((1, tn), lambda j: (0, j))],
        out_specs=pl.BlockSpec((8, tn), lambda j: (0, j)),
        out_shape=jax.ShapeDtypeStruct((8, n), f32),
        compiler_params=_cparams(("arbitrary",)),
        name="ada_mod",
    )(cpad, w, b.reshape(1, n))


def _inproj_kernel(x_ref, mod_ref, w_ref, *out_refs, parts, col_major):
    sh = mod_ref[0, :, 0:D_MODEL]
    sc = mod_ref[0, :, D_MODEL:2 * D_MODEL]
    if col_major:
        x = jnp.concatenate([x_ref[0, :, j, :] for j in range(x_ref.shape[2])], axis=0)
    else:
        x = x_ref[...]
    u = (x * (1.0 + sc) + sh).astype(bf16)
    for (lo, hi), o_ref in zip(parts, out_refs):
        p = jnp.dot(u, w_ref[:, lo:hi], preferred_element_type=f32).astype(o_ref.dtype)
        o_ref[...] = p.reshape(o_ref.shape)


def _inproj_rows(xt, mod3, w_cat, batch_of_tile, parts, dtypes):
    T = xt.shape[0]
    tm = TOKEN_TILE
    return pl.pallas_call(
        functools.partial(_inproj_kernel, parts=parts, col_major=False),
        grid=(T // tm,),
        in_specs=[pl.BlockSpec((tm, D_MODEL), lambda i: (i, 0)),
                  pl.BlockSpec((1, 1, 2 * D_MODEL), lambda i: (batch_of_tile(i), 0, 0)),
                  pl.BlockSpec((D_MODEL, D_PROJ), lambda i: (0, 0))],
        out_specs=[pl.BlockSpec((tm, hi - lo), lambda i: (i, 0)) for lo, hi in parts],
        out_shape=[jax.ShapeDtypeStruct((T, hi - lo), dt) for (lo, hi), dt in zip(parts, dtypes)],
        compiler_params=_cparams(("arbitrary",)),
        name="inproj_rows",
    )(xt, mod3, w_cat)


def _inproj_cols(x4, mod3, w_cat, part):
    n_batch, rows, cols, _ = x4.shape
    lo, hi = part
    cb = TOKEN_TILE // rows
    return pl.pallas_call(
        functools.partial(_inproj_kernel, parts=(part,), col_major=True),
        grid=(n_batch, cols // cb),
        in_specs=[pl.BlockSpec((1, rows, cb, D_MODEL), lambda b, n: (b, 0, n, 0)),
                  pl.BlockSpec((1, 1, 2 * D_MODEL), lambda b, n: (b, 0, 0)),
                  pl.BlockSpec((D_MODEL, D_PROJ), lambda b, n: (0, 0))],
        out_specs=[pl.BlockSpec((1, cb, rows, hi - lo), lambda b, n: (b, n, 0, 0))],
        out_shape=[jax.ShapeDtypeStruct((n_batch, cols, rows, hi - lo), bf16)],
        compiler_params=_cparams(("arbitrary", "arbitrary")),
        name="inproj_cols",
    )(x4, mod3, w_cat)[0]


def _gelu_tanh(x):
    return 0.5 * x * (1.0 + jnp.tanh(0.7978845608028654 * (x + 0.044715 * (x * x * x))))


def _lru_kernel(*refs, row_w, reverse, merge):
    if merge:
        u_ref, wg_ref, bg_ref, lam_ref, h0_ref, hb_ref, gl_ref, out_ref, hfin_ref, carry, h_nat = refs
    else:
        xl_ref, cw_ref, cb_ref, wg_ref, bg_ref, lam_ref, h0_ref, out_ref, u_out_ref, hfin_ref, carry = refs
    t = pl.program_id(1)

    @pl.when(t == 0)
    def _():
        carry[...] = h0_ref[0]

    n_seg = 8
    blk = lambda v, j: v[j * n_seg:(j + 1) * n_seg]
    if merge:
        u = u_ref[...]
        tt, ch = u.shape
        seg = tt // n_seg
    else:
        _, seg, ch = xl_ref.shape
        tt = n_seg * seg
        x = jnp.concatenate([xl_ref[:, j, :] for j in range(seg)], axis=0)
        segs_per_row = row_w // seg
        s_idx = lax.broadcasted_iota(jnp.int32, (n_seg, ch), 0)
        has_prev = (s_idx % segs_per_row) != 0
        has_next = (s_idx % segs_per_row) != segs_per_row - 1
        from_prev = lambda v: jnp.where(has_prev, pltpu.roll(v, 1, 0), 0.0)
        from_next = lambda v: jnp.where(has_next, pltpu.roll(v, n_seg - 1, 0), 0.0)
        x_m1 = jnp.concatenate([from_prev(blk(x, seg - 1)), x[:tt - n_seg]], axis=0)
        x_m2 = jnp.concatenate([from_prev(blk(x, seg - 2)), from_prev(blk(x, seg - 1)), x[:tt - 2 * n_seg]],
                               axis=0)
        x_p1 = jnp.concatenate([x[n_seg:], from_next(blk(x, 0))], axis=0)
        cw = cw_ref[...]
        u = cb_ref[...] + x_m2 * cw[0:1] + x_m1 * cw[1:2] + x * cw[2:3] + x_p1 * cw[3:4]
        u_out_ref[...] = u
    g = jnp.dot(u.astype(bf16), wg_ref[...], preferred_element_type=f32) + bg_ref[...]
    lam = lam_ref[...]
    softplus_neg = jnp.maximum(-lam, 0.0) + jnp.log(1.0 + jnp.exp(-jnp.abs(lam)))
    log_a_scale = -LRU_C * softplus_neg

    def recurrence_terms(j):
        g_j, u_j = blk(g, j), blk(u, j)
        log_a = log_a_scale * jax.nn.sigmoid(g_j[:, 0:ch])
        a_j = jnp.exp(log_a)
        series = log_a * (-2.0 + log_a * (-2.0 + log_a * (-4.0 / 3.0)))
        one_minus_a2 = jnp.where(log_a > -0.005, series, 1.0 - a_j * a_j)
        root = jnp.where(one_minus_a2 > 0.0, one_minus_a2 * lax.rsqrt(one_minus_a2), 0.0)
        return a_j, root * (jax.nn.sigmoid(g_j[:, ch:2 * ch]) * u_j)

    half = seg // 2
    h_loc, a_cum = [None] * seg, [None] * seg
    h_run, a_run = [None, None], [None, None]
    for i in range(half):
        for p in range(2):
            j = p * half + (half - 1 - i if reverse else i)
            a_j, b_j = recurrence_terms(j)
            h_run[p] = b_j if i == 0 else a_j * h_run[p] + b_j
            a_run[p] = a_j if i == 0 else a_j * a_run[p]
            h_loc[j], a_cum[j] = h_run[p], a_run[p]
    pieces = [(s, p) for s in range(n_seg) for p in range(2)]
    if reverse:
        pieces.reverse()
    entering = [[None] * n_seg, [None] * n_seg]
    state = carry[...]
    for s, p in pieces:
        entering[p][s] = state
        state = h_run[p][s:s + 1] + a_run[p][s:s + 1] * state
    carry[...] = state
    hfin_ref[0] = state
    enter = [jnp.concatenate(e, axis=0) for e in entering]
    h_rows = [h_loc[j] + a_cum[j] * enter[j // half] for j in range(seg)]
    if merge:
        for j in range(seg):
            h_nat[:, j, :] = h_rows[j] + blk(hb_ref, j)
        h = h_nat[...].reshape(tt, ch)
        out_ref[...] = (h * _gelu_tanh(gl_ref[...].astype(f32))).astype(out_ref.dtype)
    else:
        out_ref[...] = jnp.concatenate(h_rows, axis=0)


def _lru_pass(x_in, wg, bg, lam, h0, *, n_batch, tile, reverse, conv=None, merge_with=None):
    T = x_in.shape[0]
    nt = T // n_batch // tile
    merge = merge_with is not None
    assert merge != (conv is not None)
    seg = tile // 8

    def tok(b, t):
        return (b * nt + (nt - 1 - t if reverse else t), 0)

    tile_spec = pl.BlockSpec((tile, D_LRU), tok)
    const = lambda b, t: (0, 0)
    gate_specs = [pl.BlockSpec((D_LRU, 2 * D_LRU), const),
                  pl.BlockSpec((1, 2 * D_LRU), const),
                  pl.BlockSpec((1, D_LRU), const),
                  pl.BlockSpec((1, 1, D_LRU), lambda b, t: (b, 0, 0))]
    state_spec = pl.BlockSpec((1, 1, D_LRU), lambda b, t: (b, 0, 0))
    state_shape = jax.ShapeDtypeStruct((n_batch, 1, D_LRU), f32)
    scratch = [pltpu.VMEM((1, D_LRU), f32)]
    if merge:
        hb, gl = merge_with
        row_w = None
        in_specs = [tile_spec] + gate_specs + [tile_spec, tile_spec]
        args = [x_in, wg, bg, lam, h0, hb, gl]
        scratch += [pltpu.VMEM((8, seg, D_LRU), f32)]
        out_specs = [tile_spec, state_spec]
        out_shape = [jax.ShapeDtypeStruct((T, D_LRU), bf16), state_shape]
    else:
        conv_w, conv_b, row_w = conv
        assert row_w % seg == 0 and seg >= 2
        in_specs = [pl.BlockSpec((8, seg, D_LRU), lambda b, t: tok(b, t) + (0,)),
                    pl.BlockSpec((CONV_W, D_LRU), const),
                    pl.BlockSpec((1, D_LRU), const)] + gate_specs
        args = [x_in.reshape(T // seg, seg, D_LRU), conv_w, conv_b, wg, bg, lam, h0]
        out_specs = [tile_spec, tile_spec, state_spec]
        out_shape = [jax.ShapeDtypeStruct((T, D_LRU), f32), jax.ShapeDtypeStruct((T, D_LRU), f32), state_shape]
    return pl.pallas_call(
        functools.partial(_lru_kernel, row_w=row_w, reverse=reverse, merge=merge),
        grid=(n_batch, nt),
        in_specs=in_specs,
        out_specs=out_specs,
        out_shape=out_shape,
        scratch_shapes=scratch,
        compiler_params=_cparams(("arbitrary", "arbitrary")),
        name="lru_merge" if merge else "lru_scan",
    )(*args)


def _gla_kernel(*refs, reverse, merge):
    if merge:
        pg_ref, wa_ref, ba_ref, s0_ref, ob_ref, gn_ref, out_ref, sfin_ref, state = refs
    else:
        pg_ref, wa_ref, ba_ref, s0_ref, out_ref, sfin_ref, state = refs
    n = pl.program_id(0)

    @pl.when(n == 0)
    def _():
        state[...] = s0_ref[...]

    n_batch = pg_ref.shape[0]
    ck = GLA_CHUNK
    nh = GLA_HEADS
    ii = lax.broadcasted_iota(jnp.int32, (nh * ck, ck), 0) & (ck - 1)
    jj = lax.broadcasted_iota(jnp.int32, (nh * ck, ck), 1)
    seen = (jj >= ii) if reverse else (jj <= ii)
    head_of_lane = lax.broadcasted_iota(jnp.int32, (1, D_GLA_K), 1) // GLA_DK
    nt_dims = (((1,), (1,)), ((), ()))
    tn_dims = (((0,), (0,)), ((), ()))
    o_q, o_k, o_v, o_g, o_a = 0, D_GLA_K, 2 * D_GLA_K, 2 * D_GLA_K + D_GLA_V, 2 * D_GLA_K + 2 * D_GLA_V

    rows_all = n_batch * ck
    a_low = jnp.concatenate([pg_ref[bi, :, o_a:o_a + A_LOW_PAD] for bi in range(n_batch)], axis=0)
    z = jnp.dot(a_low, wa_ref[...], preferred_element_type=f32) + ba_ref[...]
    bcum_all = (jnp.minimum(z, 0.0) - jnp.log(1.0 + jnp.exp(-jnp.abs(z)))) * (1.0 / GATE_TAU)
    step = lax.broadcasted_iota(jnp.int32, (rows_all, D_GLA_K), 0) & (ck - 1)
    d = 1
    while d < ck:
        if reverse:
            bcum_all = bcum_all + jnp.where(step < ck - d, pltpu.roll(bcum_all, rows_all - d, 0), 0.0)
        else:
            bcum_all = bcum_all + jnp.where(step >= d, pltpu.roll(bcum_all, d, 0), 0.0)
        d *= 2

    for bi in range(n_batch):
        q = pg_ref[bi, :, o_q:o_k].astype(f32) * (GLA_DK ** -0.5)
        k = pg_ref[bi, :, o_k:o_v].astype(f32)
        v = pg_ref[bi, :, o_v:o_g]
        bcum = bcum_all[bi * ck:(bi + 1) * ck]
        btot = bcum[0:1] if reverse else bcum[ck - 1:ck]
        q_dec = q * jnp.exp(bcum)
        k_dec = (k * jnp.exp(-bcum)).astype(bf16)
        k_end = (k * jnp.exp(btot - bcum)).astype(bf16)
        s_t = state[bi]
        q_blk = jnp.concatenate([jnp.where(head_of_lane == hd, q_dec, 0.0) for hd in range(nh)],
                                axis=0).astype(bf16)
        scores = lax.dot_general(q_blk, k_dec, nt_dims, preferred_element_type=f32)
        scores = jnp.where(seen, scores, 0.0).astype(bf16)
        intra = jnp.dot(scores, v, preferred_element_type=f32)
        inter = lax.dot_general(q_blk, s_t.astype(bf16), nt_dims, preferred_element_type=f32)
        outs = [intra[hd * ck:(hd + 1) * ck, hd * GLA_DV:(hd + 1) * GLA_DV] + inter[hd * ck:(hd + 1) * ck]
                for hd in range(nh)]
        kv_full = lax.dot_general(v, k_end, tn_dims, preferred_element_type=f32)
        kv_t = jnp.zeros_like(s_t)
        for hd in range(nh):
            kv_t = jnp.where(head_of_lane == hd, kv_full[hd * GLA_DV:(hd + 1) * GLA_DV], kv_t)
        state[bi] = s_t * jnp.exp(btot) + kv_t
        if merge:
            g = pg_ref[bi, :, o_g:o_a].astype(f32)
            normed = []
            for hd in range(GLA_HEADS):
                oh = outs[hd] + ob_ref[bi, :, hd * GLA_DV:(hd + 1) * GLA_DV]
                ms = jnp.mean(oh * oh, axis=-1, keepdims=True)
                normed.append(oh * lax.rsqrt(ms + RMS_EPS))
            y = jnp.concatenate(normed, axis=-1) * gn_ref[...] * (g * jax.nn.sigmoid(g))
            out_ref[bi] = y
        else:
            out_ref[bi] = jnp.concatenate(outs, axis=-1)

    @pl.when(n == pl.num_programs(0) - 1)
    def _():
        sfin_ref[...] = state[...]


def _gla_pass(pg, wa, ba, s0, *, reverse, ob=None, gn=None):
    n_batch, n_chunks = pg.shape[0], pg.shape[1]
    merge = ob is not None
    order = (lambda n: n_chunks - 1 - n) if reverse else (lambda n: n)
    const2 = lambda n: (0, 0)
    const3 = lambda n: (0, 0, 0)
    chunk = lambda width: pl.BlockSpec((n_batch, None, GLA_CHUNK, width), lambda n: (0, order(n), 0, 0))
    in_specs = [chunk(D_PG),
                pl.BlockSpec((A_LOW_PAD, D_GLA_K), const2),
                pl.BlockSpec((1, D_GLA_K), const2),
                pl.BlockSpec((n_batch, GLA_DV, D_GLA_K), const3)]
    args = [pg, wa, ba, s0]
    if merge:
        in_specs += [chunk(D_GLA_V), pl.BlockSpec((1, D_GLA_V), const2)]
        args += [ob, gn]
    out_spec = chunk(D_GLA_V)
    out_shape = (n_batch, n_chunks, GLA_CHUNK, D_GLA_V)
    return pl.pallas_call(
        functools.partial(_gla_kernel, reverse=reverse, merge=merge),
        grid=(n_chunks,),
        in_specs=in_specs,
        out_specs=[out_spec,
                   pl.BlockSpec((n_batch, GLA_DV, D_GLA_K), const3)],
        out_shape=[jax.ShapeDtypeStruct(out_shape, f32),
                   jax.ShapeDtypeStruct((n_batch, GLA_DV, D_GLA_K), f32)],
        scratch_shapes=[pltpu.VMEM((n_batch, GLA_DV, D_GLA_K), f32)],
        compiler_params=_cparams(("arbitrary",)),
        name="gla_merge" if merge else "gla_scan",
    )(*args)


def _pack_rows(v):
    n = v.shape[1] // 2
    w = lax.bitcast_convert_type(v.astype(bf16).astype(f32), jnp.int32)
    return (w[:, :n] & jnp.int32(-65536)) | lax.shift_right_logical(w[:, n:], 16)


def _unpack_rows(w):
    hi = lax.bitcast_convert_type(w & jnp.int32(-65536), f32)
    lo = lax.bitcast_convert_type(lax.shift_left(w, 16), f32)
    return hi, lo


def _outproj_router_kernel(yl_ref, yg_ref, x_ref, mod_ref, wo_ref, lg_ref, lb_ref, wrh_ref, wrl_ref, br_ref,
                           x1_ref, hp_ref, ids_ref, rank_ref, wts_ref, cnt_ref, running):
    i = pl.program_id(0)

    @pl.when(i == 0)
    def _():
        running[...] = jnp.zeros_like(running)

    d = D_MODEL
    g1 = mod_ref[0, :, 2 * d:3 * d]
    sh2 = mod_ref[0, :, 3 * d:4 * d]
    sc2 = mod_ref[0, :, 4 * d:5 * d]
    yg = jnp.concatenate([yg_ref[0, :, r, :] for r in range(yg_ref.shape[2])], axis=0).astype(bf16)
    y = (jnp.dot(yl_ref[...], wo_ref[0:D_LRU, :], preferred_element_type=f32)
         + jnp.dot(yg, wo_ref[D_LRU:2 * D_LRU, :], preferred_element_type=f32))
    z = DEEPNORM_ALPHA * x_ref[...] + g1 * y
    mu = jnp.mean(z, axis=-1, keepdims=True)
    zc = z - mu
    var = jnp.mean(zc * zc, axis=-1, keepdims=True)
    x1 = zc * lax.rsqrt(var + LN_EPS) * lg_ref[...] + lb_ref[...]
    x1_ref[...] = x1
    hmod = x1 * (1.0 + sc2) + sh2
    hp_ref[...] = _pack_rows(hmod)

    nt_dims = (((1,), (1,)), ((), ()))
    h_hi = hmod.astype(bf16)
    h_lo = (hmod - h_hi.astype(f32)).astype(bf16)
    logits = (lax.dot_general(wrh_ref[...], h_hi, nt_dims, preferred_element_type=f32)
              + lax.dot_general(wrh_ref[...], h_lo, nt_dims, preferred_element_type=f32)
              + lax.dot_general(wrl_ref[...], h_hi, nt_dims, preferred_element_type=f32)) + br_ref[...]
    ne, tm = logits.shape
    expert = lax.broadcasted_iota(jnp.int32, (ne, tm), 0).astype(f32)
    neg_inf = jnp.float32(-jnp.inf)
    live = logits
    sel = jnp.zeros((ne, tm), f32)
    ids, vals = [], []
    for _ in range(TOP_K):
        m = jnp.max(live, axis=0, keepdims=True)
        j = jnp.min(jnp.where(live == m, expert, float(ne)), axis=0, keepdims=True)
        pick = expert == j
        sel = jnp.where(pick, 1.0, sel)
        live = jnp.where(pick, neg_inf, live)
        ids.append(j)
        vals.append(m)
    exps = [jnp.exp(vk - vals[0]) for vk in vals]
    denom = exps[0] + exps[1] + exps[2] + exps[3]

    ri = lax.broadcasted_iota(jnp.int32, (tm, tm), 0)
    ci = lax.broadcasted_iota(jnp.int32, (tm, tm), 1)
    earlier = (ri < ci).astype(bf16)
    rank_dense = running[...] + jnp.dot(sel.astype(bf16), earlier, preferred_element_type=f32)
    running[...] = running[...] + jnp.sum(sel, axis=1, keepdims=True)
    cnt_ref[...] = running[...]

    slot = lax.broadcasted_iota(jnp.int32, (8, tm), 0)
    ids_out = jnp.zeros((8, tm), f32)
    rank_out = jnp.zeros((8, tm), f32)
    wts_out = jnp.zeros((8, tm), f32)
    for kk in range(TOP_K):
        rk = jnp.sum(jnp.where(expert == ids[kk], rank_dense, 0.0), axis=0, keepdims=True)
        here = slot == kk
        ids_out = jnp.where(here, ids[kk], ids_out)
        rank_out = jnp.where(here, rk, rank_out)
        wts_out = jnp.where(here, exps[kk] / denom, wts_out)
    ids_ref[...] = ids_out.astype(jnp.int32)
    rank_ref[...] = rank_out.astype(jnp.int32)
    wts_ref[...] = jnp.concatenate([wts_out, jnp.zeros((LANES - 8, tm), f32)], axis=0).T


def _outproj_router(yl, yg, xt, mod3, wo, lg, lb, wr_hi, wr_lo, br, tokens_per_batch):
    T = xt.shape[0]
    tm = TOKEN_TILE
    tiles_per_batch = tokens_per_batch // tm
    tokrow = lambda i: (i, 0)
    const = lambda i: (0, 0)
    cols, rows_per_tile = yg.shape[1], tm // yg.shape[1]
    return pl.pallas_call(
        _outproj_router_kernel,
        grid=(T // tm,),
        in_specs=[pl.BlockSpec((tm, D_LRU), tokrow),
                  pl.BlockSpec((1, cols, rows_per_tile, D_GLA_V),
                               lambda i: (i // tiles_per_batch, 0, i % tiles_per_batch, 0)),
                  pl.BlockSpec((tm, D_MODEL), tokrow),
                  pl.BlockSpec((1, 1, N_MOD * D_MODEL), lambda i: (i // tiles_per_batch, 0, 0)),
                  pl.BlockSpec((D_MODEL, D_MODEL), const),
                  pl.BlockSpec((1, D_MODEL), const),
                  pl.BlockSpec((1, D_MODEL), const),
                  pl.BlockSpec((N_EXPERTS, D_MODEL), const),
                  pl.BlockSpec((N_EXPERTS, D_MODEL), const),
                  pl.BlockSpec((N_EXPERTS, 1), const)],
        out_specs=[pl.BlockSpec((tm, D_MODEL), tokrow),
                   pl.BlockSpec((tm, D_MODEL // 2), tokrow),
                   pl.BlockSpec((8, tm), lambda i: (0, i)),
                   pl.BlockSpec((8, tm), lambda i: (0, i)),
                   pl.BlockSpec((tm, LANES), tokrow),
                   pl.BlockSpec((N_EXPERTS, 1), const)],
        out_shape=[jax.ShapeDtypeStruct((T, D_MODEL), f32),
                   jax.ShapeDtypeStruct((T, D_MODEL // 2), jnp.int32),
                   jax.ShapeDtypeStruct((8, T), jnp.int32),
                   jax.ShapeDtypeStruct((8, T), jnp.int32),
                   jax.ShapeDtypeStruct((T, LANES), f32),
                   jax.ShapeDtypeStruct((N_EXPERTS, 1), f32)],
        scratch_shapes=[pltpu.VMEM((N_EXPERTS, 1), f32)],
        compiler_params=_cparams(("arbitrary",)),
        name="outproj_router",
    )(yl, yg, xt, mod3, wo, lg, lb, wr_hi, wr_lo, br)


def _route_kernel(ids_ref, rank_ref, cnt_ref, dest_ref, tiles_ref, *, bm):
    ne = cnt_ref.shape[0]
    cnt = cnt_ref[...]
    padded = jnp.floor((cnt + (bm - 1.0)) * (1.0 / bm)) * bm
    sub = lax.broadcasted_iota(jnp.int32, (ne, LANES), 0)
    lane = lax.broadcasted_iota(jnp.int32, (ne, LANES), 1)
    padded_row = jnp.sum(jnp.where(sub == lane, padded, 0.0), axis=0, keepdims=True)
    base = jnp.sum(jnp.where(lane < sub, padded_row, 0.0), axis=1, keepdims=True)
    ends = base + padded

    tc = ids_ref.shape[1]
    expert = lax.broadcasted_iota(jnp.int32, (ne, tc), 0)
    slot = lax.broadcasted_iota(jnp.int32, (8, tc), 0)
    ids = ids_ref[...]
    group_base = jnp.zeros((8, tc), f32)
    for kk in range(TOP_K):
        b_k = jnp.sum(jnp.where(expert == ids[kk:kk + 1], base, 0.0), axis=0, keepdims=True)
        group_base = jnp.where(slot == kk, b_k, group_base)
    dest_ref[...] = group_base.astype(jnp.int32) + rank_ref[...]

    nl = tiles_ref.shape[1]
    start = lax.broadcasted_iota(jnp.int32, (ne, nl), 1).astype(f32) * bm
    te = jnp.minimum(jnp.sum(jnp.where(start >= ends, 1.0, 0.0), axis=0, keepdims=True), ne - 1.0)
    at_te = lax.broadcasted_iota(jnp.int32, (ne, nl), 0).astype(f32) == te
    cnt_te = jnp.sum(jnp.where(at_te, cnt, 0.0), axis=0, keepdims=True)
    base_te = jnp.sum(jnp.where(at_te, base, 0.0), axis=0, keepdims=True)
    valid = jnp.clip(cnt_te - (start[0:1] - base_te), 0.0, float(bm))
    next_group = jnp.sum(jnp.where(at_te, ends, 0.0), axis=0, keepdims=True) * (1.0 / bm)
    last_used = jnp.sum(padded, axis=0, keepdims=True) * (1.0 / bm) - 1.0
    srow = lax.broadcasted_iota(jnp.int32, (8, nl), 0)
    table = jnp.where(srow == 0, te, jnp.where(srow == 1, valid, jnp.where(srow == 2, next_group,
                                                                           jnp.where(srow == 3, last_used, 0.0))))
    tiles_ref[...] = table.astype(jnp.int32)


def _route(ids, rank, cnt, bm, n_tiles):
    T = ids.shape[1]
    tc = 2048
    nl = -(-n_tiles // LANES) * LANES
    tok = lambda i: (0, i)
    const = lambda i: (0, 0)
    return pl.pallas_call(
        functools.partial(_route_kernel, bm=bm),
        grid=(T // tc,),
        in_specs=[pl.BlockSpec((8, tc), tok), pl.BlockSpec((8, tc), tok),
                  pl.BlockSpec((N_EXPERTS, 1), const)],
        out_specs=[pl.BlockSpec((8, tc), tok), pl.BlockSpec((8, nl), const)],
        out_shape=[jax.ShapeDtypeStruct((8, T), jnp.int32), jax.ShapeDtypeStruct((8, nl), jnp.int32)],
        compiler_params=_cparams(("arbitrary",)),
        name="route",
    )(ids, rank, cnt)


def _sc_workers():
    info = plsc.get_sparse_core_info()
    return info.num_cores, info.num_subcores


def _sc_dispatch(rows, dest_flat, n_out):
    T, D = rows.shape
    nc, ns = _sc_workers()
    per_w = T // (nc * ns)
    n_chunks = per_w // SC_CHUNK
    mesh = plsc.VectorSubcoreMesh(core_axis_name="c", subcore_axis_name="s")

    @functools.partial(
        pl.kernel, mesh=mesh,
        out_type=jax.ShapeDtypeStruct((n_out, D), rows.dtype),
        scratch_types=([pltpu.VMEM((SC_CHUNK,), jnp.int32)] * TOP_K
                       + [pltpu.VMEM((SC_CHUNK, D), rows.dtype)]
                       + [pltpu.SemaphoreType.DMA] * TOP_K),
    )
    def k(rows_hbm, dest_hbm, out_hbm, *scratch):
        idx_v = scratch[:TOP_K]
        rows_v = scratch[TOP_K]
        sems = scratch[TOP_K + 1:]
        wid = lax.axis_index("s") * nc + lax.axis_index("c")
        base = wid * per_w

        @pl.loop(0, n_chunks)
        def _(j):
            off = base + j * SC_CHUNK
            pltpu.sync_copy(rows_hbm.at[pl.ds(off, SC_CHUNK)], rows_v)
            for kk in range(TOP_K):
                pltpu.sync_copy(dest_hbm.at[pl.ds(kk * T + off, SC_CHUNK)], idx_v[kk])
            copies = [pltpu.async_copy(rows_v, out_hbm.at[idx_v[kk]], sems[kk]) for kk in range(TOP_K)]
            for cp in copies:
                cp.wait()

    return k(rows, dest_flat)


def _sc_gather(table, idx):
    _, D = table.shape
    N = idx.shape[0]
    nc, ns = _sc_workers()
    per_w = N // (nc * ns)
    n_chunks = per_w // SC_CHUNK
    assert n_chunks % 2 == 0
    mesh = plsc.VectorSubcoreMesh(core_axis_name="c", subcore_axis_name="s")

    @functools.partial(
        pl.kernel, mesh=mesh,
        out_type=jax.ShapeDtypeStruct((N, D), table.dtype),
        scratch_types=([pltpu.VMEM((SC_CHUNK,), jnp.int32)] * 2
                       + [pltpu.VMEM((SC_CHUNK, D), table.dtype)] * 2
                       + [pltpu.SemaphoreType.DMA] * 2),
    )
    def k(table_hbm, idx_hbm, out_hbm, idx0, idx1, buf0, buf1, sem0, sem1):
        idxs, bufs, sems = (idx0, idx1), (buf0, buf1), (sem0, sem1)
        wid = lax.axis_index("s") * nc + lax.axis_index("c")
        base = wid * per_w

        def gather(b):
            return pltpu.make_async_copy(table_hbm.at[idxs[b]], bufs[b], sems[b])

        def start(chunk, b):
            pltpu.sync_copy(idx_hbm.at[pl.ds(base + chunk * SC_CHUNK, SC_CHUNK)], idxs[b])
            gather(b).start()

        def finish(chunk, b):
            gather(b).wait()
            pltpu.sync_copy(bufs[b], out_hbm.at[pl.ds(base + chunk * SC_CHUNK, SC_CHUNK)])

        start(0, 0)

        @pl.loop(0, n_chunks, step=2)
        def _(j):
            start(j + 1, 1)
            finish(j, 0)

            @pl.when(j + 2 < n_chunks)
            def _():
                start(j + 2, 0)

            finish(j + 1, 1)

    return k(table, idx)


def _ffn_kernel(te_ref, tv_ref, tn_ref, tl_ref, xs_ref, wg_hbm, bg_ref, wu_hbm, bu_ref, wd_hbm, bd_ref, eo_ref,
                w_stage, wg_b, wu_b, wd_b, sems):
    i = pl.program_id(0)
    n_tiles = pl.num_programs(0)
    w_hbm = (wg_hbm, wu_hbm, wd_hbm)
    w_b = (wg_b, wu_b, wd_b)

    def fetch(e):
        return [pltpu.make_async_copy(w_hbm[m].at[e], w_stage.at[m], sems.at[m]) for m in range(3)]

    @pl.when(i == 0)
    def _():
        for cp in fetch(te_ref[0]):
            cp.start()

    prev = te_ref[jnp.maximum(i - 1, 0)]

    @pl.when((i == 0) | (te_ref[i] != prev))
    def _():
        for m, cp in enumerate(fetch(te_ref[i])):
            cp.wait()
            w_b[m][...] = w_stage[m].astype(bf16)
        nxt = tn_ref[i]
        e_nxt = te_ref[jnp.minimum(nxt, n_tiles - 1)]

        @pl.when((nxt > i) & (nxt < n_tiles) & (e_nxt != te_ref[i]))
        def _():
            for cp in fetch(e_nxt):
                cp.start()

    valid = tv_ref[i]

    def ffn_rows(r0, m):
        row = lax.broadcasted_iota(jnp.int32, (m, 1), 0) + r0
        xw = jnp.where(row < valid, xs_ref[r0:r0 + m, :], 0)
        x_hi, x_lo = _unpack_rows(xw)
        x = jnp.concatenate([x_hi, x_lo], axis=1).astype(bf16)
        gate = jnp.minimum(jnp.dot(x, wg_b[...], preferred_element_type=f32) + bg_ref[...], SWIGLU_LIMIT)
        up = jnp.clip(jnp.dot(x, wu_b[...], preferred_element_type=f32) + bu_ref[...],
                      -SWIGLU_LIMIT, SWIGLU_LIMIT)
        act = (up + 1.0) * gate * jax.nn.sigmoid(SWIGLU_ALPHA * gate)
        out = jnp.dot(act.astype(bf16), wd_b[...], preferred_element_type=f32) + bd_ref[...]
        eo_ref[r0:r0 + m, :] = _pack_rows(out)

    def zero_rows(r0, m):
        eo_ref[r0:r0 + m, :] = jnp.zeros((m, eo_ref.shape[1]), eo_ref.dtype)

    for r0 in range(0, xs_ref.shape[0], MOE_PASS):
        lo = 0
        for m in MOE_PASS_SIZES:
            @pl.when((valid > r0 + lo) & ((valid <= r0 + m) | (m == MOE_PASS)))
            def _(r0=r0, m=m):
                ffn_rows(r0, m)
                if m < MOE_PASS:
                    zero_rows(r0 + m, MOE_PASS - m)
            lo = m

        @pl.when(valid <= r0)
        def _(r0=r0):
            zero_rows(r0, MOE_PASS)


def _expert_ffn(tile_expert, tile_valid, tile_next, last_used, xs, w_gate, b_gate, w_up, b_up, w_down, b_down):
    n_rows, dp = xs.shape
    d = 2 * dp
    bm = MOE_TILE
    d_e = w_gate.shape[-1]
    assert d == d_e
    bspec = lambda n_: pl.BlockSpec((None, 1, n_), lambda i, te, tv, tn, tl: (te[i], 0, 0))
    hbm = pl.BlockSpec(memory_space=pl.ANY)
    grid_spec = pltpu.PrefetchScalarGridSpec(
        num_scalar_prefetch=4,
        grid=(n_rows // bm,),
        in_specs=[pl.BlockSpec((bm, dp), lambda i, te, tv, tn, tl: (jnp.minimum(i, tl[0]), 0)),
                  hbm, bspec(d_e), hbm, bspec(d_e), hbm, bspec(d)],
        out_specs=pl.BlockSpec((bm, dp), lambda i, te, tv, tn, tl: (i, 0)),
        scratch_shapes=[pltpu.VMEM((3, d, d_e), f32),
                        pltpu.VMEM((d, d_e), bf16), pltpu.VMEM((d, d_e), bf16), pltpu.VMEM((d_e, d), bf16),
                        pltpu.SemaphoreType.DMA((3,))],
    )
    return pl.pallas_call(
        _ffn_kernel,
        grid_spec=grid_spec,
        out_shape=jax.ShapeDtypeStruct((n_rows, dp), jnp.int32),
        compiler_params=_cparams(("arbitrary",)),
        name="expert_ffn",
    )(tile_expert, tile_valid, tile_next, last_used, xs, w_gate, b_gate.reshape(N_EXPERTS, 1, d_e), w_up,
      b_up.reshape(N_EXPERTS, 1, d_e), w_down, b_down.reshape(N_EXPERTS, 1, d))


def _combine_kernel(eg_ref, wts_ref, x1_ref, mod_ref, lg_ref, lb_ref, o_ref):
    w = wts_ref[...]
    y_hi, y_lo = _unpack_rows(eg_ref[0])
    y_hi, y_lo = y_hi * w[:, 0:1], y_lo * w[:, 0:1]
    for kk in range(1, TOP_K):
        e_hi, e_lo = _unpack_rows(eg_ref[kk])
        y_hi = y_hi + e_hi * w[:, kk:kk + 1]
        y_lo = y_lo + e_lo * w[:, kk:kk + 1]
    y = jnp.concatenate([y_hi, y_lo], axis=1)
    z = DEEPNORM_ALPHA * x1_ref[...] + mod_ref[0] * y
    mu = jnp.mean(z, axis=-1, keepdims=True)
    zc = z - mu
    var = jnp.mean(zc * zc, axis=-1, keepdims=True)
    o_ref[...] = zc * lax.rsqrt(var + LN_EPS) * lg_ref[...] + lb_ref[...]


def _combine_ln(eg, wts, x1, mod3, lg, lb, tokens_per_batch):
    T = x1.shape[0]
    tm = TOKEN_TILE
    tiles_per_batch = tokens_per_batch // tm
    const = lambda i: (0, 0)
    return pl.pallas_call(
        _combine_kernel,
        grid=(T // tm,),
        in_specs=[pl.BlockSpec((TOP_K, tm, D_MODEL // 2), lambda i: (0, i, 0)),
                  pl.BlockSpec((tm, LANES), lambda i: (i, 0)),
                  pl.BlockSpec((tm, D_MODEL), lambda i: (i, 0)),
                  pl.BlockSpec((1, 1, D_MODEL), lambda i: (i // tiles_per_batch, 0, N_MOD - 1)),
                  pl.BlockSpec((1, D_MODEL), const),
                  pl.BlockSpec((1, D_MODEL), const)],
        out_specs=pl.BlockSpec((tm, D_MODEL), lambda i: (i, 0)),
        out_shape=jax.ShapeDtypeStruct((T, D_MODEL), f32),
        compiler_params=_cparams(("arbitrary",)),
        name="combine_ln",
    )(eg, wts, x1, mod3, lg, lb)


def _block_diag(w):
    n, c, d = w.shape
    eye = jnp.eye(n, dtype=w.dtype)
    return jnp.einsum('ncd,nm->ncmd', w, eye).reshape(n * c, n * d)


def kernel(x, c, ctx, c_ctx, w_ada, b_ada, w_in, conv_w, conv_b, lru_wa, lru_ba, lru_wx, lru_bx,
           lru_lam, gla_wa, gla_ba, gla_norm_g, w_out, ln1_g, ln1_b, w_router, b_router, w_gate,
           b_gate, w_up, b_up, w_down, b_down, ln2_g, ln2_b):
    B, L, D = x.shape
    Lc = ctx.shape[1]
    T = B * L
    rows = L // GRID_W
    l = 0

    cpad = jnp.zeros((8, D), f32).at[0:B].set(c).at[B].set(c_ctx)
    w_cat = jnp.pad(w_in[l], ((0, 0), (0, D_PROJ - w_in.shape[-1]))).astype(bf16)
    wg = [jnp.concatenate([_block_diag(lru_wa[l, d]), _block_diag(lru_wx[l, d])], axis=1).astype(bf16)
          for d in range(2)]
    bg = [jnp.concatenate([lru_ba[l, d], lru_bx[l, d]])[None] for d in range(2)]
    lam = [lru_lam[l, d][None] for d in range(2)]
    wa = [jnp.pad(gla_wa[l, d], ((0, A_LOW_PAD - GATE_RANK), (0, 0))).astype(bf16) for d in range(2)]
    ba = [gla_ba[l, d][None] for d in range(2)]
    cw, cb = conv_w[l], conv_b[l][None]
    wr_t = w_router[l].T
    wr_hi = wr_t.astype(bf16)
    wr_lo = (wr_t - wr_hi.astype(f32)).astype(bf16)
    br = b_router[l][:, None]
    c_xl, c_gl, c_pg = (0, D_LRU), (D_LRU, 2 * D_LRU), (2 * D_LRU, D_PROJ)

    mod3 = _ada_mod(cpad, w_ada[l], b_ada[l]).reshape(8, 1, N_MOD * D)

    tiles_per_batch = L // TOKEN_TILE
    xl_c, pg_c = _inproj_rows(ctx.reshape(B * Lc, D), mod3, w_cat, lambda i: B, (c_xl, c_pg), (f32, bf16))
    zero_h = jnp.zeros((B, 1, D_LRU), f32)
    zero_s = jnp.zeros((B, GLA_DV, D_GLA_K), f32)
    pg_c = pg_c.reshape(B, Lc // GLA_CHUNK, GLA_CHUNK, D_PG)
    h_ctx, s_ctx = [], []
    for d in range(2):
        _, _, hf = _lru_pass(xl_c, wg[d], bg[d], lam[d], zero_h, n_batch=B, tile=Lc, reverse=bool(d),
                             conv=(cw, cb, Lc))
        _, sf = _gla_pass(pg_c, wa[d], ba[d], zero_s, reverse=bool(d))
        h_ctx.append(hf)
        s_ctx.append(sf)

    xt = x.reshape(T, D)
    xl, gl = _inproj_rows(xt, mod3, w_cat, lambda i: i // tiles_per_batch, (c_xl, c_gl), (f32, bf16))
    hb, u_lru, _ = _lru_pass(xl, wg[1], bg[1], lam[1], h_ctx[1], n_batch=B, tile=LRU_TILE, reverse=True,
                             conv=(cw, cb, GRID_W))
    y_lru, _ = _lru_pass(u_lru, wg[0], bg[0], lam[0], h_ctx[0], n_batch=B, tile=LRU_TILE, reverse=False,
                         merge_with=(hb, gl))
    pg = _inproj_cols(x.reshape(B, rows, GRID_W, D), mod3, w_cat, c_pg)
    ob, _ = _gla_pass(pg, wa[1], ba[1], s_ctx[1], reverse=True)
    y_gla, _ = _gla_pass(pg, wa[0], ba[0], s_ctx[0], reverse=False, ob=ob, gn=gla_norm_g[l][None])

    x1, hp, ids, rank, wts, cnt = _outproj_router(
        y_lru, y_gla, xt, mod3, w_out[l].astype(bf16), ln1_g[l][None], ln1_b[l][None], wr_hi, wr_lo, br, L)

    n_tiles = T * TOP_K // MOE_TILE + N_EXPERTS
    dest, tiles = _route(ids, rank, cnt, MOE_TILE, n_tiles)
    xs = _sc_dispatch(hp, dest.reshape(-1), n_tiles * MOE_TILE)
    eo = _expert_ffn(tiles[0, :n_tiles], tiles[1, :n_tiles], tiles[2, :n_tiles], tiles[3, :1], xs, w_gate[l],
                     b_gate[l], w_up[l], b_up[l], w_down[l], b_down[l])
    eg = _sc_gather(eo, dest[:TOP_K].reshape(-1)).reshape(TOP_K, T, D // 2)
    out = _combine_ln(eg, wts, x1, mod3, ln2_g[l][None], ln2_b[l][None], L)
    return out.reshape(B, L, D)
```

```python
import functools

import jax
import jax.numpy as jnp
from jax import lax
from jax.experimental import pallas as pl
from jax.experimental.pallas import tpu as pltpu
from jax.experimental.pallas import tpu_sc as plsc

D_MODEL = 1024
DEPTH = 1
GRID_W = 64
D_LRU = 512
LRU_BLOCKS = 8
CONV_W = 4
LRU_C = 8.0
GLA_HEADS = 4
D_GLA_V = 512
D_GLA_K = 256
GLA_DK = 64
GLA_DV = 128
GATE_RANK = 16
GATE_TAU = 16.0
GLA_CHUNK = 64
N_EXPERTS = 32
TOP_K = 4
SWIGLU_LIMIT = 7.0
SWIGLU_ALPHA = 1.702
N_MOD = 6
DEEPNORM_ALPHA = (2.0 * DEPTH) ** 0.25
LN_EPS = 1e-5
RMS_EPS = 1e-6

LANES = 128
A_LOW_PAD = LANES
D_PG = 2 * D_GLA_K + 2 * D_GLA_V + A_LOW_PAD
D_PROJ = 2 * D_LRU + D_PG
TOKEN_TILE = 512
LRU_TILE = 512
MOE_TILE = 1024
MOE_PASS = 512
MOE_PASS_SIZES = (128, 256, MOE_PASS)
SC_CHUNK = 64
VMEM_LIMIT = 48 * 1024 * 1024

f32 = jnp.float32
bf16 = jnp.bfloat16


def _cparams(sem):
    return pltpu.CompilerParams(dimension_semantics=sem, vmem_limit_bytes=VMEM_LIMIT)


def _ada_kernel(c_ref, w_ref, b_ref, o_ref):
    s = c_ref[...]
    s = s * jax.nn.sigmoid(s)
    o_ref[...] = jnp.dot(s.astype(bf16), w_ref[...].astype(bf16), preferred_element_type=f32) + b_ref[...]


def _ada_mod(cpad, w, b):
    n = w.shape[1]
    tn = 1024
    return pl.pallas_call(
        _ada_kernel,
        grid=(n // tn,),
        in_specs=[pl.BlockSpec((8, D_MODEL), lambda j: (0, 0)),
                  pl.BlockSpec((D_MODEL, tn), lambda j: (0, j)),
                  pl.BlockSpec((1, tn), lambda j: (0, j))],
        out_specs=pl.BlockSpec((8, tn), lambda j: (0, j)),
        out_shape=jax.ShapeDtypeStruct((8, n), f32),
        compiler_params=_cparams(("arbitrary",)),
        name="ada_mod",
    )(cpad, w, b.reshape(1, n))


def _inproj_kernel(x_ref, mod_ref, w_ref, *out_refs, parts, col_major):
    sh = mod_ref[0, :, 0:D_MODEL]
    sc = mod_ref[0, :, D_MODEL:2 * D_MODEL]
    if col_major:
        x = jnp.concatenate([x_ref[0, :, j, :] for j in range(x_ref.shape[2])], axis=0)
    else:
        x = x_ref[...]
    u = (x * (1.0 + sc) + sh).astype(bf16)
    for (lo, hi), o_ref in zip(parts, out_refs):
        p = jnp.dot(u, w_ref[:, lo:hi], preferred_element_type=f32).astype(o_ref.dtype)
        o_ref[...] = p.reshape(o_ref.shape)


def _inproj_rows(xt, mod3, w_cat, batch_of_tile, parts, dtypes):
    T = xt.shape[0]
    tm = TOKEN_TILE
    return pl.pallas_call(
        functools.partial(_inproj_kernel, parts=parts, col_major=False),
        grid=(T // tm,),
        in_specs=[pl.BlockSpec((tm, D_MODEL), lambda i: (i, 0)),
                  pl.BlockSpec((1, 1, 2 * D_MODEL), lambda i: (batch_of_tile(i), 0, 0)),
                  pl.BlockSpec((D_MODEL, D_PROJ), lambda i: (0, 0))],
        out_specs=[pl.BlockSpec((tm, hi - lo), lambda i: (i, 0)) for lo, hi in parts],
        out_shape=[jax.ShapeDtypeStruct((T, hi - lo), dt) for (lo, hi), dt in zip(parts, dtypes)],
        compiler_params=_cparams(("arbitrary",)),
        name="inproj_rows",
    )(xt, mod3, w_cat)


def _inproj_cols(x4, mod3, w_cat, part):
    n_batch, rows, cols, _ = x4.shape
    lo, hi = part
    cb = TOKEN_TILE // rows
    return pl.pallas_call(
        functools.partial(_inproj_kernel, parts=(part,), col_major=True),
        grid=(n_batch, cols // cb),
        in_specs=[pl.BlockSpec((1, rows, cb, D_MODEL), lambda b, n: (b, 0, n, 0)),
                  pl.BlockSpec((1, 1, 2 * D_MODEL), lambda b, n: (b, 0, 0)),
                  pl.BlockSpec((D_MODEL, D_PROJ), lambda b, n: (0, 0))],
        out_specs=[pl.BlockSpec((1, cb, rows, hi - lo), lambda b, n: (b, n, 0, 0))],
        out_shape=[jax.ShapeDtypeStruct((n_batch, cols, rows, hi - lo), bf16)],
        compiler_params=_cparams(("arbitrary", "arbitrary")),
        name="inproj_cols",
    )(x4, mod3, w_cat)[0]


def _gelu_tanh(x):
    return 0.5 * x * (1.0 + jnp.tanh(0.7978845608028654 * (x + 0.044715 * (x * x * x))))


def _lru_kernel(*refs, row_w, reverse, merge):
    if merge:
        u_ref, wg_ref, bg_ref, lam_ref, h0_ref, hb_ref, gl_ref, out_ref, hfin_ref, carry, h_nat = refs
    else:
        xl_ref, cw_ref, cb_ref, wg_ref, bg_ref, lam_ref, h0_ref, out_ref, u_out_ref, hfin_ref, carry = refs
    t = pl.program_id(1)

    @pl.when(t == 0)
    def _():
        carry[...] = h0_ref[0]

    n_seg = 8
    blk = lambda v, j: v[j * n_seg:(j + 1) * n_seg]
    if merge:
        u = u_ref[...]
        tt, ch = u.shape
        seg = tt // n_seg
    else:
        _, seg, ch = xl_ref.shape
        tt = n_seg * seg
        x = jnp.concatenate([xl_ref[:, j, :] for j in range(seg)], axis=0)
        segs_per_row = row_w // seg
        s_idx = lax.broadcasted_iota(jnp.int32, (n_seg, ch), 0)
        has_prev = (s_idx % segs_per_row) != 0
        has_next = (s_idx % segs_per_row) != segs_per_row - 1
        from_prev = lambda v: jnp.where(has_prev, pltpu.roll(v, 1, 0), 0.0)
        from_next = lambda v: jnp.where(has_next, pltpu.roll(v, n_seg - 1, 0), 0.0)
        x_m1 = jnp.concatenate([from_prev(blk(x, seg - 1)), x[:tt - n_seg]], axis=0)
        x_m2 = jnp.concatenate([from_prev(blk(x, seg - 2)), from_prev(blk(x, seg - 1)), x[:tt - 2 * n_seg]],
                               axis=0)
        x_p1 = jnp.concatenate([x[n_seg:], from_next(blk(x, 0))], axis=0)
        cw = cw_ref[...]
        u = cb_ref[...] + x_m2 * cw[0:1] + x_m1 * cw[1:2] + x * cw[2:3] + x_p1 * cw[3:4]
        u_out_ref[...] = u
    g = jnp.dot(u.astype(bf16), wg_ref[...], preferred_element_type=f32) + bg_ref[...]
    lam = lam_ref[...]
    softplus_neg = jnp.maximum(-lam, 0.0) + jnp.log(1.0 + jnp.exp(-jnp.abs(lam)))
    log_a_scale = -LRU_C * softplus_neg

    def recurrence_terms(j):
        g_j, u_j = blk(g, j), blk(u, j)
        log_a = log_a_scale * jax.nn.sigmoid(g_j[:, 0:ch])
        a_j = jnp.exp(log_a)
        series = log_a * (-2.0 + log_a * (-2.0 + log_a * (-4.0 / 3.0)))
        one_minus_a2 = jnp.where(log_a > -0.005, series, 1.0 - a_j * a_j)
        root = jnp.where(one_minus_a2 > 0.0, one_minus_a2 * lax.rsqrt(one_minus_a2), 0.0)
        return a_j, root * (jax.nn.sigmoid(g_j[:, ch:2 * ch]) * u_j)

    half = seg // 2
    h_loc, a_cum = [None] * seg, [None] * seg
    h_run, a_run = [None, None], [None, None]
    for i in range(half):
        for p in range(2):
            j = p * half + (half - 1 - i if reverse else i)
            a_j, b_j = recurrence_terms(j)
            h_run[p] = b_j if i == 0 else a_j * h_run[p] + b_j
            a_run[p] = a_j if i == 0 else a_j * a_run[p]
            h_loc[j], a_cum[j] = h_run[p], a_run[p]
    pieces = [(s, p) for s in range(n_seg) for p in range(2)]
    if reverse:
        pieces.reverse()
    entering = [[None] * n_seg, [None] * n_seg]
    state = carry[...]
    for s, p in pieces:
        entering[p][s] = state
        state = h_run[p][s:s + 1] + a_run[p][s:s + 1] * state
    carry[...] = state
    hfin_ref[0] = state
    enter = [jnp.concatenate(e, axis=0) for e in entering]
    h_rows = [h_loc[j] + a_cum[j] * enter[j // half] for j in range(seg)]
    if merge:
        for j in range(seg):
            h_nat[:, j, :] = h_rows[j] + blk(hb_ref, j)
        h = h_nat[...].reshape(tt, ch)
        out_ref[...] = (h * _gelu_tanh(gl_ref[...].astype(f32))).astype(out_ref.dtype)
    else:
        out_ref[...] = jnp.concatenate(h_rows, axis=0)


def _lru_pass(x_in, wg, bg, lam, h0, *, n_batch, tile, reverse, conv=None, merge_with=None):
    T = x_in.shape[0]
    nt = T // n_batch // tile
    merge = merge_with is not None
    assert merge != (conv is not None)
    seg = tile // 8

    def tok(b, t):
        return (b * nt + (nt - 1 - t if reverse else t), 0)

    tile_spec = pl.BlockSpec((tile, D_LRU), tok)
    const = lambda b, t: (0, 0)
    gate_specs = [pl.BlockSpec((D_LRU, 2 * D_LRU), const),
                  pl.BlockSpec((1, 2 * D_LRU), const),
                  pl.BlockSpec((1, D_LRU), const),
                  pl.BlockSpec((1, 1, D_LRU), lambda b, t: (b, 0, 0))]
    state_spec = pl.BlockSpec((1, 1, D_LRU), lambda b, t: (b, 0, 0))
    state_shape = jax.ShapeDtypeStruct((n_batch, 1, D_LRU), f32)
    scratch = [pltpu.VMEM((1, D_LRU), f32)]
    if merge:
        hb, gl = merge_with
        row_w = None
        in_specs = [tile_spec] + gate_specs + [tile_spec, tile_spec]
        args = [x_in, wg, bg, lam, h0, hb, gl]
        scratch += [pltpu.VMEM((8, seg, D_LRU), f32)]
        out_specs = [tile_spec, state_spec]
        out_shape = [jax.ShapeDtypeStruct((T, D_LRU), bf16), state_shape]
    else:
        conv_w, conv_b, row_w = conv
        assert row_w % seg == 0 and seg >= 2
        in_specs = [pl.BlockSpec((8, seg, D_LRU), lambda b, t: tok(b, t) + (0,)),
                    pl.BlockSpec((CONV_W, D_LRU), const),
                    pl.BlockSpec((1, D_LRU), const)] + gate_specs
        args = [x_in.reshape(T // seg, seg, D_LRU), conv_w, conv_b, wg, bg, lam, h0]
        out_specs = [tile_spec, tile_spec, state_spec]
        out_shape = [jax.ShapeDtypeStruct((T, D_LRU), f32), jax.ShapeDtypeStruct((T, D_LRU), f32), state_shape]
    return pl.pallas_call(
        functools.partial(_lru_kernel, row_w=row_w, reverse=reverse, merge=merge),
        grid=(n_batch, nt),
        in_specs=in_specs,
        out_specs=out_specs,
        out_shape=out_shape,
        scratch_shapes=scratch,
        compiler_params=_cparams(("arbitrary", "arbitrary")),
        name="lru_merge" if merge else "lru_scan",
    )(*args)


_PG_Q, _PG_K, _PG_V = 0, D_GLA_K, 2 * D_GLA_K
_PG_G, _PG_A = 2 * D_GLA_K + D_GLA_V, 2 * D_GLA_K + 2 * D_GLA_V


def _gla_chunks(streams):
    ck = GLA_CHUNK
    nh = GLA_HEADS
    head_of_lane = lax.broadcasted_iota(jnp.int32, (1, D_GLA_K), 1) // GLA_DK
    nt_dims = (((1,), (1,)), ((), ()))
    tn_dims = (((0,), (0,)), ((), ()))
    ii = lax.broadcasted_iota(jnp.int32, (nh * ck, ck), 0) & (ck - 1)
    jj = lax.broadcasted_iota(jnp.int32, (nh * ck, ck), 1)

    units = []
    for pg_ref, wa_ref, ba_ref, state, reverse in streams:
        n_batch = pg_ref.shape[0]
        rows_all = n_batch * ck
        a_low = jnp.concatenate([pg_ref[bi, :, _PG_A:_PG_A + A_LOW_PAD] for bi in range(n_batch)], axis=0)
        z = jnp.dot(a_low, wa_ref[...], preferred_element_type=f32) + ba_ref[...]
        bcum_all = (jnp.minimum(z, 0.0) - jnp.log(1.0 + jnp.exp(-jnp.abs(z)))) * (1.0 / GATE_TAU)
        step = lax.broadcasted_iota(jnp.int32, (rows_all, D_GLA_K), 0) & (ck - 1)
        d = 1
        while d < ck:
            if reverse:
                bcum_all = bcum_all + jnp.where(step < ck - d, pltpu.roll(bcum_all, rows_all - d, 0), 0.0)
            else:
                bcum_all = bcum_all + jnp.where(step >= d, pltpu.roll(bcum_all, d, 0), 0.0)
            d *= 2
        seen = (jj >= ii) if reverse else (jj <= ii)
        for bi in range(n_batch):
            units.append(dict(pg=pg_ref, bi=bi, state=state, reverse=reverse, seen=seen,
                              bcum=bcum_all[bi * ck:(bi + 1) * ck]))

    for u in units:
        pg_ref, bi, bcum = u["pg"], u["bi"], u["bcum"]
        q = pg_ref[bi, :, _PG_Q:_PG_K].astype(f32) * (GLA_DK ** -0.5)
        k = pg_ref[bi, :, _PG_K:_PG_V].astype(f32)
        u["btot"] = bcum[0:1] if u["reverse"] else bcum[ck - 1:ck]
        q_dec = q * jnp.exp(bcum)
        k_dec = (k * jnp.exp(-bcum)).astype(bf16)
        u["k_end"] = (k * jnp.exp(u["btot"] - bcum)).astype(bf16)
        u["s_t"] = u["state"][bi]
        u["q_blk"] = jnp.concatenate([jnp.where(head_of_lane == hd, q_dec, 0.0) for hd in range(nh)],
                                     axis=0).astype(bf16)
        u["rhs"] = jnp.concatenate([u["s_t"].astype(bf16), k_dec], axis=0)
    for u in units:
        u["qs"] = lax.dot_general(u["q_blk"], u["rhs"], nt_dims, preferred_element_type=f32)
    for u in units:
        u["scores"] = jnp.where(u["seen"], u["qs"][:, GLA_DV:GLA_DV + ck], 0.0).astype(bf16)
    for u in units:
        v = u["pg"][u["bi"], :, _PG_V:_PG_G]
        u["intra"] = jnp.dot(u["scores"], v, preferred_element_type=f32)
        u["kv_full"] = lax.dot_general(v, u["k_end"], tn_dims, preferred_element_type=f32)
    outs = []
    for u in units:
        outs.append([u["intra"][hd * ck:(hd + 1) * ck, hd * GLA_DV:(hd + 1) * GLA_DV]
                     + u["qs"][hd * ck:(hd + 1) * ck, 0:GLA_DV] for hd in range(nh)])
        kv_t = jnp.zeros_like(u["s_t"])
        for hd in range(nh):
            kv_t = jnp.where(head_of_lane == hd, u["kv_full"][hd * GLA_DV:(hd + 1) * GLA_DV], kv_t)
        u["state"][u["bi"]] = u["s_t"] * jnp.exp(u["btot"]) + kv_t
    n_batch = streams[0][0].shape[0]
    return [outs[i * n_batch:(i + 1) * n_batch] for i in range(len(streams))]


def _gla_gate(o_heads, g, gn):
    normed = []
    for oh in o_heads:
        ms = jnp.mean(oh * oh, axis=-1, keepdims=True)
        normed.append(oh * lax.rsqrt(ms + RMS_EPS))
    return jnp.concatenate(normed, axis=-1) * gn * (g * jax.nn.sigmoid(g))


def _gla_kernel(*refs, merge):
    if merge:
        (pgf_ref, pgb_ref, waf_ref, baf_ref, wab_ref, bab_ref, s0f_ref, s0b_ref, gn_ref,
         ylo_ref, yhi_ref, st_f, st_b, keep_f, keep_b) = refs
    else:
        (pgf_ref, pgb_ref, waf_ref, baf_ref, wab_ref, bab_ref, s0f_ref, s0b_ref,
         sfin_f_ref, sfin_b_ref, st_f, st_b) = refs
    n = pl.program_id(0)
    n_chunks = pl.num_programs(0)

    @pl.when(n == 0)
    def _():
        st_f[...] = s0f_ref[...]
        st_b[...] = s0b_ref[...]

    outs_f, outs_b = _gla_chunks([(pgf_ref, waf_ref, baf_ref, st_f, False), (pgb_ref, wab_ref, bab_ref, st_b, True)])
    n_batch = pgf_ref.shape[0]
    heads = range(GLA_HEADS)
    if merge:
        half = n_chunks // 2
        m = n_chunks - 1 - n

        @pl.when(n < half)
        def _():
            for bi in range(n_batch):
                keep_f[n, bi] = jnp.concatenate(outs_f[bi], axis=-1).astype(keep_f.dtype)
                keep_b[m - half, bi] = jnp.concatenate(outs_b[bi], axis=-1).astype(keep_b.dtype)

        @pl.when(n >= half)
        def _():
            for bi in range(n_batch):
                kb = keep_b[n - half, bi].astype(f32)
                o_heads = [outs_f[bi][hd] + kb[:, hd * GLA_DV:(hd + 1) * GLA_DV] for hd in heads]
                yhi_ref[bi] = _gla_gate(o_heads, pgf_ref[bi, :, _PG_G:_PG_A].astype(f32), gn_ref[...])
                kf = keep_f[m, bi].astype(f32)
                o_heads = [kf[:, hd * GLA_DV:(hd + 1) * GLA_DV] + outs_b[bi][hd] for hd in heads]
                ylo_ref[bi] = _gla_gate(o_heads, pgb_ref[bi, :, _PG_G:_PG_A].astype(f32), gn_ref[...])
    else:
        @pl.when(n == n_chunks - 1)
        def _():
            sfin_f_ref[...] = st_f[...]
            sfin_b_ref[...] = st_b[...]


def _gla_pass(pg, wa, ba, s0, gn=None):
    n_batch, n_chunks = pg.shape[0], pg.shape[1]
    merge = gn is not None
    half = n_chunks // 2
    const2 = lambda n: (0, 0)
    const3 = lambda n: (0, 0, 0)
    chunk = lambda width, idx: pl.BlockSpec((n_batch, None, GLA_CHUNK, width), lambda n: (0, idx(n), 0, 0))
    w_specs = [pl.BlockSpec((A_LOW_PAD, D_GLA_K), const2), pl.BlockSpec((1, D_GLA_K), const2)]
    s_spec = pl.BlockSpec((n_batch, GLA_DV, D_GLA_K), const3)
    s_shape = jax.ShapeDtypeStruct((n_batch, GLA_DV, D_GLA_K), f32)
    in_specs = [chunk(D_PG, lambda n: n), chunk(D_PG, lambda n: n_chunks - 1 - n)] + w_specs + w_specs + [s_spec,
                                                                                                            s_spec]
    args = [pg, pg, wa[0], ba[0], wa[1], ba[1], s0[0], s0[1]]
    scratch = [pltpu.VMEM((n_batch, GLA_DV, D_GLA_K), f32)] * 2
    if merge:
        assert n_chunks % 2 == 0
        in_specs.append(pl.BlockSpec((1, D_GLA_V), const2))
        args.append(gn)
        out_specs = [chunk(D_GLA_V, lambda n: jnp.minimum(n_chunks - 1 - n, half - 1)),
                     chunk(D_GLA_V, lambda n: jnp.maximum(n - half, 0))]
        y_shape = jax.ShapeDtypeStruct((n_batch, half, GLA_CHUNK, D_GLA_V), f32)
        out_shape = [y_shape, y_shape]
        scratch += [pltpu.VMEM((half, n_batch, GLA_CHUNK, D_GLA_V), bf16)] * 2
    else:
        out_specs = [s_spec, s_spec]
        out_shape = [s_shape, s_shape]
    return pl.pallas_call(
        functools.partial(_gla_kernel, merge=merge),
        grid=(n_chunks,),
        in_specs=in_specs,
        out_specs=out_specs,
        out_shape=out_shape,
        scratch_shapes=scratch,
        compiler_params=_cparams(("arbitrary",)),
        name="gla_merge" if merge else "gla_scan",
    )(*args)


def _pack_rows(v):
    n = v.shape[1] // 2
    w = lax.bitcast_convert_type(v.astype(bf16).astype(f32), jnp.int32)
    return (w[:, :n] & jnp.int32(-65536)) | lax.shift_right_logical(w[:, n:], 16)


def _unpack_rows(w):
    hi = lax.bitcast_convert_type(w & jnp.int32(-65536), f32)
    lo = lax.bitcast_convert_type(lax.shift_left(w, 16), f32)
    return hi, lo


def _outproj_router_kernel(yl_ref, ygl_ref, ygr_ref, x_ref, mod_ref, wo_ref, lg_ref, lb_ref, wrh_ref, wrl_ref,
                           br_ref, x1_ref, hp_ref, ids_ref, rank_ref, wts_ref, cnt_ref, running):
    i = pl.program_id(0)

    @pl.when(i == 0)
    def _():
        running[...] = jnp.zeros_like(running)

    d = D_MODEL
    g1 = mod_ref[0, :, 2 * d:3 * d]
    sh2 = mod_ref[0, :, 3 * d:4 * d]
    sc2 = mod_ref[0, :, 4 * d:5 * d]
    yg = jnp.concatenate([ref[0, :, r, :] for r in range(ygl_ref.shape[2]) for ref in (ygl_ref, ygr_ref)],
                         axis=0).astype(bf16)
    y = (jnp.dot(yl_ref[...], wo_ref[0:D_LRU, :], preferred_element_type=f32)
         + jnp.dot(yg, wo_ref[D_LRU:2 * D_LRU, :], preferred_element_type=f32))
    z = DEEPNORM_ALPHA * x_ref[...] + g1 * y
    mu = jnp.mean(z, axis=-1, keepdims=True)
    zc = z - mu
    var = jnp.mean(zc * zc, axis=-1, keepdims=True)
    x1 = zc * lax.rsqrt(var + LN_EPS) * lg_ref[...] + lb_ref[...]
    x1_ref[...] = x1
    hmod = x1 * (1.0 + sc2) + sh2
    hp_ref[...] = _pack_rows(hmod)

    nt_dims = (((1,), (1,)), ((), ()))
    h_hi = hmod.astype(bf16)
    h_lo = (hmod - h_hi.astype(f32)).astype(bf16)
    logits = (lax.dot_general(wrh_ref[...], h_hi, nt_dims, preferred_element_type=f32)
              + lax.dot_general(wrh_ref[...], h_lo, nt_dims, preferred_element_type=f32)
              + lax.dot_general(wrl_ref[...], h_hi, nt_dims, preferred_element_type=f32)) + br_ref[...]
    ne, tm = logits.shape
    expert = lax.broadcasted_iota(jnp.int32, (ne, tm), 0).astype(f32)
    neg_inf = jnp.float32(-jnp.inf)
    live = logits
    sel = jnp.zeros((ne, tm), f32)
    ids, vals = [], []
    for _ in range(TOP_K):
        m = jnp.max(live, axis=0, keepdims=True)
        j = jnp.min(jnp.where(live == m, expert, float(ne)), axis=0, keepdims=True)
        pick = expert == j
        sel = jnp.where(pick, 1.0, sel)
        live = jnp.where(pick, neg_inf, live)
        ids.append(j)
        vals.append(m)
    exps = [jnp.exp(vk - vals[0]) for vk in vals]
    denom = exps[0] + exps[1] + exps[2] + exps[3]

    ri = lax.broadcasted_iota(jnp.int32, (tm, tm), 0)
    ci = lax.broadcasted_iota(jnp.int32, (tm, tm), 1)
    earlier = (ri < ci).astype(bf16)
    rank_dense = running[...] + jnp.dot(sel.astype(bf16), earlier, preferred_element_type=f32)
    running[...] = running[...] + jnp.sum(sel, axis=1, keepdims=True)
    cnt_ref[...] = running[...]

    slot = lax.broadcasted_iota(jnp.int32, (8, tm), 0)
    ids_out = jnp.zeros((8, tm), f32)
    rank_out = jnp.zeros((8, tm), f32)
    wts_out = jnp.zeros((8, tm), f32)
    for kk in range(TOP_K):
        rk = jnp.sum(jnp.where(expert == ids[kk], rank_dense, 0.0), axis=0, keepdims=True)
        here = slot == kk
        ids_out = jnp.where(here, ids[kk], ids_out)
        rank_out = jnp.where(here, rk, rank_out)
        wts_out = jnp.where(here, exps[kk] / denom, wts_out)
    ids_ref[...] = ids_out.astype(jnp.int32)
    rank_ref[...] = rank_out.astype(jnp.int32)
    wts_ref[...] = jnp.concatenate([wts_out, jnp.zeros((LANES - 8, tm), f32)], axis=0).T


def _outproj_router(yl, yg, xt, mod3, wo, lg, lb, wr_hi, wr_lo, br, tokens_per_batch):
    T = xt.shape[0]
    tm = TOKEN_TILE
    tiles_per_batch = tokens_per_batch // tm
    tokrow = lambda i: (i, 0)
    const = lambda i: (0, 0)
    half_cols = yg[0].shape[1]
    rows_per_tile = tm // (2 * half_cols)
    yg_spec = pl.BlockSpec((1, half_cols, rows_per_tile, D_GLA_V),
                           lambda i: (i // tiles_per_batch, 0, i % tiles_per_batch, 0))
    return pl.pallas_call(
        _outproj_router_kernel,
        grid=(T // tm,),
        in_specs=[pl.BlockSpec((tm, D_LRU), tokrow),
                  yg_spec, yg_spec,
                  pl.BlockSpec((tm, D_MODEL), tokrow),
                  pl.BlockSpec((1, 1, N_MOD * D_MODEL), lambda i: (i // tiles_per_batch, 0, 0)),
                  pl.BlockSpec((D_MODEL, D_MODEL), const),
                  pl.BlockSpec((1, D_MODEL), const),
                  pl.BlockSpec((1, D_MODEL), const),
                  pl.BlockSpec((N_EXPERTS, D_MODEL), const),
                  pl.BlockSpec((N_EXPERTS, D_MODEL), const),
                  pl.BlockSpec((N_EXPERTS, 1), const)],
        out_specs=[pl.BlockSpec((tm, D_MODEL), tokrow),
                   pl.BlockSpec((tm, D_MODEL // 2), tokrow),
                   pl.BlockSpec((8, tm), lambda i: (0, i)),
                   pl.BlockSpec((8, tm), lambda i: (0, i)),
                   pl.BlockSpec((tm, LANES), tokrow),
                   pl.BlockSpec((N_EXPERTS, 1), const)],
        out_shape=[jax.ShapeDtypeStruct((T, D_MODEL), f32),
                   jax.ShapeDtypeStruct((T, D_MODEL // 2), jnp.int32),
                   jax.ShapeDtypeStruct((8, T), jnp.int32),
                   jax.ShapeDtypeStruct((8, T), jnp.int32),
                   jax.ShapeDtypeStruct((T, LANES), f32),
                   jax.ShapeDtypeStruct((N_EXPERTS, 1), f32)],
        scratch_shapes=[pltpu.VMEM((N_EXPERTS, 1), f32)],
        compiler_params=_cparams(("arbitrary",)),
        name="outproj_router",
    )(yl, yg[0], yg[1], xt, mod3, wo, lg, lb, wr_hi, wr_lo, br)


def _route_kernel(ids_ref, rank_ref, cnt_ref, dest_ref, tiles_ref, *, bm):
    ne = cnt_ref.shape[0]
    cnt = cnt_ref[...]
    padded = jnp.floor((cnt + (bm - 1.0)) * (1.0 / bm)) * bm
    sub = lax.broadcasted_iota(jnp.int32, (ne, LANES), 0)
    lane = lax.broadcasted_iota(jnp.int32, (ne, LANES), 1)
    padded_row = jnp.sum(jnp.where(sub == lane, padded, 0.0), axis=0, keepdims=True)
    base = jnp.sum(jnp.where(lane < sub, padded_row, 0.0), axis=1, keepdims=True)
    ends = base + padded

    tc = ids_ref.shape[1]
    expert = lax.broadcasted_iota(jnp.int32, (ne, tc), 0)
    slot = lax.broadcasted_iota(jnp.int32, (8, tc), 0)
    ids = ids_ref[...]
    group_base = jnp.zeros((8, tc), f32)
    for kk in range(TOP_K):
        b_k = jnp.sum(jnp.where(expert == ids[kk:kk + 1], base, 0.0), axis=0, keepdims=True)
        group_base = jnp.where(slot == kk, b_k, group_base)
    dest_ref[...] = group_base.astype(jnp.int32) + rank_ref[...]

    nl = tiles_ref.shape[1]
    start = lax.broadcasted_iota(jnp.int32, (ne, nl), 1).astype(f32) * bm
    te = jnp.minimum(jnp.sum(jnp.where(start >= ends, 1.0, 0.0), axis=0, keepdims=True), ne - 1.0)
    at_te = lax.broadcasted_iota(jnp.int32, (ne, nl), 0).astype(f32) == te
    cnt_te = jnp.sum(jnp.where(at_te, cnt, 0.0), axis=0, keepdims=True)
    base_te = jnp.sum(jnp.where(at_te, base, 0.0), axis=0, keepdims=True)
    valid = jnp.clip(cnt_te - (start[0:1] - base_te), 0.0, float(bm))
    next_group = jnp.sum(jnp.where(at_te, ends, 0.0), axis=0, keepdims=True) * (1.0 / bm)
    last_used = jnp.sum(padded, axis=0, keepdims=True) * (1.0 / bm) - 1.0
    srow = lax.broadcasted_iota(jnp.int32, (8, nl), 0)
    table = jnp.where(srow == 0, te, jnp.where(srow == 1, valid, jnp.where(srow == 2, next_group,
                                                                           jnp.where(srow == 3, last_used, 0.0))))
    tiles_ref[...] = table.astype(jnp.int32)


def _route(ids, rank, cnt, bm, n_tiles):
    T = ids.shape[1]
    tc = 2048
    nl = -(-n_tiles // LANES) * LANES
    tok = lambda i: (0, i)
    const = lambda i: (0, 0)
    return pl.pallas_call(
        functools.partial(_route_kernel, bm=bm),
        grid=(T // tc,),
        in_specs=[pl.BlockSpec((8, tc), tok), pl.BlockSpec((8, tc), tok),
                  pl.BlockSpec((N_EXPERTS, 1), const)],
        out_specs=[pl.BlockSpec((8, tc), tok), pl.BlockSpec((8, nl), const)],
        out_shape=[jax.ShapeDtypeStruct((8, T), jnp.int32), jax.ShapeDtypeStruct((8, nl), jnp.int32)],
        compiler_params=_cparams(("arbitrary",)),
        name="route",
    )(ids, rank, cnt)


def _sc_workers():
    info = plsc.get_sparse_core_info()
    return info.num_cores, info.num_subcores


def _sc_dispatch(rows, dest_flat, n_out):
    T, D = rows.shape
    nc, ns = _sc_workers()
    per_w = T // (nc * ns)
    n_chunks = per_w // SC_CHUNK
    mesh = plsc.VectorSubcoreMesh(core_axis_name="c", subcore_axis_name="s")

    @functools.partial(
        pl.kernel, mesh=mesh,
        out_type=jax.ShapeDtypeStruct((n_out, D), rows.dtype),
        scratch_types=([pltpu.VMEM((SC_CHUNK,), jnp.int32)] * TOP_K
                       + [pltpu.VMEM((SC_CHUNK, D), rows.dtype)]
                       + [pltpu.SemaphoreType.DMA] * TOP_K),
    )
    def k(rows_hbm, dest_hbm, out_hbm, *scratch):
        idx_v = scratch[:TOP_K]
        rows_v = scratch[TOP_K]
        sems = scratch[TOP_K + 1:]
        wid = lax.axis_index("s") * nc + lax.axis_index("c")
        base = wid * per_w

        @pl.loop(0, n_chunks)
        def _(j):
            off = base + j * SC_CHUNK
            pltpu.sync_copy(rows_hbm.at[pl.ds(off, SC_CHUNK)], rows_v)
            for kk in range(TOP_K):
                pltpu.sync_copy(dest_hbm.at[pl.ds(kk * T + off, SC_CHUNK)], idx_v[kk])
            copies = [pltpu.async_copy(rows_v, out_hbm.at[idx_v[kk]], sems[kk]) for kk in range(TOP_K)]
            for cp in copies:
                cp.wait()

    return k(rows, dest_flat)


def _sc_gather(table, idx):
    _, D = table.shape
    N = idx.shape[0]
    nc, ns = _sc_workers()
    per_w = N // (nc * ns)
    n_chunks = per_w // SC_CHUNK
    assert n_chunks % 2 == 0
    mesh = plsc.VectorSubcoreMesh(core_axis_name="c", subcore_axis_name="s")

    @functools.partial(
        pl.kernel, mesh=mesh,
        out_type=jax.ShapeDtypeStruct((N, D), table.dtype),
        scratch_types=([pltpu.VMEM((SC_CHUNK,), jnp.int32)] * 2
                       + [pltpu.VMEM((SC_CHUNK, D), table.dtype)] * 2
                       + [pltpu.SemaphoreType.DMA] * 2),
    )
    def k(table_hbm, idx_hbm, out_hbm, idx0, idx1, buf0, buf1, sem0, sem1):
        idxs, bufs, sems = (idx0, idx1), (buf0, buf1), (sem0, sem1)
        wid = lax.axis_index("s") * nc + lax.axis_index("c")
        base = wid * per_w

        def gather(b):
            return pltpu.make_async_copy(table_hbm.at[idxs[b]], bufs[b], sems[b])

        def start(chunk, b):
            pltpu.sync_copy(idx_hbm.at[pl.ds(base + chunk * SC_CHUNK, SC_CHUNK)], idxs[b])
            gather(b).start()

        def finish(chunk, b):
            gather(b).wait()
            pltpu.sync_copy(bufs[b], out_hbm.at[pl.ds(base + chunk * SC_CHUNK, SC_CHUNK)])

        start(0, 0)

        @pl.loop(0, n_chunks, step=2)
        def _(j):
            start(j + 1, 1)
            finish(j, 0)

            @pl.when(j + 2 < n_chunks)
            def _():
                start(j + 2, 0)

            finish(j + 1, 1)

    return k(table, idx)


def _ffn_kernel(te_ref, tv_ref, tn_ref, tl_ref, xs_ref, wg_hbm, bg_ref, wu_hbm, bu_ref, wd_hbm, bd_ref, eo_ref,
                w_stage, wg_b, wu_b, wd_b, sems):
    i = pl.program_id(0)
    n_tiles = pl.num_programs(0)
    w_hbm = (wg_hbm, wu_hbm, wd_hbm)
    w_b = (wg_b, wu_b, wd_b)

    def fetch(e):
        return [pltpu.make_async_copy(w_hbm[m].at[e], w_stage.at[m], sems.at[m]) for m in range(3)]

    @pl.when(i == 0)
    def _():
        for cp in fetch(te_ref[0]):
            cp.start()

    prev = te_ref[jnp.maximum(i - 1, 0)]

    @pl.when((i == 0) | (te_ref[i] != prev))
    def _():
        for m, cp in enumerate(fetch(te_ref[i])):
            cp.wait()
            w_b[m][...] = w_stage[m].astype(bf16)
        nxt = tn_ref[i]
        e_nxt = te_ref[jnp.minimum(nxt, n_tiles - 1)]

        @pl.when((nxt > i) & (nxt < n_tiles) & (e_nxt != te_ref[i]))
        def _():
            for cp in fetch(e_nxt):
                cp.start()

    valid = tv_ref[i]

    def ffn_rows(r0, m):
        row = lax.broadcasted_iota(jnp.int32, (m, 1), 0) + r0
        xw = jnp.where(row < valid, xs_ref[r0:r0 + m, :], 0)
        x_hi, x_lo = _unpack_rows(xw)
        x = jnp.concatenate([x_hi, x_lo], axis=1).astype(bf16)
        gate = jnp.minimum(jnp.dot(x, wg_b[...], preferred_element_type=f32) + bg_ref[...], SWIGLU_LIMIT)
        up = jnp.clip(jnp.dot(x, wu_b[...], preferred_element_type=f32) + bu_ref[...],
                      -SWIGLU_LIMIT, SWIGLU_LIMIT)
        act = (up + 1.0) * gate * jax.nn.sigmoid(SWIGLU_ALPHA * gate)
        out = jnp.dot(act.astype(bf16), wd_b[...], preferred_element_type=f32) + bd_ref[...]
        eo_ref[r0:r0 + m, :] = _pack_rows(out)

    def zero_rows(r0, m):
        eo_ref[r0:r0 + m, :] = jnp.zeros((m, eo_ref.shape[1]), eo_ref.dtype)

    for r0 in range(0, xs_ref.shape[0], MOE_PASS):
        lo = 0
        for m in MOE_PASS_SIZES:
            @pl.when((valid > r0 + lo) & ((valid <= r0 + m) | (m == MOE_PASS)))
            def _(r0=r0, m=m):
                ffn_rows(r0, m)
                if m < MOE_PASS:
                    zero_rows(r0 + m, MOE_PASS - m)
            lo = m

        @pl.when(valid <= r0)
        def _(r0=r0):
            zero_rows(r0, MOE_PASS)


def _expert_ffn(tile_expert, tile_valid, tile_next, last_used, xs, w_gate, b_gate, w_up, b_up, w_down, b_down):
    n_rows, dp = xs.shape
    d = 2 * dp
    bm = MOE_TILE
    d_e = w_gate.shape[-1]
    assert d == d_e
    bspec = lambda n_: pl.BlockSpec((None, 1, n_), lambda i, te, tv, tn, tl: (te[i], 0, 0))
    hbm = pl.BlockSpec(memory_space=pl.ANY)
    grid_spec = pltpu.PrefetchScalarGridSpec(
        num_scalar_prefetch=4,
        grid=(n_rows // bm,),
        in_specs=[pl.BlockSpec((bm, dp), lambda i, te, tv, tn, tl: (jnp.minimum(i, tl[0]), 0)),
                  hbm, bspec(d_e), hbm, bspec(d_e), hbm, bspec(d)],
        out_specs=pl.BlockSpec((bm, dp), lambda i, te, tv, tn, tl: (i, 0)),
        scratch_shapes=[pltpu.VMEM((3, d, d_e), f32),
                        pltpu.VMEM((d, d_e), bf16), pltpu.VMEM((d, d_e), bf16), pltpu.VMEM((d_e, d), bf16),
                        pltpu.SemaphoreType.DMA((3,))],
    )
    return pl.pallas_call(
        _ffn_kernel,
        grid_spec=grid_spec,
        out_shape=jax.ShapeDtypeStruct((n_rows, dp), jnp.int32),
        compiler_params=_cparams(("arbitrary",)),
        name="expert_ffn",
    )(tile_expert, tile_valid, tile_next, last_used, xs, w_gate, b_gate.reshape(N_EXPERTS, 1, d_e), w_up,
      b_up.reshape(N_EXPERTS, 1, d_e), w_down, b_down.reshape(N_EXPERTS, 1, d))


def _combine_kernel(eg_ref, wts_ref, x1_ref, mod_ref, lg_ref, lb_ref, o_ref):
    w = wts_ref[...]
    y_hi, y_lo = _unpack_rows(eg_ref[0])
    y_hi, y_lo = y_hi * w[:, 0:1], y_lo * w[:, 0:1]
    for kk in range(1, TOP_K):
        e_hi, e_lo = _unpack_rows(eg_ref[kk])
        y_hi = y_hi + e_hi * w[:, kk:kk + 1]
        y_lo = y_lo + e_lo * w[:, kk:kk + 1]
    y = jnp.concatenate([y_hi, y_lo], axis=1)
    z = DEEPNORM_ALPHA * x1_ref[...] + mod_ref[0] * y
    mu = jnp.mean(z, axis=-1, keepdims=True)
    zc = z - mu
    var = jnp.mean(zc * zc, axis=-1, keepdims=True)
    o_ref[...] = zc * lax.rsqrt(var + LN_EPS) * lg_ref[...] + lb_ref[...]


def _combine_ln(eg, wts, x1, mod3, lg, lb, tokens_per_batch):
    T = x1.shape[0]
    tm = TOKEN_TILE
    tiles_per_batch = tokens_per_batch // tm
    const = lambda i: (0, 0)
    return pl.pallas_call(
        _combine_kernel,
        grid=(T // tm,),
        in_specs=[pl.BlockSpec((TOP_K, tm, D_MODEL // 2), lambda i: (0, i, 0)),
                  pl.BlockSpec((tm, LANES), lambda i: (i, 0)),
                  pl.BlockSpec((tm, D_MODEL), lambda i: (i, 0)),
                  pl.BlockSpec((1, 1, D_MODEL), lambda i: (i // tiles_per_batch, 0, N_MOD - 1)),
                  pl.BlockSpec((1, D_MODEL), const),
                  pl.BlockSpec((1, D_MODEL), const)],
        out_specs=pl.BlockSpec((tm, D_MODEL), lambda i: (i, 0)),
        out_shape=jax.ShapeDtypeStruct((T, D_MODEL), f32),
        compiler_params=_cparams(("arbitrary",)),
        name="combine_ln",
    )(eg, wts, x1, mod3, lg, lb)


def _block_diag(w):
    n, c, d = w.shape
    eye = jnp.eye(n, dtype=w.dtype)
    return jnp.einsum('ncd,nm->ncmd', w, eye).reshape(n * c, n * d)


def kernel(x, c, ctx, c_ctx, w_ada, b_ada, w_in, conv_w, conv_b, lru_wa, lru_ba, lru_wx, lru_bx,
           lru_lam, gla_wa, gla_ba, gla_norm_g, w_out, ln1_g, ln1_b, w_router, b_router, w_gate,
           b_gate, w_up, b_up, w_down, b_down, ln2_g, ln2_b):
    B, L, D = x.shape
    Lc = ctx.shape[1]
    T = B * L
    rows = L // GRID_W
    l = 0

    cpad = jnp.zeros((8, D), f32).at[0:B].set(c).at[B].set(c_ctx)
    w_cat = jnp.pad(w_in[l], ((0, 0), (0, D_PROJ - w_in.shape[-1]))).astype(bf16)
    wg = [jnp.concatenate([_block_diag(lru_wa[l, d]), _block_diag(lru_wx[l, d])], axis=1).astype(bf16)
          for d in range(2)]
    bg = [jnp.concatenate([lru_ba[l, d], lru_bx[l, d]])[None] for d in range(2)]
    lam = [lru_lam[l, d][None] for d in range(2)]
    wa = [jnp.pad(gla_wa[l, d], ((0, A_LOW_PAD - GATE_RANK), (0, 0))).astype(bf16) for d in range(2)]
    ba = [gla_ba[l, d][None] for d in range(2)]
    cw, cb = conv_w[l], conv_b[l][None]
    wr_t = w_router[l].T
    wr_hi = wr_t.astype(bf16)
    wr_lo = (wr_t - wr_hi.astype(f32)).astype(bf16)
    br = b_router[l][:, None]
    c_xl, c_gl, c_pg = (0, D_LRU), (D_LRU, 2 * D_LRU), (2 * D_LRU, D_PROJ)

    mod3 = _ada_mod(cpad, w_ada[l], b_ada[l]).reshape(8, 1, N_MOD * D)

    tiles_per_batch = L // TOKEN_TILE
    xl_c, pg_c = _inproj_rows(ctx.reshape(B * Lc, D), mod3, w_cat, lambda i: B, (c_xl, c_pg), (f32, bf16))
    zero_h = jnp.zeros((B, 1, D_LRU), f32)
    zero_s = jnp.zeros((B, GLA_DV, D_GLA_K), f32)
    pg_c = pg_c.reshape(B, Lc // GLA_CHUNK, GLA_CHUNK, D_PG)
    h_ctx = []
    for d in range(2):
        _, _, hf = _lru_pass(xl_c, wg[d], bg[d], lam[d], zero_h, n_batch=B, tile=Lc, reverse=bool(d),
                             conv=(cw, cb, Lc))
        h_ctx.append(hf)
    s_ctx = _gla_pass(pg_c, wa, ba, (zero_s, zero_s))

    xt = x.reshape(T, D)
    xl, gl = _inproj_rows(xt, mod3, w_cat, lambda i: i // tiles_per_batch, (c_xl, c_gl), (f32, bf16))
    hb, u_lru, _ = _lru_pass(xl, wg[1], bg[1], lam[1], h_ctx[1], n_batch=B, tile=LRU_TILE, reverse=True,
                             conv=(cw, cb, GRID_W))
    y_lru, _ = _lru_pass(u_lru, wg[0], bg[0], lam[0], h_ctx[0], n_batch=B, tile=LRU_TILE, reverse=False,
                         merge_with=(hb, gl))
    pg = _inproj_cols(x.reshape(B, rows, GRID_W, D), mod3, w_cat, c_pg)
    y_gla = _gla_pass(pg, wa, ba, s_ctx, gn=gla_norm_g[l][None])

    x1, hp, ids, rank, wts, cnt = _outproj_router(
        y_lru, y_gla, xt, mod3, w_out[l].astype(bf16), ln1_g[l][None], ln1_b[l][None], wr_hi, wr_lo, br, L)

    n_tiles = T * TOP_K // MOE_TILE + N_EXPERTS
    dest, tiles = _route(ids, rank, cnt, MOE_TILE, n_tiles)
    xs = _sc_dispatch(hp, dest.reshape(-1), n_tiles * MOE_TILE)
    eo = _expert_ffn(tiles[0, :n_tiles], tiles[1, :n_tiles], tiles[2, :n_tiles], tiles[3, :1], xs, w_gate[l],
                     b_gate[l], w_up[l], b_up[l], w_down[l], b_down[l])
    eg = _sc_gather(eo, dest[:TOP_K].reshape(-1)).reshape(TOP_K, T, D // 2)
    out = _combine_ln(eg, wts, x1, mod3, ln2_g[l][None], ln2_b[l][None], L)
    return out.reshape(B, L, D)
```

```python
import functools

import jax
import jax.numpy as jnp
from jax import lax
from jax.experimental import pallas as pl
from jax.experimental.pallas import tpu as pltpu
from jax.experimental.pallas import tpu_sc as plsc

D_MODEL = 1024
DEPTH = 1
GRID_W = 64
D_LRU = 512
LRU_BLOCKS = 8
CONV_W = 4
LRU_C = 8.0
GLA_HEADS = 4
D_GLA_V = 512
D_GLA_K = 256
GLA_DK = 64
GLA_DV = 128
GATE_RANK = 16
GATE_TAU = 16.0
GLA_CHUNK = 64
N_EXPERTS = 32
TOP_K = 4
SWIGLU_LIMIT = 7.0
SWIGLU_ALPHA = 1.702
N_MOD = 6
DEEPNORM_ALPHA = (2.0 * DEPTH) ** 0.25
LN_EPS = 1e-5
RMS_EPS = 1e-6

LANES = 128
A_LOW_PAD = LANES
D_PG = 2 * D_GLA_K + 2 * D_GLA_V + A_LOW_PAD
D_PROJ = 2 * D_LRU + D_PG
TOKEN_TILE = 512
LRU_TILE = 512
MOE_TILE = 1024
MOE_PASS = 512
MOE_PASS_SIZES = (128, 256, MOE_PASS)
MOE_BLOCK = 256
OUTPROJ_PARTS = 1
SC_CHUNK = 64
VMEM_LIMIT = 48 * 1024 * 1024

f32 = jnp.float32
bf16 = jnp.bfloat16


def _cparams(sem):
    return pltpu.CompilerParams(dimension_semantics=sem, vmem_limit_bytes=VMEM_LIMIT)


def _ada_kernel(c_ref, w_ref, b_ref, o_ref):
    s = c_ref[...]
    s = s * jax.nn.sigmoid(s)
    o_ref[...] = jnp.dot(s.astype(bf16), w_ref[...].astype(bf16), preferred_element_type=f32) + b_ref[...]


def _ada_mod(cpad, w, b):
    n = w.shape[1]
    tn = 1024
    return pl.pallas_call(
        _ada_kernel,
        grid=(n // tn,),
        in_specs=[pl.BlockSpec((8, D_MODEL), lambda j: (0, 0)),
                  pl.BlockSpec((D_MODEL, tn), lambda j: (0, j)),
                  pl.BlockSpec((1, tn), lambda j: (0, j))],
        out_specs=pl.BlockSpec((8, tn), lambda j: (0, j)),
        out_shape=jax.ShapeDtypeStruct((8, n), f32),
        compiler_params=_cparams(("arbitrary",)),
        name="ada_mod",
    )(cpad, w, b.reshape(1, n))


def _inproj_kernel(x_ref, mod_ref, w_ref, *out_refs, parts, col_major):
    sh = mod_ref[0, :, 0:D_MODEL]
    sc = mod_ref[0, :, D_MODEL:2 * D_MODEL]
    if col_major:
        x = jnp.concatenate([x_ref[0, :, j, :] for j in range(x_ref.shape[2])], axis=0)
    else:
        x = x_ref[...]
    u = (x * (1.0 + sc) + sh).astype(bf16)
    for (lo, hi), o_ref in zip(parts, out_refs):
        p = jnp.dot(u, w_ref[:, lo:hi], preferred_element_type=f32).astype(o_ref.dtype)
        o_ref[...] = p.reshape(o_ref.shape)


def _inproj_rows(xt, mod3, w_cat, batch_of_tile, parts, dtypes):
    T = xt.shape[0]
    tm = TOKEN_TILE
    return pl.pallas_call(
        functools.partial(_inproj_kernel, parts=parts, col_major=False),
        grid=(T // tm,),
        in_specs=[pl.BlockSpec((tm, D_MODEL), lambda i: (i, 0)),
                  pl.BlockSpec((1, 1, 2 * D_MODEL), lambda i: (batch_of_tile(i), 0, 0)),
                  pl.BlockSpec((D_MODEL, D_PROJ), lambda i: (0, 0))],
        out_specs=[pl.BlockSpec((tm, hi - lo), lambda i: (i, 0)) for lo, hi in parts],
        out_shape=[jax.ShapeDtypeStruct((T, hi - lo), dt) for (lo, hi), dt in zip(parts, dtypes)],
        compiler_params=_cparams(("arbitrary",)),
        name="inproj_rows",
    )(xt, mod3, w_cat)


def _inproj_cols(x4, mod3, w_cat, part):
    n_batch, rows, cols, _ = x4.shape
    lo, hi = part
    cb = TOKEN_TILE // rows
    return pl.pallas_call(
        functools.partial(_inproj_kernel, parts=(part,), col_major=True),
        grid=(n_batch, cols // cb),
        in_specs=[pl.BlockSpec((1, rows, cb, D_MODEL), lambda b, n: (b, 0, n, 0)),
                  pl.BlockSpec((1, 1, 2 * D_MODEL), lambda b, n: (b, 0, 0)),
                  pl.BlockSpec((D_MODEL, D_PROJ), lambda b, n: (0, 0))],
        out_specs=[pl.BlockSpec((1, cb, rows, hi - lo), lambda b, n: (b, n, 0, 0))],
        out_shape=[jax.ShapeDtypeStruct((n_batch, cols, rows, hi - lo), bf16)],
        compiler_params=_cparams(("arbitrary", "arbitrary")),
        name="inproj_cols",
    )(x4, mod3, w_cat)[0]


def _gelu_tanh(x):
    return 0.5 * x * (1.0 + jnp.tanh(0.7978845608028654 * (x + 0.044715 * (x * x * x))))


def _lru_kernel(*refs, row_w, reverse, merge):
    if merge:
        u_ref, wg_ref, bg_ref, lam_ref, h0_ref, hb_ref, gl_ref, out_ref, hfin_ref, carry, h_nat = refs
    else:
        xl_ref, cw_ref, cb_ref, wg_ref, bg_ref, lam_ref, h0_ref, out_ref, u_out_ref, hfin_ref, carry = refs
    t = pl.program_id(1)

    @pl.when(t == 0)
    def _():
        carry[...] = h0_ref[0]

    n_seg = 8
    blk = lambda v, j: v[j * n_seg:(j + 1) * n_seg]
    if merge:
        u = u_ref[...]
        tt, ch = u.shape
        seg = tt // n_seg
    else:
        _, seg, ch = xl_ref.shape
        tt = n_seg * seg
        x = jnp.concatenate([xl_ref[:, j, :] for j in range(seg)], axis=0)
        segs_per_row = row_w // seg
        s_idx = lax.broadcasted_iota(jnp.int32, (n_seg, ch), 0)
        has_prev = (s_idx % segs_per_row) != 0
        has_next = (s_idx % segs_per_row) != segs_per_row - 1
        from_prev = lambda v: jnp.where(has_prev, pltpu.roll(v, 1, 0), 0.0)
        from_next = lambda v: jnp.where(has_next, pltpu.roll(v, n_seg - 1, 0), 0.0)
        x_m1 = jnp.concatenate([from_prev(blk(x, seg - 1)), x[:tt - n_seg]], axis=0)
        x_m2 = jnp.concatenate([from_prev(blk(x, seg - 2)), from_prev(blk(x, seg - 1)), x[:tt - 2 * n_seg]],
                               axis=0)
        x_p1 = jnp.concatenate([x[n_seg:], from_next(blk(x, 0))], axis=0)
        cw = cw_ref[...]
        u = cb_ref[...] + x_m2 * cw[0:1] + x_m1 * cw[1:2] + x * cw[2:3] + x_p1 * cw[3:4]
        u_out_ref[...] = u
    g = jnp.dot(u.astype(bf16), wg_ref[...], preferred_element_type=f32) + bg_ref[...]
    lam = lam_ref[...]
    softplus_neg = jnp.maximum(-lam, 0.0) + jnp.log(1.0 + jnp.exp(-jnp.abs(lam)))
    log_a_scale = -LRU_C * softplus_neg

    def recurrence_terms(j):
        g_j, u_j = blk(g, j), blk(u, j)
        log_a = log_a_scale * jax.nn.sigmoid(g_j[:, 0:ch])
        a_j = jnp.exp(log_a)
        series = log_a * (-2.0 + log_a * (-2.0 + log_a * (-4.0 / 3.0)))
        one_minus_a2 = jnp.where(log_a > -0.005, series, 1.0 - a_j * a_j)
        root = jnp.where(one_minus_a2 > 0.0, one_minus_a2 * lax.rsqrt(one_minus_a2), 0.0)
        return a_j, root * (jax.nn.sigmoid(g_j[:, ch:2 * ch]) * u_j)

    half = seg // 2
    h_loc, a_cum = [None] * seg, [None] * seg
    h_run, a_run = [None, None], [None, None]
    for i in range(half):
        for p in range(2):
            j = p * half + (half - 1 - i if reverse else i)
            a_j, b_j = recurrence_terms(j)
            h_run[p] = b_j if i == 0 else a_j * h_run[p] + b_j
            a_run[p] = a_j if i == 0 else a_j * a_run[p]
            h_loc[j], a_cum[j] = h_run[p], a_run[p]
    pieces = [(s, p) for s in range(n_seg) for p in range(2)]
    if reverse:
        pieces.reverse()
    entering = [[None] * n_seg, [None] * n_seg]
    state = carry[...]
    for s, p in pieces:
        entering[p][s] = state
        state = h_run[p][s:s + 1] + a_run[p][s:s + 1] * state
    carry[...] = state
    hfin_ref[0] = state
    enter = [jnp.concatenate(e, axis=0) for e in entering]
    h_rows = [h_loc[j] + a_cum[j] * enter[j // half] for j in range(seg)]
    if merge:
        for j in range(seg):
            h_nat[:, j, :] = h_rows[j] + blk(hb_ref, j)
        h = h_nat[...].reshape(tt, ch)
        out_ref[...] = (h * _gelu_tanh(gl_ref[...].astype(f32))).astype(out_ref.dtype)
    else:
        out_ref[...] = jnp.concatenate(h_rows, axis=0)


def _lru_pass(x_in, wg, bg, lam, h0, *, n_batch, tile, reverse, conv=None, merge_with=None):
    T = x_in.shape[0]
    nt = T // n_batch // tile
    merge = merge_with is not None
    assert merge != (conv is not None)
    seg = tile // 8

    def tok(b, t):
        return (b * nt + (nt - 1 - t if reverse else t), 0)

    tile_spec = pl.BlockSpec((tile, D_LRU), tok)
    const = lambda b, t: (0, 0)
    gate_specs = [pl.BlockSpec((D_LRU, 2 * D_LRU), const),
                  pl.BlockSpec((1, 2 * D_LRU), const),
                  pl.BlockSpec((1, D_LRU), const),
                  pl.BlockSpec((1, 1, D_LRU), lambda b, t: (b, 0, 0))]
    state_spec = pl.BlockSpec((1, 1, D_LRU), lambda b, t: (b, 0, 0))
    state_shape = jax.ShapeDtypeStruct((n_batch, 1, D_LRU), f32)
    scratch = [pltpu.VMEM((1, D_LRU), f32)]
    if merge:
        hb, gl = merge_with
        row_w = None
        in_specs = [tile_spec] + gate_specs + [tile_spec, tile_spec]
        args = [x_in, wg, bg, lam, h0, hb, gl]
        scratch += [pltpu.VMEM((8, seg, D_LRU), f32)]
        out_specs = [tile_spec, state_spec]
        out_shape = [jax.ShapeDtypeStruct((T, D_LRU), bf16), state_shape]
    else:
        conv_w, conv_b, row_w = conv
        assert row_w % seg == 0 and seg >= 2
        in_specs = [pl.BlockSpec((8, seg, D_LRU), lambda b, t: tok(b, t) + (0,)),
                    pl.BlockSpec((CONV_W, D_LRU), const),
                    pl.BlockSpec((1, D_LRU), const)] + gate_specs
        args = [x_in.reshape(T // seg, seg, D_LRU), conv_w, conv_b, wg, bg, lam, h0]
        out_specs = [tile_spec, tile_spec, state_spec]
        out_shape = [jax.ShapeDtypeStruct((T, D_LRU), f32), jax.ShapeDtypeStruct((T, D_LRU), f32), state_shape]
    return pl.pallas_call(
        functools.partial(_lru_kernel, row_w=row_w, reverse=reverse, merge=merge),
        grid=(n_batch, nt),
        in_specs=in_specs,
        out_specs=out_specs,
        out_shape=out_shape,
        scratch_shapes=scratch,
        compiler_params=_cparams(("arbitrary", "arbitrary")),
        name="lru_merge" if merge else "lru_scan",
    )(*args)


_PG_Q, _PG_K, _PG_V = 0, D_GLA_K, 2 * D_GLA_K
_PG_G, _PG_A = 2 * D_GLA_K + D_GLA_V, 2 * D_GLA_K + 2 * D_GLA_V


def _gla_chunks(streams):
    ck = GLA_CHUNK
    nh = GLA_HEADS
    head_of_lane = lax.broadcasted_iota(jnp.int32, (1, D_GLA_K), 1) // GLA_DK
    nt_dims = (((1,), (1,)), ((), ()))
    tn_dims = (((0,), (0,)), ((), ()))
    ii = lax.broadcasted_iota(jnp.int32, (nh * ck, ck), 0) & (ck - 1)
    jj = lax.broadcasted_iota(jnp.int32, (nh * ck, ck), 1)

    units = []
    for pg_ref, wa_ref, ba_ref, state, reverse in streams:
        n_batch = pg_ref.shape[0]
        rows_all = n_batch * ck
        a_low = jnp.concatenate([pg_ref[bi, :, _PG_A:_PG_A + A_LOW_PAD] for bi in range(n_batch)], axis=0)
        z = jnp.dot(a_low, wa_ref[...], preferred_element_type=f32) + ba_ref[...]
        bcum_all = (jnp.minimum(z, 0.0) - jnp.log(1.0 + jnp.exp(-jnp.abs(z)))) * (1.0 / GATE_TAU)
        step = lax.broadcasted_iota(jnp.int32, (rows_all, D_GLA_K), 0) & (ck - 1)
        d = 1
        while d < ck:
            if reverse:
                bcum_all = bcum_all + jnp.where(step < ck - d, pltpu.roll(bcum_all, rows_all - d, 0), 0.0)
            else:
                bcum_all = bcum_all + jnp.where(step >= d, pltpu.roll(bcum_all, d, 0), 0.0)
            d *= 2
        seen = (jj >= ii) if reverse else (jj <= ii)
        for bi in range(n_batch):
            units.append(dict(pg=pg_ref, bi=bi, state=state, reverse=reverse, seen=seen,
                              bcum=bcum_all[bi * ck:(bi + 1) * ck]))

    for u in units:
        pg_ref, bi, bcum = u["pg"], u["bi"], u["bcum"]
        q = pg_ref[bi, :, _PG_Q:_PG_K].astype(f32) * (GLA_DK ** -0.5)
        k = pg_ref[bi, :, _PG_K:_PG_V].astype(f32)
        u["btot"] = bcum[0:1] if u["reverse"] else bcum[ck - 1:ck]
        q_dec = q * jnp.exp(bcum)
        k_dec = (k * jnp.exp(-bcum)).astype(bf16)
        k_end = k * jnp.exp(u["btot"] - bcum)
        u["s_t"] = u["state"][bi]
        by_head = lambda t: jnp.concatenate([jnp.where(head_of_lane == hd, t, 0.0) for hd in range(nh)],
                                            axis=0).astype(bf16)
        u["q_blk"] = by_head(q_dec)
        u["k_blk"] = by_head(k_end)
        u["rhs"] = jnp.concatenate([u["s_t"].astype(bf16), k_dec], axis=0)
    for u in units:
        u["qs"] = lax.dot_general(u["q_blk"], u["rhs"], nt_dims, preferred_element_type=f32)
    for u in units:
        u["scores"] = jnp.where(u["seen"], u["qs"][:, GLA_DV:GLA_DV + ck], 0.0).astype(bf16)
    for u in units:
        v = u["pg"][u["bi"], :, _PG_V:_PG_G]
        u["intra"] = [jnp.dot(u["scores"][hd * ck:(hd + 1) * ck], v[:, hd * GLA_DV:(hd + 1) * GLA_DV],
                              preferred_element_type=f32) for hd in range(nh)]
        v_stack = jnp.concatenate([v[:, hd * GLA_DV:(hd + 1) * GLA_DV] for hd in range(nh)], axis=0)
        u["kv_t"] = lax.dot_general(v_stack, u["k_blk"], tn_dims, preferred_element_type=f32)
    for u in units:
        u["outs"] = [u["intra"][hd] + u["qs"][hd * ck:(hd + 1) * ck, 0:GLA_DV] for hd in range(nh)]
        u["state"][u["bi"]] = u["s_t"] * jnp.exp(u["btot"]) + u["kv_t"]
    n_batch = streams[0][0].shape[0]
    return [[u["outs"] for u in units[si * n_batch:(si + 1) * n_batch]] for si in range(len(streams))]


def _gla_gate(o_heads, g, gn):
    normed = []
    for oh in o_heads:
        ms = jnp.mean(oh * oh, axis=-1, keepdims=True)
        normed.append(oh * lax.rsqrt(ms + RMS_EPS))
    return jnp.concatenate(normed, axis=-1) * gn * (g * jax.nn.sigmoid(g))


def _gla_kernel(*refs, merge):
    if merge:
        (pgf_ref, pgb_ref, waf_ref, baf_ref, wab_ref, bab_ref, s0f_ref, s0b_ref, gn_ref,
         ylo_ref, yhi_ref, st_f, st_b, keep_f, keep_b) = refs
    else:
        (pgf_ref, pgb_ref, waf_ref, baf_ref, wab_ref, bab_ref, s0f_ref, s0b_ref,
         sfin_f_ref, sfin_b_ref, st_f, st_b) = refs
    n = pl.program_id(0)
    n_chunks = pl.num_programs(0)

    @pl.when(n == 0)
    def _():
        st_f[...] = s0f_ref[...]
        st_b[...] = s0b_ref[...]

    outs_f, outs_b = _gla_chunks([(pgf_ref, waf_ref, baf_ref, st_f, False), (pgb_ref, wab_ref, bab_ref, st_b, True)])
    n_batch = pgf_ref.shape[0]
    heads = range(GLA_HEADS)
    if merge:
        half = n_chunks // 2
        m = n_chunks - 1 - n

        @pl.when(n < half)
        def _():
            for bi in range(n_batch):
                keep_f[n, bi] = jnp.concatenate(outs_f[bi], axis=-1).astype(keep_f.dtype)
                keep_b[m - half, bi] = jnp.concatenate(outs_b[bi], axis=-1).astype(keep_b.dtype)

        @pl.when(n >= half)
        def _():
            for bi in range(n_batch):
                kb = keep_b[n - half, bi].astype(f32)
                o_heads = [outs_f[bi][hd] + kb[:, hd * GLA_DV:(hd + 1) * GLA_DV] for hd in heads]
                yhi_ref[bi] = _gla_gate(o_heads, pgf_ref[bi, :, _PG_G:_PG_A].astype(f32), gn_ref[...])
                kf = keep_f[m, bi].astype(f32)
                o_heads = [kf[:, hd * GLA_DV:(hd + 1) * GLA_DV] + outs_b[bi][hd] for hd in heads]
                ylo_ref[bi] = _gla_gate(o_heads, pgb_ref[bi, :, _PG_G:_PG_A].astype(f32), gn_ref[...])
    else:
        @pl.when(n == n_chunks - 1)
        def _():
            sfin_f_ref[...] = st_f[...]
            sfin_b_ref[...] = st_b[...]


def _gla_pass(pg, wa, ba, s0, gn=None):
    n_batch, n_chunks = pg.shape[0], pg.shape[1]
    merge = gn is not None
    half = n_chunks // 2
    const2 = lambda n: (0, 0)
    const3 = lambda n: (0, 0, 0)
    chunk = lambda width, idx: pl.BlockSpec((n_batch, None, GLA_CHUNK, width), lambda n: (0, idx(n), 0, 0))
    w_specs = [pl.BlockSpec((A_LOW_PAD, D_GLA_K), const2), pl.BlockSpec((1, D_GLA_K), const2)]
    s_spec = pl.BlockSpec((n_batch, GLA_DV, D_GLA_K), const3)
    s_shape = jax.ShapeDtypeStruct((n_batch, GLA_DV, D_GLA_K), f32)
    in_specs = [chunk(D_PG, lambda n: n), chunk(D_PG, lambda n: n_chunks - 1 - n)] + w_specs + w_specs + [s_spec,
                                                                                                            s_spec]
    args = [pg, pg, wa[0], ba[0], wa[1], ba[1], s0[0], s0[1]]
    scratch = [pltpu.VMEM((n_batch, GLA_DV, D_GLA_K), f32)] * 2
    if merge:
        assert n_chunks % 2 == 0
        in_specs.append(pl.BlockSpec((1, D_GLA_V), const2))
        args.append(gn)
        out_specs = [chunk(D_GLA_V, lambda n: jnp.minimum(n_chunks - 1 - n, half - 1)),
                     chunk(D_GLA_V, lambda n: jnp.maximum(n - half, 0))]
        y_shape = jax.ShapeDtypeStruct((n_batch, half, GLA_CHUNK, D_GLA_V), f32)
        out_shape = [y_shape, y_shape]
        scratch += [pltpu.VMEM((half, n_batch, GLA_CHUNK, D_GLA_V), bf16)] * 2
    else:
        out_specs = [s_spec, s_spec]
        out_shape = [s_shape, s_shape]
    return pl.pallas_call(
        functools.partial(_gla_kernel, merge=merge),
        grid=(n_chunks,),
        in_specs=in_specs,
        out_specs=out_specs,
        out_shape=out_shape,
        scratch_shapes=scratch,
        compiler_params=_cparams(("arbitrary",)),
        name="gla_merge" if merge else "gla_scan",
    )(*args)


def _pack_rows(v):
    n = v.shape[1] // 2
    w = lax.bitcast_convert_type(v.astype(bf16).astype(f32), jnp.int32)
    return (w[:, :n] & jnp.int32(-65536)) | lax.shift_right_logical(w[:, n:], 16)


def _unpack_rows(w):
    hi = lax.bitcast_convert_type(w & jnp.int32(-65536), f32)
    lo = lax.bitcast_convert_type(lax.shift_left(w, 16), f32)
    return hi, lo


def _outproj_router_kernel(yl_ref, ygl_ref, ygr_ref, x_ref, mod_ref, wo_ref, lg_ref, lb_ref, wrh_ref, wrl_ref,
                           br_ref, x1_ref, hp_ref, ids_ref, rank_ref, wts_ref, cnt_ref, running):
    i = pl.program_id(0)

    @pl.when(i == 0)
    def _():
        running[...] = jnp.zeros_like(running)

    d = D_MODEL
    g1 = mod_ref[0, :, 2 * d:3 * d]
    sh2 = mod_ref[0, :, 3 * d:4 * d]
    sc2 = mod_ref[0, :, 4 * d:5 * d]
    n_parts = OUTPROJ_PARTS
    tm = x_ref.shape[0] // n_parts
    grid_rows = ygl_ref.shape[2] // n_parts
    tok = [slice(p * tm, (p + 1) * tm) for p in range(n_parts)]
    nt_dims = (((1,), (1,)), ((), ()))
    ne = wrh_ref.shape[0]
    expert = lax.broadcasted_iota(jnp.int32, (ne, tm), 0).astype(f32)
    neg_inf = jnp.float32(-jnp.inf)

    ygs = [jnp.concatenate([ref[0, :, r, :] for r in range(p * grid_rows, (p + 1) * grid_rows)
                            for ref in (ygl_ref, ygr_ref)], axis=0).astype(bf16) for p in range(n_parts)]
    ys = [jnp.dot(yl_ref[tok[p], :], wo_ref[0:D_LRU, :], preferred_element_type=f32)
          + jnp.dot(ygs[p], wo_ref[D_LRU:2 * D_LRU, :], preferred_element_type=f32) for p in range(n_parts)]
    hmods = []
    for p in range(n_parts):
        z = DEEPNORM_ALPHA * x_ref[tok[p], :] + g1 * ys[p]
        mu = jnp.mean(z, axis=-1, keepdims=True)
        zc = z - mu
        var = jnp.mean(zc * zc, axis=-1, keepdims=True)
        x1 = zc * lax.rsqrt(var + LN_EPS) * lg_ref[...] + lb_ref[...]
        x1_ref[tok[p], :] = x1
        hmod = x1 * (1.0 + sc2) + sh2
        hp_ref[tok[p], :] = _pack_rows(hmod)
        hmods.append(hmod)

    logits = []
    for hmod in hmods:
        h_hi = hmod.astype(bf16)
        h_lo = (hmod - h_hi.astype(f32)).astype(bf16)
        logits.append(lax.dot_general(wrh_ref[...], h_hi, nt_dims, preferred_element_type=f32)
                      + lax.dot_general(wrh_ref[...], h_lo, nt_dims, preferred_element_type=f32)
                      + lax.dot_general(wrl_ref[...], h_hi, nt_dims, preferred_element_type=f32) + br_ref[...])
    picks = []
    for live in logits:
        sel = jnp.zeros((ne, tm), f32)
        ids, vals = [], []
        for _ in range(TOP_K):
            m = jnp.max(live, axis=0, keepdims=True)
            j = jnp.min(jnp.where(live == m, expert, float(ne)), axis=0, keepdims=True)
            pick = expert == j
            sel = jnp.where(pick, 1.0, sel)
            live = jnp.where(pick, neg_inf, live)
            ids.append(j)
            vals.append(m)
        picks.append((sel, ids, vals))

    ri = lax.broadcasted_iota(jnp.int32, (tm, tm), 0)
    ci = lax.broadcasted_iota(jnp.int32, (tm, tm), 1)
    earlier = (ri < ci).astype(bf16)
    slot = lax.broadcasted_iota(jnp.int32, (8, tm), 0)
    for p, (sel, ids, vals) in enumerate(picks):
        rank_dense = running[...] + jnp.dot(sel.astype(bf16), earlier, preferred_element_type=f32)
        running[...] = running[...] + jnp.sum(sel, axis=1, keepdims=True)
        exps = [jnp.exp(vk - vals[0]) for vk in vals]
        denom = exps[0] + exps[1] + exps[2] + exps[3]
        ids_out = jnp.zeros((8, tm), f32)
        rank_out = jnp.zeros((8, tm), f32)
        wts_out = jnp.zeros((8, tm), f32)
        for kk in range(TOP_K):
            rk = jnp.sum(jnp.where(expert == ids[kk], rank_dense, 0.0), axis=0, keepdims=True)
            here = slot == kk
            ids_out = jnp.where(here, ids[kk], ids_out)
            rank_out = jnp.where(here, rk, rank_out)
            wts_out = jnp.where(here, exps[kk] / denom, wts_out)
        ids_ref[:, tok[p]] = ids_out.astype(jnp.int32)
        rank_ref[:, tok[p]] = rank_out.astype(jnp.int32)
        wts_ref[tok[p], :] = jnp.concatenate([wts_out, jnp.zeros((LANES - 8, tm), f32)], axis=0).T
    cnt_ref[...] = running[...]


def _outproj_router(yl, yg, xt, mod3, wo, lg, lb, wr_hi, wr_lo, br, tokens_per_batch):
    T = xt.shape[0]
    tm = TOKEN_TILE
    tiles_per_batch = tokens_per_batch // tm
    tokrow = lambda i: (i, 0)
    const = lambda i: (0, 0)
    half_cols = yg[0].shape[1]
    rows_per_tile = tm // (2 * half_cols)
    yg_spec = pl.BlockSpec((1, half_cols, rows_per_tile, D_GLA_V),
                           lambda i: (i // tiles_per_batch, 0, i % tiles_per_batch, 0))
    return pl.pallas_call(
        _outproj_router_kernel,
        grid=(T // tm,),
        in_specs=[pl.BlockSpec((tm, D_LRU), tokrow),
                  yg_spec, yg_spec,
                  pl.BlockSpec((tm, D_MODEL), tokrow),
                  pl.BlockSpec((1, 1, N_MOD * D_MODEL), lambda i: (i // tiles_per_batch, 0, 0)),
                  pl.BlockSpec((D_MODEL, D_MODEL), const),
                  pl.BlockSpec((1, D_MODEL), const),
                  pl.BlockSpec((1, D_MODEL), const),
                  pl.BlockSpec((N_EXPERTS, D_MODEL), const),
                  pl.BlockSpec((N_EXPERTS, D_MODEL), const),
                  pl.BlockSpec((N_EXPERTS, 1), const)],
        out_specs=[pl.BlockSpec((tm, D_MODEL), tokrow),
                   pl.BlockSpec((tm, D_MODEL // 2), tokrow),
                   pl.BlockSpec((8, tm), lambda i: (0, i)),
                   pl.BlockSpec((8, tm), lambda i: (0, i)),
                   pl.BlockSpec((tm, LANES), tokrow),
                   pl.BlockSpec((N_EXPERTS, 1), const)],
        out_shape=[jax.ShapeDtypeStruct((T, D_MODEL), f32),
                   jax.ShapeDtypeStruct((T, D_MODEL // 2), jnp.int32),
                   jax.ShapeDtypeStruct((8, T), jnp.int32),
                   jax.ShapeDtypeStruct((8, T), jnp.int32),
                   jax.ShapeDtypeStruct((T, LANES), f32),
                   jax.ShapeDtypeStruct((N_EXPERTS, 1), f32)],
        scratch_shapes=[pltpu.VMEM((N_EXPERTS, 1), f32)],
        compiler_params=_cparams(("arbitrary",)),
        name="outproj_router",
    )(yl, yg[0], yg[1], xt, mod3, wo, lg, lb, wr_hi, wr_lo, br)


def _route_kernel(ids_ref, rank_ref, cnt_ref, dest_ref, tiles_ref, *, bm):
    ne = cnt_ref.shape[0]
    cnt = cnt_ref[...]
    padded = jnp.floor((cnt + (bm - 1.0)) * (1.0 / bm)) * bm
    sub = lax.broadcasted_iota(jnp.int32, (ne, LANES), 0)
    lane = lax.broadcasted_iota(jnp.int32, (ne, LANES), 1)
    padded_row = jnp.sum(jnp.where(sub == lane, padded, 0.0), axis=0, keepdims=True)
    base = jnp.sum(jnp.where(lane < sub, padded_row, 0.0), axis=1, keepdims=True)
    ends = base + padded

    tc = ids_ref.shape[1]
    expert = lax.broadcasted_iota(jnp.int32, (ne, tc), 0)
    slot = lax.broadcasted_iota(jnp.int32, (8, tc), 0)
    ids = ids_ref[...]
    group_base = jnp.zeros((8, tc), f32)
    for kk in range(TOP_K):
        b_k = jnp.sum(jnp.where(expert == ids[kk:kk + 1], base, 0.0), axis=0, keepdims=True)
        group_base = jnp.where(slot == kk, b_k, group_base)
    dest_ref[...] = group_base.astype(jnp.int32) + rank_ref[...]

    nl = tiles_ref.shape[1]
    start = lax.broadcasted_iota(jnp.int32, (ne, nl), 1).astype(f32) * bm
    te = jnp.minimum(jnp.sum(jnp.where(start >= ends, 1.0, 0.0), axis=0, keepdims=True), ne - 1.0)
    at_te = lax.broadcasted_iota(jnp.int32, (ne, nl), 0).astype(f32) == te
    cnt_te = jnp.sum(jnp.where(at_te, cnt, 0.0), axis=0, keepdims=True)
    base_te = jnp.sum(jnp.where(at_te, base, 0.0), axis=0, keepdims=True)
    valid = jnp.clip(cnt_te - (start[0:1] - base_te), 0.0, float(bm))
    next_group = jnp.sum(jnp.where(at_te, ends, 0.0), axis=0, keepdims=True) * (1.0 / bm)
    last_used = jnp.sum(padded, axis=0, keepdims=True) * (1.0 / bm) - 1.0
    srow = lax.broadcasted_iota(jnp.int32, (8, nl), 0)
    table = jnp.where(srow == 0, te, jnp.where(srow == 1, valid, jnp.where(srow == 2, next_group,
                                                                           jnp.where(srow == 3, last_used, 0.0))))
    tiles_ref[...] = table.astype(jnp.int32)


def _route(ids, rank, cnt, bm, n_tiles):
    T = ids.shape[1]
    tc = 2048
    nl = -(-n_tiles // LANES) * LANES
    tok = lambda i: (0, i)
    const = lambda i: (0, 0)
    return pl.pallas_call(
        functools.partial(_route_kernel, bm=bm),
        grid=(T // tc,),
        in_specs=[pl.BlockSpec((8, tc), tok), pl.BlockSpec((8, tc), tok),
                  pl.BlockSpec((N_EXPERTS, 1), const)],
        out_specs=[pl.BlockSpec((8, tc), tok), pl.BlockSpec((8, nl), const)],
        out_shape=[jax.ShapeDtypeStruct((8, T), jnp.int32), jax.ShapeDtypeStruct((8, nl), jnp.int32)],
        compiler_params=_cparams(("arbitrary",)),
        name="route",
    )(ids, rank, cnt)


def _sc_workers():
    info = plsc.get_sparse_core_info()
    return info.num_cores, info.num_subcores


def _sc_dispatch(rows, dest_flat, n_out):
    T, D = rows.shape
    nc, ns = _sc_workers()
    per_w = T // (nc * ns)
    n_chunks = per_w // SC_CHUNK
    mesh = plsc.VectorSubcoreMesh(core_axis_name="c", subcore_axis_name="s")

    @functools.partial(
        pl.kernel, mesh=mesh,
        out_type=jax.ShapeDtypeStruct((n_out, D), rows.dtype),
        scratch_types=([pltpu.VMEM((SC_CHUNK,), jnp.int32)] * TOP_K
                       + [pltpu.VMEM((SC_CHUNK, D), rows.dtype)]
                       + [pltpu.SemaphoreType.DMA] * TOP_K),
    )
    def k(rows_hbm, dest_hbm, out_hbm, *scratch):
        idx_v = scratch[:TOP_K]
        rows_v = scratch[TOP_K]
        sems = scratch[TOP_K + 1:]
        wid = lax.axis_index("s") * nc + lax.axis_index("c")
        base = wid * per_w

        @pl.loop(0, n_chunks)
        def _(j):
            off = base + j * SC_CHUNK
            pltpu.sync_copy(rows_hbm.at[pl.ds(off, SC_CHUNK)], rows_v)
            for kk in range(TOP_K):
                pltpu.sync_copy(dest_hbm.at[pl.ds(kk * T + off, SC_CHUNK)], idx_v[kk])
            copies = [pltpu.async_copy(rows_v, out_hbm.at[idx_v[kk]], sems[kk]) for kk in range(TOP_K)]
            for cp in copies:
                cp.wait()

    return k(rows, dest_flat)


def _sc_gather(table, idx):
    _, D = table.shape
    N = idx.shape[0]
    nc, ns = _sc_workers()
    per_w = N // (nc * ns)
    n_chunks = per_w // SC_CHUNK
    assert n_chunks % 2 == 0
    mesh = plsc.VectorSubcoreMesh(core_axis_name="c", subcore_axis_name="s")

    @functools.partial(
        pl.kernel, mesh=mesh,
        out_type=jax.ShapeDtypeStruct((N, D), table.dtype),
        scratch_types=([pltpu.VMEM((SC_CHUNK,), jnp.int32)] * 2
                       + [pltpu.VMEM((SC_CHUNK, D), table.dtype)] * 2
                       + [pltpu.SemaphoreType.DMA] * 2),
    )
    def k(table_hbm, idx_hbm, out_hbm, idx0, idx1, buf0, buf1, sem0, sem1):
        idxs, bufs, sems = (idx0, idx1), (buf0, buf1), (sem0, sem1)
        wid = lax.axis_index("s") * nc + lax.axis_index("c")
        base = wid * per_w

        def gather(b):
            return pltpu.make_async_copy(table_hbm.at[idxs[b]], bufs[b], sems[b])

        def start(chunk, b):
            pltpu.sync_copy(idx_hbm.at[pl.ds(base + chunk * SC_CHUNK, SC_CHUNK)], idxs[b])
            gather(b).start()

        def finish(chunk, b):
            gather(b).wait()
            pltpu.sync_copy(bufs[b], out_hbm.at[pl.ds(base + chunk * SC_CHUNK, SC_CHUNK)])

        start(0, 0)

        @pl.loop(0, n_chunks, step=2)
        def _(j):
            start(j + 1, 1)
            finish(j, 0)

            @pl.when(j + 2 < n_chunks)
            def _():
                start(j + 2, 0)

            finish(j + 1, 1)

    return k(table, idx)


def _ffn_kernel(te_ref, tv_ref, tn_ref, tl_ref, xs_ref, wg_hbm, bg_ref, wu_hbm, bu_ref, wd_hbm, bd_ref, eo_ref,
                w_stage, wg_b, wu_b, wd_b, sems):
    i = pl.program_id(0)
    n_tiles = pl.num_programs(0)
    w_hbm = (wg_hbm, wu_hbm, wd_hbm)
    w_b = (wg_b, wu_b, wd_b)

    def fetch(e):
        return [pltpu.make_async_copy(w_hbm[m].at[e], w_stage.at[m], sems.at[m]) for m in range(3)]

    @pl.when(i == 0)
    def _():
        for cp in fetch(te_ref[0]):
            cp.start()

    prev = te_ref[jnp.maximum(i - 1, 0)]

    @pl.when((i == 0) | (te_ref[i] != prev))
    def _():
        for m, cp in enumerate(fetch(te_ref[i])):
            cp.wait()
            w_b[m][...] = w_stage[m].astype(bf16)
        nxt = tn_ref[i]
        e_nxt = te_ref[jnp.minimum(nxt, n_tiles - 1)]

        @pl.when((nxt > i) & (nxt < n_tiles) & (e_nxt != te_ref[i]))
        def _():
            for cp in fetch(e_nxt):
                cp.start()

    valid = tv_ref[i]

    def ffn_rows(r0, m):
        blocks = [(r0 + o, min(MOE_BLOCK, m)) for o in range(0, m, MOE_BLOCK)]
        xs, gates, ups, acts = [], [], [], []
        for b0, bm_ in blocks:
            row = lax.broadcasted_iota(jnp.int32, (bm_, 1), 0) + b0
            xw = jnp.where(row < valid, xs_ref[b0:b0 + bm_, :], 0)
            x_hi, x_lo = _unpack_rows(xw)
            xs.append(jnp.concatenate([x_hi, x_lo], axis=1).astype(bf16))
        for x in xs:
            gates.append(jnp.minimum(jnp.dot(x, wg_b[...], preferred_element_type=f32) + bg_ref[...],
                                     SWIGLU_LIMIT))
            ups.append(jnp.clip(jnp.dot(x, wu_b[...], preferred_element_type=f32) + bu_ref[...],
                                -SWIGLU_LIMIT, SWIGLU_LIMIT))
        for gate, up in zip(gates, ups):
            acts.append(((up + 1.0) * gate * jax.nn.sigmoid(SWIGLU_ALPHA * gate)).astype(bf16))
        for (b0, bm_), act in zip(blocks, acts):
            out = jnp.dot(act, wd_b[...], preferred_element_type=f32) + bd_ref[...]
            eo_ref[b0:b0 + bm_, :] = _pack_rows(out)

    def zero_rows(r0, m):
        eo_ref[r0:r0 + m, :] = jnp.zeros((m, eo_ref.shape[1]), eo_ref.dtype)

    for r0 in range(0, xs_ref.shape[0], MOE_PASS):
        lo = 0
        for m in MOE_PASS_SIZES:
            @pl.when((valid > r0 + lo) & ((valid <= r0 + m) | (m == MOE_PASS)))
            def _(r0=r0, m=m):
                ffn_rows(r0, m)
                if m < MOE_PASS:
                    zero_rows(r0 + m, MOE_PASS - m)
            lo = m

        @pl.when(valid <= r0)
        def _(r0=r0):
            zero_rows(r0, MOE_PASS)


def _expert_ffn(tile_expert, tile_valid, tile_next, last_used, xs, w_gate, b_gate, w_up, b_up, w_down, b_down):
    n_rows, dp = xs.shape
    d = 2 * dp
    bm = MOE_TILE
    d_e = w_gate.shape[-1]
    assert d == d_e
    bspec = lambda n_: pl.BlockSpec((None, 1, n_), lambda i, te, tv, tn, tl: (te[i], 0, 0))
    hbm = pl.BlockSpec(memory_space=pl.ANY)
    grid_spec = pltpu.PrefetchScalarGridSpec(
        num_scalar_prefetch=4,
        grid=(n_rows // bm,),
        in_specs=[pl.BlockSpec((bm, dp), lambda i, te, tv, tn, tl: (jnp.minimum(i, tl[0]), 0)),
                  hbm, bspec(d_e), hbm, bspec(d_e), hbm, bspec(d)],
        out_specs=pl.BlockSpec((bm, dp), lambda i, te, tv, tn, tl: (i, 0)),
        scratch_shapes=[pltpu.VMEM((3, d, d_e), f32),
                        pltpu.VMEM((d, d_e), bf16), pltpu.VMEM((d, d_e), bf16), pltpu.VMEM((d_e, d), bf16),
                        pltpu.SemaphoreType.DMA((3,))],
    )
    return pl.pallas_call(
        _ffn_kernel,
        grid_spec=grid_spec,
        out_shape=jax.ShapeDtypeStruct((n_rows, dp), jnp.int32),
        compiler_params=_cparams(("arbitrary",)),
        name="expert_ffn",
    )(tile_expert, tile_valid, tile_next, last_used, xs, w_gate, b_gate.reshape(N_EXPERTS, 1, d_e), w_up,
      b_up.reshape(N_EXPERTS, 1, d_e), w_down, b_down.reshape(N_EXPERTS, 1, d))


def _combine_kernel(eg_ref, wts_ref, x1_ref, mod_ref, lg_ref, lb_ref, o_ref):
    w = wts_ref[...]
    y_hi, y_lo = _unpack_rows(eg_ref[0])
    y_hi, y_lo = y_hi * w[:, 0:1], y_lo * w[:, 0:1]
    for kk in range(1, TOP_K):
        e_hi, e_lo = _unpack_rows(eg_ref[kk])
        y_hi = y_hi + e_hi * w[:, kk:kk + 1]
        y_lo = y_lo + e_lo * w[:, kk:kk + 1]
    y = jnp.concatenate([y_hi, y_lo], axis=1)
    z = DEEPNORM_ALPHA * x1_ref[...] + mod_ref[0] * y
    mu = jnp.mean(z, axis=-1, keepdims=True)
    zc = z - mu
    var = jnp.mean(zc * zc, axis=-1, keepdims=True)
    o_ref[...] = zc * lax.rsqrt(var + LN_EPS) * lg_ref[...] + lb_ref[...]


def _combine_ln(eg, wts, x1, mod3, lg, lb, tokens_per_batch):
    T = x1.shape[0]
    tm = TOKEN_TILE
    tiles_per_batch = tokens_per_batch // tm
    const = lambda i: (0, 0)
    return pl.pallas_call(
        _combine_kernel,
        grid=(T // tm,),
        in_specs=[pl.BlockSpec((TOP_K, tm, D_MODEL // 2), lambda i: (0, i, 0)),
                  pl.BlockSpec((tm, LANES), lambda i: (i, 0)),
                  pl.BlockSpec((tm, D_MODEL), lambda i: (i, 0)),
                  pl.BlockSpec((1, 1, D_MODEL), lambda i: (i // tiles_per_batch, 0, N_MOD - 1)),
                  pl.BlockSpec((1, D_MODEL), const),
                  pl.BlockSpec((1, D_MODEL), const)],
        out_specs=pl.BlockSpec((tm, D_MODEL), lambda i: (i, 0)),
        out_shape=jax.ShapeDtypeStruct((T, D_MODEL), f32),
        compiler_params=_cparams(("arbitrary",)),
        name="combine_ln",
    )(eg, wts, x1, mod3, lg, lb)


def _block_diag(w):
    n, c, d = w.shape
    eye = jnp.eye(n, dtype=w.dtype)
    return jnp.einsum('ncd,nm->ncmd', w, eye).reshape(n * c, n * d)


def kernel(x, c, ctx, c_ctx, w_ada, b_ada, w_in, conv_w, conv_b, lru_wa, lru_ba, lru_wx, lru_bx,
           lru_lam, gla_wa, gla_ba, gla_norm_g, w_out, ln1_g, ln1_b, w_router, b_router, w_gate,
           b_gate, w_up, b_up, w_down, b_down, ln2_g, ln2_b):
    B, L, D = x.shape
    Lc = ctx.shape[1]
    T = B * L
    rows = L // GRID_W
    l = 0

    cpad = jnp.zeros((8, D), f32).at[0:B].set(c).at[B].set(c_ctx)
    w_cat = jnp.pad(w_in[l], ((0, 0), (0, D_PROJ - w_in.shape[-1]))).astype(bf16)
    wg = [jnp.concatenate([_block_diag(lru_wa[l, d]), _block_diag(lru_wx[l, d])], axis=1).astype(bf16)
          for d in range(2)]
    bg = [jnp.concatenate([lru_ba[l, d], lru_bx[l, d]])[None] for d in range(2)]
    lam = [lru_lam[l, d][None] for d in range(2)]
    wa = [jnp.pad(gla_wa[l, d], ((0, A_LOW_PAD - GATE_RANK), (0, 0))).astype(bf16) for d in range(2)]
    ba = [gla_ba[l, d][None] for d in range(2)]
    cw, cb = conv_w[l], conv_b[l][None]
    wr_t = w_router[l].T
    wr_hi = wr_t.astype(bf16)
    wr_lo = (wr_t - wr_hi.astype(f32)).astype(bf16)
    br = b_router[l][:, None]
    c_xl, c_gl, c_pg = (0, D_LRU), (D_LRU, 2 * D_LRU), (2 * D_LRU, D_PROJ)

    mod3 = _ada_mod(cpad, w_ada[l], b_ada[l]).reshape(8, 1, N_MOD * D)

    tiles_per_batch = L // TOKEN_TILE
    xl_c, pg_c = _inproj_rows(ctx.reshape(B * Lc, D), mod3, w_cat, lambda i: B, (c_xl, c_pg), (f32, bf16))
    zero_h = jnp.zeros((B, 1, D_LRU), f32)
    zero_s = jnp.zeros((B, GLA_DV, D_GLA_K), f32)
    pg_c = pg_c.reshape(B, Lc // GLA_CHUNK, GLA_CHUNK, D_PG)
    h_ctx = []
    for d in range(2):
        _, _, hf = _lru_pass(xl_c, wg[d], bg[d], lam[d], zero_h, n_batch=B, tile=Lc, reverse=bool(d),
                             conv=(cw, cb, Lc))
        h_ctx.append(hf)
    s_ctx = _gla_pass(pg_c, wa, ba, (zero_s, zero_s))

    xt = x.reshape(T, D)
    xl, gl = _inproj_rows(xt, mod3, w_cat, lambda i: i // tiles_per_batch, (c_xl, c_gl), (f32, bf16))
    hb, u_lru, _ = _lru_pass(xl, wg[1], bg[1], lam[1], h_ctx[1], n_batch=B, tile=LRU_TILE, reverse=True,
                             conv=(cw, cb, GRID_W))
    y_lru, _ = _lru_pass(u_lru, wg[0], bg[0], lam[0], h_ctx[0], n_batch=B, tile=LRU_TILE, reverse=False,
                         merge_with=(hb, gl))
    pg = _inproj_cols(x.reshape(B, rows, GRID_W, D), mod3, w_cat, c_pg)
    y_gla = _gla_pass(pg, wa, ba, s_ctx, gn=gla_norm_g[l][None])

    x1, hp, ids, rank, wts, cnt = _outproj_router(
        y_lru, y_gla, xt, mod3, w_out[l].astype(bf16), ln1_g[l][None], ln1_b[l][None], wr_hi, wr_lo, br, L)

    n_tiles = T * TOP_K // MOE_TILE + N_EXPERTS
    dest, tiles = _route(ids, rank, cnt, MOE_TILE, n_tiles)
    xs = _sc_dispatch(hp, dest.reshape(-1), n_tiles * MOE_TILE)
    eo = _expert_ffn(tiles[0, :n_tiles], tiles[1, :n_tiles], tiles[2, :n_tiles], tiles[3, :1], xs, w_gate[l],
                     b_gate[l], w_up[l], b_up[l], w_down[l], b_down[l])
    eg = _sc_gather(eo, dest[:TOP_K].reshape(-1)).reshape(TOP_K, T, D // 2)
    out = _combine_ln(eg, wts, x1, mod3, ln2_g[l][None], ln2_b[l][None], L)
    return out.reshape(B, L, D)
```

```python
import functools

import jax
import jax.numpy as jnp
from jax import lax
from jax.experimental import pallas as pl
from jax.experimental.pallas import tpu as pltpu
from jax.experimental.pallas import tpu_sc as plsc

D_MODEL = 1024
DEPTH = 1
GRID_W = 64
D_LRU = 512
LRU_BLOCKS = 8
CONV_W = 4
LRU_C = 8.0
GLA_HEADS = 4
D_GLA_V = 512
D_GLA_K = 256
GLA_DK = 64
GLA_DV = 128
GATE_RANK = 16
GATE_TAU = 16.0
GLA_CHUNK = 64
N_EXPERTS = 32
TOP_K = 4
SWIGLU_LIMIT = 7.0
SWIGLU_ALPHA = 1.702
N_MOD = 6
DEEPNORM_ALPHA = (2.0 * DEPTH) ** 0.25
LN_EPS = 1e-5
RMS_EPS = 1e-6

LANES = 128
A_LOW_PAD = LANES
D_PG = 2 * D_GLA_K + 2 * D_GLA_V + A_LOW_PAD
D_PROJ = 2 * D_LRU + D_PG
TOKEN_TILE = 512
INPROJ_TILE = 1024
LRU_TILE = 512
MOE_TILE = 1024
MOE_PASS = 512
MOE_PASS_SIZES = (128, 256, MOE_PASS)
MOE_BLOCK = 256
OUTPROJ_PARTS = 1
SC_CHUNK = 64
VMEM_LIMIT = 48 * 1024 * 1024

f32 = jnp.float32
bf16 = jnp.bfloat16


def _cparams(sem):
    return pltpu.CompilerParams(dimension_semantics=sem, vmem_limit_bytes=VMEM_LIMIT)


def _ada_kernel(c_ref, w_ref, b_ref, o_ref):
    s = c_ref[...]
    s = s * jax.nn.sigmoid(s)
    o_ref[...] = jnp.dot(s.astype(bf16), w_ref[...].astype(bf16), preferred_element_type=f32) + b_ref[...]


def _ada_mod(cpad, w, b):
    n = w.shape[1]
    tn = 1024
    return pl.pallas_call(
        _ada_kernel,
        grid=(n // tn,),
        in_specs=[pl.BlockSpec((8, D_MODEL), lambda j: (0, 0)),
                  pl.BlockSpec((D_MODEL, tn), lambda j: (0, j)),
                  pl.BlockSpec((1, tn), lambda j: (0, j))],
        out_specs=pl.BlockSpec((8, tn), lambda j: (0, j)),
        out_shape=jax.ShapeDtypeStruct((8, n), f32),
        compiler_params=_cparams(("arbitrary",)),
        name="ada_mod",
    )(cpad, w, b.reshape(1, n))


def _inproj_kernel(x_ref, mod_ref, w_ref, *out_refs, parts, col_major):
    sh = mod_ref[0, :, 0:D_MODEL]
    sc = mod_ref[0, :, D_MODEL:2 * D_MODEL]
    if col_major:
        x = jnp.concatenate([x_ref[0, :, j, :] for j in range(x_ref.shape[2])], axis=0)
    else:
        x = x_ref[...]
    u = (x * (1.0 + sc) + sh).astype(bf16)
    for (lo, hi), o_ref in zip(parts, out_refs):
        p = jnp.dot(u, w_ref[:, lo:hi], preferred_element_type=f32).astype(o_ref.dtype)
        o_ref[...] = p.reshape(o_ref.shape)


def _inproj_rows(xt, mod3, w_cat, batch_of_tile, parts, dtypes):
    T = xt.shape[0]
    tm = INPROJ_TILE
    return pl.pallas_call(
        functools.partial(_inproj_kernel, parts=parts, col_major=False),
        grid=(T // tm,),
        in_specs=[pl.BlockSpec((tm, D_MODEL), lambda i: (i, 0)),
                  pl.BlockSpec((1, 1, 2 * D_MODEL), lambda i: (batch_of_tile(i), 0, 0)),
                  pl.BlockSpec((D_MODEL, D_PROJ), lambda i: (0, 0))],
        out_specs=[pl.BlockSpec((tm, hi - lo), lambda i: (i, 0)) for lo, hi in parts],
        out_shape=[jax.ShapeDtypeStruct((T, hi - lo), dt) for (lo, hi), dt in zip(parts, dtypes)],
        compiler_params=_cparams(("arbitrary",)),
        name="inproj_rows",
    )(xt, mod3, w_cat)


def _inproj_cols(x4, mod3, w_cat, part):
    n_batch, rows, cols, _ = x4.shape
    lo, hi = part
    cb = TOKEN_TILE // rows
    return pl.pallas_call(
        functools.partial(_inproj_kernel, parts=(part,), col_major=True),
        grid=(n_batch, cols // cb),
        in_specs=[pl.BlockSpec((1, rows, cb, D_MODEL), lambda b, n: (b, 0, n, 0)),
                  pl.BlockSpec((1, 1, 2 * D_MODEL), lambda b, n: (b, 0, 0)),
                  pl.BlockSpec((D_MODEL, D_PROJ), lambda b, n: (0, 0))],
        out_specs=[pl.BlockSpec((1, cb, rows, hi - lo), lambda b, n: (b, n, 0, 0))],
        out_shape=[jax.ShapeDtypeStruct((n_batch, cols, rows, hi - lo), bf16)],
        compiler_params=_cparams(("arbitrary", "arbitrary")),
        name="inproj_cols",
    )(x4, mod3, w_cat)[0]


def _gelu_tanh(x):
    return 0.5 * x * (1.0 + jnp.tanh(0.7978845608028654 * (x + 0.044715 * (x * x * x))))


def _lru_kernel(*refs, row_w, reverse, merge):
    if merge:
        u_ref, wg_ref, bg_ref, lam_ref, h0_ref, hb_ref, gl_ref, out_ref, hfin_ref, carry, h_nat = refs
    else:
        xl_ref, cw_ref, cb_ref, wg_ref, bg_ref, lam_ref, h0_ref, out_ref, u_out_ref, hfin_ref, carry = refs
    t = pl.program_id(1)

    @pl.when(t == 0)
    def _():
        carry[...] = h0_ref[0]

    n_seg = 8
    blk = lambda v, j: v[j * n_seg:(j + 1) * n_seg]
    if merge:
        u = u_ref[...]
        tt, ch = u.shape
        seg = tt // n_seg
    else:
        _, seg, ch = xl_ref.shape
        tt = n_seg * seg
        x = jnp.concatenate([xl_ref[:, j, :] for j in range(seg)], axis=0)
        segs_per_row = row_w // seg
        s_idx = lax.broadcasted_iota(jnp.int32, (n_seg, ch), 0)
        has_prev = (s_idx % segs_per_row) != 0
        has_next = (s_idx % segs_per_row) != segs_per_row - 1
        from_prev = lambda v: jnp.where(has_prev, pltpu.roll(v, 1, 0), 0.0)
        from_next = lambda v: jnp.where(has_next, pltpu.roll(v, n_seg - 1, 0), 0.0)
        x_m1 = jnp.concatenate([from_prev(blk(x, seg - 1)), x[:tt - n_seg]], axis=0)
        x_m2 = jnp.concatenate([from_prev(blk(x, seg - 2)), from_prev(blk(x, seg - 1)), x[:tt - 2 * n_seg]],
                               axis=0)
        x_p1 = jnp.concatenate([x[n_seg:], from_next(blk(x, 0))], axis=0)
        cw = cw_ref[...]
        u = cb_ref[...] + x_m2 * cw[0:1] + x_m1 * cw[1:2] + x * cw[2:3] + x_p1 * cw[3:4]
        u_out_ref[...] = u
    g = jnp.dot(u.astype(bf16), wg_ref[...], preferred_element_type=f32) + bg_ref[...]
    lam = lam_ref[...]
    softplus_neg = jnp.maximum(-lam, 0.0) + jnp.log(1.0 + jnp.exp(-jnp.abs(lam)))
    log_a_scale = -LRU_C * softplus_neg

    def recurrence_terms(j):
        g_j, u_j = blk(g, j), blk(u, j)
        log_a = log_a_scale * jax.nn.sigmoid(g_j[:, 0:ch])
        a_j = jnp.exp(log_a)
        series = log_a * (-2.0 + log_a * (-2.0 + log_a * (-4.0 / 3.0)))
        one_minus_a2 = jnp.where(log_a > -0.005, series, 1.0 - a_j * a_j)
        root = jnp.where(one_minus_a2 > 0.0, one_minus_a2 * lax.rsqrt(one_minus_a2), 0.0)
        return a_j, root * (jax.nn.sigmoid(g_j[:, ch:2 * ch]) * u_j)

    half = seg // 2
    h_loc, a_cum = [None] * seg, [None] * seg
    h_run, a_run = [None, None], [None, None]
    for i in range(half):
        for p in range(2):
            j = p * half + (half - 1 - i if reverse else i)
            a_j, b_j = recurrence_terms(j)
            h_run[p] = b_j if i == 0 else a_j * h_run[p] + b_j
            a_run[p] = a_j if i == 0 else a_j * a_run[p]
            h_loc[j], a_cum[j] = h_run[p], a_run[p]
    pieces = [(s, p) for s in range(n_seg) for p in range(2)]
    if reverse:
        pieces.reverse()
    entering = [[None] * n_seg, [None] * n_seg]
    state = carry[...]
    for s, p in pieces:
        entering[p][s] = state
        state = h_run[p][s:s + 1] + a_run[p][s:s + 1] * state
    carry[...] = state
    hfin_ref[0] = state
    enter = [jnp.concatenate(e, axis=0) for e in entering]
    h_rows = [h_loc[j] + a_cum[j] * enter[j // half] for j in range(seg)]
    if merge:
        for j in range(seg):
            h_nat[:, j, :] = h_rows[j] + blk(hb_ref, j)
        h = h_nat[...].reshape(tt, ch)
        out_ref[...] = (h * _gelu_tanh(gl_ref[...].astype(f32))).astype(out_ref.dtype)
    else:
        out_ref[...] = jnp.concatenate(h_rows, axis=0)


def _lru_pass(x_in, wg, bg, lam, h0, *, n_batch, tile, reverse, conv=None, merge_with=None):
    T = x_in.shape[0]
    nt = T // n_batch // tile
    merge = merge_with is not None
    assert merge != (conv is not None)
    seg = tile // 8

    def tok(b, t):
        return (b * nt + (nt - 1 - t if reverse else t), 0)

    tile_spec = pl.BlockSpec((tile, D_LRU), tok)
    const = lambda b, t: (0, 0)
    gate_specs = [pl.BlockSpec((D_LRU, 2 * D_LRU), const),
                  pl.BlockSpec((1, 2 * D_LRU), const),
                  pl.BlockSpec((1, D_LRU), const),
                  pl.BlockSpec((1, 1, D_LRU), lambda b, t: (b, 0, 0))]
    state_spec = pl.BlockSpec((1, 1, D_LRU), lambda b, t: (b, 0, 0))
    state_shape = jax.ShapeDtypeStruct((n_batch, 1, D_LRU), f32)
    scratch = [pltpu.VMEM((1, D_LRU), f32)]
    if merge:
        hb, gl = merge_with
        row_w = None
        in_specs = [tile_spec] + gate_specs + [tile_spec, tile_spec]
        args = [x_in, wg, bg, lam, h0, hb, gl]
        scratch += [pltpu.VMEM((8, seg, D_LRU), f32)]
        out_specs = [tile_spec, state_spec]
        out_shape = [jax.ShapeDtypeStruct((T, D_LRU), bf16), state_shape]
    else:
        conv_w, conv_b, row_w = conv
        assert row_w % seg == 0 and seg >= 2
        in_specs = [pl.BlockSpec((8, seg, D_LRU), lambda b, t: tok(b, t) + (0,)),
                    pl.BlockSpec((CONV_W, D_LRU), const),
                    pl.BlockSpec((1, D_LRU), const)] + gate_specs
        args = [x_in.reshape(T // seg, seg, D_LRU), conv_w, conv_b, wg, bg, lam, h0]
        out_specs = [tile_spec, tile_spec, state_spec]
        out_shape = [jax.ShapeDtypeStruct((T, D_LRU), f32), jax.ShapeDtypeStruct((T, D_LRU), f32), state_shape]
    return pl.pallas_call(
        functools.partial(_lru_kernel, row_w=row_w, reverse=reverse, merge=merge),
        grid=(n_batch, nt),
        in_specs=in_specs,
        out_specs=out_specs,
        out_shape=out_shape,
        scratch_shapes=scratch,
        compiler_params=_cparams(("arbitrary", "arbitrary")),
        name="lru_merge" if merge else "lru_scan",
    )(*args)


_PG_Q, _PG_K, _PG_V = 0, D_GLA_K, 2 * D_GLA_K
_PG_G, _PG_A = 2 * D_GLA_K + D_GLA_V, 2 * D_GLA_K + 2 * D_GLA_V


def _gla_chunks(streams):
    ck = GLA_CHUNK
    nh = GLA_HEADS
    head_of_lane = lax.broadcasted_iota(jnp.int32, (1, D_GLA_K), 1) // GLA_DK
    nt_dims = (((1,), (1,)), ((), ()))
    tn_dims = (((0,), (0,)), ((), ()))
    ii = lax.broadcasted_iota(jnp.int32, (nh * ck, ck), 0) & (ck - 1)
    jj = lax.broadcasted_iota(jnp.int32, (nh * ck, ck), 1)

    units = []
    for pg_ref, wa_ref, ba_ref, state, reverse in streams:
        n_batch = pg_ref.shape[0]
        rows_all = n_batch * ck
        a_low = jnp.concatenate([pg_ref[bi, :, _PG_A:_PG_A + A_LOW_PAD] for bi in range(n_batch)], axis=0)
        z = jnp.dot(a_low, wa_ref[...], preferred_element_type=f32) + ba_ref[...]
        bcum_all = (jnp.minimum(z, 0.0) - jnp.log(1.0 + jnp.exp(-jnp.abs(z)))) * (1.0 / GATE_TAU)
        step = lax.broadcasted_iota(jnp.int32, (rows_all, D_GLA_K), 0) & (ck - 1)
        d = 1
        while d < ck:
            if reverse:
                bcum_all = bcum_all + jnp.where(step < ck - d, pltpu.roll(bcum_all, rows_all - d, 0), 0.0)
            else:
                bcum_all = bcum_all + jnp.where(step >= d, pltpu.roll(bcum_all, d, 0), 0.0)
            d *= 2
        seen = (jj >= ii) if reverse else (jj <= ii)
        for bi in range(n_batch):
            units.append(dict(pg=pg_ref, bi=bi, state=state, reverse=reverse, seen=seen,
                              bcum=bcum_all[bi * ck:(bi + 1) * ck]))

    for u in units:
        pg_ref, bi, bcum = u["pg"], u["bi"], u["bcum"]
        q = pg_ref[bi, :, _PG_Q:_PG_K].astype(f32) * (GLA_DK ** -0.5)
        k = pg_ref[bi, :, _PG_K:_PG_V].astype(f32)
        u["btot"] = bcum[0:1] if u["reverse"] else bcum[ck - 1:ck]
        q_dec = q * jnp.exp(bcum)
        k_dec = (k * jnp.exp(-bcum)).astype(bf16)
        k_end = k * jnp.exp(u["btot"] - bcum)
        u["s_t"] = u["state"][bi]
        by_head = lambda t: jnp.concatenate([jnp.where(head_of_lane == hd, t, 0.0) for hd in range(nh)],
                                            axis=0).astype(bf16)
        u["q_blk"] = by_head(q_dec)
        u["k_blk"] = by_head(k_end)
        u["rhs"] = jnp.concatenate([u["s_t"].astype(bf16), k_dec], axis=0)
    for u in units:
        u["qs"] = lax.dot_general(u["q_blk"], u["rhs"], nt_dims, preferred_element_type=f32)
    for u in units:
        u["scores"] = jnp.where(u["seen"], u["qs"][:, GLA_DV:GLA_DV + ck], 0.0).astype(bf16)
    for u in units:
        v = u["pg"][u["bi"], :, _PG_V:_PG_G]
        u["intra"] = [jnp.dot(u["scores"][hd * ck:(hd + 1) * ck], v[:, hd * GLA_DV:(hd + 1) * GLA_DV],
                              preferred_element_type=f32) for hd in range(nh)]
        v_stack = jnp.concatenate([v[:, hd * GLA_DV:(hd + 1) * GLA_DV] for hd in range(nh)], axis=0)
        u["kv_t"] = lax.dot_general(v_stack, u["k_blk"], tn_dims, preferred_element_type=f32)
    for u in units:
        u["outs"] = [u["intra"][hd] + u["qs"][hd * ck:(hd + 1) * ck, 0:GLA_DV] for hd in range(nh)]
        u["state"][u["bi"]] = u["s_t"] * jnp.exp(u["btot"]) + u["kv_t"]
    n_batch = streams[0][0].shape[0]
    return [[u["outs"] for u in units[si * n_batch:(si + 1) * n_batch]] for si in range(len(streams))]


def _gla_gate(o_heads, g, gn):
    normed = []
    for oh in o_heads:
        ms = jnp.mean(oh * oh, axis=-1, keepdims=True)
        normed.append(oh * lax.rsqrt(ms + RMS_EPS))
    return jnp.concatenate(normed, axis=-1) * gn * (g * jax.nn.sigmoid(g))


def _gla_kernel(*refs, merge):
    if merge:
        (pgf_ref, pgb_ref, waf_ref, baf_ref, wab_ref, bab_ref, s0f_ref, s0b_ref, gn_ref,
         ylo_ref, yhi_ref, st_f, st_b, keep_f, keep_b) = refs
    else:
        (pgf_ref, pgb_ref, waf_ref, baf_ref, wab_ref, bab_ref, s0f_ref, s0b_ref,
         sfin_f_ref, sfin_b_ref, st_f, st_b) = refs
    n = pl.program_id(0)
    n_chunks = pl.num_programs(0)

    @pl.when(n == 0)
    def _():
        st_f[...] = s0f_ref[...]
        st_b[...] = s0b_ref[...]

    outs_f, outs_b = _gla_chunks([(pgf_ref, waf_ref, baf_ref, st_f, False), (pgb_ref, wab_ref, bab_ref, st_b, True)])
    n_batch = pgf_ref.shape[0]
    heads = range(GLA_HEADS)
    if merge:
        half = n_chunks // 2
        m = n_chunks - 1 - n

        @pl.when(n < half)
        def _():
            for bi in range(n_batch):
                keep_f[n, bi] = jnp.concatenate(outs_f[bi], axis=-1).astype(keep_f.dtype)
                keep_b[m - half, bi] = jnp.concatenate(outs_b[bi], axis=-1).astype(keep_b.dtype)

        @pl.when(n >= half)
        def _():
            for bi in range(n_batch):
                kb = keep_b[n - half, bi].astype(f32)
                o_heads = [outs_f[bi][hd] + kb[:, hd * GLA_DV:(hd + 1) * GLA_DV] for hd in heads]
                yhi_ref[bi] = _gla_gate(o_heads, pgf_ref[bi, :, _PG_G:_PG_A].astype(f32), gn_ref[...])
                kf = keep_f[m, bi].astype(f32)
                o_heads = [kf[:, hd * GLA_DV:(hd + 1) * GLA_DV] + outs_b[bi][hd] for hd in heads]
                ylo_ref[bi] = _gla_gate(o_heads, pgb_ref[bi, :, _PG_G:_PG_A].astype(f32), gn_ref[...])
    else:
        @pl.when(n == n_chunks - 1)
        def _():
            sfin_f_ref[...] = st_f[...]
            sfin_b_ref[...] = st_b[...]


def _gla_pass(pg, wa, ba, s0, gn=None):
    n_batch, n_chunks = pg.shape[0], pg.shape[1]
    merge = gn is not None
    half = n_chunks // 2
    const2 = lambda n: (0, 0)
    const3 = lambda n: (0, 0, 0)
    chunk = lambda width, idx: pl.BlockSpec((n_batch, None, GLA_CHUNK, width), lambda n: (0, idx(n), 0, 0))
    w_specs = [pl.BlockSpec((A_LOW_PAD, D_GLA_K), const2), pl.BlockSpec((1, D_GLA_K), const2)]
    s_spec = pl.BlockSpec((n_batch, GLA_DV, D_GLA_K), const3)
    s_shape = jax.ShapeDtypeStruct((n_batch, GLA_DV, D_GLA_K), f32)
    in_specs = [chunk(D_PG, lambda n: n), chunk(D_PG, lambda n: n_chunks - 1 - n)] + w_specs + w_specs + [s_spec,
                                                                                                            s_spec]
    args = [pg, pg, wa[0], ba[0], wa[1], ba[1], s0[0], s0[1]]
    scratch = [pltpu.VMEM((n_batch, GLA_DV, D_GLA_K), f32)] * 2
    if merge:
        assert n_chunks % 2 == 0
        in_specs.append(pl.BlockSpec((1, D_GLA_V), const2))
        args.append(gn)
        out_specs = [chunk(D_GLA_V, lambda n: jnp.minimum(n_chunks - 1 - n, half - 1)),
                     chunk(D_GLA_V, lambda n: jnp.maximum(n - half, 0))]
        y_shape = jax.ShapeDtypeStruct((n_batch, half, GLA_CHUNK, D_GLA_V), f32)
        out_shape = [y_shape, y_shape]
        scratch += [pltpu.VMEM((half, n_batch, GLA_CHUNK, D_GLA_V), bf16)] * 2
    else:
        out_specs = [s_spec, s_spec]
        out_shape = [s_shape, s_shape]
    return pl.pallas_call(
        functools.partial(_gla_kernel, merge=merge),
        grid=(n_chunks,),
        in_specs=in_specs,
        out_specs=out_specs,
        out_shape=out_shape,
        scratch_shapes=scratch,
        compiler_params=_cparams(("arbitrary",)),
        name="gla_merge" if merge else "gla_scan",
    )(*args)


def _pack_rows(v):
    n = v.shape[1] // 2
    w = lax.bitcast_convert_type(v.astype(bf16).astype(f32), jnp.int32)
    return (w[:, :n] & jnp.int32(-65536)) | lax.shift_right_logical(w[:, n:], 16)


def _unpack_rows(w):
    hi = lax.bitcast_convert_type(w & jnp.int32(-65536), f32)
    lo = lax.bitcast_convert_type(lax.shift_left(w, 16), f32)
    return hi, lo


def _outproj_router_kernel(yl_ref, ygl_ref, ygr_ref, x_ref, mod_ref, wo_ref, lg_ref, lb_ref, wrh_ref, wrl_ref,
                           br_ref, x1_ref, hp_ref, ids_ref, rank_ref, wts_ref, cnt_ref, running):
    i = pl.program_id(0)

    @pl.when(i == 0)
    def _():
        running[...] = jnp.zeros_like(running)

    d = D_MODEL
    g1 = mod_ref[0, :, 2 * d:3 * d]
    sh2 = mod_ref[0, :, 3 * d:4 * d]
    sc2 = mod_ref[0, :, 4 * d:5 * d]
    n_parts = OUTPROJ_PARTS
    tm = x_ref.shape[0] // n_parts
    grid_rows = ygl_ref.shape[2] // n_parts
    tok = [slice(p * tm, (p + 1) * tm) for p in range(n_parts)]
    nt_dims = (((1,), (1,)), ((), ()))
    ne = wrh_ref.shape[0]
    expert = lax.broadcasted_iota(jnp.int32, (ne, tm), 0).astype(f32)
    neg_inf = jnp.float32(-jnp.inf)

    ygs = [jnp.concatenate([ref[0, :, r, :] for r in range(p * grid_rows, (p + 1) * grid_rows)
                            for ref in (ygl_ref, ygr_ref)], axis=0).astype(bf16) for p in range(n_parts)]
    ys = [jnp.dot(yl_ref[tok[p], :], wo_ref[0:D_LRU, :], preferred_element_type=f32)
          + jnp.dot(ygs[p], wo_ref[D_LRU:2 * D_LRU, :], preferred_element_type=f32) for p in range(n_parts)]
    hmods = []
    for p in range(n_parts):
        z = DEEPNORM_ALPHA * x_ref[tok[p], :] + g1 * ys[p]
        mu = jnp.mean(z, axis=-1, keepdims=True)
        zc = z - mu
        var = jnp.mean(zc * zc, axis=-1, keepdims=True)
        x1 = zc * lax.rsqrt(var + LN_EPS) * lg_ref[...] + lb_ref[...]
        x1_ref[tok[p], :] = x1
        hmod = x1 * (1.0 + sc2) + sh2
        hp_ref[tok[p], :] = _pack_rows(hmod)
        hmods.append(hmod)

    logits = []
    for hmod in hmods:
        h_hi = hmod.astype(bf16)
        h_lo = (hmod - h_hi.astype(f32)).astype(bf16)
        logits.append(lax.dot_general(wrh_ref[...], h_hi, nt_dims, preferred_element_type=f32)
                      + lax.dot_general(wrh_ref[...], h_lo, nt_dims, preferred_element_type=f32)
                      + lax.dot_general(wrl_ref[...], h_hi, nt_dims, preferred_element_type=f32) + br_ref[...])
    picks = []
    for live in logits:
        sel = jnp.zeros((ne, tm), f32)
        ids, vals = [], []
        for _ in range(TOP_K):
            m = jnp.max(live, axis=0, keepdims=True)
            j = jnp.min(jnp.where(live == m, expert, float(ne)), axis=0, keepdims=True)
            pick = expert == j
            sel = jnp.where(pick, 1.0, sel)
            live = jnp.where(pick, neg_inf, live)
            ids.append(j)
            vals.append(m)
        picks.append((sel, ids, vals))

    ri = lax.broadcasted_iota(jnp.int32, (tm, tm), 0)
    ci = lax.broadcasted_iota(jnp.int32, (tm, tm), 1)
    earlier = (ri < ci).astype(bf16)
    slot = lax.broadcasted_iota(jnp.int32, (8, tm), 0)
    for p, (sel, ids, vals) in enumerate(picks):
        rank_dense = running[...] + jnp.dot(sel.astype(bf16), earlier, preferred_element_type=f32)
        running[...] = running[...] + jnp.sum(sel, axis=1, keepdims=True)
        exps = [jnp.exp(vk - vals[0]) for vk in vals]
        denom = exps[0] + exps[1] + exps[2] + exps[3]
        ids_out = jnp.zeros((8, tm), f32)
        rank_out = jnp.zeros((8, tm), f32)
        wts_out = jnp.zeros((8, tm), f32)
        for kk in range(TOP_K):
            rk = jnp.sum(jnp.where(expert == ids[kk], rank_dense, 0.0), axis=0, keepdims=True)
            here = slot == kk
            ids_out = jnp.where(here, ids[kk], ids_out)
            rank_out = jnp.where(here, rk, rank_out)
            wts_out = jnp.where(here, exps[kk] / denom, wts_out)
        ids_ref[:, tok[p]] = ids_out.astype(jnp.int32)
        rank_ref[:, tok[p]] = rank_out.astype(jnp.int32)
        wts_ref[tok[p], :] = jnp.concatenate([wts_out, jnp.zeros((LANES - 8, tm), f32)], axis=0).T
    cnt_ref[...] = running[...]


def _outproj_router(yl, yg, xt, mod3, wo, lg, lb, wr_hi, wr_lo, br, tokens_per_batch):
    T = xt.shape[0]
    tm = TOKEN_TILE
    tiles_per_batch = tokens_per_batch // tm
    tokrow = lambda i: (i, 0)
    const = lambda i: (0, 0)
    half_cols = yg[0].shape[1]
    rows_per_tile = tm // (2 * half_cols)
    yg_spec = pl.BlockSpec((1, half_cols, rows_per_tile, D_GLA_V),
                           lambda i: (i // tiles_per_batch, 0, i % tiles_per_batch, 0))
    return pl.pallas_call(
        _outproj_router_kernel,
        grid=(T // tm,),
        in_specs=[pl.BlockSpec((tm, D_LRU), tokrow),
                  yg_spec, yg_spec,
                  pl.BlockSpec((tm, D_MODEL), tokrow),
                  pl.BlockSpec((1, 1, N_MOD * D_MODEL), lambda i: (i // tiles_per_batch, 0, 0)),
                  pl.BlockSpec((D_MODEL, D_MODEL), const),
                  pl.BlockSpec((1, D_MODEL), const),
                  pl.BlockSpec((1, D_MODEL), const),
                  pl.BlockSpec((N_EXPERTS, D_MODEL), const),
                  pl.BlockSpec((N_EXPERTS, D_MODEL), const),
                  pl.BlockSpec((N_EXPERTS, 1), const)],
        out_specs=[pl.BlockSpec((tm, D_MODEL), tokrow),
                   pl.BlockSpec((tm, D_MODEL // 2), tokrow),
                   pl.BlockSpec((8, tm), lambda i: (0, i)),
                   pl.BlockSpec((8, tm), lambda i: (0, i)),
                   pl.BlockSpec((tm, LANES), tokrow),
                   pl.BlockSpec((N_EXPERTS, 1), const)],
        out_shape=[jax.ShapeDtypeStruct((T, D_MODEL), f32),
                   jax.ShapeDtypeStruct((T, D_MODEL // 2), jnp.int32),
                   jax.ShapeDtypeStruct((8, T), jnp.int32),
                   jax.ShapeDtypeStruct((8, T), jnp.int32),
                   jax.ShapeDtypeStruct((T, LANES), f32),
                   jax.ShapeDtypeStruct((N_EXPERTS, 1), f32)],
        scratch_shapes=[pltpu.VMEM((N_EXPERTS, 1), f32)],
        compiler_params=_cparams(("arbitrary",)),
        name="outproj_router",
    )(yl, yg[0], yg[1], xt, mod3, wo, lg, lb, wr_hi, wr_lo, br)


def _route_kernel(ids_ref, rank_ref, cnt_ref, dest_ref, tiles_ref, *, bm):
    ne = cnt_ref.shape[0]
    cnt = cnt_ref[...]
    padded = jnp.floor((cnt + (bm - 1.0)) * (1.0 / bm)) * bm
    sub = lax.broadcasted_iota(jnp.int32, (ne, LANES), 0)
    lane = lax.broadcasted_iota(jnp.int32, (ne, LANES), 1)
    padded_row = jnp.sum(jnp.where(sub == lane, padded, 0.0), axis=0, keepdims=True)
    base = jnp.sum(jnp.where(lane < sub, padded_row, 0.0), axis=1, keepdims=True)
    ends = base + padded

    tc = ids_ref.shape[1]
    expert = lax.broadcasted_iota(jnp.int32, (ne, tc), 0)
    slot = lax.broadcasted_iota(jnp.int32, (8, tc), 0)
    ids = ids_ref[...]
    group_base = jnp.zeros((8, tc), f32)
    for kk in range(TOP_K):
        b_k = jnp.sum(jnp.where(expert == ids[kk:kk + 1], base, 0.0), axis=0, keepdims=True)
        group_base = jnp.where(slot == kk, b_k, group_base)
    dest_ref[...] = group_base.astype(jnp.int32) + rank_ref[...]

    nl = tiles_ref.shape[1]
    start = lax.broadcasted_iota(jnp.int32, (ne, nl), 1).astype(f32) * bm
    te = jnp.minimum(jnp.sum(jnp.where(start >= ends, 1.0, 0.0), axis=0, keepdims=True), ne - 1.0)
    at_te = lax.broadcasted_iota(jnp.int32, (ne, nl), 0).astype(f32) == te
    cnt_te = jnp.sum(jnp.where(at_te, cnt, 0.0), axis=0, keepdims=True)
    base_te = jnp.sum(jnp.where(at_te, base, 0.0), axis=0, keepdims=True)
    valid = jnp.clip(cnt_te - (start[0:1] - base_te), 0.0, float(bm))
    next_group = jnp.sum(jnp.where(at_te, ends, 0.0), axis=0, keepdims=True) * (1.0 / bm)
    last_used = jnp.sum(padded, axis=0, keepdims=True) * (1.0 / bm) - 1.0
    srow = lax.broadcasted_iota(jnp.int32, (8, nl), 0)
    table = jnp.where(srow == 0, te, jnp.where(srow == 1, valid, jnp.where(srow == 2, next_group,
                                                                           jnp.where(srow == 3, last_used, 0.0))))
    tiles_ref[...] = table.astype(jnp.int32)


def _route(ids, rank, cnt, bm, n_tiles):
    T = ids.shape[1]
    tc = 2048
    nl = -(-n_tiles // LANES) * LANES
    tok = lambda i: (0, i)
    const = lambda i: (0, 0)
    return pl.pallas_call(
        functools.partial(_route_kernel, bm=bm),
        grid=(T // tc,),
        in_specs=[pl.BlockSpec((8, tc), tok), pl.BlockSpec((8, tc), tok),
                  pl.BlockSpec((N_EXPERTS, 1), const)],
        out_specs=[pl.BlockSpec((8, tc), tok), pl.BlockSpec((8, nl), const)],
        out_shape=[jax.ShapeDtypeStruct((8, T), jnp.int32), jax.ShapeDtypeStruct((8, nl), jnp.int32)],
        compiler_params=_cparams(("arbitrary",)),
        name="route",
    )(ids, rank, cnt)


def _sc_workers():
    info = plsc.get_sparse_core_info()
    return info.num_cores, info.num_subcores


def _sc_dispatch(rows, dest_flat, n_out):
    T, D = rows.shape
    nc, ns = _sc_workers()
    per_w = T // (nc * ns)
    n_chunks = per_w // SC_CHUNK
    assert n_chunks % 2 == 0
    mesh = plsc.VectorSubcoreMesh(core_axis_name="c", subcore_axis_name="s")

    @functools.partial(
        pl.kernel, mesh=mesh,
        out_type=jax.ShapeDtypeStruct((n_out, D), rows.dtype),
        scratch_types=([pltpu.VMEM((SC_CHUNK,), jnp.int32)] * (2 * TOP_K)
                       + [pltpu.VMEM((SC_CHUNK, D), rows.dtype)] * 2
                       + [pltpu.SemaphoreType.DMA] * (2 * TOP_K)),
    )
    def k(rows_hbm, dest_hbm, out_hbm, *scratch):
        idx_v = (scratch[:TOP_K], scratch[TOP_K:2 * TOP_K])
        rows_v = scratch[2 * TOP_K:2 * TOP_K + 2]
        sems = (scratch[2 * TOP_K + 2:3 * TOP_K + 2], scratch[3 * TOP_K + 2:])
        wid = lax.axis_index("s") * nc + lax.axis_index("c")
        base = wid * per_w

        def load(chunk, b):
            off = base + chunk * SC_CHUNK
            pltpu.sync_copy(rows_hbm.at[pl.ds(off, SC_CHUNK)], rows_v[b])
            for kk in range(TOP_K):
                pltpu.sync_copy(dest_hbm.at[pl.ds(kk * T + off, SC_CHUNK)], idx_v[b][kk])

        load(0, 0)

        @pl.loop(0, n_chunks, step=2)
        def _(j):
            for b in range(2):
                copies = [pltpu.async_copy(rows_v[b], out_hbm.at[idx_v[b][kk]], sems[b][kk])
                          for kk in range(TOP_K)]

                @pl.when(j + b + 1 < n_chunks)
                def _():
                    load(j + b + 1, 1 - b)

                for cp in copies:
                    cp.wait()

    return k(rows, dest_flat)


def _sc_gather(table, idx):
    _, D = table.shape
    N = idx.shape[0]
    nc, ns = _sc_workers()
    per_w = N // (nc * ns)
    n_chunks = per_w // SC_CHUNK
    assert n_chunks % 2 == 0
    mesh = plsc.VectorSubcoreMesh(core_axis_name="c", subcore_axis_name="s")

    @functools.partial(
        pl.kernel, mesh=mesh,
        out_type=jax.ShapeDtypeStruct((N, D), table.dtype),
        scratch_types=([pltpu.VMEM((SC_CHUNK,), jnp.int32)] * 2
                       + [pltpu.VMEM((SC_CHUNK, D), table.dtype)] * 2
                       + [pltpu.SemaphoreType.DMA] * 2),
    )
    def k(table_hbm, idx_hbm, out_hbm, idx0, idx1, buf0, buf1, sem0, sem1):
        idxs, bufs, sems = (idx0, idx1), (buf0, buf1), (sem0, sem1)
        wid = lax.axis_index("s") * nc + lax.axis_index("c")
        base = wid * per_w

        def gather(b):
            return pltpu.make_async_copy(table_hbm.at[idxs[b]], bufs[b], sems[b])

        def start(chunk, b):
            pltpu.sync_copy(idx_hbm.at[pl.ds(base + chunk * SC_CHUNK, SC_CHUNK)], idxs[b])
            gather(b).start()

        def finish(chunk, b):
            gather(b).wait()
            pltpu.sync_copy(bufs[b], out_hbm.at[pl.ds(base + chunk * SC_CHUNK, SC_CHUNK)])

        start(0, 0)

        @pl.loop(0, n_chunks, step=2)
        def _(j):
            start(j + 1, 1)
            finish(j, 0)

            @pl.when(j + 2 < n_chunks)
            def _():
                start(j + 2, 0)

            finish(j + 1, 1)

    return k(table, idx)


def _ffn_kernel(te_ref, tv_ref, tn_ref, tl_ref, xs_ref, wg_hbm, bg_ref, wu_hbm, bu_ref, wd_hbm, bd_ref, eo_ref,
                w_stage, wg_b, wu_b, wd_b, sems):
    i = pl.program_id(0)
    n_tiles = pl.num_programs(0)
    w_hbm = (wg_hbm, wu_hbm, wd_hbm)
    w_b = (wg_b, wu_b, wd_b)

    def fetch(e):
        return [pltpu.make_async_copy(w_hbm[m].at[e], w_stage.at[m], sems.at[m]) for m in range(3)]

    @pl.when(i == 0)
    def _():
        for cp in fetch(te_ref[0]):
            cp.start()

    prev = te_ref[jnp.maximum(i - 1, 0)]

    @pl.when((i == 0) | (te_ref[i] != prev))
    def _():
        for m, cp in enumerate(fetch(te_ref[i])):
            cp.wait()
            w_b[m][...] = w_stage[m].astype(bf16)
        nxt = tn_ref[i]
        e_nxt = te_ref[jnp.minimum(nxt, n_tiles - 1)]

        @pl.when((nxt > i) & (nxt < n_tiles) & (e_nxt != te_ref[i]))
        def _():
            for cp in fetch(e_nxt):
                cp.start()

    valid = tv_ref[i]

    def ffn_rows(r0, m):
        blocks = [(r0 + o, min(MOE_BLOCK, m)) for o in range(0, m, MOE_BLOCK)]
        xs, gates, ups, acts = [], [], [], []
        for b0, bm_ in blocks:
            row = lax.broadcasted_iota(jnp.int32, (bm_, 1), 0) + b0
            xw = jnp.where(row < valid, xs_ref[b0:b0 + bm_, :], 0)
            x_hi, x_lo = _unpack_rows(xw)
            xs.append(jnp.concatenate([x_hi, x_lo], axis=1).astype(bf16))
        for x in xs:
            gates.append(jnp.minimum(jnp.dot(x, wg_b[...], preferred_element_type=f32) + bg_ref[...],
                                     SWIGLU_LIMIT))
            ups.append(jnp.clip(jnp.dot(x, wu_b[...], preferred_element_type=f32) + bu_ref[...],
                                -SWIGLU_LIMIT, SWIGLU_LIMIT))
        for gate, up in zip(gates, ups):
            acts.append(((up + 1.0) * gate * jax.nn.sigmoid(SWIGLU_ALPHA * gate)).astype(bf16))
        for (b0, bm_), act in zip(blocks, acts):
            out = jnp.dot(act, wd_b[...], preferred_element_type=f32) + bd_ref[...]
            eo_ref[b0:b0 + bm_, :] = _pack_rows(out)

    def zero_rows(r0, m):
        eo_ref[r0:r0 + m, :] = jnp.zeros((m, eo_ref.shape[1]), eo_ref.dtype)

    for r0 in range(0, xs_ref.shape[0], MOE_PASS):
        lo = 0
        for m in MOE_PASS_SIZES:
            @pl.when((valid > r0 + lo) & ((valid <= r0 + m) | (m == MOE_PASS)))
            def _(r0=r0, m=m):
                ffn_rows(r0, m)
                if m < MOE_PASS:
                    zero_rows(r0 + m, MOE_PASS - m)
            lo = m

        @pl.when(valid <= r0)
        def _(r0=r0):
            zero_rows(r0, MOE_PASS)


def _expert_ffn(tile_expert, tile_valid, tile_next, last_used, xs, w_gate, b_gate, w_up, b_up, w_down, b_down):
    n_rows, dp = xs.shape
    d = 2 * dp
    bm = MOE_TILE
    d_e = w_gate.shape[-1]
    assert d == d_e
    bspec = lambda n_: pl.BlockSpec((None, 1, n_), lambda i, te, tv, tn, tl: (te[i], 0, 0))
    hbm = pl.BlockSpec(memory_space=pl.ANY)
    grid_spec = pltpu.PrefetchScalarGridSpec(
        num_scalar_prefetch=4,
        grid=(n_rows // bm,),
        in_specs=[pl.BlockSpec((bm, dp), lambda i, te, tv, tn, tl: (jnp.minimum(i, tl[0]), 0)),
                  hbm, bspec(d_e), hbm, bspec(d_e), hbm, bspec(d)],
        out_specs=pl.BlockSpec((bm, dp), lambda i, te, tv, tn, tl: (i, 0)),
        scratch_shapes=[pltpu.VMEM((3, d, d_e), f32),
                        pltpu.VMEM((d, d_e), bf16), pltpu.VMEM((d, d_e), bf16), pltpu.VMEM((d_e, d), bf16),
                        pltpu.SemaphoreType.DMA((3,))],
    )
    return pl.pallas_call(
        _ffn_kernel,
        grid_spec=grid_spec,
        out_shape=jax.ShapeDtypeStruct((n_rows, dp), jnp.int32),
        compiler_params=_cparams(("arbitrary",)),
        name="expert_ffn",
    )(tile_expert, tile_valid, tile_next, last_used, xs, w_gate, b_gate.reshape(N_EXPERTS, 1, d_e), w_up,
      b_up.reshape(N_EXPERTS, 1, d_e), w_down, b_down.reshape(N_EXPERTS, 1, d))


def _combine_kernel(eg_ref, wts_ref, x1_ref, mod_ref, lg_ref, lb_ref, o_ref):
    w = wts_ref[...]
    y_hi, y_lo = _unpack_rows(eg_ref[0])
    y_hi, y_lo = y_hi * w[:, 0:1], y_lo * w[:, 0:1]
    for kk in range(1, TOP_K):
        e_hi, e_lo = _unpack_rows(eg_ref[kk])
        y_hi = y_hi + e_hi * w[:, kk:kk + 1]
        y_lo = y_lo + e_lo * w[:, kk:kk + 1]
    y = jnp.concatenate([y_hi, y_lo], axis=1)
    z = DEEPNORM_ALPHA * x1_ref[...] + mod_ref[0] * y
    mu = jnp.mean(z, axis=-1, keepdims=True)
    zc = z - mu
    var = jnp.mean(zc * zc, axis=-1, keepdims=True)
    o_ref[...] = zc * lax.rsqrt(var + LN_EPS) * lg_ref[...] + lb_ref[...]


def _combine_ln(eg, wts, x1, mod3, lg, lb, tokens_per_batch):
    T = x1.shape[0]
    tm = TOKEN_TILE
    tiles_per_batch = tokens_per_batch // tm
    const = lambda i: (0, 0)
    return pl.pallas_call(
        _combine_kernel,
        grid=(T // tm,),
        in_specs=[pl.BlockSpec((TOP_K, tm, D_MODEL // 2), lambda i: (0, i, 0)),
                  pl.BlockSpec((tm, LANES), lambda i: (i, 0)),
                  pl.BlockSpec((tm, D_MODEL), lambda i: (i, 0)),
                  pl.BlockSpec((1, 1, D_MODEL), lambda i: (i // tiles_per_batch, 0, N_MOD - 1)),
                  pl.BlockSpec((1, D_MODEL), const),
                  pl.BlockSpec((1, D_MODEL), const)],
        out_specs=pl.BlockSpec((tm, D_MODEL), lambda i: (i, 0)),
        out_shape=jax.ShapeDtypeStruct((T, D_MODEL), f32),
        compiler_params=_cparams(("arbitrary",)),
        name="combine_ln",
    )(eg, wts, x1, mod3, lg, lb)


def _block_diag(w):
    n, c, d = w.shape
    eye = jnp.eye(n, dtype=w.dtype)
    return jnp.einsum('ncd,nm->ncmd', w, eye).reshape(n * c, n * d)


def kernel(x, c, ctx, c_ctx, w_ada, b_ada, w_in, conv_w, conv_b, lru_wa, lru_ba, lru_wx, lru_bx,
           lru_lam, gla_wa, gla_ba, gla_norm_g, w_out, ln1_g, ln1_b, w_router, b_router, w_gate,
           b_gate, w_up, b_up, w_down, b_down, ln2_g, ln2_b):
    B, L, D = x.shape
    Lc = ctx.shape[1]
    T = B * L
    rows = L // GRID_W
    l = 0

    cpad = jnp.zeros((8, D), f32).at[0:B].set(c).at[B].set(c_ctx)
    w_cat = jnp.pad(w_in[l], ((0, 0), (0, D_PROJ - w_in.shape[-1]))).astype(bf16)
    wg = [jnp.concatenate([_block_diag(lru_wa[l, d]), _block_diag(lru_wx[l, d])], axis=1).astype(bf16)
          for d in range(2)]
    bg = [jnp.concatenate([lru_ba[l, d], lru_bx[l, d]])[None] for d in range(2)]
    lam = [lru_lam[l, d][None] for d in range(2)]
    wa = [jnp.pad(gla_wa[l, d], ((0, A_LOW_PAD - GATE_RANK), (0, 0))).astype(bf16) for d in range(2)]
    ba = [gla_ba[l, d][None] for d in range(2)]
    cw, cb = conv_w[l], conv_b[l][None]
    wr_t = w_router[l].T
    wr_hi = wr_t.astype(bf16)
    wr_lo = (wr_t - wr_hi.astype(f32)).astype(bf16)
    br = b_router[l][:, None]
    c_xl, c_gl, c_pg = (0, D_LRU), (D_LRU, 2 * D_LRU), (2 * D_LRU, D_PROJ)

    mod3 = _ada_mod(cpad, w_ada[l], b_ada[l]).reshape(8, 1, N_MOD * D)

    tiles_per_batch = L // INPROJ_TILE
    xl_c, pg_c = _inproj_rows(ctx.reshape(B * Lc, D), mod3, w_cat, lambda i: B, (c_xl, c_pg), (f32, bf16))
    zero_h = jnp.zeros((B, 1, D_LRU), f32)
    zero_s = jnp.zeros((B, GLA_DV, D_GLA_K), f32)
    pg_c = pg_c.reshape(B, Lc // GLA_CHUNK, GLA_CHUNK, D_PG)
    h_ctx = []
    for d in range(2):
        _, _, hf = _lru_pass(xl_c, wg[d], bg[d], lam[d], zero_h, n_batch=B, tile=Lc, reverse=bool(d),
                             conv=(cw, cb, Lc))
        h_ctx.append(hf)
    s_ctx = _gla_pass(pg_c, wa, ba, (zero_s, zero_s))

    xt = x.reshape(T, D)
    xl, gl = _inproj_rows(xt, mod3, w_cat, lambda i: i // tiles_per_batch, (c_xl, c_gl), (f32, bf16))
    hb, u_lru, _ = _lru_pass(xl, wg[1], bg[1], lam[1], h_ctx[1], n_batch=B, tile=LRU_TILE, reverse=True,
                             conv=(cw, cb, GRID_W))
    y_lru, _ = _lru_pass(u_lru, wg[0], bg[0], lam[0], h_ctx[0], n_batch=B, tile=LRU_TILE, reverse=False,
                         merge_with=(hb, gl))
    pg = _inproj_cols(x.reshape(B, rows, GRID_W, D), mod3, w_cat, c_pg)
    y_gla = _gla_pass(pg, wa, ba, s_ctx, gn=gla_norm_g[l][None])

    x1, hp, ids, rank, wts, cnt = _outproj_router(
        y_lru, y_gla, xt, mod3, w_out[l].astype(bf16), ln1_g[l][None], ln1_b[l][None], wr_hi, wr_lo, br, L)

    n_tiles = T * TOP_K // MOE_TILE + N_EXPERTS
    dest, tiles = _route(ids, rank, cnt, MOE_TILE, n_tiles)
    xs = _sc_dispatch(hp, dest.reshape(-1), n_tiles * MOE_TILE)
    eo = _expert_ffn(tiles[0, :n_tiles], tiles[1, :n_tiles], tiles[2, :n_tiles], tiles[3, :1], xs, w_gate[l],
                     b_gate[l], w_up[l], b_up[l], w_down[l], b_down[l])
    eg = _sc_gather(eo, dest[:TOP_K].reshape(-1)).reshape(TOP_K, T, D // 2)
    out = _combine_ln(eg, wts, x1, mod3, ln2_g[l][None], ln2_b[l][None], L)
    return out.reshape(B, L, D)
```

```python
import functools

import jax
import jax.numpy as jnp
from jax import lax
from jax.experimental import pallas as pl
from jax.experimental.pallas import tpu as pltpu
from jax.experimental.pallas import tpu_sc as plsc

D_MODEL = 1024
DEPTH = 1
GRID_W = 64
D_LRU = 512
LRU_BLOCKS = 8
CONV_W = 4
LRU_C = 8.0
GLA_HEADS = 4
D_GLA_V = 512
D_GLA_K = 256
GLA_DK = 64
GLA_DV = 128
GATE_RANK = 16
GATE_TAU = 16.0
GLA_CHUNK = 64
N_EXPERTS = 32
TOP_K = 4
SWIGLU_LIMIT = 7.0
SWIGLU_ALPHA = 1.702
N_MOD = 6
DEEPNORM_ALPHA = (2.0 * DEPTH) ** 0.25
LN_EPS = 1e-5
RMS_EPS = 1e-6

LANES = 128
A_LOW_PAD = LANES
D_PG = 2 * D_GLA_K + 2 * D_GLA_V + A_LOW_PAD
D_PROJ = 2 * D_LRU + D_PG
TOKEN_TILE = 512
INPROJ_TILE = 1024
LRU_TILE = 512
MOE_TILE = 1024
MOE_PASS = 512
MOE_PASS_SIZES = (128, 256, MOE_PASS)
MOE_BLOCK = 256
OUTPROJ_PARTS = 1
SC_CHUNK = 64
VMEM_LIMIT = 48 * 1024 * 1024

f32 = jnp.float32
bf16 = jnp.bfloat16


def _cparams(sem):
    return pltpu.CompilerParams(dimension_semantics=sem, vmem_limit_bytes=VMEM_LIMIT)


def _ada_kernel(c_ref, w_ref, b_ref, o_ref):
    s = c_ref[...]
    s = s * jax.nn.sigmoid(s)
    o_ref[...] = jnp.dot(s.astype(bf16), w_ref[...], preferred_element_type=f32) + b_ref[...]


def _ada_mod(cpad, w, b):
    n = w.shape[1]
    tn = 1024
    return pl.pallas_call(
        _ada_kernel,
        grid=(n // tn,),
        in_specs=[pl.BlockSpec((8, D_MODEL), lambda j: (0, 0)),
                  pl.BlockSpec((D_MODEL, tn), lambda j: (0, j)),
                  pl.BlockSpec((1, tn), lambda j: (0, j))],
        out_specs=pl.BlockSpec((8, tn), lambda j: (0, j)),
        out_shape=jax.ShapeDtypeStruct((8, n), f32),
        compiler_params=_cparams(("arbitrary",)),
        name="ada_mod",
    )(cpad, w, b.reshape(1, n))


def _inproj_kernel(x_ref, mod_ref, w_ref, *out_refs, parts, col_major):
    sh = mod_ref[0, :, 0:D_MODEL]
    sc = mod_ref[0, :, D_MODEL:2 * D_MODEL]
    if col_major:
        x = jnp.concatenate([x_ref[0, :, j, :] for j in range(x_ref.shape[2])], axis=0)
    else:
        x = x_ref[...]
    u = (x * (1.0 + sc) + sh).astype(bf16)
    for (lo, hi), o_ref in zip(parts, out_refs):
        p = jnp.dot(u, w_ref[:, lo:hi], preferred_element_type=f32).astype(o_ref.dtype)
        o_ref[...] = p.reshape(o_ref.shape)


def _inproj_rows(xt, mod3, w_cat, batch_of_tile, parts, dtypes):
    T = xt.shape[0]
    tm = INPROJ_TILE
    return pl.pallas_call(
        functools.partial(_inproj_kernel, parts=parts, col_major=False),
        grid=(T // tm,),
        in_specs=[pl.BlockSpec((tm, D_MODEL), lambda i: (i, 0)),
                  pl.BlockSpec((1, 1, 2 * D_MODEL), lambda i: (batch_of_tile(i), 0, 0)),
                  pl.BlockSpec((D_MODEL, D_PROJ), lambda i: (0, 0), pipeline_mode=pl.Buffered(1))],
        out_specs=[pl.BlockSpec((tm, hi - lo), lambda i: (i, 0)) for lo, hi in parts],
        out_shape=[jax.ShapeDtypeStruct((T, hi - lo), dt) for (lo, hi), dt in zip(parts, dtypes)],
        compiler_params=_cparams(("arbitrary",)),
        name="inproj_rows",
    )(xt, mod3, w_cat)


def _inproj_cols(x4, mod3, w_cat, part):
    n_batch, rows, cols, _ = x4.shape
    lo, hi = part
    cb = TOKEN_TILE // rows
    return pl.pallas_call(
        functools.partial(_inproj_kernel, parts=(part,), col_major=True),
        grid=(n_batch, cols // cb),
        in_specs=[pl.BlockSpec((1, rows, cb, D_MODEL), lambda b, n: (b, 0, n, 0)),
                  pl.BlockSpec((1, 1, 2 * D_MODEL), lambda b, n: (b, 0, 0)),
                  pl.BlockSpec((D_MODEL, D_PROJ), lambda b, n: (0, 0), pipeline_mode=pl.Buffered(1))],
        out_specs=[pl.BlockSpec((1, cb, rows, hi - lo), lambda b, n: (b, n, 0, 0))],
        out_shape=[jax.ShapeDtypeStruct((n_batch, cols, rows, hi - lo), bf16)],
        compiler_params=_cparams(("arbitrary", "arbitrary")),
        name="inproj_cols",
    )(x4, mod3, w_cat)[0]


def _gelu_tanh(x):
    return 0.5 * x * (1.0 + jnp.tanh(0.7978845608028654 * (x + 0.044715 * (x * x * x))))


def _lru_kernel(*refs, row_w, reverse, merge):
    if merge:
        u_ref, wg_ref, bg_ref, lam_ref, h0_ref, hb_ref, gl_ref, out_ref, hfin_ref, carry, h_nat = refs
    else:
        xl_ref, cw_ref, cb_ref, wg_ref, bg_ref, lam_ref, h0_ref, out_ref, u_out_ref, hfin_ref, carry = refs
    t = pl.program_id(1)

    @pl.when(t == 0)
    def _():
        carry[...] = h0_ref[0]

    n_seg = 8
    blk = lambda v, j: v[j * n_seg:(j + 1) * n_seg]
    if merge:
        u = u_ref[...]
        tt, ch = u.shape
        seg = tt // n_seg
    else:
        _, seg, ch = xl_ref.shape
        tt = n_seg * seg
        x = jnp.concatenate([xl_ref[:, j, :] for j in range(seg)], axis=0)
        segs_per_row = row_w // seg
        s_idx = lax.broadcasted_iota(jnp.int32, (n_seg, ch), 0)
        has_prev = (s_idx % segs_per_row) != 0
        has_next = (s_idx % segs_per_row) != segs_per_row - 1
        from_prev = lambda v: jnp.where(has_prev, pltpu.roll(v, 1, 0), 0.0)
        from_next = lambda v: jnp.where(has_next, pltpu.roll(v, n_seg - 1, 0), 0.0)
        x_m1 = jnp.concatenate([from_prev(blk(x, seg - 1)), x[:tt - n_seg]], axis=0)
        x_m2 = jnp.concatenate([from_prev(blk(x, seg - 2)), from_prev(blk(x, seg - 1)), x[:tt - 2 * n_seg]],
                               axis=0)
        x_p1 = jnp.concatenate([x[n_seg:], from_next(blk(x, 0))], axis=0)
        cw = cw_ref[...]
        u = cb_ref[...] + x_m2 * cw[0:1] + x_m1 * cw[1:2] + x * cw[2:3] + x_p1 * cw[3:4]
        u_out_ref[...] = u
    g = jnp.dot(u.astype(bf16), wg_ref[...], preferred_element_type=f32) + bg_ref[...]
    lam = lam_ref[...]
    softplus_neg = jnp.maximum(-lam, 0.0) + jnp.log(1.0 + jnp.exp(-jnp.abs(lam)))
    log_a_scale = -LRU_C * softplus_neg

    def recurrence_terms(j):
        g_j, u_j = blk(g, j), blk(u, j)
        log_a = log_a_scale * jax.nn.sigmoid(g_j[:, 0:ch])
        a_j = jnp.exp(log_a)
        series = log_a * (-2.0 + log_a * (-2.0 + log_a * (-4.0 / 3.0)))
        one_minus_a2 = jnp.where(log_a > -0.005, series, 1.0 - a_j * a_j)
        root = jnp.where(one_minus_a2 > 0.0, one_minus_a2 * lax.rsqrt(one_minus_a2), 0.0)
        return a_j, root * (jax.nn.sigmoid(g_j[:, ch:2 * ch]) * u_j)

    half = seg // 2
    h_loc, a_cum = [None] * seg, [None] * seg
    h_run, a_run = [None, None], [None, None]
    for i in range(half):
        for p in range(2):
            j = p * half + (half - 1 - i if reverse else i)
            a_j, b_j = recurrence_terms(j)
            h_run[p] = b_j if i == 0 else a_j * h_run[p] + b_j
            a_run[p] = a_j if i == 0 else a_j * a_run[p]
            h_loc[j], a_cum[j] = h_run[p], a_run[p]
    pieces = [(s, p) for s in range(n_seg) for p in range(2)]
    if reverse:
        pieces.reverse()
    entering = [[None] * n_seg, [None] * n_seg]
    state = carry[...]
    for s, p in pieces:
        entering[p][s] = state
        state = h_run[p][s:s + 1] + a_run[p][s:s + 1] * state
    carry[...] = state
    hfin_ref[0] = state
    enter = [jnp.concatenate(e, axis=0) for e in entering]
    h_rows = [h_loc[j] + a_cum[j] * enter[j // half] for j in range(seg)]
    if merge:
        for j in range(seg):
            h_nat[:, j, :] = h_rows[j] + blk(hb_ref, j)
        h = h_nat[...].reshape(tt, ch)
        out_ref[...] = (h * _gelu_tanh(gl_ref[...].astype(f32))).astype(out_ref.dtype)
    else:
        out_ref[...] = jnp.concatenate(h_rows, axis=0)


def _lru_pass(x_in, wg, bg, lam, h0, *, n_batch, tile, reverse, conv=None, merge_with=None):
    T = x_in.shape[0]
    nt = T // n_batch // tile
    merge = merge_with is not None
    assert merge != (conv is not None)
    seg = tile // 8

    def tok(b, t):
        return (b * nt + (nt - 1 - t if reverse else t), 0)

    tile_spec = pl.BlockSpec((tile, D_LRU), tok)
    const = lambda b, t: (0, 0)
    gate_specs = [pl.BlockSpec((D_LRU, 2 * D_LRU), const),
                  pl.BlockSpec((1, 2 * D_LRU), const),
                  pl.BlockSpec((1, D_LRU), const),
                  pl.BlockSpec((1, 1, D_LRU), lambda b, t: (b, 0, 0))]
    state_spec = pl.BlockSpec((1, 1, D_LRU), lambda b, t: (b, 0, 0))
    state_shape = jax.ShapeDtypeStruct((n_batch, 1, D_LRU), f32)
    scratch = [pltpu.VMEM((1, D_LRU), f32)]
    if merge:
        hb, gl = merge_with
        row_w = None
        in_specs = [tile_spec] + gate_specs + [tile_spec, tile_spec]
        args = [x_in, wg, bg, lam, h0, hb, gl]
        scratch += [pltpu.VMEM((8, seg, D_LRU), f32)]
        out_specs = [tile_spec, state_spec]
        out_shape = [jax.ShapeDtypeStruct((T, D_LRU), bf16), state_shape]
    else:
        conv_w, conv_b, row_w = conv
        assert row_w % seg == 0 and seg >= 2
        in_specs = [pl.BlockSpec((8, seg, D_LRU), lambda b, t: tok(b, t) + (0,)),
                    pl.BlockSpec((CONV_W, D_LRU), const),
                    pl.BlockSpec((1, D_LRU), const)] + gate_specs
        args = [x_in.reshape(T // seg, seg, D_LRU), conv_w, conv_b, wg, bg, lam, h0]
        out_specs = [tile_spec, tile_spec, state_spec]
        out_shape = [jax.ShapeDtypeStruct((T, D_LRU), f32), jax.ShapeDtypeStruct((T, D_LRU), f32), state_shape]
    return pl.pallas_call(
        functools.partial(_lru_kernel, row_w=row_w, reverse=reverse, merge=merge),
        grid=(n_batch, nt),
        in_specs=in_specs,
        out_specs=out_specs,
        out_shape=out_shape,
        scratch_shapes=scratch,
        compiler_params=_cparams(("arbitrary", "arbitrary")),
        name="lru_merge" if merge else "lru_scan",
    )(*args)


_PG_Q, _PG_K, _PG_V = 0, D_GLA_K, 2 * D_GLA_K
_PG_G, _PG_A = 2 * D_GLA_K + D_GLA_V, 2 * D_GLA_K + 2 * D_GLA_V


def _gla_chunks(streams):
    ck = GLA_CHUNK
    nh = GLA_HEADS
    head_of_lane = lax.broadcasted_iota(jnp.int32, (1, D_GLA_K), 1) // GLA_DK
    nt_dims = (((1,), (1,)), ((), ()))
    tn_dims = (((0,), (0,)), ((), ()))
    ii = lax.broadcasted_iota(jnp.int32, (nh * ck, ck), 0) & (ck - 1)
    jj = lax.broadcasted_iota(jnp.int32, (nh * ck, ck), 1)

    units = []
    for pg_ref, wa_ref, ba_ref, state, reverse in streams:
        n_batch = pg_ref.shape[0]
        rows_all = n_batch * ck
        a_low = jnp.concatenate([pg_ref[bi, :, _PG_A:_PG_A + A_LOW_PAD] for bi in range(n_batch)], axis=0)
        z = jnp.dot(a_low, wa_ref[...], preferred_element_type=f32) + ba_ref[...]
        bcum_all = (jnp.minimum(z, 0.0) - jnp.log(1.0 + jnp.exp(-jnp.abs(z)))) * (1.0 / GATE_TAU)
        step = lax.broadcasted_iota(jnp.int32, (rows_all, D_GLA_K), 0) & (ck - 1)
        d = 1
        while d < ck:
            if reverse:
                bcum_all = bcum_all + jnp.where(step < ck - d, pltpu.roll(bcum_all, rows_all - d, 0), 0.0)
            else:
                bcum_all = bcum_all + jnp.where(step >= d, pltpu.roll(bcum_all, d, 0), 0.0)
            d *= 2
        seen = (jj >= ii) if reverse else (jj <= ii)
        for bi in range(n_batch):
            units.append(dict(pg=pg_ref, bi=bi, state=state, reverse=reverse, seen=seen,
                              bcum=bcum_all[bi * ck:(bi + 1) * ck]))

    for u in units:
        pg_ref, bi, bcum = u["pg"], u["bi"], u["bcum"]
        q = pg_ref[bi, :, _PG_Q:_PG_K].astype(f32) * (GLA_DK ** -0.5)
        k = pg_ref[bi, :, _PG_K:_PG_V].astype(f32)
        u["btot"] = bcum[0:1] if u["reverse"] else bcum[ck - 1:ck]
        q_dec = q * jnp.exp(bcum)
        k_dec = (k * jnp.exp(-bcum)).astype(bf16)
        k_end = k * jnp.exp(u["btot"] - bcum)
        u["s_t"] = u["state"][bi]
        by_head = lambda t: jnp.concatenate([jnp.where(head_of_lane == hd, t, 0.0) for hd in range(nh)],
                                            axis=0).astype(bf16)
        u["q_blk"] = by_head(q_dec)
        u["k_blk"] = by_head(k_end)
        u["rhs"] = jnp.concatenate([u["s_t"].astype(bf16), k_dec], axis=0)
    for u in units:
        u["qs"] = lax.dot_general(u["q_blk"], u["rhs"], nt_dims, preferred_element_type=f32)
    for u in units:
        u["scores"] = jnp.where(u["seen"], u["qs"][:, GLA_DV:GLA_DV + ck], 0.0).astype(bf16)
    for u in units:
        v = u["pg"][u["bi"], :, _PG_V:_PG_G]
        u["intra"] = [jnp.dot(u["scores"][hd * ck:(hd + 1) * ck], v[:, hd * GLA_DV:(hd + 1) * GLA_DV],
                              preferred_element_type=f32) for hd in range(nh)]
        v_stack = jnp.concatenate([v[:, hd * GLA_DV:(hd + 1) * GLA_DV] for hd in range(nh)], axis=0)
        u["kv_t"] = lax.dot_general(v_stack, u["k_blk"], tn_dims, preferred_element_type=f32)
    for u in units:
        u["outs"] = [u["intra"][hd] + u["qs"][hd * ck:(hd + 1) * ck, 0:GLA_DV] for hd in range(nh)]
        u["state"][u["bi"]] = u["s_t"] * jnp.exp(u["btot"]) + u["kv_t"]
    n_batch = streams[0][0].shape[0]
    return [[u["outs"] for u in units[si * n_batch:(si + 1) * n_batch]] for si in range(len(streams))]


def _gla_gate(o_heads, g, gn):
    normed = []
    for oh in o_heads:
        ms = jnp.mean(oh * oh, axis=-1, keepdims=True)
        normed.append(oh * lax.rsqrt(ms + RMS_EPS))
    return jnp.concatenate(normed, axis=-1) * gn * (g * jax.nn.sigmoid(g))


def _gla_kernel(*refs, merge):
    if merge:
        (pgf_ref, pgb_ref, waf_ref, baf_ref, wab_ref, bab_ref, s0f_ref, s0b_ref, gn_ref,
         ylo_ref, yhi_ref, st_f, st_b, keep_f, keep_b) = refs
    else:
        (pgf_ref, pgb_ref, waf_ref, baf_ref, wab_ref, bab_ref, s0f_ref, s0b_ref,
         sfin_f_ref, sfin_b_ref, st_f, st_b) = refs
    n = pl.program_id(0)
    n_chunks = pl.num_programs(0)

    @pl.when(n == 0)
    def _():
        st_f[...] = s0f_ref[...]
        st_b[...] = s0b_ref[...]

    outs_f, outs_b = _gla_chunks([(pgf_ref, waf_ref, baf_ref, st_f, False), (pgb_ref, wab_ref, bab_ref, st_b, True)])
    n_batch = pgf_ref.shape[0]
    heads = range(GLA_HEADS)
    if merge:
        half = n_chunks // 2
        m = n_chunks - 1 - n

        @pl.when(n < half)
        def _():
            for bi in range(n_batch):
                keep_f[n, bi] = jnp.concatenate(outs_f[bi], axis=-1).astype(keep_f.dtype)
                keep_b[m - half, bi] = jnp.concatenate(outs_b[bi], axis=-1).astype(keep_b.dtype)

        @pl.when(n >= half)
        def _():
            for bi in range(n_batch):
                kb = keep_b[n - half, bi].astype(f32)
                o_heads = [outs_f[bi][hd] + kb[:, hd * GLA_DV:(hd + 1) * GLA_DV] for hd in heads]
                yhi_ref[bi] = _gla_gate(o_heads, pgf_ref[bi, :, _PG_G:_PG_A].astype(f32), gn_ref[...])
                kf = keep_f[m, bi].astype(f32)
                o_heads = [kf[:, hd * GLA_DV:(hd + 1) * GLA_DV] + outs_b[bi][hd] for hd in heads]
                ylo_ref[bi] = _gla_gate(o_heads, pgb_ref[bi, :, _PG_G:_PG_A].astype(f32), gn_ref[...])
    else:
        @pl.when(n == n_chunks - 1)
        def _():
            sfin_f_ref[...] = st_f[...]
            sfin_b_ref[...] = st_b[...]


def _gla_pass(pg, wa, ba, s0, gn=None):
    n_batch, n_chunks = pg.shape[0], pg.shape[1]
    merge = gn is not None
    half = n_chunks // 2
    const2 = lambda n: (0, 0)
    const3 = lambda n: (0, 0, 0)
    chunk = lambda width, idx: pl.BlockSpec((n_batch, None, GLA_CHUNK, width), lambda n: (0, idx(n), 0, 0))
    w_specs = [pl.BlockSpec((A_LOW_PAD, D_GLA_K), const2), pl.BlockSpec((1, D_GLA_K), const2)]
    s_spec = pl.BlockSpec((n_batch, GLA_DV, D_GLA_K), const3)
    s_shape = jax.ShapeDtypeStruct((n_batch, GLA_DV, D_GLA_K), f32)
    in_specs = [chunk(D_PG, lambda n: n), chunk(D_PG, lambda n: n_chunks - 1 - n)] + w_specs + w_specs + [s_spec,
                                                                                                            s_spec]
    args = [pg, pg, wa[0], ba[0], wa[1], ba[1], s0[0], s0[1]]
    scratch = [pltpu.VMEM((n_batch, GLA_DV, D_GLA_K), f32)] * 2
    if merge:
        assert n_chunks % 2 == 0
        in_specs.append(pl.BlockSpec((1, D_GLA_V), const2))
        args.append(gn)
        out_specs = [chunk(D_GLA_V, lambda n: jnp.minimum(n_chunks - 1 - n, half - 1)),
                     chunk(D_GLA_V, lambda n: jnp.maximum(n - half, 0))]
        y_shape = jax.ShapeDtypeStruct((n_batch, half, GLA_CHUNK, D_GLA_V), f32)
        out_shape = [y_shape, y_shape]
        scratch += [pltpu.VMEM((half, n_batch, GLA_CHUNK, D_GLA_V), bf16)] * 2
    else:
        out_specs = [s_spec, s_spec]
        out_shape = [s_shape, s_shape]
    return pl.pallas_call(
        functools.partial(_gla_kernel, merge=merge),
        grid=(n_chunks,),
        in_specs=in_specs,
        out_specs=out_specs,
        out_shape=out_shape,
        scratch_shapes=scratch,
        compiler_params=_cparams(("arbitrary",)),
        name="gla_merge" if merge else "gla_scan",
    )(*args)


def _pack_rows(v):
    n = v.shape[1] // 2
    w = lax.bitcast_convert_type(v.astype(bf16).astype(f32), jnp.int32)
    return (w[:, :n] & jnp.int32(-65536)) | lax.shift_right_logical(w[:, n:], 16)


def _unpack_rows(w):
    hi = lax.bitcast_convert_type(w & jnp.int32(-65536), f32)
    lo = lax.bitcast_convert_type(lax.shift_left(w, 16), f32)
    return hi, lo


def _outproj_router_kernel(yl_ref, ygl_ref, ygr_ref, x_ref, mod_ref, wo_ref, lg_ref, lb_ref, wrh_ref, wrl_ref,
                           br_ref, x1_ref, hp_ref, ids_ref, rank_ref, wts_ref, cnt_ref, running):
    i = pl.program_id(0)

    @pl.when(i == 0)
    def _():
        running[...] = jnp.zeros_like(running)

    d = D_MODEL
    g1 = mod_ref[0, :, 2 * d:3 * d]
    sh2 = mod_ref[0, :, 3 * d:4 * d]
    sc2 = mod_ref[0, :, 4 * d:5 * d]
    n_parts = OUTPROJ_PARTS
    tm = x_ref.shape[0] // n_parts
    grid_rows = ygl_ref.shape[2] // n_parts
    tok = [slice(p * tm, (p + 1) * tm) for p in range(n_parts)]
    nt_dims = (((1,), (1,)), ((), ()))
    ne = wrh_ref.shape[0]
    expert = lax.broadcasted_iota(jnp.int32, (ne, tm), 0).astype(f32)
    neg_inf = jnp.float32(-jnp.inf)

    ygs = [jnp.concatenate([ref[0, :, r, :] for r in range(p * grid_rows, (p + 1) * grid_rows)
                            for ref in (ygl_ref, ygr_ref)], axis=0).astype(bf16) for p in range(n_parts)]
    ys = [jnp.dot(yl_ref[tok[p], :], wo_ref[0:D_LRU, :], preferred_element_type=f32)
          + jnp.dot(ygs[p], wo_ref[D_LRU:2 * D_LRU, :], preferred_element_type=f32) for p in range(n_parts)]
    hmods = []
    for p in range(n_parts):
        z = DEEPNORM_ALPHA * x_ref[tok[p], :] + g1 * ys[p]
        mu = jnp.mean(z, axis=-1, keepdims=True)
        zc = z - mu
        var = jnp.mean(zc * zc, axis=-1, keepdims=True)
        x1 = zc * lax.rsqrt(var + LN_EPS) * lg_ref[...] + lb_ref[...]
        x1_ref[tok[p], :] = x1
        hmod = x1 * (1.0 + sc2) + sh2
        hp_ref[tok[p], :] = _pack_rows(hmod)
        hmods.append(hmod)

    logits = []
    for hmod in hmods:
        h_hi = hmod.astype(bf16)
        h_lo = (hmod - h_hi.astype(f32)).astype(bf16)
        logits.append(lax.dot_general(wrh_ref[...], h_hi, nt_dims, preferred_element_type=f32)
                      + lax.dot_general(wrh_ref[...], h_lo, nt_dims, preferred_element_type=f32)
                      + lax.dot_general(wrl_ref[...], h_hi, nt_dims, preferred_element_type=f32) + br_ref[...])
    picks = []
    for live in logits:
        sel = jnp.zeros((ne, tm), f32)
        ids, vals = [], []
        for _ in range(TOP_K):
            m = jnp.max(live, axis=0, keepdims=True)
            j = jnp.min(jnp.where(live == m, expert, float(ne)), axis=0, keepdims=True)
            pick = expert == j
            sel = jnp.where(pick, 1.0, sel)
            live = jnp.where(pick, neg_inf, live)
            ids.append(j)
            vals.append(m)
        picks.append((sel, ids, vals))

    ri = lax.broadcasted_iota(jnp.int32, (tm, tm), 0)
    ci = lax.broadcasted_iota(jnp.int32, (tm, tm), 1)
    earlier = (ri < ci).astype(bf16)
    slot = lax.broadcasted_iota(jnp.int32, (8, tm), 0)
    for p, (sel, ids, vals) in enumerate(picks):
        rank_dense = running[...] + jnp.dot(sel.astype(bf16), earlier, preferred_element_type=f32)
        running[...] = running[...] + jnp.sum(sel, axis=1, keepdims=True)
        exps = [jnp.exp(vk - vals[0]) for vk in vals]
        denom = exps[0] + exps[1] + exps[2] + exps[3]
        ids_out = jnp.zeros((8, tm), f32)
        rank_out = jnp.zeros((8, tm), f32)
        wts_out = jnp.zeros((8, tm), f32)
        for kk in range(TOP_K):
            rk = jnp.sum(jnp.where(expert == ids[kk], rank_dense, 0.0), axis=0, keepdims=True)
            here = slot == kk
            ids_out = jnp.where(here, ids[kk], ids_out)
            rank_out = jnp.where(here, rk, rank_out)
            wts_out = jnp.where(here, exps[kk] / denom, wts_out)
        ids_ref[:, tok[p]] = ids_out.astype(jnp.int32)
        rank_ref[:, tok[p]] = rank_out.astype(jnp.int32)
        wts_ref[tok[p], :] = jnp.concatenate([wts_out, jnp.zeros((LANES - 8, tm), f32)], axis=0).T
    cnt_ref[...] = running[...]


def _outproj_router(yl, yg, xt, mod3, wo, lg, lb, wr_hi, wr_lo, br, tokens_per_batch):
    T = xt.shape[0]
    tm = TOKEN_TILE
    tiles_per_batch = tokens_per_batch // tm
    tokrow = lambda i: (i, 0)
    const = lambda i: (0, 0)
    half_cols = yg[0].shape[1]
    rows_per_tile = tm // (2 * half_cols)
    yg_spec = pl.BlockSpec((1, half_cols, rows_per_tile, D_GLA_V),
                           lambda i: (i // tiles_per_batch, 0, i % tiles_per_batch, 0))
    return pl.pallas_call(
        _outproj_router_kernel,
        grid=(T // tm,),
        in_specs=[pl.BlockSpec((tm, D_LRU), tokrow),
                  yg_spec, yg_spec,
                  pl.BlockSpec((tm, D_MODEL), tokrow),
                  pl.BlockSpec((1, 1, N_MOD * D_MODEL), lambda i: (i // tiles_per_batch, 0, 0)),
                  pl.BlockSpec((D_MODEL, D_MODEL), const),
                  pl.BlockSpec((1, D_MODEL), const),
                  pl.BlockSpec((1, D_MODEL), const),
                  pl.BlockSpec((N_EXPERTS, D_MODEL), const),
                  pl.BlockSpec((N_EXPERTS, D_MODEL), const),
                  pl.BlockSpec((N_EXPERTS, 1), const)],
        out_specs=[pl.BlockSpec((tm, D_MODEL), tokrow),
                   pl.BlockSpec((tm, D_MODEL // 2), tokrow),
                   pl.BlockSpec((8, tm), lambda i: (0, i)),
                   pl.BlockSpec((8, tm), lambda i: (0, i)),
                   pl.BlockSpec((tm, LANES), tokrow),
                   pl.BlockSpec((N_EXPERTS, 1), const)],
        out_shape=[jax.ShapeDtypeStruct((T, D_MODEL), f32),
                   jax.ShapeDtypeStruct((T, D_MODEL // 2), jnp.int32),
                   jax.ShapeDtypeStruct((8, T), jnp.int32),
                   jax.ShapeDtypeStruct((8, T), jnp.int32),
                   jax.ShapeDtypeStruct((T, LANES), f32),
                   jax.ShapeDtypeStruct((N_EXPERTS, 1), f32)],
        scratch_shapes=[pltpu.VMEM((N_EXPERTS, 1), f32)],
        compiler_params=_cparams(("arbitrary",)),
        name="outproj_router",
    )(yl, yg[0], yg[1], xt, mod3, wo, lg, lb, wr_hi, wr_lo, br)


def _route_kernel(ids_ref, rank_ref, cnt_ref, dest_ref, tiles_ref, *, bm):
    ne = cnt_ref.shape[0]
    cnt = cnt_ref[...]
    padded = jnp.floor((cnt + (bm - 1.0)) * (1.0 / bm)) * bm
    sub = lax.broadcasted_iota(jnp.int32, (ne, LANES), 0)
    lane = lax.broadcasted_iota(jnp.int32, (ne, LANES), 1)
    padded_row = jnp.sum(jnp.where(sub == lane, padded, 0.0), axis=0, keepdims=True)
    base = jnp.sum(jnp.where(lane < sub, padded_row, 0.0), axis=1, keepdims=True)
    ends = base + padded

    tc = ids_ref.shape[1]
    expert = lax.broadcasted_iota(jnp.int32, (ne, tc), 0)
    slot = lax.broadcasted_iota(jnp.int32, (8, tc), 0)
    ids = ids_ref[...]
    group_base = jnp.zeros((8, tc), f32)
    for kk in range(TOP_K):
        b_k = jnp.sum(jnp.where(expert == ids[kk:kk + 1], base, 0.0), axis=0, keepdims=True)
        group_base = jnp.where(slot == kk, b_k, group_base)
    dest_ref[...] = group_base.astype(jnp.int32) + rank_ref[...]

    nl = tiles_ref.shape[1]
    start = lax.broadcasted_iota(jnp.int32, (ne, nl), 1).astype(f32) * bm
    te = jnp.minimum(jnp.sum(jnp.where(start >= ends, 1.0, 0.0), axis=0, keepdims=True), ne - 1.0)
    at_te = lax.broadcasted_iota(jnp.int32, (ne, nl), 0).astype(f32) == te
    cnt_te = jnp.sum(jnp.where(at_te, cnt, 0.0), axis=0, keepdims=True)
    base_te = jnp.sum(jnp.where(at_te, base, 0.0), axis=0, keepdims=True)
    valid = jnp.clip(cnt_te - (start[0:1] - base_te), 0.0, float(bm))
    next_group = jnp.sum(jnp.where(at_te, ends, 0.0), axis=0, keepdims=True) * (1.0 / bm)
    last_used = jnp.sum(padded, axis=0, keepdims=True) * (1.0 / bm) - 1.0
    srow = lax.broadcasted_iota(jnp.int32, (8, nl), 0)
    table = jnp.where(srow == 0, te, jnp.where(srow == 1, valid, jnp.where(srow == 2, next_group,
                                                                           jnp.where(srow == 3, last_used, 0.0))))
    tiles_ref[...] = table.astype(jnp.int32)


def _route(ids, rank, cnt, bm, n_tiles):
    T = ids.shape[1]
    tc = 2048
    nl = -(-n_tiles // LANES) * LANES
    tok = lambda i: (0, i)
    const = lambda i: (0, 0)
    return pl.pallas_call(
        functools.partial(_route_kernel, bm=bm),
        grid=(T // tc,),
        in_specs=[pl.BlockSpec((8, tc), tok), pl.BlockSpec((8, tc), tok),
                  pl.BlockSpec((N_EXPERTS, 1), const)],
        out_specs=[pl.BlockSpec((8, tc), tok), pl.BlockSpec((8, nl), const)],
        out_shape=[jax.ShapeDtypeStruct((8, T), jnp.int32), jax.ShapeDtypeStruct((8, nl), jnp.int32)],
        compiler_params=_cparams(("arbitrary",)),
        name="route",
    )(ids, rank, cnt)


def _sc_workers():
    info = plsc.get_sparse_core_info()
    return info.num_cores, info.num_subcores


def _sc_dispatch(rows, dest_flat, n_out):
    T, D = rows.shape
    nc, ns = _sc_workers()
    per_w = T // (nc * ns)
    n_chunks = per_w // SC_CHUNK
    assert n_chunks % 2 == 0
    mesh = plsc.VectorSubcoreMesh(core_axis_name="c", subcore_axis_name="s")

    @functools.partial(
        pl.kernel, mesh=mesh,
        out_type=jax.ShapeDtypeStruct((n_out, D), rows.dtype),
        scratch_types=([pltpu.VMEM((SC_CHUNK,), jnp.int32)] * (2 * TOP_K)
                       + [pltpu.VMEM((SC_CHUNK, D), rows.dtype)] * 2
                       + [pltpu.SemaphoreType.DMA] * (2 * TOP_K)),
    )
    def k(rows_hbm, dest_hbm, out_hbm, *scratch):
        idx_v = (scratch[:TOP_K], scratch[TOP_K:2 * TOP_K])
        rows_v = scratch[2 * TOP_K:2 * TOP_K + 2]
        sems = (scratch[2 * TOP_K + 2:3 * TOP_K + 2], scratch[3 * TOP_K + 2:])
        wid = lax.axis_index("s") * nc + lax.axis_index("c")
        base = wid * per_w

        def load(chunk, b):
            off = base + chunk * SC_CHUNK
            pltpu.sync_copy(rows_hbm.at[pl.ds(off, SC_CHUNK)], rows_v[b])
            for kk in range(TOP_K):
                pltpu.sync_copy(dest_hbm.at[pl.ds(kk * T + off, SC_CHUNK)], idx_v[b][kk])

        load(0, 0)

        @pl.loop(0, n_chunks, step=2)
        def _(j):
            for b in range(2):
                copies = [pltpu.async_copy(rows_v[b], out_hbm.at[idx_v[b][kk]], sems[b][kk])
                          for kk in range(TOP_K)]

                @pl.when(j + b + 1 < n_chunks)
                def _():
                    load(j + b + 1, 1 - b)

                for cp in copies:
                    cp.wait()

    return k(rows, dest_flat)


def _sc_gather(table, idx):
    _, D = table.shape
    N = idx.shape[0]
    nc, ns = _sc_workers()
    per_w = N // (nc * ns)
    n_chunks = per_w // SC_CHUNK
    assert n_chunks % 2 == 0
    mesh = plsc.VectorSubcoreMesh(core_axis_name="c", subcore_axis_name="s")

    @functools.partial(
        pl.kernel, mesh=mesh,
        out_type=jax.ShapeDtypeStruct((N, D), table.dtype),
        scratch_types=([pltpu.VMEM((SC_CHUNK,), jnp.int32)] * 2
                       + [pltpu.VMEM((SC_CHUNK, D), table.dtype)] * 2
                       + [pltpu.SemaphoreType.DMA] * 2),
    )
    def k(table_hbm, idx_hbm, out_hbm, idx0, idx1, buf0, buf1, sem0, sem1):
        idxs, bufs, sems = (idx0, idx1), (buf0, buf1), (sem0, sem1)
        wid = lax.axis_index("s") * nc + lax.axis_index("c")
        base = wid * per_w

        def gather(b):
            return pltpu.make_async_copy(table_hbm.at[idxs[b]], bufs[b], sems[b])

        def start(chunk, b):
            pltpu.sync_copy(idx_hbm.at[pl.ds(base + chunk * SC_CHUNK, SC_CHUNK)], idxs[b])
            gather(b).start()

        def finish(chunk, b):
            gather(b).wait()
            pltpu.sync_copy(bufs[b], out_hbm.at[pl.ds(base + chunk * SC_CHUNK, SC_CHUNK)])

        start(0, 0)

        @pl.loop(0, n_chunks, step=2)
        def _(j):
            start(j + 1, 1)
            finish(j, 0)

            @pl.when(j + 2 < n_chunks)
            def _():
                start(j + 2, 0)

            finish(j + 1, 1)

    return k(table, idx)


def _ffn_kernel(te_ref, tv_ref, tn_ref, tl_ref, xs_ref, wg_hbm, bg_ref, wu_hbm, bu_ref, wd_hbm, bd_ref, eo_ref,
                w_buf, sems, slot_ref):
    i = pl.program_id(0)
    n_tiles = pl.num_programs(0)
    w_hbm = (wg_hbm, wu_hbm, wd_hbm)

    def fetch(e, slot):
        return [pltpu.make_async_copy(w_hbm[m].at[e], w_buf.at[slot, m], sems.at[slot, m]) for m in range(3)]

    @pl.when(i == 0)
    def _():
        slot_ref[0] = 0
        for cp in fetch(te_ref[0], 0):
            cp.start()

    prev = te_ref[jnp.maximum(i - 1, 0)]
    first = (i == 0) | (te_ref[i] != prev)

    @pl.when(first & (i > 0))
    def _():
        slot_ref[0] = 1 - slot_ref[0]

    slot = slot_ref[0]

    @pl.when(first)
    def _():
        for cp in fetch(te_ref[i], slot):
            cp.wait()
        nxt = tn_ref[i]
        e_nxt = te_ref[jnp.minimum(nxt, n_tiles - 1)]

        @pl.when((nxt > i) & (nxt < n_tiles) & (e_nxt != te_ref[i]))
        def _():
            for cp in fetch(e_nxt, 1 - slot):
                cp.start()

    wg_b, wu_b, wd_b = (w_buf.at[slot, m] for m in range(3))
    valid = tv_ref[i]

    def ffn_rows(r0, m):
        blocks = [(r0 + o, min(MOE_BLOCK, m)) for o in range(0, m, MOE_BLOCK)]
        xs, gates, ups, acts = [], [], [], []
        for b0, bm_ in blocks:
            row = lax.broadcasted_iota(jnp.int32, (bm_, 1), 0) + b0
            xw = jnp.where(row < valid, xs_ref[b0:b0 + bm_, :], 0)
            x_hi, x_lo = _unpack_rows(xw)
            xs.append(jnp.concatenate([x_hi, x_lo], axis=1).astype(bf16))
        for x in xs:
            gates.append(jnp.minimum(jnp.dot(x, wg_b[...], preferred_element_type=f32) + bg_ref[...],
                                     SWIGLU_LIMIT))
            ups.append(jnp.clip(jnp.dot(x, wu_b[...], preferred_element_type=f32) + bu_ref[...],
                                -SWIGLU_LIMIT, SWIGLU_LIMIT))
        for gate, up in zip(gates, ups):
            acts.append(((up + 1.0) * gate * jax.nn.sigmoid(SWIGLU_ALPHA * gate)).astype(bf16))
        for (b0, bm_), act in zip(blocks, acts):
            out = jnp.dot(act, wd_b[...], preferred_element_type=f32) + bd_ref[...]
            eo_ref[b0:b0 + bm_, :] = _pack_rows(out)

    def zero_rows(r0, m):
        eo_ref[r0:r0 + m, :] = jnp.zeros((m, eo_ref.shape[1]), eo_ref.dtype)

    for r0 in range(0, xs_ref.shape[0], MOE_PASS):
        lo = 0
        for m in MOE_PASS_SIZES:
            @pl.when((valid > r0 + lo) & ((valid <= r0 + m) | (m == MOE_PASS)))
            def _(r0=r0, m=m):
                ffn_rows(r0, m)
                if m < MOE_PASS:
                    zero_rows(r0 + m, MOE_PASS - m)
            lo = m

        @pl.when(valid <= r0)
        def _(r0=r0):
            zero_rows(r0, MOE_PASS)


def _expert_ffn(tile_expert, tile_valid, tile_next, last_used, xs, w_gate, b_gate, w_up, b_up, w_down, b_down):
    n_rows, dp = xs.shape
    d = 2 * dp
    bm = MOE_TILE
    d_e = w_gate.shape[-1]
    assert d == d_e
    bspec = lambda n_: pl.BlockSpec((None, 1, n_), lambda i, te, tv, tn, tl: (te[i], 0, 0))
    hbm = pl.BlockSpec(memory_space=pl.ANY)
    grid_spec = pltpu.PrefetchScalarGridSpec(
        num_scalar_prefetch=4,
        grid=(n_rows // bm,),
        in_specs=[pl.BlockSpec((bm, dp), lambda i, te, tv, tn, tl: (jnp.minimum(i, tl[0]), 0)),
                  hbm, bspec(d_e), hbm, bspec(d_e), hbm, bspec(d)],
        out_specs=pl.BlockSpec((bm, dp), lambda i, te, tv, tn, tl: (i, 0)),
        scratch_shapes=[pltpu.VMEM((2, 3, d, d_e), f32),
                        pltpu.SemaphoreType.DMA((2, 3)),
                        pltpu.SMEM((1,), jnp.int32)],
    )
    return pl.pallas_call(
        _ffn_kernel,
        grid_spec=grid_spec,
        out_shape=jax.ShapeDtypeStruct((n_rows, dp), jnp.int32),
        compiler_params=_cparams(("arbitrary",)),
        name="expert_ffn",
    )(tile_expert, tile_valid, tile_next, last_used, xs, w_gate, b_gate.reshape(N_EXPERTS, 1, d_e), w_up,
      b_up.reshape(N_EXPERTS, 1, d_e), w_down, b_down.reshape(N_EXPERTS, 1, d))


def _combine_kernel(eg_ref, wts_ref, x1_ref, mod_ref, lg_ref, lb_ref, o_ref):
    w = wts_ref[...]
    y_hi, y_lo = _unpack_rows(eg_ref[0])
    y_hi, y_lo = y_hi * w[:, 0:1], y_lo * w[:, 0:1]
    for kk in range(1, TOP_K):
        e_hi, e_lo = _unpack_rows(eg_ref[kk])
        y_hi = y_hi + e_hi * w[:, kk:kk + 1]
        y_lo = y_lo + e_lo * w[:, kk:kk + 1]
    y = jnp.concatenate([y_hi, y_lo], axis=1)
    z = DEEPNORM_ALPHA * x1_ref[...] + mod_ref[0] * y
    mu = jnp.mean(z, axis=-1, keepdims=True)
    zc = z - mu
    var = jnp.mean(zc * zc, axis=-1, keepdims=True)
    o_ref[...] = zc * lax.rsqrt(var + LN_EPS) * lg_ref[...] + lb_ref[...]


def _combine_ln(eg, wts, x1, mod3, lg, lb, tokens_per_batch):
    T = x1.shape[0]
    tm = TOKEN_TILE
    tiles_per_batch = tokens_per_batch // tm
    const = lambda i: (0, 0)
    return pl.pallas_call(
        _combine_kernel,
        grid=(T // tm,),
        in_specs=[pl.BlockSpec((TOP_K, tm, D_MODEL // 2), lambda i: (0, i, 0)),
                  pl.BlockSpec((tm, LANES), lambda i: (i, 0)),
                  pl.BlockSpec((tm, D_MODEL), lambda i: (i, 0)),
                  pl.BlockSpec((1, 1, D_MODEL), lambda i: (i // tiles_per_batch, 0, N_MOD - 1)),
                  pl.BlockSpec((1, D_MODEL), const),
                  pl.BlockSpec((1, D_MODEL), const)],
        out_specs=pl.BlockSpec((tm, D_MODEL), lambda i: (i, 0)),
        out_shape=jax.ShapeDtypeStruct((T, D_MODEL), f32),
        compiler_params=_cparams(("arbitrary",)),
        name="combine_ln",
    )(eg, wts, x1, mod3, lg, lb)


def _block_diag(w):
    n, c, d = w.shape
    eye = jnp.eye(n, dtype=w.dtype)
    return jnp.einsum('ncd,nm->ncmd', w, eye).reshape(n * c, n * d)


def kernel(x, c, ctx, c_ctx, w_ada, b_ada, w_in, conv_w, conv_b, lru_wa, lru_ba, lru_wx, lru_bx,
           lru_lam, gla_wa, gla_ba, gla_norm_g, w_out, ln1_g, ln1_b, w_router, b_router, w_gate,
           b_gate, w_up, b_up, w_down, b_down, ln2_g, ln2_b):
    B, L, D = x.shape
    Lc = ctx.shape[1]
    T = B * L
    rows = L // GRID_W
    l = 0

    cpad = jnp.zeros((8, D), f32).at[0:B].set(c).at[B].set(c_ctx)
    w_cat = jnp.pad(w_in[l], ((0, 0), (0, D_PROJ - w_in.shape[-1])))
    wg = [jnp.concatenate([_block_diag(lru_wa[l, d]), _block_diag(lru_wx[l, d])], axis=1) for d in range(2)]
    bg = [jnp.concatenate([lru_ba[l, d], lru_bx[l, d]])[None] for d in range(2)]
    lam = [lru_lam[l, d][None] for d in range(2)]
    wa = [jnp.pad(gla_wa[l, d], ((0, A_LOW_PAD - GATE_RANK), (0, 0))) for d in range(2)]
    ba = [gla_ba[l, d][None] for d in range(2)]
    cw, cb = conv_w[l], conv_b[l][None]
    wr_t = w_router[l].T
    wr_hi = wr_t.astype(bf16)
    wr_lo = (wr_t - wr_hi.astype(f32)).astype(bf16)
    br = b_router[l][:, None]
    c_xl, c_gl, c_pg = (0, D_LRU), (D_LRU, 2 * D_LRU), (2 * D_LRU, D_PROJ)

    mod3 = _ada_mod(cpad, w_ada[l], b_ada[l]).reshape(8, 1, N_MOD * D)

    tiles_per_batch = L // INPROJ_TILE
    xl_c, pg_c = _inproj_rows(ctx.reshape(B * Lc, D), mod3, w_cat, lambda i: B, (c_xl, c_pg), (f32, bf16))
    zero_h = jnp.zeros((B, 1, D_LRU), f32)
    zero_s = jnp.zeros((B, GLA_DV, D_GLA_K), f32)
    pg_c = pg_c.reshape(B, Lc // GLA_CHUNK, GLA_CHUNK, D_PG)
    h_ctx = []
    for d in range(2):
        _, _, hf = _lru_pass(xl_c, wg[d], bg[d], lam[d], zero_h, n_batch=B, tile=Lc, reverse=bool(d),
                             conv=(cw, cb, Lc))
        h_ctx.append(hf)
    s_ctx = _gla_pass(pg_c, wa, ba, (zero_s, zero_s))

    xt = x.reshape(T, D)
    xl, gl = _inproj_rows(xt, mod3, w_cat, lambda i: i // tiles_per_batch, (c_xl, c_gl), (f32, bf16))
    hb, u_lru, _ = _lru_pass(xl, wg[1], bg[1], lam[1], h_ctx[1], n_batch=B, tile=LRU_TILE, reverse=True,
                             conv=(cw, cb, GRID_W))
    y_lru, _ = _lru_pass(u_lru, wg[0], bg[0], lam[0], h_ctx[0], n_batch=B, tile=LRU_TILE, reverse=False,
                         merge_with=(hb, gl))
    pg = _inproj_cols(x.reshape(B, rows, GRID_W, D), mod3, w_cat, c_pg)
    y_gla = _gla_pass(pg, wa, ba, s_ctx, gn=gla_norm_g[l][None])

    x1, hp, ids, rank, wts, cnt = _outproj_router(
        y_lru, y_gla, xt, mod3, w_out[l], ln1_g[l][None], ln1_b[l][None], wr_hi, wr_lo, br, L)

    n_tiles = T * TOP_K // MOE_TILE + N_EXPERTS
    dest, tiles = _route(ids, rank, cnt, MOE_TILE, n_tiles)
    xs = _sc_dispatch(hp, dest.reshape(-1), n_tiles * MOE_TILE)
    eo = _expert_ffn(tiles[0, :n_tiles], tiles[1, :n_tiles], tiles[2, :n_tiles], tiles[3, :1], xs, w_gate[l],
                     b_gate[l], w_up[l], b_up[l], w_down[l], b_down[l])
    eg = _sc_gather(eo, dest[:TOP_K].reshape(-1)).reshape(TOP_K, T, D // 2)
    out = _combine_ln(eg, wts, x1, mod3, ln2_g[l][None], ln2_b[l][None], L)
    return out.reshape(B, L, D)
```

```python
import functools

import jax
import jax.numpy as jnp
from jax import lax
from jax.experimental import pallas as pl
from jax.experimental.pallas import tpu as pltpu
from jax.experimental.pallas import tpu_sc as plsc

D_MODEL = 1024
DEPTH = 1
GRID_W = 64
D_LRU = 512
LRU_BLOCKS = 8
CONV_W = 4
LRU_C = 8.0
GLA_HEADS = 4
D_GLA_V = 512
D_GLA_K = 256
GLA_DK = 64
GLA_DV = 128
GATE_RANK = 16
GATE_TAU = 16.0
GLA_CHUNK = 64
N_EXPERTS = 32
TOP_K = 4
SWIGLU_LIMIT = 7.0
SWIGLU_ALPHA = 1.702
N_MOD = 6
DEEPNORM_ALPHA = (2.0 * DEPTH) ** 0.25
LN_EPS = 1e-5
RMS_EPS = 1e-6

LANES = 128
A_LOW_PAD = LANES
D_PG = 2 * D_GLA_K + 2 * D_GLA_V + A_LOW_PAD
D_PROJ = 2 * D_LRU + D_PG
TOKEN_TILE = 512
INPROJ_TILE = 1024
LRU_TILE = 512
MOE_TILE = 1024
MOE_PASS = 512
MOE_PASS_SIZES = (128, 256, MOE_PASS)
MOE_BLOCK = 256
OUTPROJ_PARTS = 4
SC_CHUNK = 64
VMEM_LIMIT = 48 * 1024 * 1024

f32 = jnp.float32
bf16 = jnp.bfloat16


def _cparams(sem):
    return pltpu.CompilerParams(dimension_semantics=sem, vmem_limit_bytes=VMEM_LIMIT)


def _ada_kernel(c_ref, w_ref, b_ref, o_ref):
    s = c_ref[...]
    s = s * jax.nn.sigmoid(s)
    o_ref[...] = jnp.dot(s.astype(bf16), w_ref[...], preferred_element_type=f32) + b_ref[...]


def _ada_mod(cpad, w, b):
    n = w.shape[1]
    tn = 1024
    return pl.pallas_call(
        _ada_kernel,
        grid=(n // tn,),
        in_specs=[pl.BlockSpec((8, D_MODEL), lambda j: (0, 0)),
                  pl.BlockSpec((D_MODEL, tn), lambda j: (0, j)),
                  pl.BlockSpec((1, tn), lambda j: (0, j))],
        out_specs=pl.BlockSpec((8, tn), lambda j: (0, j)),
        out_shape=jax.ShapeDtypeStruct((8, n), f32),
        compiler_params=_cparams(("arbitrary",)),
        name="ada_mod",
    )(cpad, w, b.reshape(1, n))


def _inproj_kernel(x_ref, mod_ref, w_ref, *out_refs, parts, col_major):
    sh = mod_ref[0, :, 0:D_MODEL]
    sc = mod_ref[0, :, D_MODEL:2 * D_MODEL]
    if col_major:
        x = jnp.concatenate([x_ref[0, :, j, :] for j in range(x_ref.shape[2])], axis=0)
    else:
        x = x_ref[...]
    u = (x * (1.0 + sc) + sh).astype(bf16)
    for (lo, hi), o_ref in zip(parts, out_refs):
        p = jnp.dot(u, w_ref[:, lo:hi], preferred_element_type=f32).astype(o_ref.dtype)
        o_ref[...] = p.reshape(o_ref.shape)


def _inproj_rows(xt, mod3, w_cat, batch_of_tile, parts, dtypes):
    T = xt.shape[0]
    tm = INPROJ_TILE
    return pl.pallas_call(
        functools.partial(_inproj_kernel, parts=parts, col_major=False),
        grid=(T // tm,),
        in_specs=[pl.BlockSpec((tm, D_MODEL), lambda i: (i, 0)),
                  pl.BlockSpec((1, 1, 2 * D_MODEL), lambda i: (batch_of_tile(i), 0, 0)),
                  pl.BlockSpec((D_MODEL, D_PROJ), lambda i: (0, 0))],
        out_specs=[pl.BlockSpec((tm, hi - lo), lambda i: (i, 0)) for lo, hi in parts],
        out_shape=[jax.ShapeDtypeStruct((T, hi - lo), dt) for (lo, hi), dt in zip(parts, dtypes)],
        compiler_params=_cparams(("arbitrary",)),
        name="inproj_rows",
    )(xt, mod3, w_cat)


def _inproj_cols(x4, mod3, w_cat, part):
    n_batch, rows, cols, _ = x4.shape
    lo, hi = part
    cb = TOKEN_TILE // rows
    return pl.pallas_call(
        functools.partial(_inproj_kernel, parts=(part,), col_major=True),
        grid=(n_batch, cols // cb),
        in_specs=[pl.BlockSpec((1, rows, cb, D_MODEL), lambda b, n: (b, 0, n, 0)),
                  pl.BlockSpec((1, 1, 2 * D_MODEL), lambda b, n: (b, 0, 0)),
                  pl.BlockSpec((D_MODEL, D_PROJ), lambda b, n: (0, 0))],
        out_specs=[pl.BlockSpec((1, cb, rows, hi - lo), lambda b, n: (b, n, 0, 0))],
        out_shape=[jax.ShapeDtypeStruct((n_batch, cols, rows, hi - lo), bf16)],
        compiler_params=_cparams(("arbitrary", "arbitrary")),
        name="inproj_cols",
    )(x4, mod3, w_cat)[0]


def _gelu_tanh(x):
    return 0.5 * x * (1.0 + jnp.tanh(0.7978845608028654 * (x + 0.044715 * (x * x * x))))


def _lru_kernel(*refs, row_w, reverse, merge):
    if merge:
        u_ref, wg_ref, bg_ref, lam_ref, h0_ref, hb_ref, gl_ref, out_ref, hfin_ref, carry, h_nat = refs
    else:
        xl_ref, cw_ref, cb_ref, wg_ref, bg_ref, lam_ref, h0_ref, out_ref, u_out_ref, hfin_ref, carry = refs
    t = pl.program_id(1)

    @pl.when(t == 0)
    def _():
        carry[...] = h0_ref[0]

    n_seg = 8
    blk = lambda v, j: v[j * n_seg:(j + 1) * n_seg]
    if merge:
        u = u_ref[...]
        tt, ch = u.shape
        seg = tt // n_seg
    else:
        _, seg, ch = xl_ref.shape
        tt = n_seg * seg
        x = jnp.concatenate([xl_ref[:, j, :] for j in range(seg)], axis=0)
        segs_per_row = row_w // seg
        s_idx = lax.broadcasted_iota(jnp.int32, (n_seg, ch), 0)
        has_prev = (s_idx % segs_per_row) != 0
        has_next = (s_idx % segs_per_row) != segs_per_row - 1
        from_prev = lambda v: jnp.where(has_prev, pltpu.roll(v, 1, 0), 0.0)
        from_next = lambda v: jnp.where(has_next, pltpu.roll(v, n_seg - 1, 0), 0.0)
        x_m1 = jnp.concatenate([from_prev(blk(x, seg - 1)), x[:tt - n_seg]], axis=0)
        x_m2 = jnp.concatenate([from_prev(blk(x, seg - 2)), from_prev(blk(x, seg - 1)), x[:tt - 2 * n_seg]],
                               axis=0)
        x_p1 = jnp.concatenate([x[n_seg:], from_next(blk(x, 0))], axis=0)
        cw = cw_ref[...]
        u = cb_ref[...] + x_m2 * cw[0:1] + x_m1 * cw[1:2] + x * cw[2:3] + x_p1 * cw[3:4]
        u_out_ref[...] = u
    rows_q = tt // 4
    g_parts = [jnp.dot(u[q * rows_q:(q + 1) * rows_q].astype(bf16), wg_ref[...], preferred_element_type=f32)
               + bg_ref[...] for q in range(4)]
    lam = lam_ref[...]
    softplus_neg = jnp.maximum(-lam, 0.0) + jnp.log(1.0 + jnp.exp(-jnp.abs(lam)))
    log_a_scale = -LRU_C * softplus_neg

    def recurrence_terms(j):
        q, r = divmod(j * n_seg, rows_q)
        g_j, u_j = g_parts[q][r:r + n_seg], blk(u, j)
        log_a = log_a_scale * jax.nn.sigmoid(g_j[:, 0:ch])
        a_j = jnp.exp(log_a)
        series = log_a * (-2.0 + log_a * (-2.0 + log_a * (-4.0 / 3.0)))
        one_minus_a2 = jnp.where(log_a > -0.005, series, 1.0 - a_j * a_j)
        root = jnp.where(one_minus_a2 > 0.0, one_minus_a2 * lax.rsqrt(one_minus_a2), 0.0)
        return a_j, root * (jax.nn.sigmoid(g_j[:, ch:2 * ch]) * u_j)

    half = seg // 2
    h_loc, a_cum = [None] * seg, [None] * seg
    h_run, a_run = [None, None], [None, None]
    for i in range(half):
        for p in range(2):
            j = p * half + (half - 1 - i if reverse else i)
            a_j, b_j = recurrence_terms(j)
            h_run[p] = b_j if i == 0 else a_j * h_run[p] + b_j
            a_run[p] = a_j if i == 0 else a_j * a_run[p]
            h_loc[j], a_cum[j] = h_run[p], a_run[p]
    pieces = [(s, p) for s in range(n_seg) for p in range(2)]
    if reverse:
        pieces.reverse()
    entering = [[None] * n_seg, [None] * n_seg]
    state = carry[...]
    for s, p in pieces:
        entering[p][s] = state
        state = h_run[p][s:s + 1] + a_run[p][s:s + 1] * state
    carry[...] = state
    hfin_ref[0] = state
    enter = [jnp.concatenate(e, axis=0) for e in entering]
    h_rows = [h_loc[j] + a_cum[j] * enter[j // half] for j in range(seg)]
    if merge:
        for j in range(seg):
            h_nat[:, j, :] = h_rows[j] + blk(hb_ref, j)
        h = h_nat[...].reshape(tt, ch)
        out_ref[...] = (h * _gelu_tanh(gl_ref[...].astype(f32))).astype(out_ref.dtype)
    else:
        out_ref[...] = jnp.concatenate(h_rows, axis=0)


def _lru_pass(x_in, wg, bg, lam, h0, *, n_batch, tile, reverse, conv=None, merge_with=None):
    T = x_in.shape[0]
    nt = T // n_batch // tile
    merge = merge_with is not None
    assert merge != (conv is not None)
    seg = tile // 8

    def tok(b, t):
        return (b * nt + (nt - 1 - t if reverse else t), 0)

    tile_spec = pl.BlockSpec((tile, D_LRU), tok)
    const = lambda b, t: (0, 0)
    gate_specs = [pl.BlockSpec((D_LRU, 2 * D_LRU), const),
                  pl.BlockSpec((1, 2 * D_LRU), const),
                  pl.BlockSpec((1, D_LRU), const),
                  pl.BlockSpec((1, 1, D_LRU), lambda b, t: (b, 0, 0))]
    state_spec = pl.BlockSpec((1, 1, D_LRU), lambda b, t: (b, 0, 0))
    state_shape = jax.ShapeDtypeStruct((n_batch, 1, D_LRU), f32)
    scratch = [pltpu.VMEM((1, D_LRU), f32)]
    if merge:
        hb, gl = merge_with
        row_w = None
        in_specs = [tile_spec] + gate_specs + [tile_spec, tile_spec]
        args = [x_in, wg, bg, lam, h0, hb, gl]
        scratch += [pltpu.VMEM((8, seg, D_LRU), f32)]
        out_specs = [tile_spec, state_spec]
        out_shape = [jax.ShapeDtypeStruct((T, D_LRU), bf16), state_shape]
    else:
        conv_w, conv_b, row_w = conv
        assert row_w % seg == 0 and seg >= 2
        in_specs = [pl.BlockSpec((8, seg, D_LRU), lambda b, t: tok(b, t) + (0,)),
                    pl.BlockSpec((CONV_W, D_LRU), const),
                    pl.BlockSpec((1, D_LRU), const)] + gate_specs
        args = [x_in.reshape(T // seg, seg, D_LRU), conv_w, conv_b, wg, bg, lam, h0]
        out_specs = [tile_spec, tile_spec, state_spec]
        out_shape = [jax.ShapeDtypeStruct((T, D_LRU), f32), jax.ShapeDtypeStruct((T, D_LRU), f32), state_shape]
    return pl.pallas_call(
        functools.partial(_lru_kernel, row_w=row_w, reverse=reverse, merge=merge),
        grid=(n_batch, nt),
        in_specs=in_specs,
        out_specs=out_specs,
        out_shape=out_shape,
        scratch_shapes=scratch,
        compiler_params=_cparams(("arbitrary", "arbitrary")),
        name="lru_merge" if merge else "lru_scan",
    )(*args)


_PG_Q, _PG_K, _PG_V = 0, D_GLA_K, 2 * D_GLA_K
_PG_G, _PG_A = 2 * D_GLA_K + D_GLA_V, 2 * D_GLA_K + 2 * D_GLA_V


def _gla_chunks(streams):
    ck = GLA_CHUNK
    nh = GLA_HEADS
    head_of_lane = lax.broadcasted_iota(jnp.int32, (1, D_GLA_K), 1) // GLA_DK
    nt_dims = (((1,), (1,)), ((), ()))
    tn_dims = (((0,), (0,)), ((), ()))
    ii = lax.broadcasted_iota(jnp.int32, (nh * ck, ck), 0) & (ck - 1)
    jj = lax.broadcasted_iota(jnp.int32, (nh * ck, ck), 1)

    units = []
    for pg_ref, wa_ref, ba_ref, state, reverse in streams:
        n_batch = pg_ref.shape[0]
        rows_all = n_batch * ck
        a_low = jnp.concatenate([pg_ref[bi, :, _PG_A:_PG_A + A_LOW_PAD] for bi in range(n_batch)], axis=0)
        z = jnp.dot(a_low, wa_ref[...], preferred_element_type=f32) + ba_ref[...]
        bcum_all = (jnp.minimum(z, 0.0) - jnp.log(1.0 + jnp.exp(-jnp.abs(z)))) * (1.0 / GATE_TAU)
        step = lax.broadcasted_iota(jnp.int32, (rows_all, D_GLA_K), 0) & (ck - 1)
        d = 1
        while d < ck:
            if reverse:
                bcum_all = bcum_all + jnp.where(step < ck - d, pltpu.roll(bcum_all, rows_all - d, 0), 0.0)
            else:
                bcum_all = bcum_all + jnp.where(step >= d, pltpu.roll(bcum_all, d, 0), 0.0)
            d *= 2
        seen = (jj >= ii) if reverse else (jj <= ii)
        for bi in range(n_batch):
            units.append(dict(pg=pg_ref, bi=bi, state=state, reverse=reverse, seen=seen,
                              bcum=bcum_all[bi * ck:(bi + 1) * ck]))

    for u in units:
        pg_ref, bi, bcum = u["pg"], u["bi"], u["bcum"]
        q = pg_ref[bi, :, _PG_Q:_PG_K].astype(f32) * (GLA_DK ** -0.5)
        k = pg_ref[bi, :, _PG_K:_PG_V].astype(f32)
        u["btot"] = bcum[0:1] if u["reverse"] else bcum[ck - 1:ck]
        q_dec = q * jnp.exp(bcum)
        k_dec = (k * jnp.exp(-bcum)).astype(bf16)
        k_end = k * jnp.exp(u["btot"] - bcum)
        u["s_t"] = u["state"][bi]
        by_head = lambda t: jnp.concatenate([jnp.where(head_of_lane == hd, t, 0.0) for hd in range(nh)],
                                            axis=0).astype(bf16)
        u["q_blk"] = by_head(q_dec)
        u["k_blk"] = by_head(k_end)
        u["rhs"] = jnp.concatenate([u["s_t"].astype(bf16), k_dec], axis=0)
    for u in units:
        u["qs"] = lax.dot_general(u["q_blk"], u["rhs"], nt_dims, preferred_element_type=f32)
    for u in units:
        u["scores"] = jnp.where(u["seen"], u["qs"][:, GLA_DV:GLA_DV + ck], 0.0).astype(bf16)
    for u in units:
        v = u["pg"][u["bi"], :, _PG_V:_PG_G]
        u["intra"] = [jnp.dot(u["scores"][hd * ck:(hd + 1) * ck], v[:, hd * GLA_DV:(hd + 1) * GLA_DV],
                              preferred_element_type=f32) for hd in range(nh)]
        v_stack = jnp.concatenate([v[:, hd * GLA_DV:(hd + 1) * GLA_DV] for hd in range(nh)], axis=0)
        u["kv_t"] = lax.dot_general(v_stack, u["k_blk"], tn_dims, preferred_element_type=f32)
    for u in units:
        u["outs"] = [u["intra"][hd] + u["qs"][hd * ck:(hd + 1) * ck, 0:GLA_DV] for hd in range(nh)]
        u["state"][u["bi"]] = u["s_t"] * jnp.exp(u["btot"]) + u["kv_t"]
    n_batch = streams[0][0].shape[0]
    return [[u["outs"] for u in units[si * n_batch:(si + 1) * n_batch]] for si in range(len(streams))]


def _gla_gate(o_heads, g, gn):
    normed = []
    for oh in o_heads:
        ms = jnp.mean(oh * oh, axis=-1, keepdims=True)
        normed.append(oh * lax.rsqrt(ms + RMS_EPS))
    return jnp.concatenate(normed, axis=-1) * gn * (g * jax.nn.sigmoid(g))


def _gla_kernel(*refs, merge):
    if merge:
        (pgf_ref, pgb_ref, waf_ref, baf_ref, wab_ref, bab_ref, s0f_ref, s0b_ref, gn_ref,
         ylo_ref, yhi_ref, st_f, st_b, keep_f, keep_b) = refs
    else:
        (pgf_ref, pgb_ref, waf_ref, baf_ref, wab_ref, bab_ref, s0f_ref, s0b_ref,
         sfin_f_ref, sfin_b_ref, st_f, st_b) = refs
    n = pl.program_id(0)
    n_chunks = pl.num_programs(0)

    @pl.when(n == 0)
    def _():
        st_f[...] = s0f_ref[...]
        st_b[...] = s0b_ref[...]

    outs_f, outs_b = _gla_chunks([(pgf_ref, waf_ref, baf_ref, st_f, False), (pgb_ref, wab_ref, bab_ref, st_b, True)])
    n_batch = pgf_ref.shape[0]
    heads = range(GLA_HEADS)
    if merge:
        half = n_chunks // 2
        m = n_chunks - 1 - n

        @pl.when(n < half)
        def _():
            for bi in range(n_batch):
                keep_f[n, bi] = jnp.concatenate(outs_f[bi], axis=-1).astype(keep_f.dtype)
                keep_b[m - half, bi] = jnp.concatenate(outs_b[bi], axis=-1).astype(keep_b.dtype)

        @pl.when(n >= half)
        def _():
            for bi in range(n_batch):
                kb = keep_b[n - half, bi].astype(f32)
                o_heads = [outs_f[bi][hd] + kb[:, hd * GLA_DV:(hd + 1) * GLA_DV] for hd in heads]
                yhi_ref[bi] = _gla_gate(o_heads, pgf_ref[bi, :, _PG_G:_PG_A].astype(f32), gn_ref[...])
                kf = keep_f[m, bi].astype(f32)
                o_heads = [kf[:, hd * GLA_DV:(hd + 1) * GLA_DV] + outs_b[bi][hd] for hd in heads]
                ylo_ref[bi] = _gla_gate(o_heads, pgb_ref[bi, :, _PG_G:_PG_A].astype(f32), gn_ref[...])
    else:
        @pl.when(n == n_chunks - 1)
        def _():
            sfin_f_ref[...] = st_f[...]
            sfin_b_ref[...] = st_b[...]


def _gla_pass(pg, wa, ba, s0, gn=None):
    n_batch, n_chunks = pg.shape[0], pg.shape[1]
    merge = gn is not None
    half = n_chunks // 2
    const2 = lambda n: (0, 0)
    const3 = lambda n: (0, 0, 0)
    chunk = lambda width, idx: pl.BlockSpec((n_batch, None, GLA_CHUNK, width), lambda n: (0, idx(n), 0, 0))
    w_specs = [pl.BlockSpec((A_LOW_PAD, D_GLA_K), const2), pl.BlockSpec((1, D_GLA_K), const2)]
    s_spec = pl.BlockSpec((n_batch, GLA_DV, D_GLA_K), const3)
    s_shape = jax.ShapeDtypeStruct((n_batch, GLA_DV, D_GLA_K), f32)
    in_specs = [chunk(D_PG, lambda n: n), chunk(D_PG, lambda n: n_chunks - 1 - n)] + w_specs + w_specs + [s_spec,
                                                                                                            s_spec]
    args = [pg, pg, wa[0], ba[0], wa[1], ba[1], s0[0], s0[1]]
    scratch = [pltpu.VMEM((n_batch, GLA_DV, D_GLA_K), f32)] * 2
    if merge:
        assert n_chunks % 2 == 0
        in_specs.append(pl.BlockSpec((1, D_GLA_V), const2))
        args.append(gn)
        out_specs = [chunk(D_GLA_V, lambda n: jnp.minimum(n_chunks - 1 - n, half - 1)),
                     chunk(D_GLA_V, lambda n: jnp.maximum(n - half, 0))]
        y_shape = jax.ShapeDtypeStruct((n_batch, half, GLA_CHUNK, D_GLA_V), f32)
        out_shape = [y_shape, y_shape]
        scratch += [pltpu.VMEM((half, n_batch, GLA_CHUNK, D_GLA_V), bf16)] * 2
    else:
        out_specs = [s_spec, s_spec]
        out_shape = [s_shape, s_shape]
    return pl.pallas_call(
        functools.partial(_gla_kernel, merge=merge),
        grid=(n_chunks,),
        in_specs=in_specs,
        out_specs=out_specs,
        out_shape=out_shape,
        scratch_shapes=scratch,
        compiler_params=_cparams(("arbitrary",)),
        name="gla_merge" if merge else "gla_scan",
    )(*args)


def _pack_rows(v):
    n = v.shape[1] // 2
    w = lax.bitcast_convert_type(v.astype(bf16).astype(f32), jnp.int32)
    return (w[:, :n] & jnp.int32(-65536)) | lax.shift_right_logical(w[:, n:], 16)


def _unpack_rows(w):
    hi = lax.bitcast_convert_type(w & jnp.int32(-65536), f32)
    lo = lax.bitcast_convert_type(lax.shift_left(w, 16), f32)
    return hi, lo


def _outproj_router_kernel(yl_ref, ygl_ref, ygr_ref, x_ref, mod_ref, wo_ref, lg_ref, lb_ref, wrh_ref, wrl_ref,
                           br_ref, x1_ref, hp_ref, ids_ref, rank_ref, wts_ref, cnt_ref, running):
    i = pl.program_id(0)

    @pl.when(i == 0)
    def _():
        running[...] = jnp.zeros_like(running)

    d = D_MODEL
    g1 = mod_ref[0, :, 2 * d:3 * d]
    sh2 = mod_ref[0, :, 3 * d:4 * d]
    sc2 = mod_ref[0, :, 4 * d:5 * d]
    n_parts = OUTPROJ_PARTS
    tm = x_ref.shape[0] // n_parts
    grid_rows = ygl_ref.shape[2] // n_parts
    tok = [slice(p * tm, (p + 1) * tm) for p in range(n_parts)]
    nt_dims = (((1,), (1,)), ((), ()))
    ne = wrh_ref.shape[0]
    expert = lax.broadcasted_iota(jnp.int32, (ne, tm), 0).astype(f32)
    neg_inf = jnp.float32(-jnp.inf)

    ygs = [jnp.concatenate([ref[0, :, r, :] for r in range(p * grid_rows, (p + 1) * grid_rows)
                            for ref in (ygl_ref, ygr_ref)], axis=0).astype(bf16) for p in range(n_parts)]
    ys = [jnp.dot(yl_ref[tok[p], :], wo_ref[0:D_LRU, :], preferred_element_type=f32)
          + jnp.dot(ygs[p], wo_ref[D_LRU:2 * D_LRU, :], preferred_element_type=f32) for p in range(n_parts)]
    hmods = []
    for p in range(n_parts):
        z = DEEPNORM_ALPHA * x_ref[tok[p], :] + g1 * ys[p]
        mu = jnp.mean(z, axis=-1, keepdims=True)
        zc = z - mu
        var = jnp.mean(zc * zc, axis=-1, keepdims=True)
        x1 = zc * lax.rsqrt(var + LN_EPS) * lg_ref[...] + lb_ref[...]
        x1_ref[tok[p], :] = x1
        hmod = x1 * (1.0 + sc2) + sh2
        hp_ref[tok[p], :] = _pack_rows(hmod)
        hmods.append(hmod)

    logits = []
    for hmod in hmods:
        h_hi = hmod.astype(bf16)
        h_lo = (hmod - h_hi.astype(f32)).astype(bf16)
        logits.append(lax.dot_general(wrh_ref[...], h_hi, nt_dims, preferred_element_type=f32)
                      + lax.dot_general(wrh_ref[...], h_lo, nt_dims, preferred_element_type=f32)
                      + lax.dot_general(wrl_ref[...], h_hi, nt_dims, preferred_element_type=f32) + br_ref[...])
    picks = []
    for live in logits:
        sel = jnp.zeros((ne, tm), f32)
        ids, vals = [], []
        for _ in range(TOP_K):
            m = jnp.max(live, axis=0, keepdims=True)
            j = jnp.min(jnp.where(live == m, expert, float(ne)), axis=0, keepdims=True)
            pick = expert == j
            sel = jnp.where(pick, 1.0, sel)
            live = jnp.where(pick, neg_inf, live)
            ids.append(j)
            vals.append(m)
        picks.append((sel, ids, vals))

    ri = lax.broadcasted_iota(jnp.int32, (tm, tm), 0)
    ci = lax.broadcasted_iota(jnp.int32, (tm, tm), 1)
    earlier = (ri < ci).astype(bf16)
    slot = lax.broadcasted_iota(jnp.int32, (8, tm), 0)
    for p, (sel, ids, vals) in enumerate(picks):
        rank_dense = running[...] + jnp.dot(sel.astype(bf16), earlier, preferred_element_type=f32)
        running[...] = running[...] + jnp.sum(sel, axis=1, keepdims=True)
        exps = [jnp.exp(vk - vals[0]) for vk in vals]
        denom = exps[0] + exps[1] + exps[2] + exps[3]
        ids_out = jnp.zeros((8, tm), f32)
        rank_out = jnp.zeros((8, tm), f32)
        wts_out = jnp.zeros((8, tm), f32)
        for kk in range(TOP_K):
            rk = jnp.sum(jnp.where(expert == ids[kk], rank_dense, 0.0), axis=0, keepdims=True)
            here = slot == kk
            ids_out = jnp.where(here, ids[kk], ids_out)
            rank_out = jnp.where(here, rk, rank_out)
            wts_out = jnp.where(here, exps[kk] / denom, wts_out)
        ids_ref[:, tok[p]] = ids_out.astype(jnp.int32)
        rank_ref[:, tok[p]] = rank_out.astype(jnp.int32)
        wts_ref[tok[p], :] = jnp.concatenate([wts_out, jnp.zeros((LANES - 8, tm), f32)], axis=0).T
    cnt_ref[...] = running[...]


def _outproj_router(yl, yg, xt, mod3, wo, lg, lb, wr_hi, wr_lo, br, tokens_per_batch):
    T = xt.shape[0]
    tm = TOKEN_TILE
    tiles_per_batch = tokens_per_batch // tm
    tokrow = lambda i: (i, 0)
    const = lambda i: (0, 0)
    half_cols = yg[0].shape[1]
    rows_per_tile = tm // (2 * half_cols)
    yg_spec = pl.BlockSpec((1, half_cols, rows_per_tile, D_GLA_V),
                           lambda i: (i // tiles_per_batch, 0, i % tiles_per_batch, 0))
    return pl.pallas_call(
        _outproj_router_kernel,
        grid=(T // tm,),
        in_specs=[pl.BlockSpec((tm, D_LRU), tokrow),
                  yg_spec, yg_spec,
                  pl.BlockSpec((tm, D_MODEL), tokrow),
                  pl.BlockSpec((1, 1, N_MOD * D_MODEL), lambda i: (i // tiles_per_batch, 0, 0)),
                  pl.BlockSpec((D_MODEL, D_MODEL), const),
                  pl.BlockSpec((1, D_MODEL), const),
                  pl.BlockSpec((1, D_MODEL), const),
                  pl.BlockSpec((N_EXPERTS, D_MODEL), const),
                  pl.BlockSpec((N_EXPERTS, D_MODEL), const),
                  pl.BlockSpec((N_EXPERTS, 1), const)],
        out_specs=[pl.BlockSpec((tm, D_MODEL), tokrow),
                   pl.BlockSpec((tm, D_MODEL // 2), tokrow),
                   pl.BlockSpec((8, tm), lambda i: (0, i)),
                   pl.BlockSpec((8, tm), lambda i: (0, i)),
                   pl.BlockSpec((tm, LANES), tokrow),
                   pl.BlockSpec((N_EXPERTS, 1), const)],
        out_shape=[jax.ShapeDtypeStruct((T, D_MODEL), f32),
                   jax.ShapeDtypeStruct((T, D_MODEL // 2), jnp.int32),
                   jax.ShapeDtypeStruct((8, T), jnp.int32),
                   jax.ShapeDtypeStruct((8, T), jnp.int32),
                   jax.ShapeDtypeStruct((T, LANES), f32),
                   jax.ShapeDtypeStruct((N_EXPERTS, 1), f32)],
        scratch_shapes=[pltpu.VMEM((N_EXPERTS, 1), f32)],
        compiler_params=_cparams(("arbitrary",)),
        name="outproj_router",
    )(yl, yg[0], yg[1], xt, mod3, wo, lg, lb, wr_hi, wr_lo, br)


def _route_kernel(ids_ref, rank_ref, cnt_ref, dest_ref, tiles_ref, *, bm):
    ne = cnt_ref.shape[0]
    cnt = cnt_ref[...]
    padded = jnp.floor((cnt + (bm - 1.0)) * (1.0 / bm)) * bm
    sub = lax.broadcasted_iota(jnp.int32, (ne, LANES), 0)
    lane = lax.broadcasted_iota(jnp.int32, (ne, LANES), 1)
    padded_row = jnp.sum(jnp.where(sub == lane, padded, 0.0), axis=0, keepdims=True)
    base = jnp.sum(jnp.where(lane < sub, padded_row, 0.0), axis=1, keepdims=True)
    ends = base + padded

    tc = ids_ref.shape[1]
    expert = lax.broadcasted_iota(jnp.int32, (ne, tc), 0)
    slot = lax.broadcasted_iota(jnp.int32, (8, tc), 0)
    ids = ids_ref[...]
    group_base = jnp.zeros((8, tc), f32)
    for kk in range(TOP_K):
        b_k = jnp.sum(jnp.where(expert == ids[kk:kk + 1], base, 0.0), axis=0, keepdims=True)
        group_base = jnp.where(slot == kk, b_k, group_base)
    dest_ref[...] = group_base.astype(jnp.int32) + rank_ref[...]

    nl = tiles_ref.shape[1]
    start = lax.broadcasted_iota(jnp.int32, (ne, nl), 1).astype(f32) * bm
    te = jnp.minimum(jnp.sum(jnp.where(start >= ends, 1.0, 0.0), axis=0, keepdims=True), ne - 1.0)
    at_te = lax.broadcasted_iota(jnp.int32, (ne, nl), 0).astype(f32) == te
    cnt_te = jnp.sum(jnp.where(at_te, cnt, 0.0), axis=0, keepdims=True)
    base_te = jnp.sum(jnp.where(at_te, base, 0.0), axis=0, keepdims=True)
    valid = jnp.clip(cnt_te - (start[0:1] - base_te), 0.0, float(bm))
    next_group = jnp.sum(jnp.where(at_te, ends, 0.0), axis=0, keepdims=True) * (1.0 / bm)
    last_used = jnp.sum(padded, axis=0, keepdims=True) * (1.0 / bm) - 1.0
    srow = lax.broadcasted_iota(jnp.int32, (8, nl), 0)
    table = jnp.where(srow == 0, te, jnp.where(srow == 1, valid, jnp.where(srow == 2, next_group,
                                                                           jnp.where(srow == 3, last_used, 0.0))))
    tiles_ref[...] = table.astype(jnp.int32)


def _route(ids, rank, cnt, bm, n_tiles):
    T = ids.shape[1]
    tc = 2048
    nl = -(-n_tiles // LANES) * LANES
    tok = lambda i: (0, i)
    const = lambda i: (0, 0)
    return pl.pallas_call(
        functools.partial(_route_kernel, bm=bm),
        grid=(T // tc,),
        in_specs=[pl.BlockSpec((8, tc), tok), pl.BlockSpec((8, tc), tok),
                  pl.BlockSpec((N_EXPERTS, 1), const)],
        out_specs=[pl.BlockSpec((8, tc), tok), pl.BlockSpec((8, nl), const)],
        out_shape=[jax.ShapeDtypeStruct((8, T), jnp.int32), jax.ShapeDtypeStruct((8, nl), jnp.int32)],
        compiler_params=_cparams(("arbitrary",)),
        name="route",
    )(ids, rank, cnt)


def _sc_workers():
    info = plsc.get_sparse_core_info()
    return info.num_cores, info.num_subcores


def _sc_dispatch(rows, dest_flat, n_out):
    T, D = rows.shape
    nc, ns = _sc_workers()
    per_w = T // (nc * ns)
    n_chunks = per_w // SC_CHUNK
    assert n_chunks % 2 == 0
    mesh = plsc.VectorSubcoreMesh(core_axis_name="c", subcore_axis_name="s")

    @functools.partial(
        pl.kernel, mesh=mesh,
        out_type=jax.ShapeDtypeStruct((n_out, D), rows.dtype),
        scratch_types=([pltpu.VMEM((SC_CHUNK,), jnp.int32)] * (2 * TOP_K)
                       + [pltpu.VMEM((SC_CHUNK, D), rows.dtype)] * 2
                       + [pltpu.SemaphoreType.DMA] * (2 * TOP_K)),
    )
    def k(rows_hbm, dest_hbm, out_hbm, *scratch):
        idx_v = (scratch[:TOP_K], scratch[TOP_K:2 * TOP_K])
        rows_v = scratch[2 * TOP_K:2 * TOP_K + 2]
        sems = (scratch[2 * TOP_K + 2:3 * TOP_K + 2], scratch[3 * TOP_K + 2:])
        wid = lax.axis_index("s") * nc + lax.axis_index("c")
        base = wid * per_w

        def load(chunk, b):
            off = base + chunk * SC_CHUNK
            pltpu.sync_copy(rows_hbm.at[pl.ds(off, SC_CHUNK)], rows_v[b])
            for kk in range(TOP_K):
                pltpu.sync_copy(dest_hbm.at[pl.ds(kk * T + off, SC_CHUNK)], idx_v[b][kk])

        load(0, 0)

        @pl.loop(0, n_chunks, step=2)
        def _(j):
            for b in range(2):
                copies = [pltpu.async_copy(rows_v[b], out_hbm.at[idx_v[b][kk]], sems[b][kk])
                          for kk in range(TOP_K)]

                @pl.when(j + b + 1 < n_chunks)
                def _():
                    load(j + b + 1, 1 - b)

                for cp in copies:
                    cp.wait()

    return k(rows, dest_flat)


def _sc_gather(table, idx):
    _, D = table.shape
    N = idx.shape[0]
    nc, ns = _sc_workers()
    per_w = N // (nc * ns)
    n_chunks = per_w // SC_CHUNK
    assert n_chunks % 2 == 0
    mesh = plsc.VectorSubcoreMesh(core_axis_name="c", subcore_axis_name="s")

    @functools.partial(
        pl.kernel, mesh=mesh,
        out_type=jax.ShapeDtypeStruct((N, D), table.dtype),
        scratch_types=([pltpu.VMEM((SC_CHUNK,), jnp.int32)] * 2
                       + [pltpu.VMEM((SC_CHUNK, D), table.dtype)] * 2
                       + [pltpu.SemaphoreType.DMA] * 2),
    )
    def k(table_hbm, idx_hbm, out_hbm, idx0, idx1, buf0, buf1, sem0, sem1):
        idxs, bufs, sems = (idx0, idx1), (buf0, buf1), (sem0, sem1)
        wid = lax.axis_index("s") * nc + lax.axis_index("c")
        base = wid * per_w

        def gather(b):
            return pltpu.make_async_copy(table_hbm.at[idxs[b]], bufs[b], sems[b])

        def start(chunk, b):
            pltpu.sync_copy(idx_hbm.at[pl.ds(base + chunk * SC_CHUNK, SC_CHUNK)], idxs[b])
            gather(b).start()

        def finish(chunk, b):
            gather(b).wait()
            pltpu.sync_copy(bufs[b], out_hbm.at[pl.ds(base + chunk * SC_CHUNK, SC_CHUNK)])

        start(0, 0)

        @pl.loop(0, n_chunks, step=2)
        def _(j):
            start(j + 1, 1)
            finish(j, 0)

            @pl.when(j + 2 < n_chunks)
            def _():
                start(j + 2, 0)

            finish(j + 1, 1)

    return k(table, idx)


def _ffn_kernel(te_ref, tv_ref, tn_ref, tl_ref, xs_ref, wg_hbm, bg_ref, wu_hbm, bu_ref, wd_hbm, bd_ref, eo_ref,
                w_buf, sems, slot_ref):
    i = pl.program_id(0)
    n_tiles = pl.num_programs(0)
    w_hbm = (wg_hbm, wu_hbm, wd_hbm)

    def fetch(e, slot):
        return [pltpu.make_async_copy(w_hbm[m].at[e], w_buf.at[slot, m], sems.at[slot, m]) for m in range(3)]

    @pl.when(i == 0)
    def _():
        slot_ref[0] = 0
        for cp in fetch(te_ref[0], 0):
            cp.start()

    prev = te_ref[jnp.maximum(i - 1, 0)]
    first = (i == 0) | (te_ref[i] != prev)

    @pl.when(first & (i > 0))
    def _():
        slot_ref[0] = 1 - slot_ref[0]

    slot = slot_ref[0]

    @pl.when(first)
    def _():
        for cp in fetch(te_ref[i], slot):
            cp.wait()
        nxt = tn_ref[i]
        e_nxt = te_ref[jnp.minimum(nxt, n_tiles - 1)]

        @pl.when((nxt > i) & (nxt < n_tiles) & (e_nxt != te_ref[i]))
        def _():
            for cp in fetch(e_nxt, 1 - slot):
                cp.start()

    wg_b, wu_b, wd_b = (w_buf.at[slot, m] for m in range(3))
    valid = tv_ref[i]

    def ffn_rows(r0, m):
        blocks = [(r0 + o, min(MOE_BLOCK, m)) for o in range(0, m, MOE_BLOCK)]
        xs, gates, ups, acts = [], [], [], []
        for b0, bm_ in blocks:
            row = lax.broadcasted_iota(jnp.int32, (bm_, 1), 0) + b0
            xw = jnp.where(row < valid, xs_ref[b0:b0 + bm_, :], 0)
            x_hi, x_lo = _unpack_rows(xw)
            xs.append(jnp.concatenate([x_hi, x_lo], axis=1).astype(bf16))
        for x in xs:
            gates.append(jnp.minimum(jnp.dot(x, wg_b[...], preferred_element_type=f32) + bg_ref[...],
                                     SWIGLU_LIMIT))
            ups.append(jnp.clip(jnp.dot(x, wu_b[...], preferred_element_type=f32) + bu_ref[...],
                                -SWIGLU_LIMIT, SWIGLU_LIMIT))
        for gate, up in zip(gates, ups):
            acts.append(((up + 1.0) * gate * jax.nn.sigmoid(SWIGLU_ALPHA * gate)).astype(bf16))
        for (b0, bm_), act in zip(blocks, acts):
            out = jnp.dot(act, wd_b[...], preferred_element_type=f32) + bd_ref[...]
            eo_ref[b0:b0 + bm_, :] = _pack_rows(out)

    def zero_rows(r0, m):
        eo_ref[r0:r0 + m, :] = jnp.zeros((m, eo_ref.shape[1]), eo_ref.dtype)

    for r0 in range(0, xs_ref.shape[0], MOE_PASS):
        lo = 0
        for m in MOE_PASS_SIZES:
            @pl.when((valid > r0 + lo) & ((valid <= r0 + m) | (m == MOE_PASS)))
            def _(r0=r0, m=m):
                ffn_rows(r0, m)
                if m < MOE_PASS:
                    zero_rows(r0 + m, MOE_PASS - m)
            lo = m

        @pl.when(valid <= r0)
        def _(r0=r0):
            zero_rows(r0, MOE_PASS)


def _expert_ffn(tile_expert, tile_valid, tile_next, last_used, xs, w_gate, b_gate, w_up, b_up, w_down, b_down):
    n_rows, dp = xs.shape
    d = 2 * dp
    bm = MOE_TILE
    d_e = w_gate.shape[-1]
    assert d == d_e
    bspec = lambda n_: pl.BlockSpec((None, 1, n_), lambda i, te, tv, tn, tl: (te[i], 0, 0))
    hbm = pl.BlockSpec(memory_space=pl.ANY)
    grid_spec = pltpu.PrefetchScalarGridSpec(
        num_scalar_prefetch=4,
        grid=(n_rows // bm,),
        in_specs=[pl.BlockSpec((bm, dp), lambda i, te, tv, tn, tl: (jnp.minimum(i, tl[0]), 0)),
                  hbm, bspec(d_e), hbm, bspec(d_e), hbm, bspec(d)],
        out_specs=pl.BlockSpec((bm, dp), lambda i, te, tv, tn, tl: (i, 0)),
        scratch_shapes=[pltpu.VMEM((2, 3, d, d_e), f32),
                        pltpu.SemaphoreType.DMA((2, 3)),
                        pltpu.SMEM((1,), jnp.int32)],
    )
    return pl.pallas_call(
        _ffn_kernel,
        grid_spec=grid_spec,
        out_shape=jax.ShapeDtypeStruct((n_rows, dp), jnp.int32),
        compiler_params=_cparams(("arbitrary",)),
        name="expert_ffn",
    )(tile_expert, tile_valid, tile_next, last_used, xs, w_gate, b_gate.reshape(N_EXPERTS, 1, d_e), w_up,
      b_up.reshape(N_EXPERTS, 1, d_e), w_down, b_down.reshape(N_EXPERTS, 1, d))


def _combine_kernel(eg_ref, wts_ref, x1_ref, mod_ref, lg_ref, lb_ref, o_ref):
    w = wts_ref[...]
    y_hi, y_lo = _unpack_rows(eg_ref[0])
    y_hi, y_lo = y_hi * w[:, 0:1], y_lo * w[:, 0:1]
    for kk in range(1, TOP_K):
        e_hi, e_lo = _unpack_rows(eg_ref[kk])
        y_hi = y_hi + e_hi * w[:, kk:kk + 1]
        y_lo = y_lo + e_lo * w[:, kk:kk + 1]
    y = jnp.concatenate([y_hi, y_lo], axis=1)
    z = DEEPNORM_ALPHA * x1_ref[...] + mod_ref[0] * y
    mu = jnp.mean(z, axis=-1, keepdims=True)
    zc = z - mu
    var = jnp.mean(zc * zc, axis=-1, keepdims=True)
    o_ref[...] = zc * lax.rsqrt(var + LN_EPS) * lg_ref[...] + lb_ref[...]


def _combine_ln(eg, wts, x1, mod3, lg, lb, tokens_per_batch):
    T = x1.shape[0]
    tm = TOKEN_TILE
    tiles_per_batch = tokens_per_batch // tm
    const = lambda i: (0, 0)
    return pl.pallas_call(
        _combine_kernel,
        grid=(T // tm,),
        in_specs=[pl.BlockSpec((TOP_K, tm, D_MODEL // 2), lambda i: (0, i, 0)),
                  pl.BlockSpec((tm, LANES), lambda i: (i, 0)),
                  pl.BlockSpec((tm, D_MODEL), lambda i: (i, 0)),
                  pl.BlockSpec((1, 1, D_MODEL), lambda i: (i // tiles_per_batch, 0, N_MOD - 1)),
                  pl.BlockSpec((1, D_MODEL), const),
                  pl.BlockSpec((1, D_MODEL), const)],
        out_specs=pl.BlockSpec((tm, D_MODEL), lambda i: (i, 0)),
        out_shape=jax.ShapeDtypeStruct((T, D_MODEL), f32),
        compiler_params=_cparams(("arbitrary",)),
        name="combine_ln",
    )(eg, wts, x1, mod3, lg, lb)


def _block_diag(w):
    n, c, d = w.shape
    eye = jnp.eye(n, dtype=w.dtype)
    return jnp.einsum('ncd,nm->ncmd', w, eye).reshape(n * c, n * d)


def kernel(x, c, ctx, c_ctx, w_ada, b_ada, w_in, conv_w, conv_b, lru_wa, lru_ba, lru_wx, lru_bx,
           lru_lam, gla_wa, gla_ba, gla_norm_g, w_out, ln1_g, ln1_b, w_router, b_router, w_gate,
           b_gate, w_up, b_up, w_down, b_down, ln2_g, ln2_b):
    B, L, D = x.shape
    Lc = ctx.shape[1]
    T = B * L
    rows = L // GRID_W
    l = 0

    cpad = jnp.zeros((8, D), f32).at[0:B].set(c).at[B].set(c_ctx)
    w_cat = jnp.pad(w_in[l], ((0, 0), (0, D_PROJ - w_in.shape[-1]))).astype(bf16)
    wg = [jnp.concatenate([_block_diag(lru_wa[l, d]), _block_diag(lru_wx[l, d])], axis=1).astype(bf16)
          for d in range(2)]
    bg = [jnp.concatenate([lru_ba[l, d], lru_bx[l, d]])[None] for d in range(2)]
    lam = [lru_lam[l, d][None] for d in range(2)]
    wa = [jnp.pad(gla_wa[l, d], ((0, A_LOW_PAD - GATE_RANK), (0, 0))).astype(bf16) for d in range(2)]
    ba = [gla_ba[l, d][None] for d in range(2)]
    cw, cb = conv_w[l], conv_b[l][None]
    wr_t = w_router[l].T
    wr_hi = wr_t.astype(bf16)
    wr_lo = (wr_t - wr_hi.astype(f32)).astype(bf16)
    br = b_router[l][:, None]
    c_xl, c_gl, c_pg = (0, D_LRU), (D_LRU, 2 * D_LRU), (2 * D_LRU, D_PROJ)

    mod3 = _ada_mod(cpad, w_ada[l], b_ada[l]).reshape(8, 1, N_MOD * D)

    tiles_per_batch = L // INPROJ_TILE
    xl_c, pg_c = _inproj_rows(ctx.reshape(B * Lc, D), mod3, w_cat, lambda i: B, (c_xl, c_pg), (f32, bf16))
    zero_h = jnp.zeros((B, 1, D_LRU), f32)
    zero_s = jnp.zeros((B, GLA_DV, D_GLA_K), f32)
    pg_c = pg_c.reshape(B, Lc // GLA_CHUNK, GLA_CHUNK, D_PG)
    h_ctx = []
    for d in range(2):
        _, _, hf = _lru_pass(xl_c, wg[d], bg[d], lam[d], zero_h, n_batch=B, tile=Lc, reverse=bool(d),
                             conv=(cw, cb, Lc))
        h_ctx.append(hf)
    s_ctx = _gla_pass(pg_c, wa, ba, (zero_s, zero_s))

    xt = x.reshape(T, D)
    xl, gl = _inproj_rows(xt, mod3, w_cat, lambda i: i // tiles_per_batch, (c_xl, c_gl), (f32, bf16))
    hb, u_lru, _ = _lru_pass(xl, wg[1], bg[1], lam[1], h_ctx[1], n_batch=B, tile=LRU_TILE, reverse=True,
                             conv=(cw, cb, GRID_W))
    y_lru, _ = _lru_pass(u_lru, wg[0], bg[0], lam[0], h_ctx[0], n_batch=B, tile=LRU_TILE, reverse=False,
                         merge_with=(hb, gl))
    pg = _inproj_cols(x.reshape(B, rows, GRID_W, D), mod3, w_cat, c_pg)
    y_gla = _gla_pass(pg, wa, ba, s_ctx, gn=gla_norm_g[l][None])

    x1, hp, ids, rank, wts, cnt = _outproj_router(
        y_lru, y_gla, xt, mod3, w_out[l].astype(bf16), ln1_g[l][None], ln1_b[l][None], wr_hi, wr_lo, br, L)

    n_tiles = T * TOP_K // MOE_TILE + N_EXPERTS
    dest, tiles = _route(ids, rank, cnt, MOE_TILE, n_tiles)
    xs = _sc_dispatch(hp, dest.reshape(-1), n_tiles * MOE_TILE)
    eo = _expert_ffn(tiles[0, :n_tiles], tiles[1, :n_tiles], tiles[2, :n_tiles], tiles[3, :1], xs, w_gate[l],
                     b_gate[l], w_up[l], b_up[l], w_down[l], b_down[l])
    eg = _sc_gather(eo, dest[:TOP_K].reshape(-1)).reshape(TOP_K, T, D // 2)
    out = _combine_ln(eg, wts, x1, mod3, ln2_g[l][None], ln2_b[l][None], L)
    return out.reshape(B, L, D)
```

```python
import functools

import jax
import jax.numpy as jnp
from jax import lax
from jax.experimental import pallas as pl
from jax.experimental.pallas import tpu as pltpu
from jax.experimental.pallas import tpu_sc as plsc

D_MODEL = 1024
DEPTH = 1
GRID_W = 64
D_LRU = 512
LRU_BLOCKS = 8
CONV_W = 4
LRU_C = 8.0
GLA_HEADS = 4
D_GLA_V = 512
D_GLA_K = 256
GLA_DK = 64
GLA_DV = 128
GATE_RANK = 16
GATE_TAU = 16.0
GLA_CHUNK = 64
N_EXPERTS = 32
TOP_K = 4
SWIGLU_LIMIT = 7.0
SWIGLU_ALPHA = 1.702
N_MOD = 6
DEEPNORM_ALPHA = (2.0 * DEPTH) ** 0.25
LN_EPS = 1e-5
RMS_EPS = 1e-6

LANES = 128
A_LOW_PAD = LANES
D_PG = 2 * D_GLA_K + 2 * D_GLA_V + A_LOW_PAD
D_PROJ = 2 * D_LRU + D_PG
TOKEN_TILE = 1024
INPROJ_TILE = 1024
LRU_TILE = 512
MOE_TILE = 1024
MOE_PASS = 512
MOE_PASS_SIZES = (128, 256, MOE_PASS)
MOE_BLOCK = 256
OUTPROJ_PART = 128
SC_CHUNK = 64
VMEM_LIMIT = 48 * 1024 * 1024

f32 = jnp.float32
bf16 = jnp.bfloat16


def _cparams(sem):
    return pltpu.CompilerParams(dimension_semantics=sem, vmem_limit_bytes=VMEM_LIMIT)


def _ada_kernel(c_ref, w_ref, b_ref, o_ref):
    s = c_ref[...]
    s = s * jax.nn.sigmoid(s)
    o_ref[...] = jnp.dot(s.astype(bf16), w_ref[...], preferred_element_type=f32) + b_ref[...]


def _ada_mod(cpad, w, b):
    n = w.shape[1]
    tn = 1024
    return pl.pallas_call(
        _ada_kernel,
        grid=(n // tn,),
        in_specs=[pl.BlockSpec((8, D_MODEL), lambda j: (0, 0)),
                  pl.BlockSpec((D_MODEL, tn), lambda j: (0, j)),
                  pl.BlockSpec((1, tn), lambda j: (0, j))],
        out_specs=pl.BlockSpec((8, tn), lambda j: (0, j)),
        out_shape=jax.ShapeDtypeStruct((8, n), f32),
        compiler_params=_cparams(("arbitrary",)),
        name="ada_mod",
    )(cpad, w, b.reshape(1, n))


def _inproj_kernel(x_ref, mod_ref, w_ref, *out_refs, parts, col_major):
    sh = mod_ref[0, :, 0:D_MODEL]
    sc = mod_ref[0, :, D_MODEL:2 * D_MODEL]
    if col_major:
        x = jnp.concatenate([x_ref[0, :, j, :] for j in range(x_ref.shape[2])], axis=0)
    else:
        x = x_ref[...]
    u = (x * (1.0 + sc) + sh).astype(bf16)
    for (lo, hi), o_ref in zip(parts, out_refs):
        p = jnp.dot(u, w_ref[:, lo:hi], preferred_element_type=f32).astype(o_ref.dtype)
        o_ref[...] = p.reshape(o_ref.shape)


def _inproj_rows(xt, mod3, w_cat, batch_of_tile, parts, dtypes):
    T = xt.shape[0]
    tm = INPROJ_TILE
    return pl.pallas_call(
        functools.partial(_inproj_kernel, parts=parts, col_major=False),
        grid=(T // tm,),
        in_specs=[pl.BlockSpec((tm, D_MODEL), lambda i: (i, 0)),
                  pl.BlockSpec((1, 1, 2 * D_MODEL), lambda i: (batch_of_tile(i), 0, 0)),
                  pl.BlockSpec((D_MODEL, D_PROJ), lambda i: (0, 0))],
        out_specs=[pl.BlockSpec((tm, hi - lo), lambda i: (i, 0)) for lo, hi in parts],
        out_shape=[jax.ShapeDtypeStruct((T, hi - lo), dt) for (lo, hi), dt in zip(parts, dtypes)],
        compiler_params=_cparams(("arbitrary",)),
        name="inproj_rows",
    )(xt, mod3, w_cat)


def _inproj_cols(x4, mod3, w_cat, part):
    n_batch, rows, cols, _ = x4.shape
    lo, hi = part
    cb = INPROJ_TILE // rows
    return pl.pallas_call(
        functools.partial(_inproj_kernel, parts=(part,), col_major=True),
        grid=(n_batch, cols // cb),
        in_specs=[pl.BlockSpec((1, rows, cb, D_MODEL), lambda b, n: (b, 0, n, 0)),
                  pl.BlockSpec((1, 1, 2 * D_MODEL), lambda b, n: (b, 0, 0)),
                  pl.BlockSpec((D_MODEL, D_PROJ), lambda b, n: (0, 0))],
        out_specs=[pl.BlockSpec((1, cb, rows, hi - lo), lambda b, n: (b, n, 0, 0))],
        out_shape=[jax.ShapeDtypeStruct((n_batch, cols, rows, hi - lo), bf16)],
        compiler_params=_cparams(("arbitrary", "arbitrary")),
        name="inproj_cols",
    )(x4, mod3, w_cat)[0]


def _gelu_tanh(x):
    return 0.5 * x * (1.0 + jnp.tanh(0.7978845608028654 * (x + 0.044715 * (x * x * x))))


def _lru_kernel(*refs, row_w, reverse, merge):
    if merge:
        u_ref, wg_ref, bg_ref, lam_ref, h0_ref, hb_ref, gl_ref, out_ref, hfin_ref, carry, h_nat = refs
    else:
        xl_ref, cw_ref, cb_ref, wg_ref, bg_ref, lam_ref, h0_ref, out_ref, u_out_ref, hfin_ref, carry = refs
    t = pl.program_id(1)

    @pl.when(t == 0)
    def _():
        carry[...] = h0_ref[0]

    n_seg = 8
    blk = lambda v, j: v[j * n_seg:(j + 1) * n_seg]
    if merge:
        u = u_ref[...]
        tt, ch = u.shape
        seg = tt // n_seg
    else:
        _, seg, ch = xl_ref.shape
        tt = n_seg * seg
        x = jnp.concatenate([xl_ref[:, j, :] for j in range(seg)], axis=0)
        segs_per_row = row_w // seg
        s_idx = lax.broadcasted_iota(jnp.int32, (n_seg, ch), 0)
        has_prev = (s_idx % segs_per_row) != 0
        has_next = (s_idx % segs_per_row) != segs_per_row - 1
        from_prev = lambda v: jnp.where(has_prev, pltpu.roll(v, 1, 0), 0.0)
        from_next = lambda v: jnp.where(has_next, pltpu.roll(v, n_seg - 1, 0), 0.0)
        x_m1 = jnp.concatenate([from_prev(blk(x, seg - 1)), x[:tt - n_seg]], axis=0)
        x_m2 = jnp.concatenate([from_prev(blk(x, seg - 2)), from_prev(blk(x, seg - 1)), x[:tt - 2 * n_seg]],
                               axis=0)
        x_p1 = jnp.concatenate([x[n_seg:], from_next(blk(x, 0))], axis=0)
        cw = cw_ref[...]
        u = cb_ref[...] + x_m2 * cw[0:1] + x_m1 * cw[1:2] + x * cw[2:3] + x_p1 * cw[3:4]
        u_out_ref[...] = u
    rows_q = tt // 4
    g_parts = [jnp.dot(u[q * rows_q:(q + 1) * rows_q].astype(bf16), wg_ref[...], preferred_element_type=f32)
               + bg_ref[...] for q in range(4)]
    lam = lam_ref[...]
    softplus_neg = jnp.maximum(-lam, 0.0) + jnp.log(1.0 + jnp.exp(-jnp.abs(lam)))
    log_a_scale = -LRU_C * softplus_neg

    def recurrence_terms(j):
        q, r = divmod(j * n_seg, rows_q)
        g_j, u_j = g_parts[q][r:r + n_seg], blk(u, j)
        log_a = log_a_scale * jax.nn.sigmoid(g_j[:, 0:ch])
        a_j = jnp.exp(log_a)
        series = log_a * (-2.0 + log_a * (-2.0 + log_a * (-4.0 / 3.0)))
        one_minus_a2 = jnp.where(log_a > -0.005, series, 1.0 - a_j * a_j)
        root = jnp.where(one_minus_a2 > 0.0, one_minus_a2 * lax.rsqrt(one_minus_a2), 0.0)
        return a_j, root * (jax.nn.sigmoid(g_j[:, ch:2 * ch]) * u_j)

    half = seg // 2
    h_loc, a_cum = [None] * seg, [None] * seg
    h_run, a_run = [None, None], [None, None]
    for i in range(half):
        for p in range(2):
            j = p * half + (half - 1 - i if reverse else i)
            a_j, b_j = recurrence_terms(j)
            h_run[p] = b_j if i == 0 else a_j * h_run[p] + b_j
            a_run[p] = a_j if i == 0 else a_j * a_run[p]
            h_loc[j], a_cum[j] = h_run[p], a_run[p]
    pieces = [(s, p) for s in range(n_seg) for p in range(2)]
    if reverse:
        pieces.reverse()
    entering = [[None] * n_seg, [None] * n_seg]
    state = carry[...]
    for s, p in pieces:
        entering[p][s] = state
        state = h_run[p][s:s + 1] + a_run[p][s:s + 1] * state
    carry[...] = state
    hfin_ref[0] = state
    enter = [jnp.concatenate(e, axis=0) for e in entering]
    h_rows = [h_loc[j] + a_cum[j] * enter[j // half] for j in range(seg)]
    if merge:
        for j in range(seg):
            h_nat[:, j, :] = h_rows[j] + blk(hb_ref, j)
        h = h_nat[...].reshape(tt, ch)
        out_ref[...] = (h * _gelu_tanh(gl_ref[...].astype(f32))).astype(out_ref.dtype)
    else:
        out_ref[...] = jnp.concatenate(h_rows, axis=0)


def _lru_pass(x_in, wg, bg, lam, h0, *, n_batch, tile, reverse, conv=None, merge_with=None):
    T = x_in.shape[0]
    nt = T // n_batch // tile
    merge = merge_with is not None
    assert merge != (conv is not None)
    seg = tile // 8

    def tok(b, t):
        return (b * nt + (nt - 1 - t if reverse else t), 0)

    tile_spec = pl.BlockSpec((tile, D_LRU), tok)
    const = lambda b, t: (0, 0)
    gate_specs = [pl.BlockSpec((D_LRU, 2 * D_LRU), const),
                  pl.BlockSpec((1, 2 * D_LRU), const),
                  pl.BlockSpec((1, D_LRU), const),
                  pl.BlockSpec((1, 1, D_LRU), lambda b, t: (b, 0, 0))]
    state_spec = pl.BlockSpec((1, 1, D_LRU), lambda b, t: (b, 0, 0))
    state_shape = jax.ShapeDtypeStruct((n_batch, 1, D_LRU), f32)
    scratch = [pltpu.VMEM((1, D_LRU), f32)]
    if merge:
        hb, gl = merge_with
        row_w = None
        in_specs = [tile_spec] + gate_specs + [tile_spec, tile_spec]
        args = [x_in, wg, bg, lam, h0, hb, gl]
        scratch += [pltpu.VMEM((8, seg, D_LRU), f32)]
        out_specs = [tile_spec, state_spec]
        out_shape = [jax.ShapeDtypeStruct((T, D_LRU), bf16), state_shape]
    else:
        conv_w, conv_b, row_w = conv
        assert row_w % seg == 0 and seg >= 2
        in_specs = [pl.BlockSpec((8, seg, D_LRU), lambda b, t: tok(b, t) + (0,)),
                    pl.BlockSpec((CONV_W, D_LRU), const),
                    pl.BlockSpec((1, D_LRU), const)] + gate_specs
        args = [x_in.reshape(T // seg, seg, D_LRU), conv_w, conv_b, wg, bg, lam, h0]
        out_specs = [tile_spec, tile_spec, state_spec]
        out_shape = [jax.ShapeDtypeStruct((T, D_LRU), f32), jax.ShapeDtypeStruct((T, D_LRU), f32), state_shape]
    return pl.pallas_call(
        functools.partial(_lru_kernel, row_w=row_w, reverse=reverse, merge=merge),
        grid=(n_batch, nt),
        in_specs=in_specs,
        out_specs=out_specs,
        out_shape=out_shape,
        scratch_shapes=scratch,
        compiler_params=_cparams(("arbitrary", "arbitrary")),
        name="lru_merge" if merge else "lru_scan",
    )(*args)


_PG_Q, _PG_K, _PG_V = 0, D_GLA_K, 2 * D_GLA_K
_PG_G, _PG_A = 2 * D_GLA_K + D_GLA_V, 2 * D_GLA_K + 2 * D_GLA_V


def _gla_chunks(streams):
    ck = GLA_CHUNK
    nh = GLA_HEADS
    head_of_lane = lax.broadcasted_iota(jnp.int32, (1, D_GLA_K), 1) // GLA_DK
    nt_dims = (((1,), (1,)), ((), ()))
    tn_dims = (((0,), (0,)), ((), ()))
    ii = lax.broadcasted_iota(jnp.int32, (nh * ck, ck), 0) & (ck - 1)
    jj = lax.broadcasted_iota(jnp.int32, (nh * ck, ck), 1)

    units = []
    for pg_ref, wa_ref, ba_ref, state, reverse in streams:
        n_batch = pg_ref.shape[0]
        rows_all = n_batch * ck
        a_low = jnp.concatenate([pg_ref[bi, :, _PG_A:_PG_A + A_LOW_PAD] for bi in range(n_batch)], axis=0)
        z = jnp.dot(a_low, wa_ref[...], preferred_element_type=f32) + ba_ref[...]
        bcum_all = (jnp.minimum(z, 0.0) - jnp.log(1.0 + jnp.exp(-jnp.abs(z)))) * (1.0 / GATE_TAU)
        step = lax.broadcasted_iota(jnp.int32, (rows_all, D_GLA_K), 0) & (ck - 1)
        d = 1
        while d < ck:
            if reverse:
                bcum_all = bcum_all + jnp.where(step < ck - d, pltpu.roll(bcum_all, rows_all - d, 0), 0.0)
            else:
                bcum_all = bcum_all + jnp.where(step >= d, pltpu.roll(bcum_all, d, 0), 0.0)
            d *= 2
        seen = (jj >= ii) if reverse else (jj <= ii)
        for bi in range(n_batch):
            units.append(dict(pg=pg_ref, bi=bi, state=state, reverse=reverse, seen=seen,
                              bcum=bcum_all[bi * ck:(bi + 1) * ck]))

    for u in units:
        pg_ref, bi, bcum = u["pg"], u["bi"], u["bcum"]
        q = pg_ref[bi, :, _PG_Q:_PG_K].astype(f32) * (GLA_DK ** -0.5)
        k = pg_ref[bi, :, _PG_K:_PG_V].astype(f32)
        u["btot"] = bcum[0:1] if u["reverse"] else bcum[ck - 1:ck]
        q_dec = q * jnp.exp(bcum)
        k_dec = (k * jnp.exp(-bcum)).astype(bf16)
        k_end = k * jnp.exp(u["btot"] - bcum)
        u["s_t"] = u["state"][bi]
        by_head = lambda t: jnp.concatenate([jnp.where(head_of_lane == hd, t, 0.0) for hd in range(nh)],
                                            axis=0).astype(bf16)
        u["q_blk"] = by_head(q_dec)
        u["k_blk"] = by_head(k_end)
        u["rhs"] = jnp.concatenate([u["s_t"].astype(bf16), k_dec], axis=0)
    for u in units:
        u["qs"] = lax.dot_general(u["q_blk"], u["rhs"], nt_dims, preferred_element_type=f32)
    for u in units:
        u["scores"] = jnp.where(u["seen"], u["qs"][:, GLA_DV:GLA_DV + ck], 0.0).astype(bf16)
    for u in units:
        v = u["pg"][u["bi"], :, _PG_V:_PG_G]
        u["intra"] = [jnp.dot(u["scores"][hd * ck:(hd + 1) * ck], v[:, hd * GLA_DV:(hd + 1) * GLA_DV],
                              preferred_element_type=f32) for hd in range(nh)]
        v_stack = jnp.concatenate([v[:, hd * GLA_DV:(hd + 1) * GLA_DV] for hd in range(nh)], axis=0)
        u["kv_t"] = lax.dot_general(v_stack, u["k_blk"], tn_dims, preferred_element_type=f32)
    for u in units:
        u["outs"] = [u["intra"][hd] + u["qs"][hd * ck:(hd + 1) * ck, 0:GLA_DV] for hd in range(nh)]
        u["state"][u["bi"]] = u["s_t"] * jnp.exp(u["btot"]) + u["kv_t"]
    n_batch = streams[0][0].shape[0]
    return [[u["outs"] for u in units[si * n_batch:(si + 1) * n_batch]] for si in range(len(streams))]


def _gla_gate(o_heads, g, gn):
    normed = []
    for oh in o_heads:
        ms = jnp.mean(oh * oh, axis=-1, keepdims=True)
        normed.append(oh * lax.rsqrt(ms + RMS_EPS))
    return jnp.concatenate(normed, axis=-1) * gn * (g * jax.nn.sigmoid(g))


def _gla_kernel(*refs, merge):
    if merge:
        (pgf_ref, pgb_ref, waf_ref, baf_ref, wab_ref, bab_ref, s0f_ref, s0b_ref, gn_ref,
         ylo_ref, yhi_ref, st_f, st_b, keep_f, keep_b) = refs
    else:
        (pgf_ref, pgb_ref, waf_ref, baf_ref, wab_ref, bab_ref, s0f_ref, s0b_ref,
         sfin_f_ref, sfin_b_ref, st_f, st_b) = refs
    n = pl.program_id(0)
    n_chunks = pl.num_programs(0)

    @pl.when(n == 0)
    def _():
        st_f[...] = s0f_ref[...]
        st_b[...] = s0b_ref[...]

    outs_f, outs_b = _gla_chunks([(pgf_ref, waf_ref, baf_ref, st_f, False), (pgb_ref, wab_ref, bab_ref, st_b, True)])
    n_batch = pgf_ref.shape[0]
    heads = range(GLA_HEADS)
    if merge:
        half = n_chunks // 2
        m = n_chunks - 1 - n

        @pl.when(n < half)
        def _():
            for bi in range(n_batch):
                keep_f[n, bi] = jnp.concatenate(outs_f[bi], axis=-1).astype(keep_f.dtype)
                keep_b[m - half, bi] = jnp.concatenate(outs_b[bi], axis=-1).astype(keep_b.dtype)

        @pl.when(n >= half)
        def _():
            for bi in range(n_batch):
                kb = keep_b[n - half, bi].astype(f32)
                o_heads = [outs_f[bi][hd] + kb[:, hd * GLA_DV:(hd + 1) * GLA_DV] for hd in heads]
                yhi_ref[bi] = _gla_gate(o_heads, pgf_ref[bi, :, _PG_G:_PG_A].astype(f32), gn_ref[...])
                kf = keep_f[m, bi].astype(f32)
                o_heads = [kf[:, hd * GLA_DV:(hd + 1) * GLA_DV] + outs_b[bi][hd] for hd in heads]
                ylo_ref[bi] = _gla_gate(o_heads, pgb_ref[bi, :, _PG_G:_PG_A].astype(f32), gn_ref[...])
    else:
        @pl.when(n == n_chunks - 1)
        def _():
            sfin_f_ref[...] = st_f[...]
            sfin_b_ref[...] = st_b[...]


def _gla_pass(pg, wa, ba, s0, gn=None):
    n_batch, n_chunks = pg.shape[0], pg.shape[1]
    merge = gn is not None
    half = n_chunks // 2
    const2 = lambda n: (0, 0)
    const3 = lambda n: (0, 0, 0)
    chunk = lambda width, idx: pl.BlockSpec((n_batch, None, GLA_CHUNK, width), lambda n: (0, idx(n), 0, 0))
    w_specs = [pl.BlockSpec((A_LOW_PAD, D_GLA_K), const2), pl.BlockSpec((1, D_GLA_K), const2)]
    s_spec = pl.BlockSpec((n_batch, GLA_DV, D_GLA_K), const3)
    s_shape = jax.ShapeDtypeStruct((n_batch, GLA_DV, D_GLA_K), f32)
    in_specs = [chunk(D_PG, lambda n: n), chunk(D_PG, lambda n: n_chunks - 1 - n)] + w_specs + w_specs + [s_spec,
                                                                                                            s_spec]
    args = [pg, pg, wa[0], ba[0], wa[1], ba[1], s0[0], s0[1]]
    scratch = [pltpu.VMEM((n_batch, GLA_DV, D_GLA_K), f32)] * 2
    if merge:
        assert n_chunks % 2 == 0
        in_specs.append(pl.BlockSpec((1, D_GLA_V), const2))
        args.append(gn)
        out_specs = [chunk(D_GLA_V, lambda n: jnp.minimum(n_chunks - 1 - n, half - 1)),
                     chunk(D_GLA_V, lambda n: jnp.maximum(n - half, 0))]
        y_shape = jax.ShapeDtypeStruct((n_batch, half, GLA_CHUNK, D_GLA_V), f32)
        out_shape = [y_shape, y_shape]
        scratch += [pltpu.VMEM((half, n_batch, GLA_CHUNK, D_GLA_V), bf16)] * 2
    else:
        out_specs = [s_spec, s_spec]
        out_shape = [s_shape, s_shape]
    return pl.pallas_call(
        functools.partial(_gla_kernel, merge=merge),
        grid=(n_chunks,),
        in_specs=in_specs,
        out_specs=out_specs,
        out_shape=out_shape,
        scratch_shapes=scratch,
        compiler_params=_cparams(("arbitrary",)),
        name="gla_merge" if merge else "gla_scan",
    )(*args)


def _pack_rows(v):
    n = v.shape[1] // 2
    w = lax.bitcast_convert_type(v.astype(bf16).astype(f32), jnp.int32)
    return (w[:, :n] & jnp.int32(-65536)) | lax.shift_right_logical(w[:, n:], 16)


def _unpack_rows(w):
    hi = lax.bitcast_convert_type(w & jnp.int32(-65536), f32)
    lo = lax.bitcast_convert_type(lax.shift_left(w, 16), f32)
    return hi, lo


def _outproj_router_kernel(yl_ref, ygl_ref, ygr_ref, x_ref, mod_ref, wo_ref, lg_ref, lb_ref, wrh_ref, wrl_ref,
                           br_ref, x1_ref, hp_ref, ids_ref, rank_ref, wts_ref, cnt_ref, running):
    i = pl.program_id(0)

    @pl.when(i == 0)
    def _():
        running[...] = jnp.zeros_like(running)

    d = D_MODEL
    g1 = mod_ref[0, :, 2 * d:3 * d]
    sh2 = mod_ref[0, :, 3 * d:4 * d]
    sc2 = mod_ref[0, :, 4 * d:5 * d]
    n_parts = x_ref.shape[0] // OUTPROJ_PART
    tm = OUTPROJ_PART
    grid_rows = ygl_ref.shape[2] // n_parts
    tok = [slice(p * tm, (p + 1) * tm) for p in range(n_parts)]
    nt_dims = (((1,), (1,)), ((), ()))
    ne = wrh_ref.shape[0]
    expert = lax.broadcasted_iota(jnp.int32, (ne, tm), 0).astype(f32)
    neg_inf = jnp.float32(-jnp.inf)

    ygs = [jnp.concatenate([ref[0, :, r, :] for r in range(p * grid_rows, (p + 1) * grid_rows)
                            for ref in (ygl_ref, ygr_ref)], axis=0).astype(bf16) for p in range(n_parts)]
    ys = [jnp.dot(yl_ref[tok[p], :], wo_ref[0:D_LRU, :], preferred_element_type=f32)
          + jnp.dot(ygs[p], wo_ref[D_LRU:2 * D_LRU, :], preferred_element_type=f32) for p in range(n_parts)]
    hmods = []
    for p in range(n_parts):
        z = DEEPNORM_ALPHA * x_ref[tok[p], :] + g1 * ys[p]
        mu = jnp.mean(z, axis=-1, keepdims=True)
        zc = z - mu
        var = jnp.mean(zc * zc, axis=-1, keepdims=True)
        x1 = zc * lax.rsqrt(var + LN_EPS) * lg_ref[...] + lb_ref[...]
        x1_ref[tok[p], :] = x1
        hmod = x1 * (1.0 + sc2) + sh2
        hp_ref[tok[p], :] = _pack_rows(hmod)
        hmods.append(hmod)

    logits = []
    for hmod in hmods:
        h_hi = hmod.astype(bf16)
        h_lo = (hmod - h_hi.astype(f32)).astype(bf16)
        logits.append(lax.dot_general(wrh_ref[...], h_hi, nt_dims, preferred_element_type=f32)
                      + lax.dot_general(wrh_ref[...], h_lo, nt_dims, preferred_element_type=f32)
                      + lax.dot_general(wrl_ref[...], h_hi, nt_dims, preferred_element_type=f32) + br_ref[...])
    picks = []
    for live in logits:
        sel = jnp.zeros((ne, tm), f32)
        ids, vals = [], []
        for _ in range(TOP_K):
            m = jnp.max(live, axis=0, keepdims=True)
            j = jnp.min(jnp.where(live == m, expert, float(ne)), axis=0, keepdims=True)
            pick = expert == j
            sel = jnp.where(pick, 1.0, sel)
            live = jnp.where(pick, neg_inf, live)
            ids.append(j)
            vals.append(m)
        picks.append((sel, ids, vals))

    ri = lax.broadcasted_iota(jnp.int32, (tm, tm), 0)
    ci = lax.broadcasted_iota(jnp.int32, (tm, tm), 1)
    earlier = (ri < ci).astype(bf16)
    slot = lax.broadcasted_iota(jnp.int32, (8, tm), 0)
    for p, (sel, ids, vals) in enumerate(picks):
        rank_dense = running[...] + jnp.dot(sel.astype(bf16), earlier, preferred_element_type=f32)
        running[...] = running[...] + jnp.sum(sel, axis=1, keepdims=True)
        exps = [jnp.exp(vk - vals[0]) for vk in vals]
        denom = exps[0] + exps[1] + exps[2] + exps[3]
        ids_out = jnp.zeros((8, tm), f32)
        rank_out = jnp.zeros((8, tm), f32)
        wts_out = jnp.zeros((8, tm), f32)
        for kk in range(TOP_K):
            rk = jnp.sum(jnp.where(expert == ids[kk], rank_dense, 0.0), axis=0, keepdims=True)
            here = slot == kk
            ids_out = jnp.where(here, ids[kk], ids_out)
            rank_out = jnp.where(here, rk, rank_out)
            wts_out = jnp.where(here, exps[kk] / denom, wts_out)
        ids_ref[:, tok[p]] = ids_out.astype(jnp.int32)
        rank_ref[:, tok[p]] = rank_out.astype(jnp.int32)
        wts_ref[tok[p], :] = jnp.concatenate([wts_out, jnp.zeros((LANES - 8, tm), f32)], axis=0).T
    cnt_ref[...] = running[...]


def _outproj_router(yl, yg, xt, mod3, wo, lg, lb, wr_hi, wr_lo, br, tokens_per_batch):
    T = xt.shape[0]
    tm = TOKEN_TILE
    tiles_per_batch = tokens_per_batch // tm
    tokrow = lambda i: (i, 0)
    const = lambda i: (0, 0)
    half_cols = yg[0].shape[1]
    rows_per_tile = tm // (2 * half_cols)
    yg_spec = pl.BlockSpec((1, half_cols, rows_per_tile, D_GLA_V),
                           lambda i: (i // tiles_per_batch, 0, i % tiles_per_batch, 0))
    return pl.pallas_call(
        _outproj_router_kernel,
        grid=(T // tm,),
        in_specs=[pl.BlockSpec((tm, D_LRU), tokrow),
                  yg_spec, yg_spec,
                  pl.BlockSpec((tm, D_MODEL), tokrow),
                  pl.BlockSpec((1, 1, N_MOD * D_MODEL), lambda i: (i // tiles_per_batch, 0, 0)),
                  pl.BlockSpec((D_MODEL, D_MODEL), const),
                  pl.BlockSpec((1, D_MODEL), const),
                  pl.BlockSpec((1, D_MODEL), const),
                  pl.BlockSpec((N_EXPERTS, D_MODEL), const),
                  pl.BlockSpec((N_EXPERTS, D_MODEL), const),
                  pl.BlockSpec((N_EXPERTS, 1), const)],
        out_specs=[pl.BlockSpec((tm, D_MODEL), tokrow),
                   pl.BlockSpec((tm, D_MODEL // 2), tokrow),
                   pl.BlockSpec((8, tm), lambda i: (0, i)),
                   pl.BlockSpec((8, tm), lambda i: (0, i)),
                   pl.BlockSpec((tm, LANES), tokrow),
                   pl.BlockSpec((N_EXPERTS, 1), const)],
        out_shape=[jax.ShapeDtypeStruct((T, D_MODEL), f32),
                   jax.ShapeDtypeStruct((T, D_MODEL // 2), jnp.int32),
                   jax.ShapeDtypeStruct((8, T), jnp.int32),
                   jax.ShapeDtypeStruct((8, T), jnp.int32),
                   jax.ShapeDtypeStruct((T, LANES), f32),
                   jax.ShapeDtypeStruct((N_EXPERTS, 1), f32)],
        scratch_shapes=[pltpu.VMEM((N_EXPERTS, 1), f32)],
        compiler_params=_cparams(("arbitrary",)),
        name="outproj_router",
    )(yl, yg[0], yg[1], xt, mod3, wo, lg, lb, wr_hi, wr_lo, br)


def _route_kernel(ids_ref, rank_ref, cnt_ref, dest_ref, tiles_ref, *, bm):
    ne = cnt_ref.shape[0]
    cnt = cnt_ref[...]
    padded = jnp.floor((cnt + (bm - 1.0)) * (1.0 / bm)) * bm
    sub = lax.broadcasted_iota(jnp.int32, (ne, LANES), 0)
    lane = lax.broadcasted_iota(jnp.int32, (ne, LANES), 1)
    padded_row = jnp.sum(jnp.where(sub == lane, padded, 0.0), axis=0, keepdims=True)
    base = jnp.sum(jnp.where(lane < sub, padded_row, 0.0), axis=1, keepdims=True)
    ends = base + padded

    tc = ids_ref.shape[1]
    expert = lax.broadcasted_iota(jnp.int32, (ne, tc), 0)
    slot = lax.broadcasted_iota(jnp.int32, (8, tc), 0)
    ids = ids_ref[...]
    group_base = jnp.zeros((8, tc), f32)
    for kk in range(TOP_K):
        b_k = jnp.sum(jnp.where(expert == ids[kk:kk + 1], base, 0.0), axis=0, keepdims=True)
        group_base = jnp.where(slot == kk, b_k, group_base)
    dest_ref[...] = group_base.astype(jnp.int32) + rank_ref[...]

    nl = tiles_ref.shape[1]
    start = lax.broadcasted_iota(jnp.int32, (ne, nl), 1).astype(f32) * bm
    te = jnp.minimum(jnp.sum(jnp.where(start >= ends, 1.0, 0.0), axis=0, keepdims=True), ne - 1.0)
    at_te = lax.broadcasted_iota(jnp.int32, (ne, nl), 0).astype(f32) == te
    cnt_te = jnp.sum(jnp.where(at_te, cnt, 0.0), axis=0, keepdims=True)
    base_te = jnp.sum(jnp.where(at_te, base, 0.0), axis=0, keepdims=True)
    valid = jnp.clip(cnt_te - (start[0:1] - base_te), 0.0, float(bm))
    next_group = jnp.sum(jnp.where(at_te, ends, 0.0), axis=0, keepdims=True) * (1.0 / bm)
    last_used = jnp.sum(padded, axis=0, keepdims=True) * (1.0 / bm) - 1.0
    srow = lax.broadcasted_iota(jnp.int32, (8, nl), 0)
    table = jnp.where(srow == 0, te, jnp.where(srow == 1, valid, jnp.where(srow == 2, next_group,
                                                                           jnp.where(srow == 3, last_used, 0.0))))
    tiles_ref[...] = table.astype(jnp.int32)


def _route(ids, rank, cnt, bm, n_tiles):
    T = ids.shape[1]
    tc = 2048
    nl = -(-n_tiles // LANES) * LANES
    tok = lambda i: (0, i)
    const = lambda i: (0, 0)
    return pl.pallas_call(
        functools.partial(_route_kernel, bm=bm),
        grid=(T // tc,),
        in_specs=[pl.BlockSpec((8, tc), tok), pl.BlockSpec((8, tc), tok),
                  pl.BlockSpec((N_EXPERTS, 1), const)],
        out_specs=[pl.BlockSpec((8, tc), tok), pl.BlockSpec((8, nl), const)],
        out_shape=[jax.ShapeDtypeStruct((8, T), jnp.int32), jax.ShapeDtypeStruct((8, nl), jnp.int32)],
        compiler_params=_cparams(("arbitrary",)),
        name="route",
    )(ids, rank, cnt)


def _sc_workers():
    info = plsc.get_sparse_core_info()
    return info.num_cores, info.num_subcores


def _sc_dispatch(rows, dest_flat, n_out):
    T, D = rows.shape
    nc, ns = _sc_workers()
    per_w = T // (nc * ns)
    n_chunks = per_w // SC_CHUNK
    assert n_chunks % 2 == 0
    mesh = plsc.VectorSubcoreMesh(core_axis_name="c", subcore_axis_name="s")

    @functools.partial(
        pl.kernel, mesh=mesh,
        out_type=jax.ShapeDtypeStruct((n_out, D), rows.dtype),
        scratch_types=([pltpu.VMEM((SC_CHUNK,), jnp.int32)] * (2 * TOP_K)
                       + [pltpu.VMEM((SC_CHUNK, D), rows.dtype)] * 2
                       + [pltpu.SemaphoreType.DMA] * (2 * TOP_K)),
    )
    def k(rows_hbm, dest_hbm, out_hbm, *scratch):
        idx_v = (scratch[:TOP_K], scratch[TOP_K:2 * TOP_K])
        rows_v = scratch[2 * TOP_K:2 * TOP_K + 2]
        sems = (scratch[2 * TOP_K + 2:3 * TOP_K + 2], scratch[3 * TOP_K + 2:])
        wid = lax.axis_index("s") * nc + lax.axis_index("c")
        base = wid * per_w

        def load(chunk, b):
            off = base + chunk * SC_CHUNK
            pltpu.sync_copy(rows_hbm.at[pl.ds(off, SC_CHUNK)], rows_v[b])
            for kk in range(TOP_K):
                pltpu.sync_copy(dest_hbm.at[pl.ds(kk * T + off, SC_CHUNK)], idx_v[b][kk])

        load(0, 0)

        @pl.loop(0, n_chunks, step=2)
        def _(j):
            for b in range(2):
                copies = [pltpu.async_copy(rows_v[b], out_hbm.at[idx_v[b][kk]], sems[b][kk])
                          for kk in range(TOP_K)]

                @pl.when(j + b + 1 < n_chunks)
                def _():
                    load(j + b + 1, 1 - b)

                for cp in copies:
                    cp.wait()

    return k(rows, dest_flat)


def _sc_gather(table, idx):
    _, D = table.shape
    N = idx.shape[0]
    nc, ns = _sc_workers()
    per_w = N // (nc * ns)
    n_chunks = per_w // SC_CHUNK
    assert n_chunks % 2 == 0
    mesh = plsc.VectorSubcoreMesh(core_axis_name="c", subcore_axis_name="s")

    @functools.partial(
        pl.kernel, mesh=mesh,
        out_type=jax.ShapeDtypeStruct((N, D), table.dtype),
        scratch_types=([pltpu.VMEM((SC_CHUNK,), jnp.int32)] * 2
                       + [pltpu.VMEM((SC_CHUNK, D), table.dtype)] * 2
                       + [pltpu.SemaphoreType.DMA] * 2),
    )
    def k(table_hbm, idx_hbm, out_hbm, idx0, idx1, buf0, buf1, sem0, sem1):
        idxs, bufs, sems = (idx0, idx1), (buf0, buf1), (sem0, sem1)
        wid = lax.axis_index("s") * nc + lax.axis_index("c")
        base = wid * per_w

        def gather(b):
            return pltpu.make_async_copy(table_hbm.at[idxs[b]], bufs[b], sems[b])

        def start(chunk, b):
            pltpu.sync_copy(idx_hbm.at[pl.ds(base + chunk * SC_CHUNK, SC_CHUNK)], idxs[b])
            gather(b).start()

        def finish(chunk, b):
            gather(b).wait()
            pltpu.sync_copy(bufs[b], out_hbm.at[pl.ds(base + chunk * SC_CHUNK, SC_CHUNK)])

        start(0, 0)

        @pl.loop(0, n_chunks, step=2)
        def _(j):
            start(j + 1, 1)
            finish(j, 0)

            @pl.when(j + 2 < n_chunks)
            def _():
                start(j + 2, 0)

            finish(j + 1, 1)

    return k(table, idx)


def _ffn_kernel(te_ref, tv_ref, tn_ref, tl_ref, xs_ref, wg_hbm, bg_ref, wu_hbm, bu_ref, wd_hbm, bd_ref, eo_ref,
                w_buf, sems, slot_ref):
    i = pl.program_id(0)
    n_tiles = pl.num_programs(0)
    w_hbm = (wg_hbm, wu_hbm, wd_hbm)

    def fetch(e, slot):
        return [pltpu.make_async_copy(w_hbm[m].at[e], w_buf.at[slot, m], sems.at[slot, m]) for m in range(3)]

    @pl.when(i == 0)
    def _():
        slot_ref[0] = 0
        for cp in fetch(te_ref[0], 0):
            cp.start()

    prev = te_ref[jnp.maximum(i - 1, 0)]
    first = (i == 0) | (te_ref[i] != prev)

    @pl.when(first & (i > 0))
    def _():
        slot_ref[0] = 1 - slot_ref[0]

    slot = slot_ref[0]

    @pl.when(first)
    def _():
        for cp in fetch(te_ref[i], slot):
            cp.wait()
        nxt = tn_ref[i]
        e_nxt = te_ref[jnp.minimum(nxt, n_tiles - 1)]

        @pl.when((nxt > i) & (nxt < n_tiles) & (e_nxt != te_ref[i]))
        def _():
            for cp in fetch(e_nxt, 1 - slot):
                cp.start()

    wg_b, wu_b, wd_b = (w_buf.at[slot, m] for m in range(3))
    valid = tv_ref[i]

    def ffn_rows(r0, m):
        blocks = [(r0 + o, min(MOE_BLOCK, m)) for o in range(0, m, MOE_BLOCK)]
        xs, gates, ups, acts = [], [], [], []
        for b0, bm_ in blocks:
            row = lax.broadcasted_iota(jnp.int32, (bm_, 1), 0) + b0
            xw = jnp.where(row < valid, xs_ref[b0:b0 + bm_, :], 0)
            x_hi, x_lo = _unpack_rows(xw)
            xs.append(jnp.concatenate([x_hi, x_lo], axis=1).astype(bf16))
        for x in xs:
            gates.append(jnp.minimum(jnp.dot(x, wg_b[...], preferred_element_type=f32) + bg_ref[...],
                                     SWIGLU_LIMIT))
            ups.append(jnp.clip(jnp.dot(x, wu_b[...], preferred_element_type=f32) + bu_ref[...],
                                -SWIGLU_LIMIT, SWIGLU_LIMIT))
        for gate, up in zip(gates, ups):
            acts.append(((up + 1.0) * gate * jax.nn.sigmoid(SWIGLU_ALPHA * gate)).astype(bf16))
        for (b0, bm_), act in zip(blocks, acts):
            out = jnp.dot(act, wd_b[...], preferred_element_type=f32) + bd_ref[...]
            eo_ref[b0:b0 + bm_, :] = _pack_rows(out)

    def zero_rows(r0, m):
        eo_ref[r0:r0 + m, :] = jnp.zeros((m, eo_ref.shape[1]), eo_ref.dtype)

    for r0 in range(0, xs_ref.shape[0], MOE_PASS):
        lo = 0
        for m in MOE_PASS_SIZES:
            @pl.when((valid > r0 + lo) & ((valid <= r0 + m) | (m == MOE_PASS)))
            def _(r0=r0, m=m):
                ffn_rows(r0, m)
                if m < MOE_PASS:
                    zero_rows(r0 + m, MOE_PASS - m)
            lo = m

        @pl.when(valid <= r0)
        def _(r0=r0):
            zero_rows(r0, MOE_PASS)


def _expert_ffn(tile_expert, tile_valid, tile_next, last_used, xs, w_gate, b_gate, w_up, b_up, w_down, b_down):
    n_rows, dp = xs.shape
    d = 2 * dp
    bm = MOE_TILE
    d_e = w_gate.shape[-1]
    assert d == d_e
    bspec = lambda n_: pl.BlockSpec((None, 1, n_), lambda i, te, tv, tn, tl: (te[i], 0, 0))
    hbm = pl.BlockSpec(memory_space=pl.ANY)
    grid_spec = pltpu.PrefetchScalarGridSpec(
        num_scalar_prefetch=4,
        grid=(n_rows // bm,),
        in_specs=[pl.BlockSpec((bm, dp), lambda i, te, tv, tn, tl: (jnp.minimum(i, tl[0]), 0)),
                  hbm, bspec(d_e), hbm, bspec(d_e), hbm, bspec(d)],
        out_specs=pl.BlockSpec((bm, dp), lambda i, te, tv, tn, tl: (i, 0)),
        scratch_shapes=[pltpu.VMEM((2, 3, d, d_e), f32),
                        pltpu.SemaphoreType.DMA((2, 3)),
                        pltpu.SMEM((1,), jnp.int32)],
    )
    return pl.pallas_call(
        _ffn_kernel,
        grid_spec=grid_spec,
        out_shape=jax.ShapeDtypeStruct((n_rows, dp), jnp.int32),
        compiler_params=_cparams(("arbitrary",)),
        name="expert_ffn",
    )(tile_expert, tile_valid, tile_next, last_used, xs, w_gate, b_gate.reshape(N_EXPERTS, 1, d_e), w_up,
      b_up.reshape(N_EXPERTS, 1, d_e), w_down, b_down.reshape(N_EXPERTS, 1, d))


def _combine_kernel(eg_ref, wts_ref, x1_ref, mod_ref, lg_ref, lb_ref, o_ref):
    w = wts_ref[...]
    y_hi, y_lo = _unpack_rows(eg_ref[0])
    y_hi, y_lo = y_hi * w[:, 0:1], y_lo * w[:, 0:1]
    for kk in range(1, TOP_K):
        e_hi, e_lo = _unpack_rows(eg_ref[kk])
        y_hi = y_hi + e_hi * w[:, kk:kk + 1]
        y_lo = y_lo + e_lo * w[:, kk:kk + 1]
    y = jnp.concatenate([y_hi, y_lo], axis=1)
    z = DEEPNORM_ALPHA * x1_ref[...] + mod_ref[0] * y
    mu = jnp.mean(z, axis=-1, keepdims=True)
    zc = z - mu
    var = jnp.mean(zc * zc, axis=-1, keepdims=True)
    o_ref[...] = zc * lax.rsqrt(var + LN_EPS) * lg_ref[...] + lb_ref[...]


def _combine_ln(eg, wts, x1, mod3, lg, lb, tokens_per_batch):
    T = x1.shape[0]
    tm = TOKEN_TILE
    tiles_per_batch = tokens_per_batch // tm
    const = lambda i: (0, 0)
    return pl.pallas_call(
        _combine_kernel,
        grid=(T // tm,),
        in_specs=[pl.BlockSpec((TOP_K, tm, D_MODEL // 2), lambda i: (0, i, 0)),
                  pl.BlockSpec((tm, LANES), lambda i: (i, 0)),
                  pl.BlockSpec((tm, D_MODEL), lambda i: (i, 0)),
                  pl.BlockSpec((1, 1, D_MODEL), lambda i: (i // tiles_per_batch, 0, N_MOD - 1)),
                  pl.BlockSpec((1, D_MODEL), const),
                  pl.BlockSpec((1, D_MODEL), const)],
        out_specs=pl.BlockSpec((tm, D_MODEL), lambda i: (i, 0)),
        out_shape=jax.ShapeDtypeStruct((T, D_MODEL), f32),
        compiler_params=_cparams(("arbitrary",)),
        name="combine_ln",
    )(eg, wts, x1, mod3, lg, lb)


def _block_diag(w):
    n, c, d = w.shape
    eye = jnp.eye(n, dtype=w.dtype)
    return jnp.einsum('ncd,nm->ncmd', w, eye).reshape(n * c, n * d)


def kernel(x, c, ctx, c_ctx, w_ada, b_ada, w_in, conv_w, conv_b, lru_wa, lru_ba, lru_wx, lru_bx,
           lru_lam, gla_wa, gla_ba, gla_norm_g, w_out, ln1_g, ln1_b, w_router, b_router, w_gate,
           b_gate, w_up, b_up, w_down, b_down, ln2_g, ln2_b):
    B, L, D = x.shape
    Lc = ctx.shape[1]
    T = B * L
    rows = L // GRID_W
    l = 0

    cpad = jnp.zeros((8, D), f32).at[0:B].set(c).at[B].set(c_ctx)
    w_cat = jnp.pad(w_in[l], ((0, 0), (0, D_PROJ - w_in.shape[-1]))).astype(bf16)
    wg = [jnp.concatenate([_block_diag(lru_wa[l, d]), _block_diag(lru_wx[l, d])], axis=1).astype(bf16)
          for d in range(2)]
    bg = [jnp.concatenate([lru_ba[l, d], lru_bx[l, d]])[None] for d in range(2)]
    lam = [lru_lam[l, d][None] for d in range(2)]
    wa = [jnp.pad(gla_wa[l, d], ((0, A_LOW_PAD - GATE_RANK), (0, 0))).astype(bf16) for d in range(2)]
    ba = [gla_ba[l, d][None] for d in range(2)]
    cw, cb = conv_w[l], conv_b[l][None]
    wr_t = w_router[l].T
    wr_hi = wr_t.astype(bf16)
    wr_lo = (wr_t - wr_hi.astype(f32)).astype(bf16)
    br = b_router[l][:, None]
    c_xl, c_gl, c_pg = (0, D_LRU), (D_LRU, 2 * D_LRU), (2 * D_LRU, D_PROJ)

    mod3 = _ada_mod(cpad, w_ada[l], b_ada[l]).reshape(8, 1, N_MOD * D)

    tiles_per_batch = L // INPROJ_TILE
    xl_c, pg_c = _inproj_rows(ctx.reshape(B * Lc, D), mod3, w_cat, lambda i: B, (c_xl, c_pg), (f32, bf16))
    zero_h = jnp.zeros((B, 1, D_LRU), f32)
    zero_s = jnp.zeros((B, GLA_DV, D_GLA_K), f32)
    pg_c = pg_c.reshape(B, Lc // GLA_CHUNK, GLA_CHUNK, D_PG)
    h_ctx = []
    for d in range(2):
        _, _, hf = _lru_pass(xl_c, wg[d], bg[d], lam[d], zero_h, n_batch=B, tile=Lc, reverse=bool(d),
                             conv=(cw, cb, Lc))
        h_ctx.append(hf)
    s_ctx = _gla_pass(pg_c, wa, ba, (zero_s, zero_s))

    xt = x.reshape(T, D)
    xl, gl = _inproj_rows(xt, mod3, w_cat, lambda i: i // tiles_per_batch, (c_xl, c_gl), (f32, bf16))
    hb, u_lru, _ = _lru_pass(xl, wg[1], bg[1], lam[1], h_ctx[1], n_batch=B, tile=LRU_TILE, reverse=True,
                             conv=(cw, cb, GRID_W))
    y_lru, _ = _lru_pass(u_lru, wg[0], bg[0], lam[0], h_ctx[0], n_batch=B, tile=LRU_TILE, reverse=False,
                         merge_with=(hb, gl))
    pg = _inproj_cols(x.reshape(B, rows, GRID_W, D), mod3, w_cat, c_pg)
    y_gla = _gla_pass(pg, wa, ba, s_ctx, gn=gla_norm_g[l][None])

    x1, hp, ids, rank, wts, cnt = _outproj_router(
        y_lru, y_gla, xt, mod3, w_out[l].astype(bf16), ln1_g[l][None], ln1_b[l][None], wr_hi, wr_lo, br, L)

    n_tiles = T * TOP_K // MOE_TILE + N_EXPERTS
    dest, tiles = _route(ids, rank, cnt, MOE_TILE, n_tiles)
    xs = _sc_dispatch(hp, dest.reshape(-1), n_tiles * MOE_TILE)
    eo = _expert_ffn(tiles[0, :n_tiles], tiles[1, :n_tiles], tiles[2, :n_tiles], tiles[3, :1], xs, w_gate[l],
                     b_gate[l], w_up[l], b_up[l], w_down[l], b_down[l])
    eg = _sc_gather(eo, dest[:TOP_K].reshape(-1)).reshape(TOP_K, T, D // 2)
    out = _combine_ln(eg, wts, x1, mod3, ln2_g[l][None], ln2_b[l][None], L)
    return out.reshape(B, L, D)
```

```python
import functools

import jax
import jax.numpy as jnp
from jax import lax
from jax.experimental import pallas as pl
from jax.experimental.pallas import tpu as pltpu
from jax.experimental.pallas import tpu_sc as plsc

D_MODEL = 1024
DEPTH = 1
GRID_W = 64
D_LRU = 512
CONV_W = 4
LRU_C = 8.0
GLA_HEADS = 4
D_GLA_V = 512
D_GLA_K = 256
GLA_DK = 64
GLA_DV = 128
GATE_RANK = 16
GATE_TAU = 16.0
GLA_CHUNK = 64
N_EXPERTS = 32
TOP_K = 4
SWIGLU_LIMIT = 7.0
SWIGLU_ALPHA = 1.702
N_MOD = 6
DEEPNORM_ALPHA = (2.0 * DEPTH) ** 0.25
LN_EPS = 1e-5
RMS_EPS = 1e-6

LANES = 128
A_LOW_PAD = LANES
D_PG = 2 * D_GLA_K + 2 * D_GLA_V + A_LOW_PAD
D_PROJ = 2 * D_LRU + D_PG
TOKEN_TILE = 1024
INPROJ_TILE = 1024
INPROJ_COLS = 2
LRU_TILE = 512
MOE_TILE = 1024
MOE_PASS = 512
MOE_PASS_SIZES = (128, 256, MOE_PASS)
MOE_BLOCK = 256
OUTPROJ_PART = 128
SC_CHUNK = 64
VMEM_LIMIT = 48 * 1024 * 1024

f32 = jnp.float32
bf16 = jnp.bfloat16


def _cparams(sem):
    return pltpu.CompilerParams(dimension_semantics=sem, vmem_limit_bytes=VMEM_LIMIT)


def _ada_kernel(c_ref, w_ref, b_ref, o_ref):
    s = c_ref[...]
    s = s * jax.nn.sigmoid(s)
    o_ref[...] = jnp.dot(s.astype(bf16), w_ref[...], preferred_element_type=f32) + b_ref[...]


def _ada_mod(cpad, w, b):
    n = w.shape[1]
    tn = 1024
    return pl.pallas_call(
        _ada_kernel,
        grid=(n // tn,),
        in_specs=[pl.BlockSpec((8, D_MODEL), lambda j: (0, 0)),
                  pl.BlockSpec((D_MODEL, tn), lambda j: (0, j)),
                  pl.BlockSpec((1, tn), lambda j: (0, j))],
        out_specs=pl.BlockSpec((8, tn), lambda j: (0, j)),
        out_shape=jax.ShapeDtypeStruct((8, n), f32),
        compiler_params=_cparams(("arbitrary",)),
        name="ada_mod",
    )(cpad, w, b.reshape(1, n))


def _inproj_kernel(x_ref, mod_ref, w_ref, *out_refs, parts, col_major):
    sh = mod_ref[0, :, 0:D_MODEL]
    sc = mod_ref[0, :, D_MODEL:2 * D_MODEL]
    if col_major:
        (lo, hi), o_ref = parts[0], out_refs[0]
        per = INPROJ_COLS
        groups = range(0, x_ref.shape[2], per)
        us = [(jnp.concatenate([x_ref[0, :, j, :] for j in range(c0, c0 + per)], axis=0) * (1.0 + sc)
               + sh).astype(bf16) for c0 in groups]
        for c0, u in zip(groups, us):
            p = jnp.dot(u, w_ref[:, lo:hi], preferred_element_type=f32).astype(o_ref.dtype)
            o_ref[0, c0:c0 + per] = p.reshape((per,) + o_ref.shape[2:])
        return
    u = (x_ref[...] * (1.0 + sc) + sh).astype(bf16)
    for (lo, hi), o_ref in zip(parts, out_refs):
        o_ref[...] = jnp.dot(u, w_ref[:, lo:hi], preferred_element_type=f32).astype(o_ref.dtype)


def _inproj_rows(xt, mod3, w_cat, batch_of_tile, parts, dtypes):
    T = xt.shape[0]
    tm = INPROJ_TILE
    return pl.pallas_call(
        functools.partial(_inproj_kernel, parts=parts, col_major=False),
        grid=(T // tm,),
        in_specs=[pl.BlockSpec((tm, D_MODEL), lambda i: (i, 0)),
                  pl.BlockSpec((1, 1, 2 * D_MODEL), lambda i: (batch_of_tile(i), 0, 0)),
                  pl.BlockSpec((D_MODEL, D_PROJ), lambda i: (0, 0))],
        out_specs=[pl.BlockSpec((tm, hi - lo), lambda i: (i, 0)) for lo, hi in parts],
        out_shape=[jax.ShapeDtypeStruct((T, hi - lo), dt) for (lo, hi), dt in zip(parts, dtypes)],
        compiler_params=_cparams(("arbitrary",)),
        name="inproj_rows",
    )(xt, mod3, w_cat)


def _inproj_cols(x4, mod3, w_cat, part):
    n_batch, rows, cols, _ = x4.shape
    lo, hi = part
    cb = INPROJ_TILE // rows
    return pl.pallas_call(
        functools.partial(_inproj_kernel, parts=(part,), col_major=True),
        grid=(n_batch, cols // cb),
        in_specs=[pl.BlockSpec((1, rows, cb, D_MODEL), lambda b, n: (b, 0, n, 0)),
                  pl.BlockSpec((1, 1, 2 * D_MODEL), lambda b, n: (b, 0, 0)),
                  pl.BlockSpec((D_MODEL, D_PROJ), lambda b, n: (0, 0))],
        out_specs=[pl.BlockSpec((1, cb, rows, hi - lo), lambda b, n: (b, n, 0, 0))],
        out_shape=[jax.ShapeDtypeStruct((n_batch, cols, rows, hi - lo), bf16)],
        compiler_params=_cparams(("arbitrary", "arbitrary")),
        name="inproj_cols",
    )(x4, mod3, w_cat)[0]


def _gelu_tanh(x):
    return 0.5 * x * (1.0 + jnp.tanh(0.7978845608028654 * (x + 0.044715 * (x * x * x))))


def _lru_kernel(*refs, row_w, reverse, merge):
    if merge:
        u_ref, wg_ref, bg_ref, lam_ref, h0_ref, hb_ref, gl_ref, out_ref, hfin_ref, carry, h_nat = refs
    else:
        xl_ref, cw_ref, cb_ref, wg_ref, bg_ref, lam_ref, h0_ref, out_ref, u_out_ref, hfin_ref, carry = refs
    t = pl.program_id(1)

    @pl.when(t == 0)
    def _():
        carry[...] = h0_ref[0]

    n_seg = 8
    blk = lambda v, j: v[j * n_seg:(j + 1) * n_seg]
    if merge:
        u = u_ref[...]
        tt, ch = u.shape
        seg = tt // n_seg
    else:
        _, seg, ch = xl_ref.shape
        tt = n_seg * seg
        x = jnp.concatenate([xl_ref[:, j, :] for j in range(seg)], axis=0)
        segs_per_row = row_w // seg
        s_idx = lax.broadcasted_iota(jnp.int32, (n_seg, ch), 0)
        has_prev = (s_idx % segs_per_row) != 0
        has_next = (s_idx % segs_per_row) != segs_per_row - 1
        from_prev = lambda v: jnp.where(has_prev, pltpu.roll(v, 1, 0), 0.0)
        from_next = lambda v: jnp.where(has_next, pltpu.roll(v, n_seg - 1, 0), 0.0)
        x_m1 = jnp.concatenate([from_prev(blk(x, seg - 1)), x[:tt - n_seg]], axis=0)
        x_m2 = jnp.concatenate([from_prev(blk(x, seg - 2)), from_prev(blk(x, seg - 1)), x[:tt - 2 * n_seg]],
                               axis=0)
        x_p1 = jnp.concatenate([x[n_seg:], from_next(blk(x, 0))], axis=0)
        cw = cw_ref[...]
        u = cb_ref[...] + x_m2 * cw[0:1] + x_m1 * cw[1:2] + x * cw[2:3] + x_p1 * cw[3:4]
        u_out_ref[...] = u
    rows_q = tt // 4
    g_parts = [jnp.dot(u[q * rows_q:(q + 1) * rows_q].astype(bf16), wg_ref[...], preferred_element_type=f32)
               + bg_ref[...] for q in range(4)]
    lam = lam_ref[...]
    softplus_neg = jnp.maximum(-lam, 0.0) + jnp.log(1.0 + jnp.exp(-jnp.abs(lam)))
    log_a_scale = -LRU_C * softplus_neg

    def recurrence_terms(j):
        q, r = divmod(j * n_seg, rows_q)
        g_j, u_j = g_parts[q][r:r + n_seg], blk(u, j)
        log_a = log_a_scale * jax.nn.sigmoid(g_j[:, 0:ch])
        a_j = jnp.exp(log_a)
        series = log_a * (-2.0 + log_a * (-2.0 + log_a * (-4.0 / 3.0)))
        one_minus_a2 = jnp.where(log_a > -0.005, series, 1.0 - a_j * a_j)
        root = jnp.where(one_minus_a2 > 0.0, one_minus_a2 * lax.rsqrt(one_minus_a2), 0.0)
        return a_j, root * (jax.nn.sigmoid(g_j[:, ch:2 * ch]) * u_j)

    half = seg // 2
    h_loc, a_cum = [None] * seg, [None] * seg
    h_run, a_run = [None, None], [None, None]
    for i in range(half):
        for p in range(2):
            j = p * half + (half - 1 - i if reverse else i)
            a_j, b_j = recurrence_terms(j)
            h_run[p] = b_j if i == 0 else a_j * h_run[p] + b_j
            a_run[p] = a_j if i == 0 else a_j * a_run[p]
            h_loc[j], a_cum[j] = h_run[p], a_run[p]
    pieces = [(s, p) for s in range(n_seg) for p in range(2)]
    if reverse:
        pieces.reverse()
    entering = [[None] * n_seg, [None] * n_seg]
    state = carry[...]
    for s, p in pieces:
        entering[p][s] = state
        state = h_run[p][s:s + 1] + a_run[p][s:s + 1] * state
    carry[...] = state
    hfin_ref[0] = state
    enter = [jnp.concatenate(e, axis=0) for e in entering]
    h_rows = [h_loc[j] + a_cum[j] * enter[j // half] for j in range(seg)]
    if merge:
        for j in range(seg):
            h_nat[:, j, :] = h_rows[j] + blk(hb_ref, j)
        h = h_nat[...].reshape(tt, ch)
        out_ref[...] = (h * _gelu_tanh(gl_ref[...].astype(f32))).astype(out_ref.dtype)
    else:
        out_ref[...] = jnp.concatenate(h_rows, axis=0)


def _lru_pass(x_in, wg, bg, lam, h0, *, n_batch, tile, reverse, conv=None, merge_with=None):
    T = x_in.shape[0]
    nt = T // n_batch // tile
    merge = merge_with is not None
    assert merge != (conv is not None)
    seg = tile // 8

    def tok(b, t):
        return (b * nt + (nt - 1 - t if reverse else t), 0)

    tile_spec = pl.BlockSpec((tile, D_LRU), tok)
    const = lambda b, t: (0, 0)
    gate_specs = [pl.BlockSpec((D_LRU, 2 * D_LRU), const),
                  pl.BlockSpec((1, 2 * D_LRU), const),
                  pl.BlockSpec((1, D_LRU), const),
                  pl.BlockSpec((1, 1, D_LRU), lambda b, t: (b, 0, 0))]
    state_spec = pl.BlockSpec((1, 1, D_LRU), lambda b, t: (b, 0, 0))
    state_shape = jax.ShapeDtypeStruct((n_batch, 1, D_LRU), f32)
    scratch = [pltpu.VMEM((1, D_LRU), f32)]
    if merge:
        hb, gl = merge_with
        row_w = None
        in_specs = [tile_spec] + gate_specs + [tile_spec, tile_spec]
        args = [x_in, wg, bg, lam, h0, hb, gl]
        scratch += [pltpu.VMEM((8, seg, D_LRU), f32)]
        out_specs = [tile_spec, state_spec]
        out_shape = [jax.ShapeDtypeStruct((T, D_LRU), bf16), state_shape]
    else:
        conv_w, conv_b, row_w = conv
        assert row_w % seg == 0 and seg >= 2
        in_specs = [pl.BlockSpec((8, seg, D_LRU), lambda b, t: tok(b, t) + (0,)),
                    pl.BlockSpec((CONV_W, D_LRU), const),
                    pl.BlockSpec((1, D_LRU), const)] + gate_specs
        args = [x_in.reshape(T // seg, seg, D_LRU), conv_w, conv_b, wg, bg, lam, h0]
        out_specs = [tile_spec, tile_spec, state_spec]
        out_shape = [jax.ShapeDtypeStruct((T, D_LRU), f32), jax.ShapeDtypeStruct((T, D_LRU), f32), state_shape]
    return pl.pallas_call(
        functools.partial(_lru_kernel, row_w=row_w, reverse=reverse, merge=merge),
        grid=(n_batch, nt),
        in_specs=in_specs,
        out_specs=out_specs,
        out_shape=out_shape,
        scratch_shapes=scratch,
        compiler_params=_cparams(("arbitrary", "arbitrary")),
        name="lru_merge" if merge else "lru_scan",
    )(*args)


_PG_Q, _PG_K, _PG_V = 0, D_GLA_K, 2 * D_GLA_K
_PG_G, _PG_A = 2 * D_GLA_K + D_GLA_V, 2 * D_GLA_K + 2 * D_GLA_V


def _gla_chunks(streams):
    ck = GLA_CHUNK
    nh = GLA_HEADS
    head_of_lane = lax.broadcasted_iota(jnp.int32, (1, D_GLA_K), 1) // GLA_DK
    nt_dims = (((1,), (1,)), ((), ()))
    tn_dims = (((0,), (0,)), ((), ()))
    ii = lax.broadcasted_iota(jnp.int32, (nh * ck, ck), 0) & (ck - 1)
    jj = lax.broadcasted_iota(jnp.int32, (nh * ck, ck), 1)

    units = []
    for pg_ref, wa_ref, ba_ref, state, reverse in streams:
        n_batch = pg_ref.shape[0]
        rows_all = n_batch * ck
        a_low = jnp.concatenate([pg_ref[bi, :, _PG_A:_PG_A + A_LOW_PAD] for bi in range(n_batch)], axis=0)
        z = jnp.dot(a_low, wa_ref[...], preferred_element_type=f32) + ba_ref[...]
        bcum_all = (jnp.minimum(z, 0.0) - jnp.log(1.0 + jnp.exp(-jnp.abs(z)))) * (1.0 / GATE_TAU)
        step = lax.broadcasted_iota(jnp.int32, (rows_all, D_GLA_K), 0) & (ck - 1)
        d = 1
        while d < ck:
            if reverse:
                bcum_all = bcum_all + jnp.where(step < ck - d, pltpu.roll(bcum_all, rows_all - d, 0), 0.0)
            else:
                bcum_all = bcum_all + jnp.where(step >= d, pltpu.roll(bcum_all, d, 0), 0.0)
            d *= 2
        seen = (jj >= ii) if reverse else (jj <= ii)
        for bi in range(n_batch):
            units.append(dict(pg=pg_ref, bi=bi, state=state, reverse=reverse, seen=seen,
                              bcum=bcum_all[bi * ck:(bi + 1) * ck]))

    for u in units:
        pg_ref, bi, bcum = u["pg"], u["bi"], u["bcum"]
        q = pg_ref[bi, :, _PG_Q:_PG_K].astype(f32) * (GLA_DK ** -0.5)
        k = pg_ref[bi, :, _PG_K:_PG_V].astype(f32)
        u["btot"] = bcum[0:1] if u["reverse"] else bcum[ck - 1:ck]
        q_dec = q * jnp.exp(bcum)
        k_dec = (k * jnp.exp(-bcum)).astype(bf16)
        k_end = k * jnp.exp(u["btot"] - bcum)
        u["s_t"] = u["state"][bi]
        by_head = lambda t: jnp.concatenate([jnp.where(head_of_lane == hd, t, 0.0) for hd in range(nh)],
                                            axis=0).astype(bf16)
        u["q_blk"] = by_head(q_dec)
        u["k_blk"] = by_head(k_end)
        u["rhs"] = jnp.concatenate([u["s_t"].astype(bf16), k_dec], axis=0)
    for u in units:
        u["qs"] = lax.dot_general(u["q_blk"], u["rhs"], nt_dims, preferred_element_type=f32)
    for u in units:
        u["scores"] = jnp.where(u["seen"], u["qs"][:, GLA_DV:GLA_DV + ck], 0.0).astype(bf16)
    for u in units:
        v = u["pg"][u["bi"], :, _PG_V:_PG_G]
        u["intra"] = [jnp.dot(u["scores"][hd * ck:(hd + 1) * ck], v[:, hd * GLA_DV:(hd + 1) * GLA_DV],
                              preferred_element_type=f32) for hd in range(nh)]
        v_stack = jnp.concatenate([v[:, hd * GLA_DV:(hd + 1) * GLA_DV] for hd in range(nh)], axis=0)
        u["kv_t"] = lax.dot_general(v_stack, u["k_blk"], tn_dims, preferred_element_type=f32)
    for u in units:
        u["outs"] = [u["intra"][hd] + u["qs"][hd * ck:(hd + 1) * ck, 0:GLA_DV] for hd in range(nh)]
        u["state"][u["bi"]] = u["s_t"] * jnp.exp(u["btot"]) + u["kv_t"]
    n_batch = streams[0][0].shape[0]
    return [[u["outs"] for u in units[si * n_batch:(si + 1) * n_batch]] for si in range(len(streams))]


def _gla_gate(o_heads, g, gn):
    normed = []
    for oh in o_heads:
        ms = jnp.mean(oh * oh, axis=-1, keepdims=True)
        normed.append(oh * lax.rsqrt(ms + RMS_EPS))
    return jnp.concatenate(normed, axis=-1) * gn * (g * jax.nn.sigmoid(g))


def _gla_kernel(*refs, merge):
    if merge:
        (pgf_ref, pgb_ref, waf_ref, baf_ref, wab_ref, bab_ref, s0f_ref, s0b_ref, gn_ref,
         ylo_ref, yhi_ref, st_f, st_b, keep_f, keep_b) = refs
    else:
        (pgf_ref, pgb_ref, waf_ref, baf_ref, wab_ref, bab_ref, s0f_ref, s0b_ref,
         sfin_f_ref, sfin_b_ref, st_f, st_b) = refs
    n = pl.program_id(0)
    n_chunks = pl.num_programs(0)

    @pl.when(n == 0)
    def _():
        st_f[...] = s0f_ref[...]
        st_b[...] = s0b_ref[...]

    outs_f, outs_b = _gla_chunks([(pgf_ref, waf_ref, baf_ref, st_f, False), (pgb_ref, wab_ref, bab_ref, st_b, True)])
    n_batch = pgf_ref.shape[0]
    heads = range(GLA_HEADS)
    if merge:
        half = n_chunks // 2
        m = n_chunks - 1 - n

        @pl.when(n < half)
        def _():
            for bi in range(n_batch):
                keep_f[n, bi] = jnp.concatenate(outs_f[bi], axis=-1).astype(keep_f.dtype)
                keep_b[m - half, bi] = jnp.concatenate(outs_b[bi], axis=-1).astype(keep_b.dtype)

        @pl.when(n >= half)
        def _():
            for bi in range(n_batch):
                kb = keep_b[n - half, bi].astype(f32)
                o_heads = [outs_f[bi][hd] + kb[:, hd * GLA_DV:(hd + 1) * GLA_DV] for hd in heads]
                yhi_ref[bi] = _gla_gate(o_heads, pgf_ref[bi, :, _PG_G:_PG_A].astype(f32), gn_ref[...])
                kf = keep_f[m, bi].astype(f32)
                o_heads = [kf[:, hd * GLA_DV:(hd + 1) * GLA_DV] + outs_b[bi][hd] for hd in heads]
                ylo_ref[bi] = _gla_gate(o_heads, pgb_ref[bi, :, _PG_G:_PG_A].astype(f32), gn_ref[...])
    else:
        @pl.when(n == n_chunks - 1)
        def _():
            sfin_f_ref[...] = st_f[...]
            sfin_b_ref[...] = st_b[...]


def _gla_pass(pg, wa, ba, s0, gn=None):
    n_batch, n_chunks = pg.shape[0], pg.shape[1]
    merge = gn is not None
    half = n_chunks // 2
    const2 = lambda n: (0, 0)
    const3 = lambda n: (0, 0, 0)
    chunk = lambda width, idx: pl.BlockSpec((n_batch, None, GLA_CHUNK, width), lambda n: (0, idx(n), 0, 0))
    w_specs = [pl.BlockSpec((A_LOW_PAD, D_GLA_K), const2), pl.BlockSpec((1, D_GLA_K), const2)]
    s_spec = pl.BlockSpec((n_batch, GLA_DV, D_GLA_K), const3)
    s_shape = jax.ShapeDtypeStruct((n_batch, GLA_DV, D_GLA_K), f32)
    in_specs = [chunk(D_PG, lambda n: n), chunk(D_PG, lambda n: n_chunks - 1 - n)] + w_specs + w_specs + [s_spec,
                                                                                                            s_spec]
    args = [pg, pg, wa[0], ba[0], wa[1], ba[1], s0[0], s0[1]]
    scratch = [pltpu.VMEM((n_batch, GLA_DV, D_GLA_K), f32)] * 2
    if merge:
        assert n_chunks % 2 == 0
        in_specs.append(pl.BlockSpec((1, D_GLA_V), const2))
        args.append(gn)
        out_specs = [chunk(D_GLA_V, lambda n: jnp.minimum(n_chunks - 1 - n, half - 1)),
                     chunk(D_GLA_V, lambda n: jnp.maximum(n - half, 0))]
        y_shape = jax.ShapeDtypeStruct((n_batch, half, GLA_CHUNK, D_GLA_V), f32)
        out_shape = [y_shape, y_shape]
        scratch += [pltpu.VMEM((half, n_batch, GLA_CHUNK, D_GLA_V), bf16)] * 2
    else:
        out_specs = [s_spec, s_spec]
        out_shape = [s_shape, s_shape]
    return pl.pallas_call(
        functools.partial(_gla_kernel, merge=merge),
        grid=(n_chunks,),
        in_specs=in_specs,
        out_specs=out_specs,
        out_shape=out_shape,
        scratch_shapes=scratch,
        compiler_params=_cparams(("arbitrary",)),
        name="gla_merge" if merge else "gla_scan",
    )(*args)


def _pack_rows(v):
    n = v.shape[1] // 2
    w = lax.bitcast_convert_type(v.astype(bf16).astype(f32), jnp.int32)
    return (w[:, :n] & jnp.int32(-65536)) | lax.shift_right_logical(w[:, n:], 16)


def _unpack_rows(w):
    hi = lax.bitcast_convert_type(w & jnp.int32(-65536), f32)
    lo = lax.bitcast_convert_type(lax.shift_left(w, 16), f32)
    return hi, lo


def _outproj_router_kernel(yl_ref, ygl_ref, ygr_ref, x_ref, mod_ref, wo_ref, lg_ref, lb_ref, wrh_ref, wrl_ref,
                           br_ref, x1_ref, hp_ref, ids_ref, rank_ref, wts_ref, cnt_ref, running):
    i = pl.program_id(0)

    @pl.when(i == 0)
    def _():
        running[...] = jnp.zeros_like(running)

    d = D_MODEL
    g1 = mod_ref[0, :, 2 * d:3 * d]
    sh2 = mod_ref[0, :, 3 * d:4 * d]
    sc2 = mod_ref[0, :, 4 * d:5 * d]
    n_parts = x_ref.shape[0] // OUTPROJ_PART
    tm = OUTPROJ_PART
    grid_rows = ygl_ref.shape[2] // n_parts
    tok = [slice(p * tm, (p + 1) * tm) for p in range(n_parts)]
    nt_dims = (((1,), (1,)), ((), ()))
    ne = wrh_ref.shape[0]
    expert = lax.broadcasted_iota(jnp.int32, (ne, tm), 0).astype(f32)
    neg_inf = jnp.float32(-jnp.inf)

    ygs = [jnp.concatenate([ref[0, :, r, :] for r in range(p * grid_rows, (p + 1) * grid_rows)
                            for ref in (ygl_ref, ygr_ref)], axis=0).astype(bf16) for p in range(n_parts)]
    ys = [jnp.dot(yl_ref[tok[p], :], wo_ref[0:D_LRU, :], preferred_element_type=f32)
          + jnp.dot(ygs[p], wo_ref[D_LRU:2 * D_LRU, :], preferred_element_type=f32) for p in range(n_parts)]
    hmods = []
    for p in range(n_parts):
        z = DEEPNORM_ALPHA * x_ref[tok[p], :] + g1 * ys[p]
        mu = jnp.mean(z, axis=-1, keepdims=True)
        zc = z - mu
        var = jnp.mean(zc * zc, axis=-1, keepdims=True)
        x1 = zc * lax.rsqrt(var + LN_EPS) * lg_ref[...] + lb_ref[...]
        x1_ref[tok[p], :] = x1
        hmod = x1 * (1.0 + sc2) + sh2
        hp_ref[tok[p], :] = _pack_rows(hmod)
        hmods.append(hmod)

    logits = []
    for hmod in hmods:
        h_hi = hmod.astype(bf16)
        h_lo = (hmod - h_hi.astype(f32)).astype(bf16)
        logits.append(lax.dot_general(wrh_ref[...], h_hi, nt_dims, preferred_element_type=f32)
                      + lax.dot_general(wrh_ref[...], h_lo, nt_dims, preferred_element_type=f32)
                      + lax.dot_general(wrl_ref[...], h_hi, nt_dims, preferred_element_type=f32) + br_ref[...])
    picks = []
    for live in logits:
        sel = jnp.zeros((ne, tm), f32)
        ids, vals = [], []
        for _ in range(TOP_K):
            m = jnp.max(live, axis=0, keepdims=True)
            j = jnp.min(jnp.where(live == m, expert, float(ne)), axis=0, keepdims=True)
            pick = expert == j
            sel = jnp.where(pick, 1.0, sel)
            live = jnp.where(pick, neg_inf, live)
            ids.append(j)
            vals.append(m)
        picks.append((sel, ids, vals))

    ri = lax.broadcasted_iota(jnp.int32, (tm, tm), 0)
    ci = lax.broadcasted_iota(jnp.int32, (tm, tm), 1)
    earlier = (ri < ci).astype(bf16)
    slot = lax.broadcasted_iota(jnp.int32, (8, tm), 0)
    for p, (sel, ids, vals) in enumerate(picks):
        rank_dense = running[...] + jnp.dot(sel.astype(bf16), earlier, preferred_element_type=f32)
        running[...] = running[...] + jnp.sum(sel, axis=1, keepdims=True)
        exps = [jnp.exp(vk - vals[0]) for vk in vals]
        denom = exps[0] + exps[1] + exps[2] + exps[3]
        ids_out = jnp.zeros((8, tm), f32)
        rank_out = jnp.zeros((8, tm), f32)
        wts_out = jnp.zeros((8, tm), f32)
        for kk in range(TOP_K):
            rk = jnp.sum(jnp.where(expert == ids[kk], rank_dense, 0.0), axis=0, keepdims=True)
            here = slot == kk
            ids_out = jnp.where(here, ids[kk], ids_out)
            rank_out = jnp.where(here, rk, rank_out)
            wts_out = jnp.where(here, exps[kk] / denom, wts_out)
        ids_ref[:, tok[p]] = ids_out.astype(jnp.int32)
        rank_ref[:, tok[p]] = rank_out.astype(jnp.int32)
        wts_ref[tok[p], :] = jnp.concatenate([wts_out, jnp.zeros((LANES - 8, tm), f32)], axis=0).T
    cnt_ref[...] = running[...]


def _outproj_router(yl, yg, xt, mod3, wo, lg, lb, wr_hi, wr_lo, br, tokens_per_batch):
    T = xt.shape[0]
    tm = TOKEN_TILE
    tiles_per_batch = tokens_per_batch // tm
    tokrow = lambda i: (i, 0)
    const = lambda i: (0, 0)
    half_cols = yg[0].shape[1]
    rows_per_tile = tm // (2 * half_cols)
    yg_spec = pl.BlockSpec((1, half_cols, rows_per_tile, D_GLA_V),
                           lambda i: (i // tiles_per_batch, 0, i % tiles_per_batch, 0))
    return pl.pallas_call(
        _outproj_router_kernel,
        grid=(T // tm,),
        in_specs=[pl.BlockSpec((tm, D_LRU), tokrow),
                  yg_spec, yg_spec,
                  pl.BlockSpec((tm, D_MODEL), tokrow),
                  pl.BlockSpec((1, 1, N_MOD * D_MODEL), lambda i: (i // tiles_per_batch, 0, 0)),
                  pl.BlockSpec((D_MODEL, D_MODEL), const),
                  pl.BlockSpec((1, D_MODEL), const),
                  pl.BlockSpec((1, D_MODEL), const),
                  pl.BlockSpec((N_EXPERTS, D_MODEL), const),
                  pl.BlockSpec((N_EXPERTS, D_MODEL), const),
                  pl.BlockSpec((N_EXPERTS, 1), const)],
        out_specs=[pl.BlockSpec((tm, D_MODEL), tokrow),
                   pl.BlockSpec((tm, D_MODEL // 2), tokrow),
                   pl.BlockSpec((8, tm), lambda i: (0, i)),
                   pl.BlockSpec((8, tm), lambda i: (0, i)),
                   pl.BlockSpec((tm, LANES), tokrow),
                   pl.BlockSpec((N_EXPERTS, 1), const)],
        out_shape=[jax.ShapeDtypeStruct((T, D_MODEL), f32),
                   jax.ShapeDtypeStruct((T, D_MODEL // 2), jnp.int32),
                   jax.ShapeDtypeStruct((8, T), jnp.int32),
                   jax.ShapeDtypeStruct((8, T), jnp.int32),
                   jax.ShapeDtypeStruct((T, LANES), f32),
                   jax.ShapeDtypeStruct((N_EXPERTS, 1), f32)],
        scratch_shapes=[pltpu.VMEM((N_EXPERTS, 1), f32)],
        compiler_params=_cparams(("arbitrary",)),
        name="outproj_router",
    )(yl, yg[0], yg[1], xt, mod3, wo, lg, lb, wr_hi, wr_lo, br)


def _route_kernel(ids_ref, rank_ref, cnt_ref, dest_ref, tiles_ref, *, bm):
    ne = cnt_ref.shape[0]
    cnt = cnt_ref[...]
    padded = jnp.floor((cnt + (bm - 1.0)) * (1.0 / bm)) * bm
    sub = lax.broadcasted_iota(jnp.int32, (ne, LANES), 0)
    lane = lax.broadcasted_iota(jnp.int32, (ne, LANES), 1)
    padded_row = jnp.sum(jnp.where(sub == lane, padded, 0.0), axis=0, keepdims=True)
    base = jnp.sum(jnp.where(lane < sub, padded_row, 0.0), axis=1, keepdims=True)
    ends = base + padded

    tc = ids_ref.shape[1]
    expert = lax.broadcasted_iota(jnp.int32, (ne, tc), 0)
    slot = lax.broadcasted_iota(jnp.int32, (8, tc), 0)
    ids = ids_ref[...]
    group_base = jnp.zeros((8, tc), f32)
    for kk in range(TOP_K):
        b_k = jnp.sum(jnp.where(expert == ids[kk:kk + 1], base, 0.0), axis=0, keepdims=True)
        group_base = jnp.where(slot == kk, b_k, group_base)
    dest_ref[...] = group_base.astype(jnp.int32) + rank_ref[...]

    nl = tiles_ref.shape[1]
    start = lax.broadcasted_iota(jnp.int32, (ne, nl), 1).astype(f32) * bm
    te = jnp.minimum(jnp.sum(jnp.where(start >= ends, 1.0, 0.0), axis=0, keepdims=True), ne - 1.0)
    at_te = lax.broadcasted_iota(jnp.int32, (ne, nl), 0).astype(f32) == te
    cnt_te = jnp.sum(jnp.where(at_te, cnt, 0.0), axis=0, keepdims=True)
    base_te = jnp.sum(jnp.where(at_te, base, 0.0), axis=0, keepdims=True)
    valid = jnp.clip(cnt_te - (start[0:1] - base_te), 0.0, float(bm))
    next_group = jnp.sum(jnp.where(at_te, ends, 0.0), axis=0, keepdims=True) * (1.0 / bm)
    last_used = jnp.sum(padded, axis=0, keepdims=True) * (1.0 / bm) - 1.0
    srow = lax.broadcasted_iota(jnp.int32, (8, nl), 0)
    table = jnp.where(srow == 0, te, jnp.where(srow == 1, valid, jnp.where(srow == 2, next_group,
                                                                           jnp.where(srow == 3, last_used, 0.0))))
    tiles_ref[...] = table.astype(jnp.int32)


def _route(ids, rank, cnt, bm, n_tiles):
    T = ids.shape[1]
    tc = 2048
    nl = -(-n_tiles // LANES) * LANES
    tok = lambda i: (0, i)
    const = lambda i: (0, 0)
    return pl.pallas_call(
        functools.partial(_route_kernel, bm=bm),
        grid=(T // tc,),
        in_specs=[pl.BlockSpec((8, tc), tok), pl.BlockSpec((8, tc), tok),
                  pl.BlockSpec((N_EXPERTS, 1), const)],
        out_specs=[pl.BlockSpec((8, tc), tok), pl.BlockSpec((8, nl), const)],
        out_shape=[jax.ShapeDtypeStruct((8, T), jnp.int32), jax.ShapeDtypeStruct((8, nl), jnp.int32)],
        compiler_params=_cparams(("arbitrary",)),
        name="route",
    )(ids, rank, cnt)


def _sc_workers():
    info = plsc.get_sparse_core_info()
    return info.num_cores, info.num_subcores


def _sc_dispatch(rows, dest_flat, n_out):
    T, D = rows.shape
    nc, ns = _sc_workers()
    per_w = T // (nc * ns)
    n_chunks = per_w // SC_CHUNK
    assert n_chunks % 2 == 0
    mesh = plsc.VectorSubcoreMesh(core_axis_name="c", subcore_axis_name="s")

    @functools.partial(
        pl.kernel, mesh=mesh,
        out_type=jax.ShapeDtypeStruct((n_out, D), rows.dtype),
        scratch_types=([pltpu.VMEM((SC_CHUNK,), jnp.int32)] * (2 * TOP_K)
                       + [pltpu.VMEM((SC_CHUNK, D), rows.dtype)] * 2
                       + [pltpu.SemaphoreType.DMA] * (2 * TOP_K)),
    )
    def k(rows_hbm, dest_hbm, out_hbm, *scratch):
        idx_v = (scratch[:TOP_K], scratch[TOP_K:2 * TOP_K])
        rows_v = scratch[2 * TOP_K:2 * TOP_K + 2]
        sems = (scratch[2 * TOP_K + 2:3 * TOP_K + 2], scratch[3 * TOP_K + 2:])
        wid = lax.axis_index("s") * nc + lax.axis_index("c")
        base = wid * per_w

        def load(chunk, b):
            off = base + chunk * SC_CHUNK
            pltpu.sync_copy(rows_hbm.at[pl.ds(off, SC_CHUNK)], rows_v[b])
            for kk in range(TOP_K):
                pltpu.sync_copy(dest_hbm.at[pl.ds(kk * T + off, SC_CHUNK)], idx_v[b][kk])

        load(0, 0)

        @pl.loop(0, n_chunks, step=2)
        def _(j):
            for b in range(2):
                copies = [pltpu.async_copy(rows_v[b], out_hbm.at[idx_v[b][kk]], sems[b][kk])
                          for kk in range(TOP_K)]

                @pl.when(j + b + 1 < n_chunks)
                def _():
                    load(j + b + 1, 1 - b)

                for cp in copies:
                    cp.wait()

    return k(rows, dest_flat)


def _sc_gather(table, idx):
    _, D = table.shape
    N = idx.shape[0]
    nc, ns = _sc_workers()
    per_w = N // (nc * ns)
    n_chunks = per_w // SC_CHUNK
    assert n_chunks % 2 == 0
    mesh = plsc.VectorSubcoreMesh(core_axis_name="c", subcore_axis_name="s")

    @functools.partial(
        pl.kernel, mesh=mesh,
        out_type=jax.ShapeDtypeStruct((N, D), table.dtype),
        scratch_types=([pltpu.VMEM((SC_CHUNK,), jnp.int32)] * 2
                       + [pltpu.VMEM((SC_CHUNK, D), table.dtype)] * 2
                       + [pltpu.SemaphoreType.DMA] * 2),
    )
    def k(table_hbm, idx_hbm, out_hbm, idx0, idx1, buf0, buf1, sem0, sem1):
        idxs, bufs, sems = (idx0, idx1), (buf0, buf1), (sem0, sem1)
        wid = lax.axis_index("s") * nc + lax.axis_index("c")
        base = wid * per_w

        def gather(b):
            return pltpu.make_async_copy(table_hbm.at[idxs[b]], bufs[b], sems[b])

        def start(chunk, b):
            pltpu.sync_copy(idx_hbm.at[pl.ds(base + chunk * SC_CHUNK, SC_CHUNK)], idxs[b])
            gather(b).start()

        def finish(chunk, b):
            gather(b).wait()
            pltpu.sync_copy(bufs[b], out_hbm.at[pl.ds(base + chunk * SC_CHUNK, SC_CHUNK)])

        start(0, 0)

        @pl.loop(0, n_chunks, step=2)
        def _(j):
            start(j + 1, 1)
            finish(j, 0)

            @pl.when(j + 2 < n_chunks)
            def _():
                start(j + 2, 0)

            finish(j + 1, 1)

    return k(table, idx)


def _ffn_kernel(te_ref, tv_ref, tn_ref, tl_ref, xs_ref, wg_hbm, bg_ref, wu_hbm, bu_ref, wd_hbm, bd_ref, eo_ref,
                w_buf, sems, slot_ref):
    i = pl.program_id(0)
    n_tiles = pl.num_programs(0)
    w_hbm = (wg_hbm, wu_hbm, wd_hbm)

    def fetch(e, slot):
        return [pltpu.make_async_copy(w_hbm[m].at[e], w_buf.at[slot, m], sems.at[slot, m]) for m in range(3)]

    @pl.when(i == 0)
    def _():
        slot_ref[0] = 0
        for cp in fetch(te_ref[0], 0):
            cp.start()

    prev = te_ref[jnp.maximum(i - 1, 0)]
    first = (i == 0) | (te_ref[i] != prev)

    @pl.when(first & (i > 0))
    def _():
        slot_ref[0] = 1 - slot_ref[0]

    slot = slot_ref[0]

    @pl.when(first)
    def _():
        for cp in fetch(te_ref[i], slot):
            cp.wait()
        nxt = tn_ref[i]
        e_nxt = te_ref[jnp.minimum(nxt, n_tiles - 1)]

        @pl.when((nxt > i) & (nxt < n_tiles) & (e_nxt != te_ref[i]))
        def _():
            for cp in fetch(e_nxt, 1 - slot):
                cp.start()

    wg_b, wu_b, wd_b = (w_buf.at[slot, m] for m in range(3))
    valid = tv_ref[i]

    def ffn_rows(r0, m):
        blocks = [(r0 + o, min(MOE_BLOCK, m)) for o in range(0, m, MOE_BLOCK)]
        xs, gates, ups, acts = [], [], [], []
        for b0, bm_ in blocks:
            row = lax.broadcasted_iota(jnp.int32, (bm_, 1), 0) + b0
            xw = jnp.where(row < valid, xs_ref[b0:b0 + bm_, :], 0)
            x_hi, x_lo = _unpack_rows(xw)
            xs.append(jnp.concatenate([x_hi, x_lo], axis=1).astype(bf16))
        for x in xs:
            gates.append(jnp.minimum(jnp.dot(x, wg_b[...], preferred_element_type=f32) + bg_ref[...],
                                     SWIGLU_LIMIT))
            ups.append(jnp.clip(jnp.dot(x, wu_b[...], preferred_element_type=f32) + bu_ref[...],
                                -SWIGLU_LIMIT, SWIGLU_LIMIT))
        for gate, up in zip(gates, ups):
            acts.append(((up + 1.0) * gate * jax.nn.sigmoid(SWIGLU_ALPHA * gate)).astype(bf16))
        for (b0, bm_), act in zip(blocks, acts):
            out = jnp.dot(act, wd_b[...], preferred_element_type=f32) + bd_ref[...]
            eo_ref[b0:b0 + bm_, :] = _pack_rows(out)

    def zero_rows(r0, m):
        eo_ref[r0:r0 + m, :] = jnp.zeros((m, eo_ref.shape[1]), eo_ref.dtype)

    for r0 in range(0, xs_ref.shape[0], MOE_PASS):
        lo = 0
        for m in MOE_PASS_SIZES:
            @pl.when((valid > r0 + lo) & ((valid <= r0 + m) | (m == MOE_PASS)))
            def _(r0=r0, m=m):
                ffn_rows(r0, m)
                if m < MOE_PASS:
                    zero_rows(r0 + m, MOE_PASS - m)
            lo = m

        @pl.when(valid <= r0)
        def _(r0=r0):
            zero_rows(r0, MOE_PASS)


def _expert_ffn(tile_expert, tile_valid, tile_next, last_used, xs, w_gate, b_gate, w_up, b_up, w_down, b_down):
    n_rows, dp = xs.shape
    d = 2 * dp
    bm = MOE_TILE
    d_e = w_gate.shape[-1]
    assert d == d_e
    bspec = lambda n_: pl.BlockSpec((None, 1, n_), lambda i, te, tv, tn, tl: (te[i], 0, 0))
    hbm = pl.BlockSpec(memory_space=pl.ANY)
    grid_spec = pltpu.PrefetchScalarGridSpec(
        num_scalar_prefetch=4,
        grid=(n_rows // bm,),
        in_specs=[pl.BlockSpec((bm, dp), lambda i, te, tv, tn, tl: (jnp.minimum(i, tl[0]), 0)),
                  hbm, bspec(d_e), hbm, bspec(d_e), hbm, bspec(d)],
        out_specs=pl.BlockSpec((bm, dp), lambda i, te, tv, tn, tl: (i, 0)),
        scratch_shapes=[pltpu.VMEM((2, 3, d, d_e), f32),
                        pltpu.SemaphoreType.DMA((2, 3)),
                        pltpu.SMEM((1,), jnp.int32)],
    )
    return pl.pallas_call(
        _ffn_kernel,
        grid_spec=grid_spec,
        out_shape=jax.ShapeDtypeStruct((n_rows, dp), jnp.int32),
        compiler_params=_cparams(("arbitrary",)),
        name="expert_ffn",
    )(tile_expert, tile_valid, tile_next, last_used, xs, w_gate, b_gate.reshape(N_EXPERTS, 1, d_e), w_up,
      b_up.reshape(N_EXPERTS, 1, d_e), w_down, b_down.reshape(N_EXPERTS, 1, d))


def _combine_kernel(eg_ref, wts_ref, x1_ref, mod_ref, lg_ref, lb_ref, o_ref):
    w = wts_ref[...]
    y_hi, y_lo = _unpack_rows(eg_ref[0])
    y_hi, y_lo = y_hi * w[:, 0:1], y_lo * w[:, 0:1]
    for kk in range(1, TOP_K):
        e_hi, e_lo = _unpack_rows(eg_ref[kk])
        y_hi = y_hi + e_hi * w[:, kk:kk + 1]
        y_lo = y_lo + e_lo * w[:, kk:kk + 1]
    y = jnp.concatenate([y_hi, y_lo], axis=1)
    z = DEEPNORM_ALPHA * x1_ref[...] + mod_ref[0] * y
    mu = jnp.mean(z, axis=-1, keepdims=True)
    zc = z - mu
    var = jnp.mean(zc * zc, axis=-1, keepdims=True)
    o_ref[...] = zc * lax.rsqrt(var + LN_EPS) * lg_ref[...] + lb_ref[...]


def _combine_ln(eg, wts, x1, mod3, lg, lb, tokens_per_batch):
    T = x1.shape[0]
    tm = TOKEN_TILE
    tiles_per_batch = tokens_per_batch // tm
    const = lambda i: (0, 0)
    return pl.pallas_call(
        _combine_kernel,
        grid=(T // tm,),
        in_specs=[pl.BlockSpec((TOP_K, tm, D_MODEL // 2), lambda i: (0, i, 0)),
                  pl.BlockSpec((tm, LANES), lambda i: (i, 0)),
                  pl.BlockSpec((tm, D_MODEL), lambda i: (i, 0)),
                  pl.BlockSpec((1, 1, D_MODEL), lambda i: (i // tiles_per_batch, 0, N_MOD - 1)),
                  pl.BlockSpec((1, D_MODEL), const),
                  pl.BlockSpec((1, D_MODEL), const)],
        out_specs=pl.BlockSpec((tm, D_MODEL), lambda i: (i, 0)),
        out_shape=jax.ShapeDtypeStruct((T, D_MODEL), f32),
        compiler_params=_cparams(("arbitrary",)),
        name="combine_ln",
    )(eg, wts, x1, mod3, lg, lb)


def _block_diag(w):
    n, c, d = w.shape
    eye = jnp.eye(n, dtype=w.dtype)
    return jnp.einsum('ncd,nm->ncmd', w, eye).reshape(n * c, n * d)


def kernel(x, c, ctx, c_ctx, w_ada, b_ada, w_in, conv_w, conv_b, lru_wa, lru_ba, lru_wx, lru_bx,
           lru_lam, gla_wa, gla_ba, gla_norm_g, w_out, ln1_g, ln1_b, w_router, b_router, w_gate,
           b_gate, w_up, b_up, w_down, b_down, ln2_g, ln2_b):
    B, L, D = x.shape
    Lc = ctx.shape[1]
    T = B * L
    rows = L // GRID_W
    l = 0

    cpad = jnp.zeros((8, D), f32).at[0:B].set(c).at[B].set(c_ctx)
    w_cat = jnp.pad(w_in[l], ((0, 0), (0, D_PROJ - w_in.shape[-1]))).astype(bf16)
    wg = [jnp.concatenate([_block_diag(lru_wa[l, d]), _block_diag(lru_wx[l, d])], axis=1).astype(bf16)
          for d in range(2)]
    bg = [jnp.concatenate([lru_ba[l, d], lru_bx[l, d]])[None] for d in range(2)]
    lam = [lru_lam[l, d][None] for d in range(2)]
    wa = [jnp.pad(gla_wa[l, d], ((0, A_LOW_PAD - GATE_RANK), (0, 0))).astype(bf16) for d in range(2)]
    ba = [gla_ba[l, d][None] for d in range(2)]
    cw, cb = conv_w[l], conv_b[l][None]
    wr_t = w_router[l].T
    wr_hi = wr_t.astype(bf16)
    wr_lo = (wr_t - wr_hi.astype(f32)).astype(bf16)
    br = b_router[l][:, None]
    c_xl, c_gl, c_pg = (0, D_LRU), (D_LRU, 2 * D_LRU), (2 * D_LRU, D_PROJ)

    mod3 = _ada_mod(cpad, w_ada[l], b_ada[l]).reshape(8, 1, N_MOD * D)

    tiles_per_batch = L // INPROJ_TILE
    xl_c, pg_c = _inproj_rows(ctx.reshape(B * Lc, D), mod3, w_cat, lambda i: B, (c_xl, c_pg), (f32, bf16))
    zero_h = jnp.zeros((B, 1, D_LRU), f32)
    zero_s = jnp.zeros((B, GLA_DV, D_GLA_K), f32)
    pg_c = pg_c.reshape(B, Lc // GLA_CHUNK, GLA_CHUNK, D_PG)
    h_ctx = []
    for d in range(2):
        _, _, hf = _lru_pass(xl_c, wg[d], bg[d], lam[d], zero_h, n_batch=B, tile=Lc, reverse=bool(d),
                             conv=(cw, cb, Lc))
        h_ctx.append(hf)
    s_ctx = _gla_pass(pg_c, wa, ba, (zero_s, zero_s))

    xt = x.reshape(T, D)
    xl, gl = _inproj_rows(xt, mod3, w_cat, lambda i: i // tiles_per_batch, (c_xl, c_gl), (f32, bf16))
    hb, u_lru, _ = _lru_pass(xl, wg[1], bg[1], lam[1], h_ctx[1], n_batch=B, tile=LRU_TILE, reverse=True,
                             conv=(cw, cb, GRID_W))
    y_lru, _ = _lru_pass(u_lru, wg[0], bg[0], lam[0], h_ctx[0], n_batch=B, tile=LRU_TILE, reverse=False,
                         merge_with=(hb, gl))
    pg = _inproj_cols(x.reshape(B, rows, GRID_W, D), mod3, w_cat, c_pg)
    y_gla = _gla_pass(pg, wa, ba, s_ctx, gn=gla_norm_g[l][None])

    x1, hp, ids, rank, wts, cnt = _outproj_router(
        y_lru, y_gla, xt, mod3, w_out[l].astype(bf16), ln1_g[l][None], ln1_b[l][None], wr_hi, wr_lo, br, L)

    n_tiles = T * TOP_K // MOE_TILE + N_EXPERTS
    dest, tiles = _route(ids, rank, cnt, MOE_TILE, n_tiles)
    xs = _sc_dispatch(hp, dest.reshape(-1), n_tiles * MOE_TILE)
    eo = _expert_ffn(tiles[0, :n_tiles], tiles[1, :n_tiles], tiles[2, :n_tiles], tiles[3, :1], xs, w_gate[l],
                     b_gate[l], w_up[l], b_up[l], w_down[l], b_down[l])
    eg = _sc_gather(eo, dest[:TOP_K].reshape(-1)).reshape(TOP_K, T, D // 2)
    out = _combine_ln(eg, wts, x1, mod3, ln2_g[l][None], ln2_b[l][None], L)
    return out.reshape(B, L, D)
```

```python
import functools

import jax
import jax.numpy as jnp
from jax import lax
from jax.experimental import pallas as pl
from jax.experimental.pallas import tpu as pltpu
from jax.experimental.pallas import tpu_sc as plsc

D_MODEL = 1024
DEPTH = 1
GRID_W = 64
D_LRU = 512
CONV_W = 4
LRU_C = 8.0
GLA_HEADS = 4
D_GLA_V = 512
D_GLA_K = 256
GLA_DK = 64
GLA_DV = 128
GATE_RANK = 16
GATE_TAU = 16.0
GLA_CHUNK = 64
N_EXPERTS = 32
TOP_K = 4
SWIGLU_LIMIT = 7.0
SWIGLU_ALPHA = 1.702
N_MOD = 6
DEEPNORM_ALPHA = (2.0 * DEPTH) ** 0.25
LN_EPS = 1e-5
RMS_EPS = 1e-6

LANES = 128
A_LOW_PAD = LANES
D_PG = 2 * D_GLA_K + 2 * D_GLA_V + A_LOW_PAD
D_PROJ = 2 * D_LRU + D_PG
TOKEN_TILE = 1024
INPROJ_TILE = 1024
LRU_TILE = 512
MOE_TILE = 1024
MOE_PASS = 512
MOE_PASS_SIZES = (128, 256, MOE_PASS)
MOE_BLOCK = 256
OUTPROJ_PART = 128
SC_CHUNK = 64
VMEM_LIMIT = 48 * 1024 * 1024

f32 = jnp.float32
bf16 = jnp.bfloat16


def _cparams(sem):
    return pltpu.CompilerParams(dimension_semantics=sem, vmem_limit_bytes=VMEM_LIMIT)


def _ada_kernel(c_ref, w_ref, b_ref, o_ref):
    s = c_ref[...]
    s = s * jax.nn.sigmoid(s)
    o_ref[...] = jnp.dot(s.astype(bf16), w_ref[...], preferred_element_type=f32) + b_ref[...]


def _ada_mod(cpad, w, b):
    n = w.shape[1]
    tn = 1024
    return pl.pallas_call(
        _ada_kernel,
        grid=(n // tn,),
        in_specs=[pl.BlockSpec((8, D_MODEL), lambda j: (0, 0)),
                  pl.BlockSpec((D_MODEL, tn), lambda j: (0, j)),
                  pl.BlockSpec((1, tn), lambda j: (0, j))],
        out_specs=pl.BlockSpec((8, tn), lambda j: (0, j)),
        out_shape=jax.ShapeDtypeStruct((8, n), f32),
        compiler_params=_cparams(("arbitrary",)),
        name="ada_mod",
    )(cpad, w, b.reshape(1, n))


def _inproj_kernel(x_ref, mod_ref, w_ref, *out_refs, parts, col_major):
    sh = mod_ref[0, :, 0:D_MODEL]
    sc = mod_ref[0, :, D_MODEL:2 * D_MODEL]
    if col_major:
        x = jnp.concatenate([x_ref[0, :, j, :] for j in range(x_ref.shape[2])], axis=0)
    else:
        x = x_ref[...]
    u = (x * (1.0 + sc) + sh).astype(bf16)
    for (lo, hi), o_ref in zip(parts, out_refs):
        p = jnp.dot(u, w_ref[:, lo:hi], preferred_element_type=f32).astype(o_ref.dtype)
        o_ref[...] = p.reshape(o_ref.shape)


def _inproj_rows(xt, mod3, w_cat, batch_of_tile, parts, dtypes):
    T = xt.shape[0]
    tm = INPROJ_TILE
    return pl.pallas_call(
        functools.partial(_inproj_kernel, parts=parts, col_major=False),
        grid=(T // tm,),
        in_specs=[pl.BlockSpec((tm, D_MODEL), lambda i: (i, 0)),
                  pl.BlockSpec((1, 1, 2 * D_MODEL), lambda i: (batch_of_tile(i), 0, 0)),
                  pl.BlockSpec((D_MODEL, D_PROJ), lambda i: (0, 0))],
        out_specs=[pl.BlockSpec((tm, hi - lo), lambda i: (i, 0)) for lo, hi in parts],
        out_shape=[jax.ShapeDtypeStruct((T, hi - lo), dt) for (lo, hi), dt in zip(parts, dtypes)],
        compiler_params=_cparams(("arbitrary",)),
        name="inproj_rows",
    )(xt, mod3, w_cat)


def _inproj_cols(x4, mod3, w_cat, part):
    n_batch, rows, cols, _ = x4.shape
    lo, hi = part
    cb = INPROJ_TILE // rows
    return pl.pallas_call(
        functools.partial(_inproj_kernel, parts=(part,), col_major=True),
        grid=(n_batch, cols // cb),
        in_specs=[pl.BlockSpec((1, rows, cb, D_MODEL), lambda b, n: (b, 0, n, 0)),
                  pl.BlockSpec((1, 1, 2 * D_MODEL), lambda b, n: (b, 0, 0)),
                  pl.BlockSpec((D_MODEL, D_PROJ), lambda b, n: (0, 0))],
        out_specs=[pl.BlockSpec((1, cb, rows, hi - lo), lambda b, n: (b, n, 0, 0))],
        out_shape=[jax.ShapeDtypeStruct((n_batch, cols, rows, hi - lo), bf16)],
        compiler_params=_cparams(("arbitrary", "arbitrary")),
        name="inproj_cols",
    )(x4, mod3, w_cat)[0]


def _gelu_tanh(x):
    return 0.5 * x * (1.0 + jnp.tanh(0.7978845608028654 * (x + 0.044715 * (x * x * x))))


def _lru_kernel(*refs, row_w, reverse, merge):
    if merge:
        u_ref, wg_ref, bg_ref, lam_ref, h0_ref, hb_ref, gl_ref, out_ref, hfin_ref, carry, h_nat = refs
    else:
        xl_ref, cw_ref, cb_ref, wg_ref, bg_ref, lam_ref, h0_ref, out_ref, u_out_ref, hfin_ref, carry = refs
    t = pl.program_id(1)

    @pl.when(t == 0)
    def _():
        carry[...] = h0_ref[0]

    n_seg = 8
    blk = lambda v, j: v[j * n_seg:(j + 1) * n_seg]
    if merge:
        u = u_ref[...]
        tt, ch = u.shape
        seg = tt // n_seg
    else:
        _, seg, ch = xl_ref.shape
        tt = n_seg * seg
        x = jnp.concatenate([xl_ref[:, j, :] for j in range(seg)], axis=0)
        segs_per_row = row_w // seg
        s_idx = lax.broadcasted_iota(jnp.int32, (n_seg, ch), 0)
        has_prev = (s_idx % segs_per_row) != 0
        has_next = (s_idx % segs_per_row) != segs_per_row - 1
        from_prev = lambda v: jnp.where(has_prev, pltpu.roll(v, 1, 0), 0.0)
        from_next = lambda v: jnp.where(has_next, pltpu.roll(v, n_seg - 1, 0), 0.0)
        x_m1 = jnp.concatenate([from_prev(blk(x, seg - 1)), x[:tt - n_seg]], axis=0)
        x_m2 = jnp.concatenate([from_prev(blk(x, seg - 2)), from_prev(blk(x, seg - 1)), x[:tt - 2 * n_seg]],
                               axis=0)
        x_p1 = jnp.concatenate([x[n_seg:], from_next(blk(x, 0))], axis=0)
        cw = cw_ref[...]
        u = cb_ref[...] + x_m2 * cw[0:1] + x_m1 * cw[1:2] + x * cw[2:3] + x_p1 * cw[3:4]
        u_out_ref[...] = u
    rows_q = tt // 4
    g_parts = [jnp.dot(u[q * rows_q:(q + 1) * rows_q].astype(bf16), wg_ref[...], preferred_element_type=f32)
               + bg_ref[...] for q in range(4)]
    lam = lam_ref[...]
    softplus_neg = jnp.maximum(-lam, 0.0) + jnp.log(1.0 + jnp.exp(-jnp.abs(lam)))
    log_a_scale = -LRU_C * softplus_neg

    def recurrence_terms(j):
        q, r = divmod(j * n_seg, rows_q)
        g_j, u_j = g_parts[q][r:r + n_seg], blk(u, j)
        log_a = log_a_scale * jax.nn.sigmoid(g_j[:, 0:ch])
        a_j = jnp.exp(log_a)
        series = log_a * (-2.0 + log_a * (-2.0 + log_a * (-4.0 / 3.0)))
        one_minus_a2 = jnp.where(log_a > -0.005, series, 1.0 - a_j * a_j)
        root = jnp.where(one_minus_a2 > 0.0, one_minus_a2 * lax.rsqrt(one_minus_a2), 0.0)
        return a_j, root * (jax.nn.sigmoid(g_j[:, ch:2 * ch]) * u_j)

    half = seg // 2
    h_loc, a_cum = [None] * seg, [None] * seg
    h_run, a_run = [None, None], [None, None]
    for i in range(half):
        for p in range(2):
            j = p * half + (half - 1 - i if reverse else i)
            a_j, b_j = recurrence_terms(j)
            h_run[p] = b_j if i == 0 else a_j * h_run[p] + b_j
            a_run[p] = a_j if i == 0 else a_j * a_run[p]
            h_loc[j], a_cum[j] = h_run[p], a_run[p]
    pieces = [(s, p) for s in range(n_seg) for p in range(2)]
    if reverse:
        pieces.reverse()
    entering = [[None] * n_seg, [None] * n_seg]
    state = carry[...]
    for s, p in pieces:
        entering[p][s] = state
        state = h_run[p][s:s + 1] + a_run[p][s:s + 1] * state
    carry[...] = state
    hfin_ref[0] = state
    enter = [jnp.concatenate(e, axis=0) for e in entering]
    h_rows = [h_loc[j] + a_cum[j] * enter[j // half] for j in range(seg)]
    if merge:
        for j in range(seg):
            h_nat[:, j, :] = h_rows[j] + blk(hb_ref, j)
        h = h_nat[...].reshape(tt, ch)
        out_ref[...] = (h * _gelu_tanh(gl_ref[...].astype(f32))).astype(out_ref.dtype)
    else:
        out_ref[...] = jnp.concatenate(h_rows, axis=0)


def _lru_pass(x_in, wg, bg, lam, h0, *, n_batch, tile, reverse, conv=None, merge_with=None):
    T = x_in.shape[0]
    nt = T // n_batch // tile
    merge = merge_with is not None
    assert merge != (conv is not None)
    seg = tile // 8

    def tok(b, t):
        return (b * nt + (nt - 1 - t if reverse else t), 0)

    tile_spec = pl.BlockSpec((tile, D_LRU), tok)
    const = lambda b, t: (0, 0)
    gate_specs = [pl.BlockSpec((D_LRU, 2 * D_LRU), const),
                  pl.BlockSpec((1, 2 * D_LRU), const),
                  pl.BlockSpec((1, D_LRU), const),
                  pl.BlockSpec((1, 1, D_LRU), lambda b, t: (b, 0, 0))]
    state_spec = pl.BlockSpec((1, 1, D_LRU), lambda b, t: (b, 0, 0))
    state_shape = jax.ShapeDtypeStruct((n_batch, 1, D_LRU), f32)
    scratch = [pltpu.VMEM((1, D_LRU), f32)]
    if merge:
        hb, gl = merge_with
        row_w = None
        in_specs = [tile_spec] + gate_specs + [tile_spec, tile_spec]
        args = [x_in, wg, bg, lam, h0, hb, gl]
        scratch += [pltpu.VMEM((8, seg, D_LRU), f32)]
        out_specs = [tile_spec, state_spec]
        out_shape = [jax.ShapeDtypeStruct((T, D_LRU), bf16), state_shape]
    else:
        conv_w, conv_b, row_w = conv
        assert row_w % seg == 0 and seg >= 2
        in_specs = [pl.BlockSpec((8, seg, D_LRU), lambda b, t: tok(b, t) + (0,)),
                    pl.BlockSpec((CONV_W, D_LRU), const),
                    pl.BlockSpec((1, D_LRU), const)] + gate_specs
        args = [x_in.reshape(T // seg, seg, D_LRU), conv_w, conv_b, wg, bg, lam, h0]
        out_specs = [tile_spec, tile_spec, state_spec]
        out_shape = [jax.ShapeDtypeStruct((T, D_LRU), f32), jax.ShapeDtypeStruct((T, D_LRU), f32), state_shape]
    return pl.pallas_call(
        functools.partial(_lru_kernel, row_w=row_w, reverse=reverse, merge=merge),
        grid=(n_batch, nt),
        in_specs=in_specs,
        out_specs=out_specs,
        out_shape=out_shape,
        scratch_shapes=scratch,
        compiler_params=_cparams(("arbitrary", "arbitrary")),
        name="lru_merge" if merge else "lru_scan",
    )(*args)


_PG_Q, _PG_K, _PG_V = 0, D_GLA_K, 2 * D_GLA_K
_PG_G, _PG_A = 2 * D_GLA_K + D_GLA_V, 2 * D_GLA_K + 2 * D_GLA_V


def _gla_chunks(streams):
    ck = GLA_CHUNK
    nh = GLA_HEADS
    head_of_lane = lax.broadcasted_iota(jnp.int32, (1, D_GLA_K), 1) // GLA_DK
    nt_dims = (((1,), (1,)), ((), ()))
    tn_dims = (((0,), (0,)), ((), ()))
    ii = lax.broadcasted_iota(jnp.int32, (nh * ck, ck), 0) & (ck - 1)
    jj = lax.broadcasted_iota(jnp.int32, (nh * ck, ck), 1)

    units = []
    for pg_ref, wa_ref, ba_ref, state, reverse in streams:
        n_batch = pg_ref.shape[0]
        rows_all = n_batch * ck
        a_low = jnp.concatenate([pg_ref[bi, :, _PG_A:_PG_A + A_LOW_PAD] for bi in range(n_batch)], axis=0)
        z = jnp.dot(a_low, wa_ref[...], preferred_element_type=f32) + ba_ref[...]
        bcum_all = (jnp.minimum(z, 0.0) - jnp.log(1.0 + jnp.exp(-jnp.abs(z)))) * (1.0 / GATE_TAU)
        step = lax.broadcasted_iota(jnp.int32, (rows_all, D_GLA_K), 0) & (ck - 1)
        d = 1
        while d < ck:
            if reverse:
                bcum_all = bcum_all + jnp.where(step < ck - d, pltpu.roll(bcum_all, rows_all - d, 0), 0.0)
            else:
                bcum_all = bcum_all + jnp.where(step >= d, pltpu.roll(bcum_all, d, 0), 0.0)
            d *= 2
        seen = (jj >= ii) if reverse else (jj <= ii)
        for bi in range(n_batch):
            units.append(dict(pg=pg_ref, bi=bi, state=state, reverse=reverse, seen=seen,
                              bcum=bcum_all[bi * ck:(bi + 1) * ck]))

    for u in units:
        pg_ref, bi, bcum = u["pg"], u["bi"], u["bcum"]
        q = pg_ref[bi, :, _PG_Q:_PG_K].astype(f32) * (GLA_DK ** -0.5)
        k = pg_ref[bi, :, _PG_K:_PG_V].astype(f32)
        u["btot"] = bcum[0:1] if u["reverse"] else bcum[ck - 1:ck]
        q_dec = q * jnp.exp(bcum)
        k_dec = (k * jnp.exp(-bcum)).astype(bf16)
        k_end = k * jnp.exp(u["btot"] - bcum)
        u["s_t"] = u["state"][bi]
        by_head = lambda t: jnp.concatenate([jnp.where(head_of_lane == hd, t, 0.0) for hd in range(nh)],
                                            axis=0).astype(bf16)
        u["q_blk"] = by_head(q_dec)
        u["k_blk"] = by_head(k_end)
        u["rhs"] = jnp.concatenate([u["s_t"].astype(bf16), k_dec], axis=0)
    for u in units:
        u["qs"] = lax.dot_general(u["q_blk"], u["rhs"], nt_dims, preferred_element_type=f32)
    for u in units:
        u["scores"] = jnp.where(u["seen"], u["qs"][:, GLA_DV:GLA_DV + ck], 0.0).astype(bf16)
    for u in units:
        v = u["pg"][u["bi"], :, _PG_V:_PG_G]
        u["intra"] = [jnp.dot(u["scores"][hd * ck:(hd + 1) * ck], v[:, hd * GLA_DV:(hd + 1) * GLA_DV],
                              preferred_element_type=f32) for hd in range(nh)]
        v_stack = jnp.concatenate([v[:, hd * GLA_DV:(hd + 1) * GLA_DV] for hd in range(nh)], axis=0)
        u["kv_t"] = lax.dot_general(v_stack, u["k_blk"], tn_dims, preferred_element_type=f32)
    for u in units:
        u["outs"] = [u["intra"][hd] + u["qs"][hd * ck:(hd + 1) * ck, 0:GLA_DV] for hd in range(nh)]
        u["state"][u["bi"]] = u["s_t"] * jnp.exp(u["btot"]) + u["kv_t"]
    n_batch = streams[0][0].shape[0]
    return [[u["outs"] for u in units[si * n_batch:(si + 1) * n_batch]] for si in range(len(streams))]


def _gla_gate(o_heads, g, gn):
    normed = []
    for oh in o_heads:
        ms = jnp.mean(oh * oh, axis=-1, keepdims=True)
        normed.append(oh * lax.rsqrt(ms + RMS_EPS))
    return jnp.concatenate(normed, axis=-1) * gn * (g * jax.nn.sigmoid(g))


def _gla_kernel(*refs, merge):
    if merge:
        (pgf_ref, pgb_ref, waf_ref, baf_ref, wab_ref, bab_ref, s0f_ref, s0b_ref, gn_ref,
         ylo_ref, yhi_ref, st_f, st_b, keep_f, keep_b) = refs
    else:
        (pgf_ref, pgb_ref, waf_ref, baf_ref, wab_ref, bab_ref, s0f_ref, s0b_ref,
         sfin_f_ref, sfin_b_ref, st_f, st_b) = refs
    n = pl.program_id(0)
    n_chunks = pl.num_programs(0)

    @pl.when(n == 0)
    def _():
        st_f[...] = s0f_ref[...]
        st_b[...] = s0b_ref[...]

    outs_f, outs_b = _gla_chunks([(pgf_ref, waf_ref, baf_ref, st_f, False), (pgb_ref, wab_ref, bab_ref, st_b, True)])
    n_batch = pgf_ref.shape[0]
    heads = range(GLA_HEADS)
    if merge:
        half = n_chunks // 2
        m = n_chunks - 1 - n

        @pl.when(n < half)
        def _():
            for bi in range(n_batch):
                keep_f[n, bi] = jnp.concatenate(outs_f[bi], axis=-1).astype(keep_f.dtype)
                keep_b[m - half, bi] = jnp.concatenate(outs_b[bi], axis=-1).astype(keep_b.dtype)

        @pl.when(n >= half)
        def _():
            for bi in range(n_batch):
                kb = keep_b[n - half, bi].astype(f32)
                o_heads = [outs_f[bi][hd] + kb[:, hd * GLA_DV:(hd + 1) * GLA_DV] for hd in heads]
                yhi_ref[bi] = _gla_gate(o_heads, pgf_ref[bi, :, _PG_G:_PG_A].astype(f32), gn_ref[...])
                kf = keep_f[m, bi].astype(f32)
                o_heads = [kf[:, hd * GLA_DV:(hd + 1) * GLA_DV] + outs_b[bi][hd] for hd in heads]
                ylo_ref[bi] = _gla_gate(o_heads, pgb_ref[bi, :, _PG_G:_PG_A].astype(f32), gn_ref[...])
    else:
        @pl.when(n == n_chunks - 1)
        def _():
            sfin_f_ref[...] = st_f[...]
            sfin_b_ref[...] = st_b[...]


def _gla_pass(pg, wa, ba, s0, gn=None):
    n_batch, n_chunks = pg.shape[0], pg.shape[1]
    merge = gn is not None
    half = n_chunks // 2
    const2 = lambda n: (0, 0)
    const3 = lambda n: (0, 0, 0)
    chunk = lambda width, idx: pl.BlockSpec((n_batch, None, GLA_CHUNK, width), lambda n: (0, idx(n), 0, 0))
    w_specs = [pl.BlockSpec((A_LOW_PAD, D_GLA_K), const2), pl.BlockSpec((1, D_GLA_K), const2)]
    s_spec = pl.BlockSpec((n_batch, GLA_DV, D_GLA_K), const3)
    s_shape = jax.ShapeDtypeStruct((n_batch, GLA_DV, D_GLA_K), f32)
    in_specs = [chunk(D_PG, lambda n: n), chunk(D_PG, lambda n: n_chunks - 1 - n)] + w_specs + w_specs + [s_spec,
                                                                                                            s_spec]
    args = [pg, pg, wa[0], ba[0], wa[1], ba[1], s0[0], s0[1]]
    scratch = [pltpu.VMEM((n_batch, GLA_DV, D_GLA_K), f32)] * 2
    if merge:
        assert n_chunks % 2 == 0
        in_specs.append(pl.BlockSpec((1, D_GLA_V), const2))
        args.append(gn)
        out_specs = [chunk(D_GLA_V, lambda n: jnp.minimum(n_chunks - 1 - n, half - 1)),
                     chunk(D_GLA_V, lambda n: jnp.maximum(n - half, 0))]
        y_shape = jax.ShapeDtypeStruct((n_batch, half, GLA_CHUNK, D_GLA_V), f32)
        out_shape = [y_shape, y_shape]
        scratch += [pltpu.VMEM((half, n_batch, GLA_CHUNK, D_GLA_V), bf16)] * 2
    else:
        out_specs = [s_spec, s_spec]
        out_shape = [s_shape, s_shape]
    return pl.pallas_call(
        functools.partial(_gla_kernel, merge=merge),
        grid=(n_chunks,),
        in_specs=in_specs,
        out_specs=out_specs,
        out_shape=out_shape,
        scratch_shapes=scratch,
        compiler_params=_cparams(("arbitrary",)),
        name="gla_merge" if merge else "gla_scan",
    )(*args)


def _pack_rows(v):
    n = v.shape[1] // 2
    w = lax.bitcast_convert_type(v.astype(bf16).astype(f32), jnp.int32)
    return (w[:, :n] & jnp.int32(-65536)) | lax.shift_right_logical(w[:, n:], 16)


def _unpack_rows(w):
    hi = lax.bitcast_convert_type(w & jnp.int32(-65536), f32)
    lo = lax.bitcast_convert_type(lax.shift_left(w, 16), f32)
    return hi, lo


def _outproj_router_kernel(yl_ref, ygl_ref, ygr_ref, x_ref, mod_ref, wo_ref, lg_ref, lb_ref, wrh_ref, wrl_ref,
                           br_ref, x1_ref, hp_ref, ids_ref, rank_ref, wts_ref, cnt_ref, running):
    i = pl.program_id(0)

    @pl.when(i == 0)
    def _():
        running[...] = jnp.zeros_like(running)

    d = D_MODEL
    g1 = mod_ref[0, :, 2 * d:3 * d]
    sh2 = mod_ref[0, :, 3 * d:4 * d]
    sc2 = mod_ref[0, :, 4 * d:5 * d]
    n_parts = x_ref.shape[0] // OUTPROJ_PART
    tm = OUTPROJ_PART
    grid_rows = ygl_ref.shape[2] // n_parts
    tok = [slice(p * tm, (p + 1) * tm) for p in range(n_parts)]
    nt_dims = (((1,), (1,)), ((), ()))
    ne = wrh_ref.shape[0]
    expert = lax.broadcasted_iota(jnp.int32, (ne, tm), 0).astype(f32)
    neg_inf = jnp.float32(-jnp.inf)

    ygs = [jnp.concatenate([ref[0, :, r, :] for r in range(p * grid_rows, (p + 1) * grid_rows)
                            for ref in (ygl_ref, ygr_ref)], axis=0).astype(bf16) for p in range(n_parts)]
    ys = [jnp.dot(yl_ref[tok[p], :], wo_ref[0:D_LRU, :], preferred_element_type=f32)
          + jnp.dot(ygs[p], wo_ref[D_LRU:2 * D_LRU, :], preferred_element_type=f32) for p in range(n_parts)]
    hmods = []
    for p in range(n_parts):
        z = DEEPNORM_ALPHA * x_ref[tok[p], :] + g1 * ys[p]
        mu = jnp.mean(z, axis=-1, keepdims=True)
        zc = z - mu
        var = jnp.mean(zc * zc, axis=-1, keepdims=True)
        x1 = zc * lax.rsqrt(var + LN_EPS) * lg_ref[...] + lb_ref[...]
        x1_ref[tok[p], :] = x1
        hmod = x1 * (1.0 + sc2) + sh2
        hp_ref[tok[p], :] = _pack_rows(hmod)
        hmods.append(hmod)

    logits = []
    for hmod in hmods:
        h_hi = hmod.astype(bf16)
        h_lo = (hmod - h_hi.astype(f32)).astype(bf16)
        logits.append(lax.dot_general(wrh_ref[...], h_hi, nt_dims, preferred_element_type=f32)
                      + lax.dot_general(wrh_ref[...], h_lo, nt_dims, preferred_element_type=f32)
                      + lax.dot_general(wrl_ref[...], h_hi, nt_dims, preferred_element_type=f32) + br_ref[...])
    picks = []
    for live in logits:
        sel = jnp.zeros((ne, tm), f32)
        ids, vals = [], []
        for _ in range(TOP_K):
            m = jnp.max(live, axis=0, keepdims=True)
            j = jnp.min(jnp.where(live == m, expert, float(ne)), axis=0, keepdims=True)
            pick = expert == j
            sel = jnp.where(pick, 1.0, sel)
            live = jnp.where(pick, neg_inf, live)
            ids.append(j)
            vals.append(m)
        picks.append((sel, ids, vals))

    ri = lax.broadcasted_iota(jnp.int32, (tm, tm), 0)
    ci = lax.broadcasted_iota(jnp.int32, (tm, tm), 1)
    earlier = (ri < ci).astype(bf16)
    slot = lax.broadcasted_iota(jnp.int32, (8, tm), 0)
    for p, (sel, ids, vals) in enumerate(picks):
        rank_dense = running[...] + jnp.dot(sel.astype(bf16), earlier, preferred_element_type=f32)
        running[...] = running[...] + jnp.sum(sel, axis=1, keepdims=True)
        exps = [jnp.exp(vk - vals[0]) for vk in vals]
        denom = exps[0] + exps[1] + exps[2] + exps[3]
        ids_out = jnp.zeros((8, tm), f32)
        rank_out = jnp.zeros((8, tm), f32)
        wts_out = jnp.zeros((8, tm), f32)
        for kk in range(TOP_K):
            rk = jnp.sum(jnp.where(expert == ids[kk], rank_dense, 0.0), axis=0, keepdims=True)
            here = slot == kk
            ids_out = jnp.where(here, ids[kk], ids_out)
            rank_out = jnp.where(here, rk, rank_out)
            wts_out = jnp.where(here, exps[kk] / denom, wts_out)
        ids_ref[:, tok[p]] = ids_out.astype(jnp.int32)
        rank_ref[:, tok[p]] = rank_out.astype(jnp.int32)
        wts_ref[tok[p], :] = jnp.concatenate([wts_out, jnp.zeros((LANES - 8, tm), f32)], axis=0).T
    cnt_ref[...] = running[...]


def _outproj_router(yl, yg, xt, mod3, wo, lg, lb, wr_hi, wr_lo, br, tokens_per_batch):
    T = xt.shape[0]
    tm = TOKEN_TILE
    tiles_per_batch = tokens_per_batch // tm
    tokrow = lambda i: (i, 0)
    const = lambda i: (0, 0)
    half_cols = yg[0].shape[1]
    rows_per_tile = tm // (2 * half_cols)
    yg_spec = pl.BlockSpec((1, half_cols, rows_per_tile, D_GLA_V),
                           lambda i: (i // tiles_per_batch, 0, i % tiles_per_batch, 0))
    return pl.pallas_call(
        _outproj_router_kernel,
        grid=(T // tm,),
        in_specs=[pl.BlockSpec((tm, D_LRU), tokrow),
                  yg_spec, yg_spec,
                  pl.BlockSpec((tm, D_MODEL), tokrow),
                  pl.BlockSpec((1, 1, N_MOD * D_MODEL), lambda i: (i // tiles_per_batch, 0, 0)),
                  pl.BlockSpec((D_MODEL, D_MODEL), const),
                  pl.BlockSpec((1, D_MODEL), const),
                  pl.BlockSpec((1, D_MODEL), const),
                  pl.BlockSpec((N_EXPERTS, D_MODEL), const),
                  pl.BlockSpec((N_EXPERTS, D_MODEL), const),
                  pl.BlockSpec((N_EXPERTS, 1), const)],
        out_specs=[pl.BlockSpec((tm, D_MODEL), tokrow),
                   pl.BlockSpec((tm, D_MODEL // 2), tokrow),
                   pl.BlockSpec((8, tm), lambda i: (0, i)),
                   pl.BlockSpec((8, tm), lambda i: (0, i)),
                   pl.BlockSpec((tm, LANES), tokrow),
                   pl.BlockSpec((N_EXPERTS, 1), const)],
        out_shape=[jax.ShapeDtypeStruct((T, D_MODEL), f32),
                   jax.ShapeDtypeStruct((T, D_MODEL // 2), jnp.int32),
                   jax.ShapeDtypeStruct((8, T), jnp.int32),
                   jax.ShapeDtypeStruct((8, T), jnp.int32),
                   jax.ShapeDtypeStruct((T, LANES), f32),
                   jax.ShapeDtypeStruct((N_EXPERTS, 1), f32)],
        scratch_shapes=[pltpu.VMEM((N_EXPERTS, 1), f32)],
        compiler_params=_cparams(("arbitrary",)),
        name="outproj_router",
    )(yl, yg[0], yg[1], xt, mod3, wo, lg, lb, wr_hi, wr_lo, br)


def _route_kernel(ids_ref, rank_ref, cnt_ref, dest_ref, tiles_ref, *, bm):
    ne = cnt_ref.shape[0]
    cnt = cnt_ref[...]
    padded = jnp.floor((cnt + (bm - 1.0)) * (1.0 / bm)) * bm
    sub = lax.broadcasted_iota(jnp.int32, (ne, LANES), 0)
    lane = lax.broadcasted_iota(jnp.int32, (ne, LANES), 1)
    padded_row = jnp.sum(jnp.where(sub == lane, padded, 0.0), axis=0, keepdims=True)
    base = jnp.sum(jnp.where(lane < sub, padded_row, 0.0), axis=1, keepdims=True)
    ends = base + padded

    tc = ids_ref.shape[1]
    expert = lax.broadcasted_iota(jnp.int32, (ne, tc), 0)
    slot = lax.broadcasted_iota(jnp.int32, (8, tc), 0)
    ids = ids_ref[...]
    group_base = jnp.zeros((8, tc), f32)
    for kk in range(TOP_K):
        b_k = jnp.sum(jnp.where(expert == ids[kk:kk + 1], base, 0.0), axis=0, keepdims=True)
        group_base = jnp.where(slot == kk, b_k, group_base)
    dest_ref[...] = group_base.astype(jnp.int32) + rank_ref[...]

    nl = tiles_ref.shape[1]
    start = lax.broadcasted_iota(jnp.int32, (ne, nl), 1).astype(f32) * bm
    te = jnp.minimum(jnp.sum(jnp.where(start >= ends, 1.0, 0.0), axis=0, keepdims=True), ne - 1.0)
    at_te = lax.broadcasted_iota(jnp.int32, (ne, nl), 0).astype(f32) == te
    cnt_te = jnp.sum(jnp.where(at_te, cnt, 0.0), axis=0, keepdims=True)
    base_te = jnp.sum(jnp.where(at_te, base, 0.0), axis=0, keepdims=True)
    valid = jnp.clip(cnt_te - (start[0:1] - base_te), 0.0, float(bm))
    next_group = jnp.sum(jnp.where(at_te, ends, 0.0), axis=0, keepdims=True) * (1.0 / bm)
    last_used = jnp.sum(padded, axis=0, keepdims=True) * (1.0 / bm) - 1.0
    srow = lax.broadcasted_iota(jnp.int32, (8, nl), 0)
    table = jnp.where(srow == 0, te, jnp.where(srow == 1, valid, jnp.where(srow == 2, next_group,
                                                                           jnp.where(srow == 3, last_used, 0.0))))
    tiles_ref[...] = table.astype(jnp.int32)


def _route(ids, rank, cnt, bm, n_tiles):
    T = ids.shape[1]
    tc = 2048
    nl = -(-n_tiles // LANES) * LANES
    tok = lambda i: (0, i)
    const = lambda i: (0, 0)
    return pl.pallas_call(
        functools.partial(_route_kernel, bm=bm),
        grid=(T // tc,),
        in_specs=[pl.BlockSpec((8, tc), tok), pl.BlockSpec((8, tc), tok),
                  pl.BlockSpec((N_EXPERTS, 1), const)],
        out_specs=[pl.BlockSpec((8, tc), tok), pl.BlockSpec((8, nl), const)],
        out_shape=[jax.ShapeDtypeStruct((8, T), jnp.int32), jax.ShapeDtypeStruct((8, nl), jnp.int32)],
        compiler_params=_cparams(("arbitrary",)),
        name="route",
    )(ids, rank, cnt)


def _sc_workers():
    info = plsc.get_sparse_core_info()
    return info.num_cores, info.num_subcores


def _sc_dispatch(rows, dest_flat, n_out):
    T, D = rows.shape
    nc, ns = _sc_workers()
    per_w = T // (nc * ns)
    n_chunks = per_w // SC_CHUNK
    assert n_chunks % 2 == 0
    mesh = plsc.VectorSubcoreMesh(core_axis_name="c", subcore_axis_name="s")

    @functools.partial(
        pl.kernel, mesh=mesh,
        out_type=jax.ShapeDtypeStruct((n_out, D), rows.dtype),
        scratch_types=([pltpu.VMEM((SC_CHUNK,), jnp.int32)] * (2 * TOP_K)
                       + [pltpu.VMEM((SC_CHUNK, D), rows.dtype)] * 2
                       + [pltpu.SemaphoreType.DMA] * (2 * TOP_K)),
    )
    def k(rows_hbm, dest_hbm, out_hbm, *scratch):
        idx_v = (scratch[:TOP_K], scratch[TOP_K:2 * TOP_K])
        rows_v = scratch[2 * TOP_K:2 * TOP_K + 2]
        sems = (scratch[2 * TOP_K + 2:3 * TOP_K + 2], scratch[3 * TOP_K + 2:])
        wid = lax.axis_index("s") * nc + lax.axis_index("c")
        base = wid * per_w

        def load(chunk, b):
            off = base + chunk * SC_CHUNK
            pltpu.sync_copy(rows_hbm.at[pl.ds(off, SC_CHUNK)], rows_v[b])
            for kk in range(TOP_K):
                pltpu.sync_copy(dest_hbm.at[pl.ds(kk * T + off, SC_CHUNK)], idx_v[b][kk])

        load(0, 0)

        @pl.loop(0, n_chunks, step=2)
        def _(j):
            for b in range(2):
                copies = [pltpu.async_copy(rows_v[b], out_hbm.at[idx_v[b][kk]], sems[b][kk])
                          for kk in range(TOP_K)]

                @pl.when(j + b + 1 < n_chunks)
                def _():
                    load(j + b + 1, 1 - b)

                for cp in copies:
                    cp.wait()

    return k(rows, dest_flat)


def _sc_gather(table, idx):
    _, D = table.shape
    N = idx.shape[0]
    nc, ns = _sc_workers()
    per_w = N // (nc * ns)
    n_chunks = per_w // SC_CHUNK
    assert n_chunks % 2 == 0
    mesh = plsc.VectorSubcoreMesh(core_axis_name="c", subcore_axis_name="s")

    @functools.partial(
        pl.kernel, mesh=mesh,
        out_type=jax.ShapeDtypeStruct((N, D), table.dtype),
        scratch_types=([pltpu.VMEM((SC_CHUNK,), jnp.int32)] * 2
                       + [pltpu.VMEM((SC_CHUNK, D), table.dtype)] * 2
                       + [pltpu.SemaphoreType.DMA] * 2),
    )
    def k(table_hbm, idx_hbm, out_hbm, idx0, idx1, buf0, buf1, sem0, sem1):
        idxs, bufs, sems = (idx0, idx1), (buf0, buf1), (sem0, sem1)
        wid = lax.axis_index("s") * nc + lax.axis_index("c")
        base = wid * per_w

        def gather(b):
            return pltpu.make_async_copy(table_hbm.at[idxs[b]], bufs[b], sems[b])

        def start(chunk, b):
            pltpu.sync_copy(idx_hbm.at[pl.ds(base + chunk * SC_CHUNK, SC_CHUNK)], idxs[b])
            gather(b).start()

        def finish(chunk, b):
            gather(b).wait()
            pltpu.sync_copy(bufs[b], out_hbm.at[pl.ds(base + chunk * SC_CHUNK, SC_CHUNK)])

        start(0, 0)

        @pl.loop(0, n_chunks, step=2)
        def _(j):
            start(j + 1, 1)
            finish(j, 0)

            @pl.when(j + 2 < n_chunks)
            def _():
                start(j + 2, 0)

            finish(j + 1, 1)

    return k(table, idx)


def _ffn_kernel(te_ref, tv_ref, tn_ref, tl_ref, xs_ref, wg_hbm, bg_ref, wu_hbm, bu_ref, wd_hbm, bd_ref, eo_ref,
                w_buf, sems, slot_ref):
    i = pl.program_id(0)
    n_tiles = pl.num_programs(0)
    w_hbm = (wg_hbm, wu_hbm, wd_hbm)

    def fetch(e, slot):
        return [pltpu.make_async_copy(w_hbm[m].at[e], w_buf.at[slot, m], sems.at[slot, m]) for m in range(3)]

    @pl.when(i == 0)
    def _():
        slot_ref[0] = 0
        for cp in fetch(te_ref[0], 0):
            cp.start()

    prev = te_ref[jnp.maximum(i - 1, 0)]
    first = (i == 0) | (te_ref[i] != prev)

    @pl.when(first & (i > 0))
    def _():
        slot_ref[0] = 1 - slot_ref[0]

    slot = slot_ref[0]

    @pl.when(first)
    def _():
        for cp in fetch(te_ref[i], slot):
            cp.wait()
        nxt = tn_ref[i]
        e_nxt = te_ref[jnp.minimum(nxt, n_tiles - 1)]

        @pl.when((nxt > i) & (nxt < n_tiles) & (e_nxt != te_ref[i]))
        def _():
            for cp in fetch(e_nxt, 1 - slot):
                cp.start()

    wg_b, wu_b, wd_b = (w_buf.at[slot, m] for m in range(3))
    valid = tv_ref[i]

    def ffn_rows(r0, m):
        blocks = [(r0 + o, min(MOE_BLOCK, m)) for o in range(0, m, MOE_BLOCK)]
        xs, gates, ups, acts = [], [], [], []
        for b0, bm_ in blocks:
            row = lax.broadcasted_iota(jnp.int32, (bm_, 1), 0) + b0
            xw = jnp.where(row < valid, xs_ref[b0:b0 + bm_, :], 0)
            x_hi, x_lo = _unpack_rows(xw)
            xs.append(jnp.concatenate([x_hi, x_lo], axis=1).astype(bf16))
        for x in xs:
            gates.append(jnp.minimum(jnp.dot(x, wg_b[...], preferred_element_type=f32) + bg_ref[...],
                                     SWIGLU_LIMIT))
            ups.append(jnp.clip(jnp.dot(x, wu_b[...], preferred_element_type=f32) + bu_ref[...],
                                -SWIGLU_LIMIT, SWIGLU_LIMIT))
        for gate, up in zip(gates, ups):
            acts.append(((up + 1.0) * gate * jax.nn.sigmoid(SWIGLU_ALPHA * gate)).astype(bf16))
        for (b0, bm_), act in zip(blocks, acts):
            out = jnp.dot(act, wd_b[...], preferred_element_type=f32) + bd_ref[...]
            eo_ref[b0:b0 + bm_, :] = _pack_rows(out)

    def zero_rows(r0, m):
        eo_ref[r0:r0 + m, :] = jnp.zeros((m, eo_ref.shape[1]), eo_ref.dtype)

    for r0 in range(0, xs_ref.shape[0], MOE_PASS):
        lo = 0
        for m in MOE_PASS_SIZES:
            @pl.when((valid > r0 + lo) & ((valid <= r0 + m) | (m == MOE_PASS)))
            def _(r0=r0, m=m):
                ffn_rows(r0, m)
                if m < MOE_PASS:
                    zero_rows(r0 + m, MOE_PASS - m)
            lo = m

        @pl.when(valid <= r0)
        def _(r0=r0):
            zero_rows(r0, MOE_PASS)


def _expert_ffn(tile_expert, tile_valid, tile_next, last_used, xs, w_gate, b_gate, w_up, b_up, w_down, b_down):
    n_rows, dp = xs.shape
    d = 2 * dp
    bm = MOE_TILE
    d_e = w_gate.shape[-1]
    assert d == d_e
    bspec = lambda n_: pl.BlockSpec((None, 1, n_), lambda i, te, tv, tn, tl: (te[i], 0, 0))
    hbm = pl.BlockSpec(memory_space=pl.ANY)
    grid_spec = pltpu.PrefetchScalarGridSpec(
        num_scalar_prefetch=4,
        grid=(n_rows // bm,),
        in_specs=[pl.BlockSpec((bm, dp), lambda i, te, tv, tn, tl: (jnp.minimum(i, tl[0]), 0)),
                  hbm, bspec(d_e), hbm, bspec(d_e), hbm, bspec(d)],
        out_specs=pl.BlockSpec((bm, dp), lambda i, te, tv, tn, tl: (i, 0)),
        scratch_shapes=[pltpu.VMEM((2, 3, d, d_e), f32),
                        pltpu.SemaphoreType.DMA((2, 3)),
                        pltpu.SMEM((1,), jnp.int32)],
    )
    return pl.pallas_call(
        _ffn_kernel,
        grid_spec=grid_spec,
        out_shape=jax.ShapeDtypeStruct((n_rows, dp), jnp.int32),
        compiler_params=_cparams(("arbitrary",)),
        name="expert_ffn",
    )(tile_expert, tile_valid, tile_next, last_used, xs, w_gate, b_gate.reshape(N_EXPERTS, 1, d_e), w_up,
      b_up.reshape(N_EXPERTS, 1, d_e), w_down, b_down.reshape(N_EXPERTS, 1, d))


def _combine_kernel(eg_ref, wts_ref, x1_ref, mod_ref, lg_ref, lb_ref, o_ref):
    w = wts_ref[...]
    y_hi, y_lo = _unpack_rows(eg_ref[0])
    y_hi, y_lo = y_hi * w[:, 0:1], y_lo * w[:, 0:1]
    for kk in range(1, TOP_K):
        e_hi, e_lo = _unpack_rows(eg_ref[kk])
        y_hi = y_hi + e_hi * w[:, kk:kk + 1]
        y_lo = y_lo + e_lo * w[:, kk:kk + 1]
    y = jnp.concatenate([y_hi, y_lo], axis=1)
    z = DEEPNORM_ALPHA * x1_ref[...] + mod_ref[0] * y
    mu = jnp.mean(z, axis=-1, keepdims=True)
    zc = z - mu
    var = jnp.mean(zc * zc, axis=-1, keepdims=True)
    o_ref[...] = zc * lax.rsqrt(var + LN_EPS) * lg_ref[...] + lb_ref[...]


def _combine_ln(eg, wts, x1, mod3, lg, lb, tokens_per_batch):
    T = x1.shape[0]
    tm = TOKEN_TILE
    tiles_per_batch = tokens_per_batch // tm
    const = lambda i: (0, 0)
    return pl.pallas_call(
        _combine_kernel,
        grid=(T // tm,),
        in_specs=[pl.BlockSpec((TOP_K, tm, D_MODEL // 2), lambda i: (0, i, 0)),
                  pl.BlockSpec((tm, LANES), lambda i: (i, 0)),
                  pl.BlockSpec((tm, D_MODEL), lambda i: (i, 0)),
                  pl.BlockSpec((1, 1, D_MODEL), lambda i: (i // tiles_per_batch, 0, N_MOD - 1)),
                  pl.BlockSpec((1, D_MODEL), const),
                  pl.BlockSpec((1, D_MODEL), const)],
        out_specs=pl.BlockSpec((tm, D_MODEL), lambda i: (i, 0)),
        out_shape=jax.ShapeDtypeStruct((T, D_MODEL), f32),
        compiler_params=_cparams(("arbitrary",)),
        name="combine_ln",
    )(eg, wts, x1, mod3, lg, lb)


def _block_diag(w):
    n, c, d = w.shape
    eye = jnp.eye(n, dtype=w.dtype)
    return jnp.einsum('ncd,nm->ncmd', w, eye).reshape(n * c, n * d)


def kernel(x, c, ctx, c_ctx, w_ada, b_ada, w_in, conv_w, conv_b, lru_wa, lru_ba, lru_wx, lru_bx,
           lru_lam, gla_wa, gla_ba, gla_norm_g, w_out, ln1_g, ln1_b, w_router, b_router, w_gate,
           b_gate, w_up, b_up, w_down, b_down, ln2_g, ln2_b):
    B, L, D = x.shape
    Lc = ctx.shape[1]
    T = B * L
    rows = L // GRID_W
    l = 0

    cpad = jnp.zeros((8, D), f32).at[0:B].set(c).at[B].set(c_ctx)
    w_cat = jnp.pad(w_in[l], ((0, 0), (0, D_PROJ - w_in.shape[-1]))).astype(bf16)
    wg = [jnp.concatenate([_block_diag(lru_wa[l, d]), _block_diag(lru_wx[l, d])], axis=1).astype(bf16)
          for d in range(2)]
    bg = [jnp.concatenate([lru_ba[l, d], lru_bx[l, d]])[None] for d in range(2)]
    lam = [lru_lam[l, d][None] for d in range(2)]
    wa = [jnp.pad(gla_wa[l, d], ((0, A_LOW_PAD - GATE_RANK), (0, 0))).astype(bf16) for d in range(2)]
    ba = [gla_ba[l, d][None] for d in range(2)]
    cw, cb = conv_w[l], conv_b[l][None]
    wr_t = w_router[l].T
    wr_hi = wr_t.astype(bf16)
    wr_lo = (wr_t - wr_hi.astype(f32)).astype(bf16)
    br = b_router[l][:, None]
    c_xl, c_gl, c_pg = (0, D_LRU), (D_LRU, 2 * D_LRU), (2 * D_LRU, D_PROJ)

    mod3 = _ada_mod(cpad, w_ada[l], b_ada[l]).reshape(8, 1, N_MOD * D)

    tiles_per_batch = L // INPROJ_TILE
    xl_c, pg_c = _inproj_rows(ctx.reshape(B * Lc, D), mod3, w_cat, lambda i: B, (c_xl, c_pg), (f32, bf16))
    zero_h = jnp.zeros((B, 1, D_LRU), f32)
    zero_s = jnp.zeros((B, GLA_DV, D_GLA_K), f32)
    pg_c = pg_c.reshape(B, Lc // GLA_CHUNK, GLA_CHUNK, D_PG)
    h_ctx = []
    for d in range(2):
        _, _, hf = _lru_pass(xl_c, wg[d], bg[d], lam[d], zero_h, n_batch=B, tile=Lc, reverse=bool(d),
                             conv=(cw, cb, Lc))
        h_ctx.append(hf)
    s_ctx = _gla_pass(pg_c, wa, ba, (zero_s, zero_s))

    xt = x.reshape(T, D)
    xl, gl = _inproj_rows(xt, mod3, w_cat, lambda i: i // tiles_per_batch, (c_xl, c_gl), (f32, bf16))
    hb, u_lru, _ = _lru_pass(xl, wg[1], bg[1], lam[1], h_ctx[1], n_batch=B, tile=LRU_TILE, reverse=True,
                             conv=(cw, cb, GRID_W))
    y_lru, _ = _lru_pass(u_lru, wg[0], bg[0], lam[0], h_ctx[0], n_batch=B, tile=LRU_TILE, reverse=False,
                         merge_with=(hb, gl))
    pg = _inproj_cols(x.reshape(B, rows, GRID_W, D), mod3, w_cat, c_pg)
    y_gla = _gla_pass(pg, wa, ba, s_ctx, gn=gla_norm_g[l][None])

    x1, hp, ids, rank, wts, cnt = _outproj_router(
        y_lru, y_gla, xt, mod3, w_out[l].astype(bf16), ln1_g[l][None], ln1_b[l][None], wr_hi, wr_lo, br, L)

    n_tiles = T * TOP_K // MOE_TILE + N_EXPERTS
    dest, tiles = _route(ids, rank, cnt, MOE_TILE, n_tiles)
    xs = _sc_dispatch(hp, dest.reshape(-1), n_tiles * MOE_TILE)
    eo = _expert_ffn(tiles[0, :n_tiles], tiles[1, :n_tiles], tiles[2, :n_tiles], tiles[3, :1], xs, w_gate[l],
                     b_gate[l], w_up[l], b_up[l], w_down[l], b_down[l])
    eg = _sc_gather(eo, dest[:TOP_K].reshape(-1)).reshape(TOP_K, T, D // 2)
    out = _combine_ln(eg, wts, x1, mod3, ln2_g[l][None], ln2_b[l][None], L)
    return out.reshape(B, L, D)
```

```python
import functools

import jax
import jax.numpy as jnp
from jax import lax
from jax.experimental import pallas as pl
from jax.experimental.pallas import tpu as pltpu
from jax.experimental.pallas import tpu_sc as plsc

D_MODEL = 1024
DEPTH = 1
GRID_W = 64
D_LRU = 512
CONV_W = 4
LRU_C = 8.0
GLA_HEADS = 4
D_GLA_V = 512
D_GLA_K = 256
GLA_DK = 64
GLA_DV = 128
GATE_RANK = 16
GATE_TAU = 16.0
GLA_CHUNK = 64
N_EXPERTS = 32
TOP_K = 4
SWIGLU_LIMIT = 7.0
SWIGLU_ALPHA = 1.702
N_MOD = 6
DEEPNORM_ALPHA = (2.0 * DEPTH) ** 0.25
LN_EPS = 1e-5
RMS_EPS = 1e-6

LANES = 128
A_LOW_PAD = LANES
D_PG = 2 * D_GLA_K + 2 * D_GLA_V + A_LOW_PAD
D_PROJ = 2 * D_LRU + D_PG
TOKEN_TILE = 1024
INPROJ_TILE = 1024
LRU_TILE = 512
MOE_TILE = 1024
MOE_PASS = 512
MOE_PASS_SIZES = (128, 256, MOE_PASS)
MOE_BLOCK = 256
OUTPROJ_PART = 128
RING_SLOTS = 3
SC_CHUNK = 64
VMEM_LIMIT = 48 * 1024 * 1024

f32 = jnp.float32
bf16 = jnp.bfloat16


def _cparams(sem):
    return pltpu.CompilerParams(dimension_semantics=sem, vmem_limit_bytes=VMEM_LIMIT)


def _ada_kernel(c_ref, w_ref, b_ref, o_ref):
    s = c_ref[...]
    s = s * jax.nn.sigmoid(s)
    o_ref[...] = jnp.dot(s.astype(bf16), w_ref[...], preferred_element_type=f32) + b_ref[...]


def _ada_mod(cpad, w, b):
    n = w.shape[1]
    tn = 1024
    return pl.pallas_call(
        _ada_kernel,
        grid=(n // tn,),
        in_specs=[pl.BlockSpec((8, D_MODEL), lambda j: (0, 0)),
                  pl.BlockSpec((D_MODEL, tn), lambda j: (0, j)),
                  pl.BlockSpec((1, tn), lambda j: (0, j))],
        out_specs=pl.BlockSpec((8, tn), lambda j: (0, j)),
        out_shape=jax.ShapeDtypeStruct((8, n), f32),
        compiler_params=_cparams(("arbitrary",)),
        name="ada_mod",
    )(cpad, w, b.reshape(1, n))


def _ring_tile(x_hbm, ring, sems, i, n_steps):
    n_slots, tile_rows = ring.shape[0], ring.shape[1]

    def fetch(step):
        s = step % n_slots
        return pltpu.make_async_copy(x_hbm.at[pl.ds(step * tile_rows, tile_rows), :], ring.at[s], sems.at[s])

    @pl.when(i == 0)
    def _():
        for k in range(n_slots - 1):
            @pl.when(k < n_steps)
            def _(k=k):
                fetch(k).start()

    @pl.when(i + n_slots - 1 < n_steps)
    def _():
        fetch(i + n_slots - 1).start()

    fetch(i).wait()
    return ring.at[i % n_slots]


def _inproj_kernel(x_hbm, mod_ref, w_ref, *refs, parts):
    out_refs, (x_ring, x_sems) = refs[:len(parts)], refs[len(parts):]
    x_ref = _ring_tile(x_hbm, x_ring, x_sems, pl.program_id(0), pl.num_programs(0))
    sh = mod_ref[0, :, 0:D_MODEL]
    sc = mod_ref[0, :, D_MODEL:2 * D_MODEL]
    u = (x_ref[...] * (1.0 + sc) + sh).astype(bf16)
    for (lo, hi), o_ref in zip(parts, out_refs):
        o_ref[...] = jnp.dot(u, w_ref[:, lo:hi], preferred_element_type=f32).astype(o_ref.dtype)


def _inproj_cols_kernel(x_hbm, mod_ref, w_ref, o_ref, xbuf, sems, *, part):
    b, n = pl.program_id(0), pl.program_id(1)
    nb, nn = pl.num_programs(0), pl.num_programs(1)
    cb = xbuf.shape[1]
    g = b * nn + n
    slot = g % 2

    def fetch(bb, nb_, s):
        return [pltpu.make_async_copy(x_hbm.at[bb, :, nb_ * cb + j, :], xbuf.at[s, j], sems.at[s])
                for j in range(cb)]

    @pl.when(g == 0)
    def _():
        for cp in fetch(b, n, slot):
            cp.start()

    @pl.when(g + 1 < nb * nn)
    def _():
        g1 = g + 1
        for cp in fetch(g1 // nn, g1 % nn, 1 - slot):
            cp.start()

    for cp in fetch(b, n, slot):
        cp.wait()
    sh = mod_ref[0, :, 0:D_MODEL]
    sc = mod_ref[0, :, D_MODEL:2 * D_MODEL]
    x = xbuf[slot].reshape(cb * xbuf.shape[2], D_MODEL)
    u = (x * (1.0 + sc) + sh).astype(bf16)
    lo, hi = part
    p = jnp.dot(u, w_ref[:, lo:hi], preferred_element_type=f32).astype(o_ref.dtype)
    o_ref[...] = p.reshape(o_ref.shape)


def _inproj_rows(xt, mod3, w_cat, batch_of_tile, parts, dtypes):
    T = xt.shape[0]
    tm = INPROJ_TILE
    return pl.pallas_call(
        functools.partial(_inproj_kernel, parts=parts),
        grid=(T // tm,),
        in_specs=[pl.BlockSpec(memory_space=pl.ANY),
                  pl.BlockSpec((1, 1, 2 * D_MODEL), lambda i: (batch_of_tile(i), 0, 0)),
                  pl.BlockSpec((D_MODEL, D_PROJ), lambda i: (0, 0))],
        out_specs=[pl.BlockSpec((tm, hi - lo), lambda i: (i, 0)) for lo, hi in parts],
        out_shape=[jax.ShapeDtypeStruct((T, hi - lo), dt) for (lo, hi), dt in zip(parts, dtypes)],
        scratch_shapes=[pltpu.VMEM((RING_SLOTS, tm, D_MODEL), f32), pltpu.SemaphoreType.DMA((RING_SLOTS,))],
        compiler_params=_cparams(("arbitrary",)),
        name="inproj_rows",
    )(xt, mod3, w_cat)


def _inproj_cols(x4, mod3, w_cat, part):
    n_batch, rows, cols, _ = x4.shape
    lo, hi = part
    cb = INPROJ_TILE // rows
    return pl.pallas_call(
        functools.partial(_inproj_cols_kernel, part=part),
        grid=(n_batch, cols // cb),
        in_specs=[pl.BlockSpec(memory_space=pl.ANY),
                  pl.BlockSpec((1, 1, 2 * D_MODEL), lambda b, n: (b, 0, 0)),
                  pl.BlockSpec((D_MODEL, D_PROJ), lambda b, n: (0, 0))],
        out_specs=pl.BlockSpec((1, cb, rows, hi - lo), lambda b, n: (b, n, 0, 0)),
        out_shape=jax.ShapeDtypeStruct((n_batch, cols, rows, hi - lo), bf16),
        scratch_shapes=[pltpu.VMEM((2, cb, rows, D_MODEL), f32), pltpu.SemaphoreType.DMA((2,))],
        compiler_params=_cparams(("arbitrary", "arbitrary")),
        name="inproj_cols",
    )(x4, mod3, w_cat)


def _gelu_tanh(x):
    return 0.5 * x * (1.0 + jnp.tanh(0.7978845608028654 * (x + 0.044715 * (x * x * x))))


def _lru_kernel(*refs, row_w, reverse, merge):
    if merge:
        u_ref, wg_ref, bg_ref, lam_ref, h0_ref, hb_ref, gl_ref, out_ref, hfin_ref, carry, h_nat = refs
    else:
        (xl_hbm, cw_ref, cb_ref, wg_ref, bg_ref, lam_ref, h0_ref, out_ref, u_out_ref, hfin_ref, carry,
         x_buf, x_sems) = refs
    t = pl.program_id(1)

    @pl.when(t == 0)
    def _():
        carry[...] = h0_ref[0]

    n_seg = 8
    blk = lambda v, j: v[j * n_seg:(j + 1) * n_seg]
    if merge:
        u = u_ref[...]
        tt, ch = u.shape
        seg = tt // n_seg
    else:
        _, seg, _, ch = x_buf.shape
        tt = n_seg * seg
        nb, nt = pl.num_programs(0), pl.num_programs(1)
        g = pl.program_id(0) * nt + t
        slot = g % 2

        def fetch(step, s_):
            bb, tt_ = step // nt, step % nt
            tile = bb * nt + (nt - 1 - tt_ if reverse else tt_)
            return [pltpu.make_async_copy(xl_hbm.at[pl.ds(tile * tt + s * seg, seg), :], x_buf.at[s_, :, s, :],
                                          x_sems.at[s_]) for s in range(n_seg)]

        @pl.when(g == 0)
        def _():
            for cp in fetch(g, slot):
                cp.start()

        @pl.when(g + 1 < nb * nt)
        def _():
            for cp in fetch(g + 1, 1 - slot):
                cp.start()

        for cp in fetch(g, slot):
            cp.wait()
        x = x_buf[slot].reshape(tt, ch)
        segs_per_row = row_w // seg
        s_idx = lax.broadcasted_iota(jnp.int32, (n_seg, ch), 0)
        has_prev = (s_idx % segs_per_row) != 0
        has_next = (s_idx % segs_per_row) != segs_per_row - 1
        from_prev = lambda v: jnp.where(has_prev, pltpu.roll(v, 1, 0), 0.0)
        from_next = lambda v: jnp.where(has_next, pltpu.roll(v, n_seg - 1, 0), 0.0)
        x_m1 = jnp.concatenate([from_prev(blk(x, seg - 1)), x[:tt - n_seg]], axis=0)
        x_m2 = jnp.concatenate([from_prev(blk(x, seg - 2)), from_prev(blk(x, seg - 1)), x[:tt - 2 * n_seg]],
                               axis=0)
        x_p1 = jnp.concatenate([x[n_seg:], from_next(blk(x, 0))], axis=0)
        cw = cw_ref[...]
        u = cb_ref[...] + x_m2 * cw[0:1] + x_m1 * cw[1:2] + x * cw[2:3] + x_p1 * cw[3:4]
        u_out_ref[...] = u
    rows_q = tt // 4
    g_parts = [jnp.dot(u[q * rows_q:(q + 1) * rows_q].astype(bf16), wg_ref[...], preferred_element_type=f32)
               + bg_ref[...] for q in range(4)]
    lam = lam_ref[...]
    softplus_neg = jnp.maximum(-lam, 0.0) + jnp.log(1.0 + jnp.exp(-jnp.abs(lam)))
    log_a_scale = -LRU_C * softplus_neg

    def recurrence_terms(j):
        q, r = divmod(j * n_seg, rows_q)
        g_j, u_j = g_parts[q][r:r + n_seg], blk(u, j)
        log_a = log_a_scale * jax.nn.sigmoid(g_j[:, 0:ch])
        a_j = jnp.exp(log_a)
        series = log_a * (-2.0 + log_a * (-2.0 + log_a * (-4.0 / 3.0)))
        one_minus_a2 = jnp.where(log_a > -0.005, series, 1.0 - a_j * a_j)
        root = jnp.where(one_minus_a2 > 0.0, one_minus_a2 * lax.rsqrt(one_minus_a2), 0.0)
        return a_j, root * (jax.nn.sigmoid(g_j[:, ch:2 * ch]) * u_j)

    half = seg // 2
    h_loc, a_cum = [None] * seg, [None] * seg
    h_run, a_run = [None, None], [None, None]
    for i in range(half):
        for p in range(2):
            j = p * half + (half - 1 - i if reverse else i)
            a_j, b_j = recurrence_terms(j)
            h_run[p] = b_j if i == 0 else a_j * h_run[p] + b_j
            a_run[p] = a_j if i == 0 else a_j * a_run[p]
            h_loc[j], a_cum[j] = h_run[p], a_run[p]
    pieces = [(s, p) for s in range(n_seg) for p in range(2)]
    if reverse:
        pieces.reverse()
    entering = [[None] * n_seg, [None] * n_seg]
    state = carry[...]
    for s, p in pieces:
        entering[p][s] = state
        state = h_run[p][s:s + 1] + a_run[p][s:s + 1] * state
    carry[...] = state
    hfin_ref[0] = state
    enter = [jnp.concatenate(e, axis=0) for e in entering]
    h_rows = [h_loc[j] + a_cum[j] * enter[j // half] for j in range(seg)]
    if merge:
        for j in range(seg):
            h_nat[:, j, :] = h_rows[j] + blk(hb_ref, j)
        h = h_nat[...].reshape(tt, ch)
        out_ref[...] = (h * _gelu_tanh(gl_ref[...].astype(f32))).astype(out_ref.dtype)
    else:
        out_ref[...] = jnp.concatenate(h_rows, axis=0)


def _lru_pass(x_in, wg, bg, lam, h0, *, n_batch, tile, reverse, conv=None, merge_with=None):
    T = x_in.shape[0]
    nt = T // n_batch // tile
    merge = merge_with is not None
    assert merge != (conv is not None)
    seg = tile // 8

    def tok(b, t):
        return (b * nt + (nt - 1 - t if reverse else t), 0)

    tile_spec = pl.BlockSpec((tile, D_LRU), tok)
    const = lambda b, t: (0, 0)
    gate_specs = [pl.BlockSpec((D_LRU, 2 * D_LRU), const),
                  pl.BlockSpec((1, 2 * D_LRU), const),
                  pl.BlockSpec((1, D_LRU), const),
                  pl.BlockSpec((1, 1, D_LRU), lambda b, t: (b, 0, 0))]
    state_spec = pl.BlockSpec((1, 1, D_LRU), lambda b, t: (b, 0, 0))
    state_shape = jax.ShapeDtypeStruct((n_batch, 1, D_LRU), f32)
    scratch = [pltpu.VMEM((1, D_LRU), f32)]
    if merge:
        hb, gl = merge_with
        row_w = None
        in_specs = [tile_spec] + gate_specs + [tile_spec, tile_spec]
        args = [x_in, wg, bg, lam, h0, hb, gl]
        scratch += [pltpu.VMEM((8, seg, D_LRU), f32)]
        out_specs = [tile_spec, state_spec]
        out_shape = [jax.ShapeDtypeStruct((T, D_LRU), bf16), state_shape]
    else:
        conv_w, conv_b, row_w = conv
        assert row_w % seg == 0 and seg >= 2
        in_specs = [pl.BlockSpec(memory_space=pl.ANY),
                    pl.BlockSpec((CONV_W, D_LRU), const),
                    pl.BlockSpec((1, D_LRU), const)] + gate_specs
        args = [x_in, conv_w, conv_b, wg, bg, lam, h0]
        scratch += [pltpu.VMEM((2, seg, 8, D_LRU), f32), pltpu.SemaphoreType.DMA((2,))]
        out_specs = [tile_spec, tile_spec, state_spec]
        out_shape = [jax.ShapeDtypeStruct((T, D_LRU), f32), jax.ShapeDtypeStruct((T, D_LRU), f32), state_shape]
    return pl.pallas_call(
        functools.partial(_lru_kernel, row_w=row_w, reverse=reverse, merge=merge),
        grid=(n_batch, nt),
        in_specs=in_specs,
        out_specs=out_specs,
        out_shape=out_shape,
        scratch_shapes=scratch,
        compiler_params=_cparams(("arbitrary", "arbitrary")),
        name="lru_merge" if merge else "lru_scan",
    )(*args)


_PG_Q, _PG_K, _PG_V = 0, D_GLA_K, 2 * D_GLA_K
_PG_G, _PG_A = 2 * D_GLA_K + D_GLA_V, 2 * D_GLA_K + 2 * D_GLA_V


def _gla_chunks(streams):
    ck = GLA_CHUNK
    nh = GLA_HEADS
    head_of_lane = lax.broadcasted_iota(jnp.int32, (1, D_GLA_K), 1) // GLA_DK
    nt_dims = (((1,), (1,)), ((), ()))
    tn_dims = (((0,), (0,)), ((), ()))
    ii = lax.broadcasted_iota(jnp.int32, (nh * ck, ck), 0) & (ck - 1)
    jj = lax.broadcasted_iota(jnp.int32, (nh * ck, ck), 1)

    units = []
    for pg_ref, wa_ref, ba_ref, state, reverse in streams:
        n_batch = pg_ref.shape[0]
        rows_all = n_batch * ck
        a_low = jnp.concatenate([pg_ref[bi, :, _PG_A:_PG_A + A_LOW_PAD] for bi in range(n_batch)], axis=0)
        z = jnp.dot(a_low, wa_ref[...], preferred_element_type=f32) + ba_ref[...]
        bcum_all = (jnp.minimum(z, 0.0) - jnp.log(1.0 + jnp.exp(-jnp.abs(z)))) * (1.0 / GATE_TAU)
        step = lax.broadcasted_iota(jnp.int32, (rows_all, D_GLA_K), 0) & (ck - 1)
        d = 1
        while d < ck:
            if reverse:
                bcum_all = bcum_all + jnp.where(step < ck - d, pltpu.roll(bcum_all, rows_all - d, 0), 0.0)
            else:
                bcum_all = bcum_all + jnp.where(step >= d, pltpu.roll(bcum_all, d, 0), 0.0)
            d *= 2
        seen = (jj >= ii) if reverse else (jj <= ii)
        for bi in range(n_batch):
            units.append(dict(pg=pg_ref, bi=bi, state=state, reverse=reverse, seen=seen,
                              bcum=bcum_all[bi * ck:(bi + 1) * ck]))

    for u in units:
        pg_ref, bi, bcum = u["pg"], u["bi"], u["bcum"]
        q = pg_ref[bi, :, _PG_Q:_PG_K].astype(f32) * (GLA_DK ** -0.5)
        k = pg_ref[bi, :, _PG_K:_PG_V].astype(f32)
        u["btot"] = bcum[0:1] if u["reverse"] else bcum[ck - 1:ck]
        q_dec = q * jnp.exp(bcum)
        k_dec = (k * jnp.exp(-bcum)).astype(bf16)
        k_end = k * jnp.exp(u["btot"] - bcum)
        u["s_t"] = u["state"][bi]
        by_head = lambda t: jnp.concatenate([jnp.where(head_of_lane == hd, t, 0.0) for hd in range(nh)],
                                            axis=0).astype(bf16)
        u["q_blk"] = by_head(q_dec)
        u["k_blk"] = by_head(k_end)
        u["rhs"] = jnp.concatenate([u["s_t"].astype(bf16), k_dec], axis=0)
    for u in units:
        u["qs"] = lax.dot_general(u["q_blk"], u["rhs"], nt_dims, preferred_element_type=f32)
    for u in units:
        u["scores"] = jnp.where(u["seen"], u["qs"][:, GLA_DV:GLA_DV + ck], 0.0).astype(bf16)
    for u in units:
        v = u["pg"][u["bi"], :, _PG_V:_PG_G]
        u["intra"] = [jnp.dot(u["scores"][hd * ck:(hd + 1) * ck], v[:, hd * GLA_DV:(hd + 1) * GLA_DV],
                              preferred_element_type=f32) for hd in range(nh)]
        v_stack = jnp.concatenate([v[:, hd * GLA_DV:(hd + 1) * GLA_DV] for hd in range(nh)], axis=0)
        u["kv_t"] = lax.dot_general(v_stack, u["k_blk"], tn_dims, preferred_element_type=f32)
    for u in units:
        u["outs"] = [u["intra"][hd] + u["qs"][hd * ck:(hd + 1) * ck, 0:GLA_DV] for hd in range(nh)]
        u["state"][u["bi"]] = u["s_t"] * jnp.exp(u["btot"]) + u["kv_t"]
    n_batch = streams[0][0].shape[0]
    return [[u["outs"] for u in units[si * n_batch:(si + 1) * n_batch]] for si in range(len(streams))]


def _gla_gate(o_heads, g, gn):
    normed = []
    for oh in o_heads:
        ms = jnp.mean(oh * oh, axis=-1, keepdims=True)
        normed.append(oh * lax.rsqrt(ms + RMS_EPS))
    return jnp.concatenate(normed, axis=-1) * gn * (g * jax.nn.sigmoid(g))


def _gla_kernel(*refs, merge):
    if merge:
        (pgf_ref, pgb_ref, waf_ref, baf_ref, wab_ref, bab_ref, s0f_ref, s0b_ref, gn_ref,
         ylo_ref, yhi_ref, st_f, st_b, keep_f, keep_b) = refs
    else:
        (pgf_ref, pgb_ref, waf_ref, baf_ref, wab_ref, bab_ref, s0f_ref, s0b_ref,
         sfin_f_ref, sfin_b_ref, st_f, st_b) = refs
    n = pl.program_id(0)
    n_chunks = pl.num_programs(0)

    @pl.when(n == 0)
    def _():
        st_f[...] = s0f_ref[...]
        st_b[...] = s0b_ref[...]

    outs_f, outs_b = _gla_chunks([(pgf_ref, waf_ref, baf_ref, st_f, False), (pgb_ref, wab_ref, bab_ref, st_b, True)])
    n_batch = pgf_ref.shape[0]
    heads = range(GLA_HEADS)
    if merge:
        half = n_chunks // 2
        m = n_chunks - 1 - n

        @pl.when(n < half)
        def _():
            for bi in range(n_batch):
                keep_f[n, bi] = jnp.concatenate(outs_f[bi], axis=-1).astype(keep_f.dtype)
                keep_b[m - half, bi] = jnp.concatenate(outs_b[bi], axis=-1).astype(keep_b.dtype)

        @pl.when(n >= half)
        def _():
            for bi in range(n_batch):
                kb = keep_b[n - half, bi].astype(f32)
                o_heads = [outs_f[bi][hd] + kb[:, hd * GLA_DV:(hd + 1) * GLA_DV] for hd in heads]
                yhi_ref[bi] = _gla_gate(o_heads, pgf_ref[bi, :, _PG_G:_PG_A].astype(f32), gn_ref[...])
                kf = keep_f[m, bi].astype(f32)
                o_heads = [kf[:, hd * GLA_DV:(hd + 1) * GLA_DV] + outs_b[bi][hd] for hd in heads]
                ylo_ref[bi] = _gla_gate(o_heads, pgb_ref[bi, :, _PG_G:_PG_A].astype(f32), gn_ref[...])
    else:
        @pl.when(n == n_chunks - 1)
        def _():
            sfin_f_ref[...] = st_f[...]
            sfin_b_ref[...] = st_b[...]


def _gla_pass(pg, wa, ba, s0, gn=None):
    n_batch, n_chunks = pg.shape[0], pg.shape[1]
    merge = gn is not None
    half = n_chunks // 2
    const2 = lambda n: (0, 0)
    const3 = lambda n: (0, 0, 0)
    chunk = lambda width, idx: pl.BlockSpec((n_batch, None, GLA_CHUNK, width), lambda n: (0, idx(n), 0, 0))
    w_specs = [pl.BlockSpec((A_LOW_PAD, D_GLA_K), const2), pl.BlockSpec((1, D_GLA_K), const2)]
    s_spec = pl.BlockSpec((n_batch, GLA_DV, D_GLA_K), const3)
    s_shape = jax.ShapeDtypeStruct((n_batch, GLA_DV, D_GLA_K), f32)
    in_specs = [chunk(D_PG, lambda n: n), chunk(D_PG, lambda n: n_chunks - 1 - n)] + w_specs + w_specs + [s_spec,
                                                                                                            s_spec]
    args = [pg, pg, wa[0], ba[0], wa[1], ba[1], s0[0], s0[1]]
    scratch = [pltpu.VMEM((n_batch, GLA_DV, D_GLA_K), f32)] * 2
    if merge:
        assert n_chunks % 2 == 0
        in_specs.append(pl.BlockSpec((1, D_GLA_V), const2))
        args.append(gn)
        out_specs = [chunk(D_GLA_V, lambda n: jnp.minimum(n_chunks - 1 - n, half - 1)),
                     chunk(D_GLA_V, lambda n: jnp.maximum(n - half, 0))]
        y_shape = jax.ShapeDtypeStruct((n_batch, half, GLA_CHUNK, D_GLA_V), f32)
        out_shape = [y_shape, y_shape]
        scratch += [pltpu.VMEM((half, n_batch, GLA_CHUNK, D_GLA_V), bf16)] * 2
    else:
        out_specs = [s_spec, s_spec]
        out_shape = [s_shape, s_shape]
    return pl.pallas_call(
        functools.partial(_gla_kernel, merge=merge),
        grid=(n_chunks,),
        in_specs=in_specs,
        out_specs=out_specs,
        out_shape=out_shape,
        scratch_shapes=scratch,
        compiler_params=_cparams(("arbitrary",)),
        name="gla_merge" if merge else "gla_scan",
    )(*args)


def _pack_rows(v):
    n = v.shape[1] // 2
    w = lax.bitcast_convert_type(v.astype(bf16).astype(f32), jnp.int32)
    return (w[:, :n] & jnp.int32(-65536)) | lax.shift_right_logical(w[:, n:], 16)


def _unpack_rows(w):
    hi = lax.bitcast_convert_type(w & jnp.int32(-65536), f32)
    lo = lax.bitcast_convert_type(lax.shift_left(w, 16), f32)
    return hi, lo


def _outproj_router_kernel(yl_ref, ygl_ref, ygr_ref, x_hbm, mod_ref, wo_ref, lg_ref, lb_ref, wrh_ref, wrl_ref,
                           br_ref, x1_ref, hp_ref, ids_ref, rank_ref, wts_ref, cnt_ref, running, x_ring, x_sems):
    i = pl.program_id(0)

    @pl.when(i == 0)
    def _():
        running[...] = jnp.zeros_like(running)

    x_ref = _ring_tile(x_hbm, x_ring, x_sems, i, pl.num_programs(0))

    d = D_MODEL
    g1 = mod_ref[0, :, 2 * d:3 * d]
    sh2 = mod_ref[0, :, 3 * d:4 * d]
    sc2 = mod_ref[0, :, 4 * d:5 * d]
    n_parts = x_ref.shape[0] // OUTPROJ_PART
    tm = OUTPROJ_PART
    grid_rows = ygl_ref.shape[2] // n_parts
    tok = [slice(p * tm, (p + 1) * tm) for p in range(n_parts)]
    nt_dims = (((1,), (1,)), ((), ()))
    ne = wrh_ref.shape[0]
    expert = lax.broadcasted_iota(jnp.int32, (ne, tm), 0).astype(f32)
    neg_inf = jnp.float32(-jnp.inf)

    ygs = [jnp.concatenate([ref[0, :, r, :] for r in range(p * grid_rows, (p + 1) * grid_rows)
                            for ref in (ygl_ref, ygr_ref)], axis=0).astype(bf16) for p in range(n_parts)]
    ys = [jnp.dot(yl_ref[tok[p], :], wo_ref[0:D_LRU, :], preferred_element_type=f32)
          + jnp.dot(ygs[p], wo_ref[D_LRU:2 * D_LRU, :], preferred_element_type=f32) for p in range(n_parts)]
    hmods = []
    for p in range(n_parts):
        z = DEEPNORM_ALPHA * x_ref[tok[p], :] + g1 * ys[p]
        mu = jnp.mean(z, axis=-1, keepdims=True)
        zc = z - mu
        var = jnp.mean(zc * zc, axis=-1, keepdims=True)
        x1 = zc * lax.rsqrt(var + LN_EPS) * lg_ref[...] + lb_ref[...]
        x1_ref[tok[p], :] = x1
        hmod = x1 * (1.0 + sc2) + sh2
        hp_ref[tok[p], :] = _pack_rows(hmod)
        hmods.append(hmod)

    logits = []
    for hmod in hmods:
        h_hi = hmod.astype(bf16)
        h_lo = (hmod - h_hi.astype(f32)).astype(bf16)
        logits.append(lax.dot_general(wrh_ref[...], h_hi, nt_dims, preferred_element_type=f32)
                      + lax.dot_general(wrh_ref[...], h_lo, nt_dims, preferred_element_type=f32)
                      + lax.dot_general(wrl_ref[...], h_hi, nt_dims, preferred_element_type=f32) + br_ref[...])
    picks = []
    for live in logits:
        sel = jnp.zeros((ne, tm), f32)
        ids, vals = [], []
        for _ in range(TOP_K):
            m = jnp.max(live, axis=0, keepdims=True)
            j = jnp.min(jnp.where(live == m, expert, float(ne)), axis=0, keepdims=True)
            pick = expert == j
            sel = jnp.where(pick, 1.0, sel)
            live = jnp.where(pick, neg_inf, live)
            ids.append(j)
            vals.append(m)
        picks.append((sel, ids, vals))

    ri = lax.broadcasted_iota(jnp.int32, (tm, tm), 0)
    ci = lax.broadcasted_iota(jnp.int32, (tm, tm), 1)
    earlier = (ri < ci).astype(bf16)
    slot = lax.broadcasted_iota(jnp.int32, (8, tm), 0)
    for p, (sel, ids, vals) in enumerate(picks):
        rank_dense = running[...] + jnp.dot(sel.astype(bf16), earlier, preferred_element_type=f32)
        running[...] = running[...] + jnp.sum(sel, axis=1, keepdims=True)
        exps = [jnp.exp(vk - vals[0]) for vk in vals]
        denom = exps[0] + exps[1] + exps[2] + exps[3]
        ids_out = jnp.zeros((8, tm), f32)
        rank_out = jnp.zeros((8, tm), f32)
        wts_out = jnp.zeros((8, tm), f32)
        for kk in range(TOP_K):
            rk = jnp.sum(jnp.where(expert == ids[kk], rank_dense, 0.0), axis=0, keepdims=True)
            here = slot == kk
            ids_out = jnp.where(here, ids[kk], ids_out)
            rank_out = jnp.where(here, rk, rank_out)
            wts_out = jnp.where(here, exps[kk] / denom, wts_out)
        ids_ref[:, tok[p]] = ids_out.astype(jnp.int32)
        rank_ref[:, tok[p]] = rank_out.astype(jnp.int32)
        wts_ref[tok[p], :] = jnp.concatenate([wts_out, jnp.zeros((LANES - 8, tm), f32)], axis=0).T
    cnt_ref[...] = running[...]


def _outproj_router(yl, yg, xt, mod3, wo, lg, lb, wr_hi, wr_lo, br, tokens_per_batch):
    T = xt.shape[0]
    tm = TOKEN_TILE
    tiles_per_batch = tokens_per_batch // tm
    tokrow = lambda i: (i, 0)
    const = lambda i: (0, 0)
    half_cols = yg[0].shape[1]
    rows_per_tile = tm // (2 * half_cols)
    yg_spec = pl.BlockSpec((1, half_cols, rows_per_tile, D_GLA_V),
                           lambda i: (i // tiles_per_batch, 0, i % tiles_per_batch, 0))
    return pl.pallas_call(
        _outproj_router_kernel,
        grid=(T // tm,),
        in_specs=[pl.BlockSpec((tm, D_LRU), tokrow),
                  yg_spec, yg_spec,
                  pl.BlockSpec(memory_space=pl.ANY),
                  pl.BlockSpec((1, 1, N_MOD * D_MODEL), lambda i: (i // tiles_per_batch, 0, 0)),
                  pl.BlockSpec((D_MODEL, D_MODEL), const),
                  pl.BlockSpec((1, D_MODEL), const),
                  pl.BlockSpec((1, D_MODEL), const),
                  pl.BlockSpec((N_EXPERTS, D_MODEL), const),
                  pl.BlockSpec((N_EXPERTS, D_MODEL), const),
                  pl.BlockSpec((N_EXPERTS, 1), const)],
        out_specs=[pl.BlockSpec((tm, D_MODEL), tokrow),
                   pl.BlockSpec((tm, D_MODEL // 2), tokrow),
                   pl.BlockSpec((8, tm), lambda i: (0, i)),
                   pl.BlockSpec((8, tm), lambda i: (0, i)),
                   pl.BlockSpec((tm, LANES), tokrow),
                   pl.BlockSpec((N_EXPERTS, 1), const)],
        out_shape=[jax.ShapeDtypeStruct((T, D_MODEL), f32),
                   jax.ShapeDtypeStruct((T, D_MODEL // 2), jnp.int32),
                   jax.ShapeDtypeStruct((8, T), jnp.int32),
                   jax.ShapeDtypeStruct((8, T), jnp.int32),
                   jax.ShapeDtypeStruct((T, LANES), f32),
                   jax.ShapeDtypeStruct((N_EXPERTS, 1), f32)],
        scratch_shapes=[pltpu.VMEM((N_EXPERTS, 1), f32),
                        pltpu.VMEM((RING_SLOTS, tm, D_MODEL), f32), pltpu.SemaphoreType.DMA((RING_SLOTS,))],
        compiler_params=_cparams(("arbitrary",)),
        name="outproj_router",
    )(yl, yg[0], yg[1], xt, mod3, wo, lg, lb, wr_hi, wr_lo, br)


def _route_kernel(ids_ref, rank_ref, cnt_ref, dest_ref, tiles_ref, *, bm):
    ne = cnt_ref.shape[0]
    cnt = cnt_ref[...]
    padded = jnp.floor((cnt + (bm - 1.0)) * (1.0 / bm)) * bm
    sub = lax.broadcasted_iota(jnp.int32, (ne, LANES), 0)
    lane = lax.broadcasted_iota(jnp.int32, (ne, LANES), 1)
    padded_row = jnp.sum(jnp.where(sub == lane, padded, 0.0), axis=0, keepdims=True)
    base = jnp.sum(jnp.where(lane < sub, padded_row, 0.0), axis=1, keepdims=True)
    ends = base + padded

    tc = ids_ref.shape[1]
    expert = lax.broadcasted_iota(jnp.int32, (ne, tc), 0)
    slot = lax.broadcasted_iota(jnp.int32, (8, tc), 0)
    ids = ids_ref[...]
    group_base = jnp.zeros((8, tc), f32)
    for kk in range(TOP_K):
        b_k = jnp.sum(jnp.where(expert == ids[kk:kk + 1], base, 0.0), axis=0, keepdims=True)
        group_base = jnp.where(slot == kk, b_k, group_base)
    dest_ref[...] = group_base.astype(jnp.int32) + rank_ref[...]

    nl = tiles_ref.shape[1]
    start = lax.broadcasted_iota(jnp.int32, (ne, nl), 1).astype(f32) * bm
    te = jnp.minimum(jnp.sum(jnp.where(start >= ends, 1.0, 0.0), axis=0, keepdims=True), ne - 1.0)
    at_te = lax.broadcasted_iota(jnp.int32, (ne, nl), 0).astype(f32) == te
    cnt_te = jnp.sum(jnp.where(at_te, cnt, 0.0), axis=0, keepdims=True)
    base_te = jnp.sum(jnp.where(at_te, base, 0.0), axis=0, keepdims=True)
    valid = jnp.clip(cnt_te - (start[0:1] - base_te), 0.0, float(bm))
    next_group = jnp.sum(jnp.where(at_te, ends, 0.0), axis=0, keepdims=True) * (1.0 / bm)
    last_used = jnp.sum(padded, axis=0, keepdims=True) * (1.0 / bm) - 1.0
    srow = lax.broadcasted_iota(jnp.int32, (8, nl), 0)
    table = jnp.where(srow == 0, te, jnp.where(srow == 1, valid, jnp.where(srow == 2, next_group,
                                                                           jnp.where(srow == 3, last_used, 0.0))))
    tiles_ref[...] = table.astype(jnp.int32)


def _route(ids, rank, cnt, bm, n_tiles):
    T = ids.shape[1]
    tc = 2048
    nl = -(-n_tiles // LANES) * LANES
    tok = lambda i: (0, i)
    const = lambda i: (0, 0)
    return pl.pallas_call(
        functools.partial(_route_kernel, bm=bm),
        grid=(T // tc,),
        in_specs=[pl.BlockSpec((8, tc), tok), pl.BlockSpec((8, tc), tok),
                  pl.BlockSpec((N_EXPERTS, 1), const)],
        out_specs=[pl.BlockSpec((8, tc), tok), pl.BlockSpec((8, nl), const)],
        out_shape=[jax.ShapeDtypeStruct((8, T), jnp.int32), jax.ShapeDtypeStruct((8, nl), jnp.int32)],
        compiler_params=_cparams(("arbitrary",)),
        name="route",
    )(ids, rank, cnt)


def _sc_workers():
    info = plsc.get_sparse_core_info()
    return info.num_cores, info.num_subcores


def _sc_dispatch(rows, dest_flat, n_out):
    T, D = rows.shape
    nc, ns = _sc_workers()
    per_w = T // (nc * ns)
    n_chunks = per_w // SC_CHUNK
    assert n_chunks % 2 == 0
    mesh = plsc.VectorSubcoreMesh(core_axis_name="c", subcore_axis_name="s")

    @functools.partial(
        pl.kernel, mesh=mesh,
        out_type=jax.ShapeDtypeStruct((n_out, D), rows.dtype),
        scratch_types=([pltpu.VMEM((SC_CHUNK,), jnp.int32)] * (2 * TOP_K)
                       + [pltpu.VMEM((SC_CHUNK, D), rows.dtype)] * 2
                       + [pltpu.SemaphoreType.DMA] * (2 * TOP_K)),
    )
    def k(rows_hbm, dest_hbm, out_hbm, *scratch):
        idx_v = (scratch[:TOP_K], scratch[TOP_K:2 * TOP_K])
        rows_v = scratch[2 * TOP_K:2 * TOP_K + 2]
        sems = (scratch[2 * TOP_K + 2:3 * TOP_K + 2], scratch[3 * TOP_K + 2:])
        wid = lax.axis_index("s") * nc + lax.axis_index("c")
        base = wid * per_w

        def load(chunk, b):
            off = base + chunk * SC_CHUNK
            pltpu.sync_copy(rows_hbm.at[pl.ds(off, SC_CHUNK)], rows_v[b])
            for kk in range(TOP_K):
                pltpu.sync_copy(dest_hbm.at[pl.ds(kk * T + off, SC_CHUNK)], idx_v[b][kk])

        load(0, 0)

        @pl.loop(0, n_chunks, step=2)
        def _(j):
            for b in range(2):
                copies = [pltpu.async_copy(rows_v[b], out_hbm.at[idx_v[b][kk]], sems[b][kk])
                          for kk in range(TOP_K)]

                @pl.when(j + b + 1 < n_chunks)
                def _():
                    load(j + b + 1, 1 - b)

                for cp in copies:
                    cp.wait()

    return k(rows, dest_flat)


def _sc_gather(table, idx):
    _, D = table.shape
    N = idx.shape[0]
    nc, ns = _sc_workers()
    per_w = N // (nc * ns)
    n_chunks = per_w // SC_CHUNK
    assert n_chunks % 2 == 0
    mesh = plsc.VectorSubcoreMesh(core_axis_name="c", subcore_axis_name="s")

    @functools.partial(
        pl.kernel, mesh=mesh,
        out_type=jax.ShapeDtypeStruct((N, D), table.dtype),
        scratch_types=([pltpu.VMEM((SC_CHUNK,), jnp.int32)] * 2
                       + [pltpu.VMEM((SC_CHUNK, D), table.dtype)] * 2
                       + [pltpu.SemaphoreType.DMA] * 2),
    )
    def k(table_hbm, idx_hbm, out_hbm, idx0, idx1, buf0, buf1, sem0, sem1):
        idxs, bufs, sems = (idx0, idx1), (buf0, buf1), (sem0, sem1)
        wid = lax.axis_index("s") * nc + lax.axis_index("c")
        base = wid * per_w

        def gather(b):
            return pltpu.make_async_copy(table_hbm.at[idxs[b]], bufs[b], sems[b])

        def start(chunk, b):
            pltpu.sync_copy(idx_hbm.at[pl.ds(base + chunk * SC_CHUNK, SC_CHUNK)], idxs[b])
            gather(b).start()

        def finish(chunk, b):
            gather(b).wait()
            pltpu.sync_copy(bufs[b], out_hbm.at[pl.ds(base + chunk * SC_CHUNK, SC_CHUNK)])

        start(0, 0)

        @pl.loop(0, n_chunks, step=2)
        def _(j):
            start(j + 1, 1)
            finish(j, 0)

            @pl.when(j + 2 < n_chunks)
            def _():
                start(j + 2, 0)

            finish(j + 1, 1)

    return k(table, idx)


def _ffn_kernel(te_ref, tv_ref, tn_ref, tl_ref, xs_ref, wg_hbm, bg_ref, wu_hbm, bu_ref, wd_hbm, bd_ref, eo_ref,
                w_buf, sems, slot_ref):
    i = pl.program_id(0)
    n_tiles = pl.num_programs(0)
    w_hbm = (wg_hbm, wu_hbm, wd_hbm)

    def fetch(e, slot):
        return [pltpu.make_async_copy(w_hbm[m].at[e], w_buf.at[slot, m], sems.at[slot, m]) for m in range(3)]

    @pl.when(i == 0)
    def _():
        slot_ref[0] = 0
        for cp in fetch(te_ref[0], 0):
            cp.start()

    prev = te_ref[jnp.maximum(i - 1, 0)]
    first = (i == 0) | (te_ref[i] != prev)

    @pl.when(first & (i > 0))
    def _():
        slot_ref[0] = 1 - slot_ref[0]

    slot = slot_ref[0]

    @pl.when(first)
    def _():
        for cp in fetch(te_ref[i], slot):
            cp.wait()
        nxt = tn_ref[i]
        e_nxt = te_ref[jnp.minimum(nxt, n_tiles - 1)]

        @pl.when((nxt > i) & (nxt < n_tiles) & (e_nxt != te_ref[i]))
        def _():
            for cp in fetch(e_nxt, 1 - slot):
                cp.start()

    wg_b, wu_b, wd_b = (w_buf.at[slot, m] for m in range(3))
    valid = tv_ref[i]

    def ffn_rows(r0, m):
        blocks = [(r0 + o, min(MOE_BLOCK, m)) for o in range(0, m, MOE_BLOCK)]
        xs, gates, ups, acts = [], [], [], []
        for b0, bm_ in blocks:
            row = lax.broadcasted_iota(jnp.int32, (bm_, 1), 0) + b0
            xw = jnp.where(row < valid, xs_ref[b0:b0 + bm_, :], 0)
            x_hi, x_lo = _unpack_rows(xw)
            xs.append(jnp.concatenate([x_hi, x_lo], axis=1).astype(bf16))
        for x in xs:
            gates.append(jnp.minimum(jnp.dot(x, wg_b[...], preferred_element_type=f32) + bg_ref[...],
                                     SWIGLU_LIMIT))
            ups.append(jnp.clip(jnp.dot(x, wu_b[...], preferred_element_type=f32) + bu_ref[...],
                                -SWIGLU_LIMIT, SWIGLU_LIMIT))
        for gate, up in zip(gates, ups):
            acts.append(((up + 1.0) * gate * jax.nn.sigmoid(SWIGLU_ALPHA * gate)).astype(bf16))
        for (b0, bm_), act in zip(blocks, acts):
            out = jnp.dot(act, wd_b[...], preferred_element_type=f32) + bd_ref[...]
            eo_ref[b0:b0 + bm_, :] = _pack_rows(out)

    def zero_rows(r0, m):
        eo_ref[r0:r0 + m, :] = jnp.zeros((m, eo_ref.shape[1]), eo_ref.dtype)

    for r0 in range(0, xs_ref.shape[0], MOE_PASS):
        lo = 0
        for m in MOE_PASS_SIZES:
            @pl.when((valid > r0 + lo) & ((valid <= r0 + m) | (m == MOE_PASS)))
            def _(r0=r0, m=m):
                ffn_rows(r0, m)
                if m < MOE_PASS:
                    zero_rows(r0 + m, MOE_PASS - m)
            lo = m

        @pl.when(valid <= r0)
        def _(r0=r0):
            zero_rows(r0, MOE_PASS)


def _expert_ffn(tile_expert, tile_valid, tile_next, last_used, xs, w_gate, b_gate, w_up, b_up, w_down, b_down):
    n_rows, dp = xs.shape
    d = 2 * dp
    bm = MOE_TILE
    d_e = w_gate.shape[-1]
    assert d == d_e
    bspec = lambda n_: pl.BlockSpec((None, 1, n_), lambda i, te, tv, tn, tl: (te[i], 0, 0))
    hbm = pl.BlockSpec(memory_space=pl.ANY)
    grid_spec = pltpu.PrefetchScalarGridSpec(
        num_scalar_prefetch=4,
        grid=(n_rows // bm,),
        in_specs=[pl.BlockSpec((bm, dp), lambda i, te, tv, tn, tl: (jnp.minimum(i, tl[0]), 0)),
                  hbm, bspec(d_e), hbm, bspec(d_e), hbm, bspec(d)],
        out_specs=pl.BlockSpec((bm, dp), lambda i, te, tv, tn, tl: (i, 0)),
        scratch_shapes=[pltpu.VMEM((2, 3, d, d_e), f32),
                        pltpu.SemaphoreType.DMA((2, 3)),
                        pltpu.SMEM((1,), jnp.int32)],
    )
    return pl.pallas_call(
        _ffn_kernel,
        grid_spec=grid_spec,
        out_shape=jax.ShapeDtypeStruct((n_rows, dp), jnp.int32),
        compiler_params=_cparams(("arbitrary",)),
        name="expert_ffn",
    )(tile_expert, tile_valid, tile_next, last_used, xs, w_gate, b_gate.reshape(N_EXPERTS, 1, d_e), w_up,
      b_up.reshape(N_EXPERTS, 1, d_e), w_down, b_down.reshape(N_EXPERTS, 1, d))


def _combine_kernel(eg_ref, wts_ref, x1_ref, mod_ref, lg_ref, lb_ref, o_ref):
    w = wts_ref[...]
    y_hi, y_lo = _unpack_rows(eg_ref[0])
    y_hi, y_lo = y_hi * w[:, 0:1], y_lo * w[:, 0:1]
    for kk in range(1, TOP_K):
        e_hi, e_lo = _unpack_rows(eg_ref[kk])
        y_hi = y_hi + e_hi * w[:, kk:kk + 1]
        y_lo = y_lo + e_lo * w[:, kk:kk + 1]
    y = jnp.concatenate([y_hi, y_lo], axis=1)
    z = DEEPNORM_ALPHA * x1_ref[...] + mod_ref[0] * y
    mu = jnp.mean(z, axis=-1, keepdims=True)
    zc = z - mu
    var = jnp.mean(zc * zc, axis=-1, keepdims=True)
    o_ref[...] = zc * lax.rsqrt(var + LN_EPS) * lg_ref[...] + lb_ref[...]


def _combine_ln(eg, wts, x1, mod3, lg, lb, tokens_per_batch):
    T = x1.shape[0]
    tm = TOKEN_TILE
    tiles_per_batch = tokens_per_batch // tm
    const = lambda i: (0, 0)
    return pl.pallas_call(
        _combine_kernel,
        grid=(T // tm,),
        in_specs=[pl.BlockSpec((TOP_K, tm, D_MODEL // 2), lambda i: (0, i, 0)),
                  pl.BlockSpec((tm, LANES), lambda i: (i, 0)),
                  pl.BlockSpec((tm, D_MODEL), lambda i: (i, 0)),
                  pl.BlockSpec((1, 1, D_MODEL), lambda i: (i // tiles_per_batch, 0, N_MOD - 1)),
                  pl.BlockSpec((1, D_MODEL), const),
                  pl.BlockSpec((1, D_MODEL), const)],
        out_specs=pl.BlockSpec((tm, D_MODEL), lambda i: (i, 0)),
        out_shape=jax.ShapeDtypeStruct((T, D_MODEL), f32),
        compiler_params=_cparams(("arbitrary",)),
        name="combine_ln",
    )(eg, wts, x1, mod3, lg, lb)


def _block_diag(w):
    n, c, d = w.shape
    eye = jnp.eye(n, dtype=w.dtype)
    return jnp.einsum('ncd,nm->ncmd', w, eye).reshape(n * c, n * d)


def kernel(x, c, ctx, c_ctx, w_ada, b_ada, w_in, conv_w, conv_b, lru_wa, lru_ba, lru_wx, lru_bx,
           lru_lam, gla_wa, gla_ba, gla_norm_g, w_out, ln1_g, ln1_b, w_router, b_router, w_gate,
           b_gate, w_up, b_up, w_down, b_down, ln2_g, ln2_b):
    B, L, D = x.shape
    Lc = ctx.shape[1]
    T = B * L
    rows = L // GRID_W
    l = 0

    cpad = jnp.zeros((8, D), f32).at[0:B].set(c).at[B].set(c_ctx)
    w_cat = jnp.pad(w_in[l], ((0, 0), (0, D_PROJ - w_in.shape[-1]))).astype(bf16)
    wg = [jnp.concatenate([_block_diag(lru_wa[l, d]), _block_diag(lru_wx[l, d])], axis=1).astype(bf16)
          for d in range(2)]
    bg = [jnp.concatenate([lru_ba[l, d], lru_bx[l, d]])[None] for d in range(2)]
    lam = [lru_lam[l, d][None] for d in range(2)]
    wa = [jnp.pad(gla_wa[l, d], ((0, A_LOW_PAD - GATE_RANK), (0, 0))).astype(bf16) for d in range(2)]
    ba = [gla_ba[l, d][None] for d in range(2)]
    cw, cb = conv_w[l], conv_b[l][None]
    wr_t = w_router[l].T
    wr_hi = wr_t.astype(bf16)
    wr_lo = (wr_t - wr_hi.astype(f32)).astype(bf16)
    br = b_router[l][:, None]
    c_xl, c_gl, c_pg = (0, D_LRU), (D_LRU, 2 * D_LRU), (2 * D_LRU, D_PROJ)

    mod3 = _ada_mod(cpad, w_ada[l], b_ada[l]).reshape(8, 1, N_MOD * D)

    tiles_per_batch = L // INPROJ_TILE
    xl_c, pg_c = _inproj_rows(ctx.reshape(B * Lc, D), mod3, w_cat, lambda i: B, (c_xl, c_pg), (f32, bf16))
    zero_h = jnp.zeros((B, 1, D_LRU), f32)
    zero_s = jnp.zeros((B, GLA_DV, D_GLA_K), f32)
    pg_c = pg_c.reshape(B, Lc // GLA_CHUNK, GLA_CHUNK, D_PG)
    h_ctx = []
    for d in range(2):
        _, _, hf = _lru_pass(xl_c, wg[d], bg[d], lam[d], zero_h, n_batch=B, tile=Lc, reverse=bool(d),
                             conv=(cw, cb, Lc))
        h_ctx.append(hf)
    s_ctx = _gla_pass(pg_c, wa, ba, (zero_s, zero_s))

    xt = x.reshape(T, D)
    xl, gl = _inproj_rows(xt, mod3, w_cat, lambda i: i // tiles_per_batch, (c_xl, c_gl), (f32, bf16))
    hb, u_lru, _ = _lru_pass(xl, wg[1], bg[1], lam[1], h_ctx[1], n_batch=B, tile=LRU_TILE, reverse=True,
                             conv=(cw, cb, GRID_W))
    y_lru, _ = _lru_pass(u_lru, wg[0], bg[0], lam[0], h_ctx[0], n_batch=B, tile=LRU_TILE, reverse=False,
                         merge_with=(hb, gl))
    pg = _inproj_cols(x.reshape(B, rows, GRID_W, D), mod3, w_cat, c_pg)
    y_gla = _gla_pass(pg, wa, ba, s_ctx, gn=gla_norm_g[l][None])

    x1, hp, ids, rank, wts, cnt = _outproj_router(
        y_lru, y_gla, xt, mod3, w_out[l].astype(bf16), ln1_g[l][None], ln1_b[l][None], wr_hi, wr_lo, br, L)

    n_tiles = T * TOP_K // MOE_TILE + N_EXPERTS
    dest, tiles = _route(ids, rank, cnt, MOE_TILE, n_tiles)
    xs = _sc_dispatch(hp, dest.reshape(-1), n_tiles * MOE_TILE)
    eo = _expert_ffn(tiles[0, :n_tiles], tiles[1, :n_tiles], tiles[2, :n_tiles], tiles[3, :1], xs, w_gate[l],
                     b_gate[l], w_up[l], b_up[l], w_down[l], b_down[l])
    eg = _sc_gather(eo, dest[:TOP_K].reshape(-1)).reshape(TOP_K, T, D // 2)
    out = _combine_ln(eg, wts, x1, mod3, ln2_g[l][None], ln2_b[l][None], L)
    return out.reshape(B, L, D)
```

```python
import functools

import jax
import jax.numpy as jnp
from jax import lax
from jax.experimental import pallas as pl
from jax.experimental.pallas import tpu as pltpu
from jax.experimental.pallas import tpu_sc as plsc

D_MODEL = 1024
DEPTH = 1
GRID_W = 64
D_LRU = 512
CONV_W = 4
LRU_C = 8.0
GLA_HEADS = 4
D_GLA_V = 512
D_GLA_K = 256
GLA_DK = 64
GLA_DV = 128
GATE_RANK = 16
GATE_TAU = 16.0
GLA_CHUNK = 64
N_EXPERTS = 32
TOP_K = 4
SWIGLU_LIMIT = 7.0
SWIGLU_ALPHA = 1.702
N_MOD = 6
DEEPNORM_ALPHA = (2.0 * DEPTH) ** 0.25
LN_EPS = 1e-5
RMS_EPS = 1e-6

LANES = 128
A_LOW_PAD = LANES
D_PG = 2 * D_GLA_K + 2 * D_GLA_V + A_LOW_PAD
D_PROJ = 2 * D_LRU + D_PG
TOKEN_TILE = 1024
INPROJ_TILE = 1024
INPROJ_RING_SLOTS = 3
LRU_TILE = 512
MOE_TILE = 1024
MOE_PASS = 512
MOE_PASS_SIZES = (128, 256, MOE_PASS)
MOE_BLOCK = 256
OUTPROJ_PART = 128
SC_CHUNK = 64
VMEM_LIMIT = 48 * 1024 * 1024

f32 = jnp.float32
bf16 = jnp.bfloat16


def _cparams(sem):
    return pltpu.CompilerParams(dimension_semantics=sem, vmem_limit_bytes=VMEM_LIMIT)


def _ada_kernel(c_ref, w_ref, b_ref, o_ref):
    s = c_ref[...]
    s = s * jax.nn.sigmoid(s)
    o_ref[...] = jnp.dot(s.astype(bf16), w_ref[...], preferred_element_type=f32) + b_ref[...]


def _ada_mod(cpad, w, b):
    n = w.shape[1]
    tn = 1024
    return pl.pallas_call(
        _ada_kernel,
        grid=(n // tn,),
        in_specs=[pl.BlockSpec((8, D_MODEL), lambda j: (0, 0)),
                  pl.BlockSpec((D_MODEL, tn), lambda j: (0, j)),
                  pl.BlockSpec((1, tn), lambda j: (0, j))],
        out_specs=pl.BlockSpec((8, tn), lambda j: (0, j)),
        out_shape=jax.ShapeDtypeStruct((8, n), f32),
        compiler_params=_cparams(("arbitrary",)),
        name="ada_mod",
    )(cpad, w, b.reshape(1, n))


def _ring_tile(x_hbm, ring, sems, i, n_steps):
    n_slots, tile_rows = ring.shape[0], ring.shape[1]

    def fetch(step):
        s = step % n_slots
        return pltpu.make_async_copy(x_hbm.at[pl.ds(step * tile_rows, tile_rows), :], ring.at[s], sems.at[s])

    @pl.when(i == 0)
    def _():
        for k in range(n_slots - 1):
            @pl.when(k < n_steps)
            def _(k=k):
                fetch(k).start()

    @pl.when(i + n_slots - 1 < n_steps)
    def _():
        fetch(i + n_slots - 1).start()

    fetch(i).wait()
    return ring.at[i % n_slots]


def _inproj_kernel(x_ref, mod_ref, w_ref, *refs, parts, ring):
    out_refs = refs[:len(parts)]
    if ring:
        x_ref = _ring_tile(x_ref, refs[-2], refs[-1], pl.program_id(0), pl.num_programs(0))
    sh = mod_ref[0, :, 0:D_MODEL]
    sc = mod_ref[0, :, D_MODEL:2 * D_MODEL]
    u = (x_ref[...] * (1.0 + sc) + sh).astype(bf16)
    for (lo, hi), o_ref in zip(parts, out_refs):
        o_ref[...] = jnp.dot(u, w_ref[:, lo:hi], preferred_element_type=f32).astype(o_ref.dtype)


def _inproj_cols_kernel(x_hbm, mod_ref, w_ref, o_ref, xbuf, sems, *, part):
    b, n = pl.program_id(0), pl.program_id(1)
    nb, nn = pl.num_programs(0), pl.num_programs(1)
    cb = xbuf.shape[1]
    g = b * nn + n
    slot = g % 2

    def fetch(bb, nb_, s):
        return [pltpu.make_async_copy(x_hbm.at[bb, :, nb_ * cb + j, :], xbuf.at[s, j], sems.at[s])
                for j in range(cb)]

    @pl.when(g == 0)
    def _():
        for cp in fetch(b, n, slot):
            cp.start()

    @pl.when(g + 1 < nb * nn)
    def _():
        g1 = g + 1
        for cp in fetch(g1 // nn, g1 % nn, 1 - slot):
            cp.start()

    for cp in fetch(b, n, slot):
        cp.wait()
    sh = mod_ref[0, :, 0:D_MODEL]
    sc = mod_ref[0, :, D_MODEL:2 * D_MODEL]
    x = xbuf[slot].reshape(cb * xbuf.shape[2], D_MODEL)
    u = (x * (1.0 + sc) + sh).astype(bf16)
    lo, hi = part
    p = jnp.dot(u, w_ref[:, lo:hi], preferred_element_type=f32).astype(o_ref.dtype)
    o_ref[...] = p.reshape(o_ref.shape)


def _inproj_rows(xt, mod3, w_cat, batch_of_tile, parts, dtypes):
    T = xt.shape[0]
    tm = INPROJ_TILE
    ring = T // tm >= INPROJ_RING_SLOTS
    x_spec = pl.BlockSpec(memory_space=pl.ANY) if ring else pl.BlockSpec((tm, D_MODEL), lambda i: (i, 0))
    scratch = ([pltpu.VMEM((INPROJ_RING_SLOTS, tm, D_MODEL), f32), pltpu.SemaphoreType.DMA((INPROJ_RING_SLOTS,))]
               if ring else [])
    return pl.pallas_call(
        functools.partial(_inproj_kernel, parts=parts, ring=ring),
        grid=(T // tm,),
        in_specs=[x_spec,
                  pl.BlockSpec((1, 1, 2 * D_MODEL), lambda i: (batch_of_tile(i), 0, 0)),
                  pl.BlockSpec((D_MODEL, D_PROJ), lambda i: (0, 0))],
        out_specs=[pl.BlockSpec((tm, hi - lo), lambda i: (i, 0)) for lo, hi in parts],
        out_shape=[jax.ShapeDtypeStruct((T, hi - lo), dt) for (lo, hi), dt in zip(parts, dtypes)],
        scratch_shapes=scratch,
        compiler_params=_cparams(("arbitrary",)),
        name="inproj_rows",
    )(xt, mod3, w_cat)


def _inproj_cols(x4, mod3, w_cat, part):
    n_batch, rows, cols, _ = x4.shape
    lo, hi = part
    cb = INPROJ_TILE // rows
    return pl.pallas_call(
        functools.partial(_inproj_cols_kernel, part=part),
        grid=(n_batch, cols // cb),
        in_specs=[pl.BlockSpec(memory_space=pl.ANY),
                  pl.BlockSpec((1, 1, 2 * D_MODEL), lambda b, n: (b, 0, 0)),
                  pl.BlockSpec((D_MODEL, D_PROJ), lambda b, n: (0, 0))],
        out_specs=pl.BlockSpec((1, cb, rows, hi - lo), lambda b, n: (b, n, 0, 0)),
        out_shape=jax.ShapeDtypeStruct((n_batch, cols, rows, hi - lo), bf16),
        scratch_shapes=[pltpu.VMEM((2, cb, rows, D_MODEL), f32), pltpu.SemaphoreType.DMA((2,))],
        compiler_params=_cparams(("arbitrary", "arbitrary")),
        name="inproj_cols",
    )(x4, mod3, w_cat)


def _gelu_tanh(x):
    return 0.5 * x * (1.0 + jnp.tanh(0.7978845608028654 * (x + 0.044715 * (x * x * x))))


def _lru_kernel(*refs, row_w, reverse, merge):
    if merge:
        u_ref, wg_ref, bg_ref, lam_ref, h0_ref, hb_ref, gl_ref, out_ref, hfin_ref, carry, h_nat = refs
    else:
        (xl_hbm, cw_ref, cb_ref, wg_ref, bg_ref, lam_ref, h0_ref, out_ref, u_out_ref, hfin_ref, carry,
         x_buf, x_sems) = refs
    t = pl.program_id(1)

    @pl.when(t == 0)
    def _():
        carry[...] = h0_ref[0]

    n_seg = 8
    blk = lambda v, j: v[j * n_seg:(j + 1) * n_seg]
    if merge:
        u = u_ref[...]
        tt, ch = u.shape
        seg = tt // n_seg
    else:
        _, seg, _, ch = x_buf.shape
        tt = n_seg * seg
        nb, nt = pl.num_programs(0), pl.num_programs(1)
        g = pl.program_id(0) * nt + t
        slot = g % 2

        def fetch(step, s_):
            bb, tt_ = step // nt, step % nt
            tile = bb * nt + (nt - 1 - tt_ if reverse else tt_)
            return [pltpu.make_async_copy(xl_hbm.at[pl.ds(tile * tt + s * seg, seg), :], x_buf.at[s_, :, s, :],
                                          x_sems.at[s_]) for s in range(n_seg)]

        @pl.when(g == 0)
        def _():
            for cp in fetch(g, slot):
                cp.start()

        @pl.when(g + 1 < nb * nt)
        def _():
            for cp in fetch(g + 1, 1 - slot):
                cp.start()

        for cp in fetch(g, slot):
            cp.wait()
        x = x_buf[slot].reshape(tt, ch)
        segs_per_row = row_w // seg
        s_idx = lax.broadcasted_iota(jnp.int32, (n_seg, ch), 0)
        has_prev = (s_idx % segs_per_row) != 0
        has_next = (s_idx % segs_per_row) != segs_per_row - 1
        from_prev = lambda v: jnp.where(has_prev, pltpu.roll(v, 1, 0), 0.0)
        from_next = lambda v: jnp.where(has_next, pltpu.roll(v, n_seg - 1, 0), 0.0)
        x_m1 = jnp.concatenate([from_prev(blk(x, seg - 1)), x[:tt - n_seg]], axis=0)
        x_m2 = jnp.concatenate([from_prev(blk(x, seg - 2)), from_prev(blk(x, seg - 1)), x[:tt - 2 * n_seg]],
                               axis=0)
        x_p1 = jnp.concatenate([x[n_seg:], from_next(blk(x, 0))], axis=0)
        cw = cw_ref[...]
        u = cb_ref[...] + x_m2 * cw[0:1] + x_m1 * cw[1:2] + x * cw[2:3] + x_p1 * cw[3:4]
        u_out_ref[...] = u
    rows_q = tt // 4
    g_parts = [jnp.dot(u[q * rows_q:(q + 1) * rows_q].astype(bf16), wg_ref[...], preferred_element_type=f32)
               + bg_ref[...] for q in range(4)]
    lam = lam_ref[...]
    softplus_neg = jnp.maximum(-lam, 0.0) + jnp.log(1.0 + jnp.exp(-jnp.abs(lam)))
    log_a_scale = -LRU_C * softplus_neg

    def recurrence_terms(j):
        q, r = divmod(j * n_seg, rows_q)
        g_j, u_j = g_parts[q][r:r + n_seg], blk(u, j)
        log_a = log_a_scale * jax.nn.sigmoid(g_j[:, 0:ch])
        a_j = jnp.exp(log_a)
        series = log_a * (-2.0 + log_a * (-2.0 + log_a * (-4.0 / 3.0)))
        one_minus_a2 = jnp.where(log_a > -0.005, series, 1.0 - a_j * a_j)
        root = jnp.where(one_minus_a2 > 0.0, one_minus_a2 * lax.rsqrt(one_minus_a2), 0.0)
        return a_j, root * (jax.nn.sigmoid(g_j[:, ch:2 * ch]) * u_j)

    half = seg // 2
    h_loc, a_cum = [None] * seg, [None] * seg
    h_run, a_run = [None, None], [None, None]
    for i in range(half):
        for p in range(2):
            j = p * half + (half - 1 - i if reverse else i)
            a_j, b_j = recurrence_terms(j)
            h_run[p] = b_j if i == 0 else a_j * h_run[p] + b_j
            a_run[p] = a_j if i == 0 else a_j * a_run[p]
            h_loc[j], a_cum[j] = h_run[p], a_run[p]
    pieces = [(s, p) for s in range(n_seg) for p in range(2)]
    if reverse:
        pieces.reverse()
    entering = [[None] * n_seg, [None] * n_seg]
    state = carry[...]
    for s, p in pieces:
        entering[p][s] = state
        state = h_run[p][s:s + 1] + a_run[p][s:s + 1] * state
    carry[...] = state
    hfin_ref[0] = state
    enter = [jnp.concatenate(e, axis=0) for e in entering]
    h_rows = [h_loc[j] + a_cum[j] * enter[j // half] for j in range(seg)]
    if merge:
        for j in range(seg):
            h_nat[:, j, :] = h_rows[j] + blk(hb_ref, j)
        h = h_nat[...].reshape(tt, ch)
        out_ref[...] = (h * _gelu_tanh(gl_ref[...].astype(f32))).astype(out_ref.dtype)
    else:
        out_ref[...] = jnp.concatenate(h_rows, axis=0)


def _lru_pass(x_in, wg, bg, lam, h0, *, n_batch, tile, reverse, conv=None, merge_with=None):
    T = x_in.shape[0]
    nt = T // n_batch // tile
    merge = merge_with is not None
    assert merge != (conv is not None)
    seg = tile // 8

    def tok(b, t):
        return (b * nt + (nt - 1 - t if reverse else t), 0)

    tile_spec = pl.BlockSpec((tile, D_LRU), tok)
    const = lambda b, t: (0, 0)
    gate_specs = [pl.BlockSpec((D_LRU, 2 * D_LRU), const),
                  pl.BlockSpec((1, 2 * D_LRU), const),
                  pl.BlockSpec((1, D_LRU), const),
                  pl.BlockSpec((1, 1, D_LRU), lambda b, t: (b, 0, 0))]
    state_spec = pl.BlockSpec((1, 1, D_LRU), lambda b, t: (b, 0, 0))
    state_shape = jax.ShapeDtypeStruct((n_batch, 1, D_LRU), f32)
    scratch = [pltpu.VMEM((1, D_LRU), f32)]
    if merge:
        hb, gl = merge_with
        row_w = None
        in_specs = [tile_spec] + gate_specs + [tile_spec, tile_spec]
        args = [x_in, wg, bg, lam, h0, hb, gl]
        scratch += [pltpu.VMEM((8, seg, D_LRU), f32)]
        out_specs = [tile_spec, state_spec]
        out_shape = [jax.ShapeDtypeStruct((T, D_LRU), bf16), state_shape]
    else:
        conv_w, conv_b, row_w = conv
        assert row_w % seg == 0 and seg >= 2
        in_specs = [pl.BlockSpec(memory_space=pl.ANY),
                    pl.BlockSpec((CONV_W, D_LRU), const),
                    pl.BlockSpec((1, D_LRU), const)] + gate_specs
        args = [x_in, conv_w, conv_b, wg, bg, lam, h0]
        scratch += [pltpu.VMEM((2, seg, 8, D_LRU), f32), pltpu.SemaphoreType.DMA((2,))]
        out_specs = [tile_spec, tile_spec, state_spec]
        out_shape = [jax.ShapeDtypeStruct((T, D_LRU), f32), jax.ShapeDtypeStruct((T, D_LRU), f32), state_shape]
    return pl.pallas_call(
        functools.partial(_lru_kernel, row_w=row_w, reverse=reverse, merge=merge),
        grid=(n_batch, nt),
        in_specs=in_specs,
        out_specs=out_specs,
        out_shape=out_shape,
        scratch_shapes=scratch,
        compiler_params=_cparams(("arbitrary", "arbitrary")),
        name="lru_merge" if merge else "lru_scan",
    )(*args)


_PG_Q, _PG_K, _PG_V = 0, D_GLA_K, 2 * D_GLA_K
_PG_G, _PG_A = 2 * D_GLA_K + D_GLA_V, 2 * D_GLA_K + 2 * D_GLA_V


def _gla_chunks(streams):
    ck = GLA_CHUNK
    nh = GLA_HEADS
    head_of_lane = lax.broadcasted_iota(jnp.int32, (1, D_GLA_K), 1) // GLA_DK
    nt_dims = (((1,), (1,)), ((), ()))
    tn_dims = (((0,), (0,)), ((), ()))
    ii = lax.broadcasted_iota(jnp.int32, (nh * ck, ck), 0) & (ck - 1)
    jj = lax.broadcasted_iota(jnp.int32, (nh * ck, ck), 1)

    units = []
    for pg_ref, wa_ref, ba_ref, state, reverse in streams:
        n_batch = pg_ref.shape[0]
        rows_all = n_batch * ck
        a_low = jnp.concatenate([pg_ref[bi, :, _PG_A:_PG_A + A_LOW_PAD] for bi in range(n_batch)], axis=0)
        z = jnp.dot(a_low, wa_ref[...], preferred_element_type=f32) + ba_ref[...]
        bcum_all = (jnp.minimum(z, 0.0) - jnp.log(1.0 + jnp.exp(-jnp.abs(z)))) * (1.0 / GATE_TAU)
        step = lax.broadcasted_iota(jnp.int32, (rows_all, D_GLA_K), 0) & (ck - 1)
        d = 1
        while d < ck:
            if reverse:
                bcum_all = bcum_all + jnp.where(step < ck - d, pltpu.roll(bcum_all, rows_all - d, 0), 0.0)
            else:
                bcum_all = bcum_all + jnp.where(step >= d, pltpu.roll(bcum_all, d, 0), 0.0)
            d *= 2
        seen = (jj >= ii) if reverse else (jj <= ii)
        for bi in range(n_batch):
            units.append(dict(pg=pg_ref, bi=bi, state=state, reverse=reverse, seen=seen,
                              bcum=bcum_all[bi * ck:(bi + 1) * ck]))

    for u in units:
        pg_ref, bi, bcum = u["pg"], u["bi"], u["bcum"]
        q = pg_ref[bi, :, _PG_Q:_PG_K].astype(f32) * (GLA_DK ** -0.5)
        k = pg_ref[bi, :, _PG_K:_PG_V].astype(f32)
        u["btot"] = bcum[0:1] if u["reverse"] else bcum[ck - 1:ck]
        q_dec = q * jnp.exp(bcum)
        k_dec = (k * jnp.exp(-bcum)).astype(bf16)
        k_end = k * jnp.exp(u["btot"] - bcum)
        u["s_t"] = u["state"][bi]
        by_head = lambda t: jnp.concatenate([jnp.where(head_of_lane == hd, t, 0.0) for hd in range(nh)],
                                            axis=0).astype(bf16)
        u["q_blk"] = by_head(q_dec)
        u["k_blk"] = by_head(k_end)
        u["rhs"] = jnp.concatenate([u["s_t"].astype(bf16), k_dec], axis=0)
    for u in units:
        u["qs"] = lax.dot_general(u["q_blk"], u["rhs"], nt_dims, preferred_element_type=f32)
    for u in units:
        u["scores"] = jnp.where(u["seen"], u["qs"][:, GLA_DV:GLA_DV + ck], 0.0).astype(bf16)
    for u in units:
        v = u["pg"][u["bi"], :, _PG_V:_PG_G]
        u["intra"] = [jnp.dot(u["scores"][hd * ck:(hd + 1) * ck], v[:, hd * GLA_DV:(hd + 1) * GLA_DV],
                              preferred_element_type=f32) for hd in range(nh)]
        v_stack = jnp.concatenate([v[:, hd * GLA_DV:(hd + 1) * GLA_DV] for hd in range(nh)], axis=0)
        u["kv_t"] = lax.dot_general(v_stack, u["k_blk"], tn_dims, preferred_element_type=f32)
    for u in units:
        u["outs"] = [u["intra"][hd] + u["qs"][hd * ck:(hd + 1) * ck, 0:GLA_DV] for hd in range(nh)]
        u["state"][u["bi"]] = u["s_t"] * jnp.exp(u["btot"]) + u["kv_t"]
    n_batch = streams[0][0].shape[0]
    return [[u["outs"] for u in units[si * n_batch:(si + 1) * n_batch]] for si in range(len(streams))]


def _gla_gate(o_heads, g, gn):
    normed = []
    for oh in o_heads:
        ms = jnp.mean(oh * oh, axis=-1, keepdims=True)
        normed.append(oh * lax.rsqrt(ms + RMS_EPS))
    return jnp.concatenate(normed, axis=-1) * gn * (g * jax.nn.sigmoid(g))


def _gla_kernel(*refs, merge):
    if merge:
        (pgf_ref, pgb_ref, waf_ref, baf_ref, wab_ref, bab_ref, s0f_ref, s0b_ref, gn_ref,
         ylo_ref, yhi_ref, st_f, st_b, keep_f, keep_b) = refs
    else:
        (pgf_ref, pgb_ref, waf_ref, baf_ref, wab_ref, bab_ref, s0f_ref, s0b_ref,
         sfin_f_ref, sfin_b_ref, st_f, st_b) = refs
    n = pl.program_id(0)
    n_chunks = pl.num_programs(0)

    @pl.when(n == 0)
    def _():
        st_f[...] = s0f_ref[...]
        st_b[...] = s0b_ref[...]

    outs_f, outs_b = _gla_chunks([(pgf_ref, waf_ref, baf_ref, st_f, False), (pgb_ref, wab_ref, bab_ref, st_b, True)])
    n_batch = pgf_ref.shape[0]
    heads = range(GLA_HEADS)
    if merge:
        half = n_chunks // 2
        m = n_chunks - 1 - n

        @pl.when(n < half)
        def _():
            for bi in range(n_batch):
                keep_f[n, bi] = jnp.concatenate(outs_f[bi], axis=-1).astype(keep_f.dtype)
                keep_b[m - half, bi] = jnp.concatenate(outs_b[bi], axis=-1).astype(keep_b.dtype)

        @pl.when(n >= half)
        def _():
            for bi in range(n_batch):
                kb = keep_b[n - half, bi].astype(f32)
                o_heads = [outs_f[bi][hd] + kb[:, hd * GLA_DV:(hd + 1) * GLA_DV] for hd in heads]
                yhi_ref[bi] = _gla_gate(o_heads, pgf_ref[bi, :, _PG_G:_PG_A].astype(f32), gn_ref[...])
                kf = keep_f[m, bi].astype(f32)
                o_heads = [kf[:, hd * GLA_DV:(hd + 1) * GLA_DV] + outs_b[bi][hd] for hd in heads]
                ylo_ref[bi] = _gla_gate(o_heads, pgb_ref[bi, :, _PG_G:_PG_A].astype(f32), gn_ref[...])
    else:
        @pl.when(n == n_chunks - 1)
        def _():
            sfin_f_ref[...] = st_f[...]
            sfin_b_ref[...] = st_b[...]


def _gla_pass(pg, wa, ba, s0, gn=None):
    n_batch, n_chunks = pg.shape[0], pg.shape[1]
    merge = gn is not None
    half = n_chunks // 2
    const2 = lambda n: (0, 0)
    const3 = lambda n: (0, 0, 0)
    chunk = lambda width, idx: pl.BlockSpec((n_batch, None, GLA_CHUNK, width), lambda n: (0, idx(n), 0, 0))
    w_specs = [pl.BlockSpec((A_LOW_PAD, D_GLA_K), const2), pl.BlockSpec((1, D_GLA_K), const2)]
    s_spec = pl.BlockSpec((n_batch, GLA_DV, D_GLA_K), const3)
    s_shape = jax.ShapeDtypeStruct((n_batch, GLA_DV, D_GLA_K), f32)
    in_specs = [chunk(D_PG, lambda n: n), chunk(D_PG, lambda n: n_chunks - 1 - n)] + w_specs + w_specs + [s_spec,
                                                                                                            s_spec]
    args = [pg, pg, wa[0], ba[0], wa[1], ba[1], s0[0], s0[1]]
    scratch = [pltpu.VMEM((n_batch, GLA_DV, D_GLA_K), f32)] * 2
    if merge:
        assert n_chunks % 2 == 0
        in_specs.append(pl.BlockSpec((1, D_GLA_V), const2))
        args.append(gn)
        out_specs = [chunk(D_GLA_V, lambda n: jnp.minimum(n_chunks - 1 - n, half - 1)),
                     chunk(D_GLA_V, lambda n: jnp.maximum(n - half, 0))]
        y_shape = jax.ShapeDtypeStruct((n_batch, half, GLA_CHUNK, D_GLA_V), f32)
        out_shape = [y_shape, y_shape]
        scratch += [pltpu.VMEM((half, n_batch, GLA_CHUNK, D_GLA_V), bf16)] * 2
    else:
        out_specs = [s_spec, s_spec]
        out_shape = [s_shape, s_shape]
    return pl.pallas_call(
        functools.partial(_gla_kernel, merge=merge),
        grid=(n_chunks,),
        in_specs=in_specs,
        out_specs=out_specs,
        out_shape=out_shape,
        scratch_shapes=scratch,
        compiler_params=_cparams(("arbitrary",)),
        name="gla_merge" if merge else "gla_scan",
    )(*args)


def _pack_rows(v):
    n = v.shape[1] // 2
    w = lax.bitcast_convert_type(v.astype(bf16).astype(f32), jnp.int32)
    return (w[:, :n] & jnp.int32(-65536)) | lax.shift_right_logical(w[:, n:], 16)


def _unpack_rows(w):
    hi = lax.bitcast_convert_type(w & jnp.int32(-65536), f32)
    lo = lax.bitcast_convert_type(lax.shift_left(w, 16), f32)
    return hi, lo


def _outproj_router_kernel(yl_ref, ygl_ref, ygr_ref, x_ref, mod_ref, wo_ref, lg_ref, lb_ref, wrh_ref, wrl_ref,
                           br_ref, x1_ref, hp_ref, ids_ref, rank_ref, wts_ref, cnt_ref, running):
    i = pl.program_id(0)

    @pl.when(i == 0)
    def _():
        running[...] = jnp.zeros_like(running)

    d = D_MODEL
    g1 = mod_ref[0, :, 2 * d:3 * d]
    sh2 = mod_ref[0, :, 3 * d:4 * d]
    sc2 = mod_ref[0, :, 4 * d:5 * d]
    n_parts = x_ref.shape[0] // OUTPROJ_PART
    tm = OUTPROJ_PART
    grid_rows = ygl_ref.shape[2] // n_parts
    tok = [slice(p * tm, (p + 1) * tm) for p in range(n_parts)]
    nt_dims = (((1,), (1,)), ((), ()))
    ne = wrh_ref.shape[0]
    expert = lax.broadcasted_iota(jnp.int32, (ne, tm), 0).astype(f32)
    neg_inf = jnp.float32(-jnp.inf)

    ygs = [jnp.concatenate([ref[0, :, r, :] for r in range(p * grid_rows, (p + 1) * grid_rows)
                            for ref in (ygl_ref, ygr_ref)], axis=0).astype(bf16) for p in range(n_parts)]
    ys = [jnp.dot(yl_ref[tok[p], :], wo_ref[0:D_LRU, :], preferred_element_type=f32)
          + jnp.dot(ygs[p], wo_ref[D_LRU:2 * D_LRU, :], preferred_element_type=f32) for p in range(n_parts)]
    hmods = []
    for p in range(n_parts):
        z = DEEPNORM_ALPHA * x_ref[tok[p], :] + g1 * ys[p]
        mu = jnp.mean(z, axis=-1, keepdims=True)
        zc = z - mu
        var = jnp.mean(zc * zc, axis=-1, keepdims=True)
        x1 = zc * lax.rsqrt(var + LN_EPS) * lg_ref[...] + lb_ref[...]
        x1_ref[tok[p], :] = x1
        hmod = x1 * (1.0 + sc2) + sh2
        hp_ref[tok[p], :] = _pack_rows(hmod)
        hmods.append(hmod)

    logits = []
    for hmod in hmods:
        h_hi = hmod.astype(bf16)
        h_lo = (hmod - h_hi.astype(f32)).astype(bf16)
        logits.append(lax.dot_general(wrh_ref[...], h_hi, nt_dims, preferred_element_type=f32)
                      + lax.dot_general(wrh_ref[...], h_lo, nt_dims, preferred_element_type=f32)
                      + lax.dot_general(wrl_ref[...], h_hi, nt_dims, preferred_element_type=f32) + br_ref[...])
    picks = []
    for live in logits:
        sel = jnp.zeros((ne, tm), f32)
        ids, vals = [], []
        for _ in range(TOP_K):
            m = jnp.max(live, axis=0, keepdims=True)
            j = jnp.min(jnp.where(live == m, expert, float(ne)), axis=0, keepdims=True)
            pick = expert == j
            sel = jnp.where(pick, 1.0, sel)
            live = jnp.where(pick, neg_inf, live)
            ids.append(j)
            vals.append(m)
        picks.append((sel, ids, vals))

    ri = lax.broadcasted_iota(jnp.int32, (tm, tm), 0)
    ci = lax.broadcasted_iota(jnp.int32, (tm, tm), 1)
    earlier = (ri < ci).astype(bf16)
    slot = lax.broadcasted_iota(jnp.int32, (8, tm), 0)
    for p, (sel, ids, vals) in enumerate(picks):
        rank_dense = running[...] + jnp.dot(sel.astype(bf16), earlier, preferred_element_type=f32)
        running[...] = running[...] + jnp.sum(sel, axis=1, keepdims=True)
        exps = [jnp.exp(vk - vals[0]) for vk in vals]
        denom = exps[0] + exps[1] + exps[2] + exps[3]
        ids_out = jnp.zeros((8, tm), f32)
        rank_out = jnp.zeros((8, tm), f32)
        wts_out = jnp.zeros((8, tm), f32)
        for kk in range(TOP_K):
            rk = jnp.sum(jnp.where(expert == ids[kk], rank_dense, 0.0), axis=0, keepdims=True)
            here = slot == kk
            ids_out = jnp.where(here, ids[kk], ids_out)
            rank_out = jnp.where(here, rk, rank_out)
            wts_out = jnp.where(here, exps[kk] / denom, wts_out)
        ids_ref[:, tok[p]] = ids_out.astype(jnp.int32)
        rank_ref[:, tok[p]] = rank_out.astype(jnp.int32)
        wts_ref[tok[p], :] = jnp.concatenate([wts_out, jnp.zeros((LANES - 8, tm), f32)], axis=0).T
    cnt_ref[...] = running[...]


def _outproj_router(yl, yg, xt, mod3, wo, lg, lb, wr_hi, wr_lo, br, tokens_per_batch):
    T = xt.shape[0]
    tm = TOKEN_TILE
    tiles_per_batch = tokens_per_batch // tm
    tokrow = lambda i: (i, 0)
    const = lambda i: (0, 0)
    half_cols = yg[0].shape[1]
    rows_per_tile = tm // (2 * half_cols)
    yg_spec = pl.BlockSpec((1, half_cols, rows_per_tile, D_GLA_V),
                           lambda i: (i // tiles_per_batch, 0, i % tiles_per_batch, 0))
    return pl.pallas_call(
        _outproj_router_kernel,
        grid=(T // tm,),
        in_specs=[pl.BlockSpec((tm, D_LRU), tokrow),
                  yg_spec, yg_spec,
                  pl.BlockSpec((tm, D_MODEL), tokrow),
                  pl.BlockSpec((1, 1, N_MOD * D_MODEL), lambda i: (i // tiles_per_batch, 0, 0)),
                  pl.BlockSpec((D_MODEL, D_MODEL), const),
                  pl.BlockSpec((1, D_MODEL), const),
                  pl.BlockSpec((1, D_MODEL), const),
                  pl.BlockSpec((N_EXPERTS, D_MODEL), const),
                  pl.BlockSpec((N_EXPERTS, D_MODEL), const),
                  pl.BlockSpec((N_EXPERTS, 1), const)],
        out_specs=[pl.BlockSpec((tm, D_MODEL), tokrow),
                   pl.BlockSpec((tm, D_MODEL // 2), tokrow),
                   pl.BlockSpec((8, tm), lambda i: (0, i)),
                   pl.BlockSpec((8, tm), lambda i: (0, i)),
                   pl.BlockSpec((tm, LANES), tokrow),
                   pl.BlockSpec((N_EXPERTS, 1), const)],
        out_shape=[jax.ShapeDtypeStruct((T, D_MODEL), f32),
                   jax.ShapeDtypeStruct((T, D_MODEL // 2), jnp.int32),
                   jax.ShapeDtypeStruct((8, T), jnp.int32),
                   jax.ShapeDtypeStruct((8, T), jnp.int32),
                   jax.ShapeDtypeStruct((T, LANES), f32),
                   jax.ShapeDtypeStruct((N_EXPERTS, 1), f32)],
        scratch_shapes=[pltpu.VMEM((N_EXPERTS, 1), f32)],
        compiler_params=_cparams(("arbitrary",)),
        name="outproj_router",
    )(yl, yg[0], yg[1], xt, mod3, wo, lg, lb, wr_hi, wr_lo, br)


def _route_kernel(ids_ref, rank_ref, cnt_ref, dest_ref, tiles_ref, *, bm):
    ne = cnt_ref.shape[0]
    cnt = cnt_ref[...]
    padded = jnp.floor((cnt + (bm - 1.0)) * (1.0 / bm)) * bm
    sub = lax.broadcasted_iota(jnp.int32, (ne, LANES), 0)
    lane = lax.broadcasted_iota(jnp.int32, (ne, LANES), 1)
    padded_row = jnp.sum(jnp.where(sub == lane, padded, 0.0), axis=0, keepdims=True)
    base = jnp.sum(jnp.where(lane < sub, padded_row, 0.0), axis=1, keepdims=True)
    ends = base + padded

    tc = ids_ref.shape[1]
    expert = lax.broadcasted_iota(jnp.int32, (ne, tc), 0)
    slot = lax.broadcasted_iota(jnp.int32, (8, tc), 0)
    ids = ids_ref[...]
    group_base = jnp.zeros((8, tc), f32)
    for kk in range(TOP_K):
        b_k = jnp.sum(jnp.where(expert == ids[kk:kk + 1], base, 0.0), axis=0, keepdims=True)
        group_base = jnp.where(slot == kk, b_k, group_base)
    dest_ref[...] = group_base.astype(jnp.int32) + rank_ref[...]

    nl = tiles_ref.shape[1]
    start = lax.broadcasted_iota(jnp.int32, (ne, nl), 1).astype(f32) * bm
    te = jnp.minimum(jnp.sum(jnp.where(start >= ends, 1.0, 0.0), axis=0, keepdims=True), ne - 1.0)
    at_te = lax.broadcasted_iota(jnp.int32, (ne, nl), 0).astype(f32) == te
    cnt_te = jnp.sum(jnp.where(at_te, cnt, 0.0), axis=0, keepdims=True)
    base_te = jnp.sum(jnp.where(at_te, base, 0.0), axis=0, keepdims=True)
    valid = jnp.clip(cnt_te - (start[0:1] - base_te), 0.0, float(bm))
    next_group = jnp.sum(jnp.where(at_te, ends, 0.0), axis=0, keepdims=True) * (1.0 / bm)
    last_used = jnp.sum(padded, axis=0, keepdims=True) * (1.0 / bm) - 1.0
    srow = lax.broadcasted_iota(jnp.int32, (8, nl), 0)
    table = jnp.where(srow == 0, te, jnp.where(srow == 1, valid, jnp.where(srow == 2, next_group,
                                                                           jnp.where(srow == 3, last_used, 0.0))))
    tiles_ref[...] = table.astype(jnp.int32)


def _route(ids, rank, cnt, bm, n_tiles):
    T = ids.shape[1]
    tc = 2048
    nl = -(-n_tiles // LANES) * LANES
    tok = lambda i: (0, i)
    const = lambda i: (0, 0)
    return pl.pallas_call(
        functools.partial(_route_kernel, bm=bm),
        grid=(T // tc,),
        in_specs=[pl.BlockSpec((8, tc), tok), pl.BlockSpec((8, tc), tok),
                  pl.BlockSpec((N_EXPERTS, 1), const)],
        out_specs=[pl.BlockSpec((8, tc), tok), pl.BlockSpec((8, nl), const)],
        out_shape=[jax.ShapeDtypeStruct((8, T), jnp.int32), jax.ShapeDtypeStruct((8, nl), jnp.int32)],
        compiler_params=_cparams(("arbitrary",)),
        name="route",
    )(ids, rank, cnt)


def _sc_workers():
    info = plsc.get_sparse_core_info()
    return info.num_cores, info.num_subcores


def _sc_dispatch(rows, dest_flat, n_out):
    T, D = rows.shape
    nc, ns = _sc_workers()
    per_w = T // (nc * ns)
    n_chunks = per_w // SC_CHUNK
    assert n_chunks % 2 == 0
    mesh = plsc.VectorSubcoreMesh(core_axis_name="c", subcore_axis_name="s")

    @functools.partial(
        pl.kernel, mesh=mesh,
        out_type=jax.ShapeDtypeStruct((n_out, D), rows.dtype),
        scratch_types=([pltpu.VMEM((SC_CHUNK,), jnp.int32)] * (2 * TOP_K)
                       + [pltpu.VMEM((SC_CHUNK, D), rows.dtype)] * 2
                       + [pltpu.SemaphoreType.DMA] * (2 * TOP_K)),
    )
    def k(rows_hbm, dest_hbm, out_hbm, *scratch):
        idx_v = (scratch[:TOP_K], scratch[TOP_K:2 * TOP_K])
        rows_v = scratch[2 * TOP_K:2 * TOP_K + 2]
        sems = (scratch[2 * TOP_K + 2:3 * TOP_K + 2], scratch[3 * TOP_K + 2:])
        wid = lax.axis_index("s") * nc + lax.axis_index("c")
        base = wid * per_w

        def load(chunk, b):
            off = base + chunk * SC_CHUNK
            pltpu.sync_copy(rows_hbm.at[pl.ds(off, SC_CHUNK)], rows_v[b])
            for kk in range(TOP_K):
                pltpu.sync_copy(dest_hbm.at[pl.ds(kk * T + off, SC_CHUNK)], idx_v[b][kk])

        load(0, 0)

        @pl.loop(0, n_chunks, step=2)
        def _(j):
            for b in range(2):
                copies = [pltpu.async_copy(rows_v[b], out_hbm.at[idx_v[b][kk]], sems[b][kk])
                          for kk in range(TOP_K)]

                @pl.when(j + b + 1 < n_chunks)
                def _():
                    load(j + b + 1, 1 - b)

                for cp in copies:
                    cp.wait()

    return k(rows, dest_flat)


def _sc_gather(table, idx):
    _, D = table.shape
    N = idx.shape[0]
    nc, ns = _sc_workers()
    per_w = N // (nc * ns)
    n_chunks = per_w // SC_CHUNK
    assert n_chunks % 2 == 0
    mesh = plsc.VectorSubcoreMesh(core_axis_name="c", subcore_axis_name="s")

    @functools.partial(
        pl.kernel, mesh=mesh,
        out_type=jax.ShapeDtypeStruct((N, D), table.dtype),
        scratch_types=([pltpu.VMEM((SC_CHUNK,), jnp.int32)] * 2
                       + [pltpu.VMEM((SC_CHUNK, D), table.dtype)] * 2
                       + [pltpu.SemaphoreType.DMA] * 2),
    )
    def k(table_hbm, idx_hbm, out_hbm, idx0, idx1, buf0, buf1, sem0, sem1):
        idxs, bufs, sems = (idx0, idx1), (buf0, buf1), (sem0, sem1)
        wid = lax.axis_index("s") * nc + lax.axis_index("c")
        base = wid * per_w

        def gather(b):
            return pltpu.make_async_copy(table_hbm.at[idxs[b]], bufs[b], sems[b])

        def start(chunk, b):
            pltpu.sync_copy(idx_hbm.at[pl.ds(base + chunk * SC_CHUNK, SC_CHUNK)], idxs[b])
            gather(b).start()

        def finish(chunk, b):
            gather(b).wait()
            pltpu.sync_copy(bufs[b], out_hbm.at[pl.ds(base + chunk * SC_CHUNK, SC_CHUNK)])

        start(0, 0)

        @pl.loop(0, n_chunks, step=2)
        def _(j):
            start(j + 1, 1)
            finish(j, 0)

            @pl.when(j + 2 < n_chunks)
            def _():
                start(j + 2, 0)

            finish(j + 1, 1)

    return k(table, idx)


def _ffn_kernel(te_ref, tv_ref, tn_ref, tl_ref, xs_ref, wg_hbm, bg_ref, wu_hbm, bu_ref, wd_hbm, bd_ref, eo_ref,
                w_buf, sems, slot_ref):
    i = pl.program_id(0)
    n_tiles = pl.num_programs(0)
    w_hbm = (wg_hbm, wu_hbm, wd_hbm)

    def fetch(e, slot):
        return [pltpu.make_async_copy(w_hbm[m].at[e], w_buf.at[slot, m], sems.at[slot, m]) for m in range(3)]

    @pl.when(i == 0)
    def _():
        slot_ref[0] = 0
        for cp in fetch(te_ref[0], 0):
            cp.start()

    prev = te_ref[jnp.maximum(i - 1, 0)]
    first = (i == 0) | (te_ref[i] != prev)

    @pl.when(first & (i > 0))
    def _():
        slot_ref[0] = 1 - slot_ref[0]

    slot = slot_ref[0]

    @pl.when(first)
    def _():
        for cp in fetch(te_ref[i], slot):
            cp.wait()
        nxt = tn_ref[i]
        e_nxt = te_ref[jnp.minimum(nxt, n_tiles - 1)]

        @pl.when((nxt > i) & (nxt < n_tiles) & (e_nxt != te_ref[i]))
        def _():
            for cp in fetch(e_nxt, 1 - slot):
                cp.start()

    wg_b, wu_b, wd_b = (w_buf.at[slot, m] for m in range(3))
    valid = tv_ref[i]

    def ffn_rows(r0, m):
        blocks = [(r0 + o, min(MOE_BLOCK, m)) for o in range(0, m, MOE_BLOCK)]
        xs, gates, ups, acts = [], [], [], []
        for b0, bm_ in blocks:
            row = lax.broadcasted_iota(jnp.int32, (bm_, 1), 0) + b0
            xw = jnp.where(row < valid, xs_ref[b0:b0 + bm_, :], 0)
            x_hi, x_lo = _unpack_rows(xw)
            xs.append(jnp.concatenate([x_hi, x_lo], axis=1).astype(bf16))
        for x in xs:
            gates.append(jnp.minimum(jnp.dot(x, wg_b[...], preferred_element_type=f32) + bg_ref[...],
                                     SWIGLU_LIMIT))
            ups.append(jnp.clip(jnp.dot(x, wu_b[...], preferred_element_type=f32) + bu_ref[...],
                                -SWIGLU_LIMIT, SWIGLU_LIMIT))
        for gate, up in zip(gates, ups):
            acts.append(((up + 1.0) * gate * jax.nn.sigmoid(SWIGLU_ALPHA * gate)).astype(bf16))
        for (b0, bm_), act in zip(blocks, acts):
            out = jnp.dot(act, wd_b[...], preferred_element_type=f32) + bd_ref[...]
            eo_ref[b0:b0 + bm_, :] = _pack_rows(out)

    def zero_rows(r0, m):
        eo_ref[r0:r0 + m, :] = jnp.zeros((m, eo_ref.shape[1]), eo_ref.dtype)

    for r0 in range(0, xs_ref.shape[0], MOE_PASS):
        lo = 0
        for m in MOE_PASS_SIZES:
            @pl.when((valid > r0 + lo) & ((valid <= r0 + m) | (m == MOE_PASS)))
            def _(r0=r0, m=m):
                ffn_rows(r0, m)
                if m < MOE_PASS:
                    zero_rows(r0 + m, MOE_PASS - m)
            lo = m

        @pl.when(valid <= r0)
        def _(r0=r0):
            zero_rows(r0, MOE_PASS)


def _expert_ffn(tile_expert, tile_valid, tile_next, last_used, xs, w_gate, b_gate, w_up, b_up, w_down, b_down):
    n_rows, dp = xs.shape
    d = 2 * dp
    bm = MOE_TILE
    d_e = w_gate.shape[-1]
    assert d == d_e
    bspec = lambda n_: pl.BlockSpec((None, 1, n_), lambda i, te, tv, tn, tl: (te[i], 0, 0))
    hbm = pl.BlockSpec(memory_space=pl.ANY)
    grid_spec = pltpu.PrefetchScalarGridSpec(
        num_scalar_prefetch=4,
        grid=(n_rows // bm,),
        in_specs=[pl.BlockSpec((bm, dp), lambda i, te, tv, tn, tl: (jnp.minimum(i, tl[0]), 0)),
                  hbm, bspec(d_e), hbm, bspec(d_e), hbm, bspec(d)],
        out_specs=pl.BlockSpec((bm, dp), lambda i, te, tv, tn, tl: (i, 0)),
        scratch_shapes=[pltpu.VMEM((2, 3, d, d_e), f32),
                        pltpu.SemaphoreType.DMA((2, 3)),
                        pltpu.SMEM((1,), jnp.int32)],
    )
    return pl.pallas_call(
        _ffn_kernel,
        grid_spec=grid_spec,
        out_shape=jax.ShapeDtypeStruct((n_rows, dp), jnp.int32),
        compiler_params=_cparams(("arbitrary",)),
        name="expert_ffn",
    )(tile_expert, tile_valid, tile_next, last_used, xs, w_gate, b_gate.reshape(N_EXPERTS, 1, d_e), w_up,
      b_up.reshape(N_EXPERTS, 1, d_e), w_down, b_down.reshape(N_EXPERTS, 1, d))


def _combine_kernel(eg_ref, wts_ref, x1_ref, mod_ref, lg_ref, lb_ref, o_ref):
    w = wts_ref[...]
    y_hi, y_lo = _unpack_rows(eg_ref[0])
    y_hi, y_lo = y_hi * w[:, 0:1], y_lo * w[:, 0:1]
    for kk in range(1, TOP_K):
        e_hi, e_lo = _unpack_rows(eg_ref[kk])
        y_hi = y_hi + e_hi * w[:, kk:kk + 1]
        y_lo = y_lo + e_lo * w[:, kk:kk + 1]
    y = jnp.concatenate([y_hi, y_lo], axis=1)
    z = DEEPNORM_ALPHA * x1_ref[...] + mod_ref[0] * y
    mu = jnp.mean(z, axis=-1, keepdims=True)
    zc = z - mu
    var = jnp.mean(zc * zc, axis=-1, keepdims=True)
    o_ref[...] = zc * lax.rsqrt(var + LN_EPS) * lg_ref[...] + lb_ref[...]


def _combine_ln(eg, wts, x1, mod3, lg, lb, tokens_per_batch):
    T = x1.shape[0]
    tm = TOKEN_TILE
    tiles_per_batch = tokens_per_batch // tm
    const = lambda i: (0, 0)
    return pl.pallas_call(
        _combine_kernel,
        grid=(T // tm,),
        in_specs=[pl.BlockSpec((TOP_K, tm, D_MODEL // 2), lambda i: (0, i, 0)),
                  pl.BlockSpec((tm, LANES), lambda i: (i, 0)),
                  pl.BlockSpec((tm, D_MODEL), lambda i: (i, 0)),
                  pl.BlockSpec((1, 1, D_MODEL), lambda i: (i // tiles_per_batch, 0, N_MOD - 1)),
                  pl.BlockSpec((1, D_MODEL), const),
                  pl.BlockSpec((1, D_MODEL), const)],
        out_specs=pl.BlockSpec((tm, D_MODEL), lambda i: (i, 0)),
        out_shape=jax.ShapeDtypeStruct((T, D_MODEL), f32),
        compiler_params=_cparams(("arbitrary",)),
        name="combine_ln",
    )(eg, wts, x1, mod3, lg, lb)


def _block_diag(w):
    n, c, d = w.shape
    eye = jnp.eye(n, dtype=w.dtype)
    return jnp.einsum('ncd,nm->ncmd', w, eye).reshape(n * c, n * d)


def kernel(x, c, ctx, c_ctx, w_ada, b_ada, w_in, conv_w, conv_b, lru_wa, lru_ba, lru_wx, lru_bx,
           lru_lam, gla_wa, gla_ba, gla_norm_g, w_out, ln1_g, ln1_b, w_router, b_router, w_gate,
           b_gate, w_up, b_up, w_down, b_down, ln2_g, ln2_b):
    B, L, D = x.shape
    Lc = ctx.shape[1]
    T = B * L
    rows = L // GRID_W
    l = 0

    cpad = jnp.zeros((8, D), f32).at[0:B].set(c).at[B].set(c_ctx)
    w_cat = jnp.pad(w_in[l], ((0, 0), (0, D_PROJ - w_in.shape[-1]))).astype(bf16)
    wg = [jnp.concatenate([_block_diag(lru_wa[l, d]), _block_diag(lru_wx[l, d])], axis=1).astype(bf16)
          for d in range(2)]
    bg = [jnp.concatenate([lru_ba[l, d], lru_bx[l, d]])[None] for d in range(2)]
    lam = [lru_lam[l, d][None] for d in range(2)]
    wa = [jnp.pad(gla_wa[l, d], ((0, A_LOW_PAD - GATE_RANK), (0, 0))).astype(bf16) for d in range(2)]
    ba = [gla_ba[l, d][None] for d in range(2)]
    cw, cb = conv_w[l], conv_b[l][None]
    wr_t = w_router[l].T
    wr_hi = wr_t.astype(bf16)
    wr_lo = (wr_t - wr_hi.astype(f32)).astype(bf16)
    br = b_router[l][:, None]
    c_xl, c_gl, c_pg = (0, D_LRU), (D_LRU, 2 * D_LRU), (2 * D_LRU, D_PROJ)

    mod3 = _ada_mod(cpad, w_ada[l], b_ada[l]).reshape(8, 1, N_MOD * D)

    tiles_per_batch = L // INPROJ_TILE
    xl_c, pg_c = _inproj_rows(ctx.reshape(B * Lc, D), mod3, w_cat, lambda i: B, (c_xl, c_pg), (f32, bf16))
    zero_h = jnp.zeros((B, 1, D_LRU), f32)
    zero_s = jnp.zeros((B, GLA_DV, D_GLA_K), f32)
    pg_c = pg_c.reshape(B, Lc // GLA_CHUNK, GLA_CHUNK, D_PG)
    h_ctx = []
    for d in range(2):
        _, _, hf = _lru_pass(xl_c, wg[d], bg[d], lam[d], zero_h, n_batch=B, tile=Lc, reverse=bool(d),
                             conv=(cw, cb, Lc))
        h_ctx.append(hf)
    s_ctx = _gla_pass(pg_c, wa, ba, (zero_s, zero_s))

    xt = x.reshape(T, D)
    xl, gl = _inproj_rows(xt, mod3, w_cat, lambda i: i // tiles_per_batch, (c_xl, c_gl), (f32, bf16))
    hb, u_lru, _ = _lru_pass(xl, wg[1], bg[1], lam[1], h_ctx[1], n_batch=B, tile=LRU_TILE, reverse=True,
                             conv=(cw, cb, GRID_W))
    y_lru, _ = _lru_pass(u_lru, wg[0], bg[0], lam[0], h_ctx[0], n_batch=B, tile=LRU_TILE, reverse=False,
                         merge_with=(hb, gl))
    pg = _inproj_cols(x.reshape(B, rows, GRID_W, D), mod3, w_cat, c_pg)
    y_gla = _gla_pass(pg, wa, ba, s_ctx, gn=gla_norm_g[l][None])

    x1, hp, ids, rank, wts, cnt = _outproj_router(
        y_lru, y_gla, xt, mod3, w_out[l].astype(bf16), ln1_g[l][None], ln1_b[l][None], wr_hi, wr_lo, br, L)

    n_tiles = T * TOP_K // MOE_TILE + N_EXPERTS
    dest, tiles = _route(ids, rank, cnt, MOE_TILE, n_tiles)
    xs = _sc_dispatch(hp, dest.reshape(-1), n_tiles * MOE_TILE)
    eo = _expert_ffn(tiles[0, :n_tiles], tiles[1, :n_tiles], tiles[2, :n_tiles], tiles[3, :1], xs, w_gate[l],
                     b_gate[l], w_up[l], b_up[l], w_down[l], b_down[l])
    eg = _sc_gather(eo, dest[:TOP_K].reshape(-1)).reshape(TOP_K, T, D // 2)
    out = _combine_ln(eg, wts, x1, mod3, ln2_g[l][None], ln2_b[l][None], L)
    return out.reshape(B, L, D)
```

```python
import functools

import jax
import jax.numpy as jnp
from jax import lax
from jax.experimental import pallas as pl
from jax.experimental.pallas import tpu as pltpu
from jax.experimental.pallas import tpu_sc as plsc

D_MODEL = 1024
DEPTH = 1
GRID_W = 64
D_LRU = 512
CONV_W = 4
LRU_C = 8.0
GLA_HEADS = 4
D_GLA_V = 512
D_GLA_K = 256
GLA_DK = 64
GLA_DV = 128
GATE_RANK = 16
GATE_TAU = 16.0
GLA_CHUNK = 64
N_EXPERTS = 32
TOP_K = 4
SWIGLU_LIMIT = 7.0
SWIGLU_ALPHA = 1.702
N_MOD = 6
DEEPNORM_ALPHA = (2.0 * DEPTH) ** 0.25
LN_EPS = 1e-5
RMS_EPS = 1e-6

LANES = 128
A_LOW_PAD = LANES
D_PG = 2 * D_GLA_K + 2 * D_GLA_V + A_LOW_PAD
D_PROJ = 2 * D_LRU + D_PG
TOKEN_TILE = 1024
INPROJ_TILE = 1024
INPROJ_RING_SLOTS = 3
LRU_TILE = 512
MOE_TILE = 1024
MOE_PASS = 512
MOE_PASS_SIZES = (128, 256, MOE_PASS)
MOE_BLOCK = 256
OUTPROJ_PART = 128
SC_CHUNK = 64
VMEM_LIMIT = 48 * 1024 * 1024

f32 = jnp.float32
bf16 = jnp.bfloat16


def _cparams(sem):
    return pltpu.CompilerParams(dimension_semantics=sem, vmem_limit_bytes=VMEM_LIMIT)


def _ada_kernel(c_ref, w_ref, b_ref, o_ref):
    s = c_ref[...]
    s = s * jax.nn.sigmoid(s)
    o_ref[...] = jnp.dot(s.astype(bf16), w_ref[...], preferred_element_type=f32) + b_ref[...]


def _ada_mod(cpad, w, b):
    n = w.shape[1]
    tn = 1024
    return pl.pallas_call(
        _ada_kernel,
        grid=(n // tn,),
        in_specs=[pl.BlockSpec((8, D_MODEL), lambda j: (0, 0)),
                  pl.BlockSpec((D_MODEL, tn), lambda j: (0, j)),
                  pl.BlockSpec((1, tn), lambda j: (0, j))],
        out_specs=pl.BlockSpec((8, tn), lambda j: (0, j)),
        out_shape=jax.ShapeDtypeStruct((8, n), f32),
        compiler_params=_cparams(("arbitrary",)),
        name="ada_mod",
    )(cpad, w, b.reshape(1, n))


def _ring_tile(x_hbm, ring, sems, i, n_steps):
    n_slots, tile_rows = ring.shape[0], ring.shape[1]

    def fetch(step):
        s = step % n_slots
        return pltpu.make_async_copy(x_hbm.at[pl.ds(step * tile_rows, tile_rows), :], ring.at[s], sems.at[s])

    @pl.when(i == 0)
    def _():
        for k in range(n_slots - 1):
            @pl.when(k < n_steps)
            def _(k=k):
                fetch(k).start()

    @pl.when(i + n_slots - 1 < n_steps)
    def _():
        fetch(i + n_slots - 1).start()

    fetch(i).wait()
    return ring.at[i % n_slots]


def _inproj_kernel(x_ref, mod_ref, w_ref, *refs, parts, ring):
    out_refs = refs[:len(parts)]
    if ring:
        x_ref = _ring_tile(x_ref, refs[-2], refs[-1], pl.program_id(0), pl.num_programs(0))
    sh = mod_ref[0, :, 0:D_MODEL]
    sc = mod_ref[0, :, D_MODEL:2 * D_MODEL]
    u = (x_ref[...] * (1.0 + sc) + sh).astype(bf16)
    for (lo, hi), o_ref in zip(parts, out_refs):
        o_ref[...] = jnp.dot(u, w_ref[:, lo:hi], preferred_element_type=f32).astype(o_ref.dtype)


def _inproj_cols_kernel(x_hbm, mod_ref, w_ref, o_ref, xbuf, sems, *, part):
    b, n = pl.program_id(0), pl.program_id(1)
    nb, nn = pl.num_programs(0), pl.num_programs(1)
    cb = xbuf.shape[1]
    g = b * nn + n
    slot = g % 2

    def fetch(bb, nb_, s):
        return [pltpu.make_async_copy(x_hbm.at[bb, :, nb_ * cb + j, :], xbuf.at[s, j], sems.at[s])
                for j in range(cb)]

    @pl.when(g == 0)
    def _():
        for j, cp in enumerate(fetch(b, n, slot)):
            cp.start(priority=j % 2)

    @pl.when(g + 1 < nb * nn)
    def _():
        g1 = g + 1
        for j, cp in enumerate(fetch(g1 // nn, g1 % nn, 1 - slot)):
            cp.start(priority=j % 2)

    for cp in fetch(b, n, slot):
        cp.wait()
    sh = mod_ref[0, :, 0:D_MODEL]
    sc = mod_ref[0, :, D_MODEL:2 * D_MODEL]
    x = xbuf[slot].reshape(cb * xbuf.shape[2], D_MODEL)
    u = (x * (1.0 + sc) + sh).astype(bf16)
    lo, hi = part
    p = jnp.dot(u, w_ref[:, lo:hi], preferred_element_type=f32).astype(o_ref.dtype)
    o_ref[...] = p.reshape(o_ref.shape)


def _inproj_rows(xt, mod3, w_cat, batch_of_tile, parts, dtypes):
    T = xt.shape[0]
    tm = INPROJ_TILE
    ring = T // tm >= INPROJ_RING_SLOTS
    x_spec = pl.BlockSpec(memory_space=pl.ANY) if ring else pl.BlockSpec((tm, D_MODEL), lambda i: (i, 0))
    scratch = ([pltpu.VMEM((INPROJ_RING_SLOTS, tm, D_MODEL), f32), pltpu.SemaphoreType.DMA((INPROJ_RING_SLOTS,))]
               if ring else [])
    return pl.pallas_call(
        functools.partial(_inproj_kernel, parts=parts, ring=ring),
        grid=(T // tm,),
        in_specs=[x_spec,
                  pl.BlockSpec((1, 1, 2 * D_MODEL), lambda i: (batch_of_tile(i), 0, 0)),
                  pl.BlockSpec((D_MODEL, D_PROJ), lambda i: (0, 0))],
        out_specs=[pl.BlockSpec((tm, hi - lo), lambda i: (i, 0)) for lo, hi in parts],
        out_shape=[jax.ShapeDtypeStruct((T, hi - lo), dt) for (lo, hi), dt in zip(parts, dtypes)],
        scratch_shapes=scratch,
        compiler_params=_cparams(("arbitrary",)),
        name="inproj_rows",
    )(xt, mod3, w_cat)


def _inproj_cols(x4, mod3, w_cat, part):
    n_batch, rows, cols, _ = x4.shape
    lo, hi = part
    cb = INPROJ_TILE // rows
    return pl.pallas_call(
        functools.partial(_inproj_cols_kernel, part=part),
        grid=(n_batch, cols // cb),
        in_specs=[pl.BlockSpec(memory_space=pl.ANY),
                  pl.BlockSpec((1, 1, 2 * D_MODEL), lambda b, n: (b, 0, 0)),
                  pl.BlockSpec((D_MODEL, D_PROJ), lambda b, n: (0, 0))],
        out_specs=pl.BlockSpec((1, cb, rows, hi - lo), lambda b, n: (b, n, 0, 0)),
        out_shape=jax.ShapeDtypeStruct((n_batch, cols, rows, hi - lo), bf16),
        scratch_shapes=[pltpu.VMEM((2, cb, rows, D_MODEL), f32), pltpu.SemaphoreType.DMA((2,))],
        compiler_params=_cparams(("arbitrary", "arbitrary")),
        name="inproj_cols",
    )(x4, mod3, w_cat)


def _gelu_tanh(x):
    return 0.5 * x * (1.0 + jnp.tanh(0.7978845608028654 * (x + 0.044715 * (x * x * x))))


def _lru_kernel(*refs, row_w, reverse, merge):
    if merge:
        u_ref, wg_ref, bg_ref, lam_ref, h0_ref, hb_ref, gl_ref, out_ref, hfin_ref, carry, h_nat = refs
    else:
        (xl_hbm, cw_ref, cb_ref, wg_ref, bg_ref, lam_ref, h0_ref, out_ref, u_out_ref, hfin_ref, carry,
         x_buf, x_sems) = refs
    t = pl.program_id(1)

    @pl.when(t == 0)
    def _():
        carry[...] = h0_ref[0]

    n_seg = 8
    blk = lambda v, j: v[j * n_seg:(j + 1) * n_seg]
    if merge:
        u = u_ref[...]
        tt, ch = u.shape
        seg = tt // n_seg
    else:
        _, seg, _, ch = x_buf.shape
        tt = n_seg * seg
        nb, nt = pl.num_programs(0), pl.num_programs(1)
        g = pl.program_id(0) * nt + t
        slot = g % 2

        def fetch(step, s_):
            bb, tt_ = step // nt, step % nt
            tile = bb * nt + (nt - 1 - tt_ if reverse else tt_)
            return [pltpu.make_async_copy(xl_hbm.at[pl.ds(tile * tt + s * seg, seg), :], x_buf.at[s_, :, s, :],
                                          x_sems.at[s_]) for s in range(n_seg)]

        @pl.when(g == 0)
        def _():
            for s, cp in enumerate(fetch(g, slot)):
                cp.start(priority=s % 2)

        @pl.when(g + 1 < nb * nt)
        def _():
            for s, cp in enumerate(fetch(g + 1, 1 - slot)):
                cp.start(priority=s % 2)

        for cp in fetch(g, slot):
            cp.wait()
        x = x_buf[slot].reshape(tt, ch)
        segs_per_row = row_w // seg
        s_idx = lax.broadcasted_iota(jnp.int32, (n_seg, ch), 0)
        has_prev = (s_idx % segs_per_row) != 0
        has_next = (s_idx % segs_per_row) != segs_per_row - 1
        from_prev = lambda v: jnp.where(has_prev, pltpu.roll(v, 1, 0), 0.0)
        from_next = lambda v: jnp.where(has_next, pltpu.roll(v, n_seg - 1, 0), 0.0)
        x_m1 = jnp.concatenate([from_prev(blk(x, seg - 1)), x[:tt - n_seg]], axis=0)
        x_m2 = jnp.concatenate([from_prev(blk(x, seg - 2)), from_prev(blk(x, seg - 1)), x[:tt - 2 * n_seg]],
                               axis=0)
        x_p1 = jnp.concatenate([x[n_seg:], from_next(blk(x, 0))], axis=0)
        cw = cw_ref[...]
        u = cb_ref[...] + x_m2 * cw[0:1] + x_m1 * cw[1:2] + x * cw[2:3] + x_p1 * cw[3:4]
        u_out_ref[...] = u
    rows_q = tt // 4
    g_parts = [jnp.dot(u[q * rows_q:(q + 1) * rows_q].astype(bf16), wg_ref[...], preferred_element_type=f32)
               + bg_ref[...] for q in range(4)]
    lam = lam_ref[...]
    softplus_neg = jnp.maximum(-lam, 0.0) + jnp.log(1.0 + jnp.exp(-jnp.abs(lam)))
    log_a_scale = -LRU_C * softplus_neg

    def recurrence_terms(j):
        q, r = divmod(j * n_seg, rows_q)
        g_j, u_j = g_parts[q][r:r + n_seg], blk(u, j)
        log_a = log_a_scale * jax.nn.sigmoid(g_j[:, 0:ch])
        a_j = jnp.exp(log_a)
        series = log_a * (-2.0 + log_a * (-2.0 + log_a * (-4.0 / 3.0)))
        one_minus_a2 = jnp.where(log_a > -0.005, series, 1.0 - a_j * a_j)
        root = jnp.where(one_minus_a2 > 0.0, one_minus_a2 * lax.rsqrt(one_minus_a2), 0.0)
        return a_j, root * (jax.nn.sigmoid(g_j[:, ch:2 * ch]) * u_j)

    half = seg // 2
    h_loc, a_cum = [None] * seg, [None] * seg
    h_run, a_run = [None, None], [None, None]
    for i in range(half):
        for p in range(2):
            j = p * half + (half - 1 - i if reverse else i)
            a_j, b_j = recurrence_terms(j)
            h_run[p] = b_j if i == 0 else a_j * h_run[p] + b_j
            a_run[p] = a_j if i == 0 else a_j * a_run[p]
            h_loc[j], a_cum[j] = h_run[p], a_run[p]
    pieces = [(s, p) for s in range(n_seg) for p in range(2)]
    if reverse:
        pieces.reverse()
    entering = [[None] * n_seg, [None] * n_seg]
    state = carry[...]
    for s, p in pieces:
        entering[p][s] = state
        state = h_run[p][s:s + 1] + a_run[p][s:s + 1] * state
    carry[...] = state
    hfin_ref[0] = state
    enter = [jnp.concatenate(e, axis=0) for e in entering]
    h_rows = [h_loc[j] + a_cum[j] * enter[j // half] for j in range(seg)]
    if merge:
        for j in range(seg):
            h_nat[:, j, :] = h_rows[j] + blk(hb_ref, j)
        h = h_nat[...].reshape(tt, ch)
        out_ref[...] = (h * _gelu_tanh(gl_ref[...].astype(f32))).astype(out_ref.dtype)
    else:
        out_ref[...] = jnp.concatenate(h_rows, axis=0)


def _lru_pass(x_in, wg, bg, lam, h0, *, n_batch, tile, reverse, conv=None, merge_with=None):
    T = x_in.shape[0]
    nt = T // n_batch // tile
    merge = merge_with is not None
    assert merge != (conv is not None)
    seg = tile // 8

    def tok(b, t):
        return (b * nt + (nt - 1 - t if reverse else t), 0)

    tile_spec = pl.BlockSpec((tile, D_LRU), tok)
    const = lambda b, t: (0, 0)
    gate_specs = [pl.BlockSpec((D_LRU, 2 * D_LRU), const),
                  pl.BlockSpec((1, 2 * D_LRU), const),
                  pl.BlockSpec((1, D_LRU), const),
                  pl.BlockSpec((1, 1, D_LRU), lambda b, t: (b, 0, 0))]
    state_spec = pl.BlockSpec((1, 1, D_LRU), lambda b, t: (b, 0, 0))
    state_shape = jax.ShapeDtypeStruct((n_batch, 1, D_LRU), f32)
    scratch = [pltpu.VMEM((1, D_LRU), f32)]
    if merge:
        hb, gl = merge_with
        row_w = None
        in_specs = [tile_spec] + gate_specs + [tile_spec, tile_spec]
        args = [x_in, wg, bg, lam, h0, hb, gl]
        scratch += [pltpu.VMEM((8, seg, D_LRU), f32)]
        out_specs = [tile_spec, state_spec]
        out_shape = [jax.ShapeDtypeStruct((T, D_LRU), bf16), state_shape]
    else:
        conv_w, conv_b, row_w = conv
        assert row_w % seg == 0 and seg >= 2
        in_specs = [pl.BlockSpec(memory_space=pl.ANY),
                    pl.BlockSpec((CONV_W, D_LRU), const),
                    pl.BlockSpec((1, D_LRU), const)] + gate_specs
        args = [x_in, conv_w, conv_b, wg, bg, lam, h0]
        scratch += [pltpu.VMEM((2, seg, 8, D_LRU), f32), pltpu.SemaphoreType.DMA((2,))]
        out_specs = [tile_spec, tile_spec, state_spec]
        out_shape = [jax.ShapeDtypeStruct((T, D_LRU), f32), jax.ShapeDtypeStruct((T, D_LRU), f32), state_shape]
    return pl.pallas_call(
        functools.partial(_lru_kernel, row_w=row_w, reverse=reverse, merge=merge),
        grid=(n_batch, nt),
        in_specs=in_specs,
        out_specs=out_specs,
        out_shape=out_shape,
        scratch_shapes=scratch,
        compiler_params=_cparams(("arbitrary", "arbitrary")),
        name="lru_merge" if merge else "lru_scan",
    )(*args)


_PG_Q, _PG_K, _PG_V = 0, D_GLA_K, 2 * D_GLA_K
_PG_G, _PG_A = 2 * D_GLA_K + D_GLA_V, 2 * D_GLA_K + 2 * D_GLA_V


def _gla_chunks(streams):
    ck = GLA_CHUNK
    nh = GLA_HEADS
    head_of_lane = lax.broadcasted_iota(jnp.int32, (1, D_GLA_K), 1) // GLA_DK
    nt_dims = (((1,), (1,)), ((), ()))
    tn_dims = (((0,), (0,)), ((), ()))
    ii = lax.broadcasted_iota(jnp.int32, (nh * ck, ck), 0) & (ck - 1)
    jj = lax.broadcasted_iota(jnp.int32, (nh * ck, ck), 1)

    units = []
    for pg_ref, wa_ref, ba_ref, state, reverse in streams:
        n_batch = pg_ref.shape[0]
        rows_all = n_batch * ck
        a_low = jnp.concatenate([pg_ref[bi, :, _PG_A:_PG_A + A_LOW_PAD] for bi in range(n_batch)], axis=0)
        z = jnp.dot(a_low, wa_ref[...], preferred_element_type=f32) + ba_ref[...]
        bcum_all = (jnp.minimum(z, 0.0) - jnp.log(1.0 + jnp.exp(-jnp.abs(z)))) * (1.0 / GATE_TAU)
        step = lax.broadcasted_iota(jnp.int32, (rows_all, D_GLA_K), 0) & (ck - 1)
        d = 1
        while d < ck:
            if reverse:
                bcum_all = bcum_all + jnp.where(step < ck - d, pltpu.roll(bcum_all, rows_all - d, 0), 0.0)
            else:
                bcum_all = bcum_all + jnp.where(step >= d, pltpu.roll(bcum_all, d, 0), 0.0)
            d *= 2
        seen = (jj >= ii) if reverse else (jj <= ii)
        for bi in range(n_batch):
            units.append(dict(pg=pg_ref, bi=bi, state=state, reverse=reverse, seen=seen,
                              bcum=bcum_all[bi * ck:(bi + 1) * ck]))

    for u in units:
        pg_ref, bi, bcum = u["pg"], u["bi"], u["bcum"]
        q = pg_ref[bi, :, _PG_Q:_PG_K].astype(f32) * (GLA_DK ** -0.5)
        k = pg_ref[bi, :, _PG_K:_PG_V].astype(f32)
        u["btot"] = bcum[0:1] if u["reverse"] else bcum[ck - 1:ck]
        q_dec = q * jnp.exp(bcum)
        k_dec = (k * jnp.exp(-bcum)).astype(bf16)
        k_end = k * jnp.exp(u["btot"] - bcum)
        u["s_t"] = u["state"][bi]
        by_head = lambda t: jnp.concatenate([jnp.where(head_of_lane == hd, t, 0.0) for hd in range(nh)],
                                            axis=0).astype(bf16)
        u["q_blk"] = by_head(q_dec)
        u["k_blk"] = by_head(k_end)
        u["rhs"] = jnp.concatenate([u["s_t"].astype(bf16), k_dec], axis=0)
    for u in units:
        u["qs"] = lax.dot_general(u["q_blk"], u["rhs"], nt_dims, preferred_element_type=f32)
    for u in units:
        u["scores"] = jnp.where(u["seen"], u["qs"][:, GLA_DV:GLA_DV + ck], 0.0).astype(bf16)
    for u in units:
        v = u["pg"][u["bi"], :, _PG_V:_PG_G]
        u["intra"] = [jnp.dot(u["scores"][hd * ck:(hd + 1) * ck], v[:, hd * GLA_DV:(hd + 1) * GLA_DV],
                              preferred_element_type=f32) for hd in range(nh)]
        v_stack = jnp.concatenate([v[:, hd * GLA_DV:(hd + 1) * GLA_DV] for hd in range(nh)], axis=0)
        u["kv_t"] = lax.dot_general(v_stack, u["k_blk"], tn_dims, preferred_element_type=f32)
    for u in units:
        u["outs"] = [u["intra"][hd] + u["qs"][hd * ck:(hd + 1) * ck, 0:GLA_DV] for hd in range(nh)]
        u["state"][u["bi"]] = u["s_t"] * jnp.exp(u["btot"]) + u["kv_t"]
    n_batch = streams[0][0].shape[0]
    return [[u["outs"] for u in units[si * n_batch:(si + 1) * n_batch]] for si in range(len(streams))]


def _gla_gate(o_heads, g, gn):
    normed = []
    for oh in o_heads:
        ms = jnp.mean(oh * oh, axis=-1, keepdims=True)
        normed.append(oh * lax.rsqrt(ms + RMS_EPS))
    return jnp.concatenate(normed, axis=-1) * gn * (g * jax.nn.sigmoid(g))


def _gla_kernel(*refs, merge):
    if merge:
        (pgf_ref, pgb_ref, waf_ref, baf_ref, wab_ref, bab_ref, s0f_ref, s0b_ref, gn_ref,
         ylo_ref, yhi_ref, st_f, st_b, keep_f, keep_b) = refs
    else:
        (pgf_ref, pgb_ref, waf_ref, baf_ref, wab_ref, bab_ref, s0f_ref, s0b_ref,
         sfin_f_ref, sfin_b_ref, st_f, st_b) = refs
    n = pl.program_id(0)
    n_chunks = pl.num_programs(0)

    @pl.when(n == 0)
    def _():
        st_f[...] = s0f_ref[...]
        st_b[...] = s0b_ref[...]

    outs_f, outs_b = _gla_chunks([(pgf_ref, waf_ref, baf_ref, st_f, False), (pgb_ref, wab_ref, bab_ref, st_b, True)])
    n_batch = pgf_ref.shape[0]
    heads = range(GLA_HEADS)
    if merge:
        half = n_chunks // 2
        m = n_chunks - 1 - n

        @pl.when(n < half)
        def _():
            for bi in range(n_batch):
                keep_f[n, bi] = jnp.concatenate(outs_f[bi], axis=-1).astype(keep_f.dtype)
                keep_b[m - half, bi] = jnp.concatenate(outs_b[bi], axis=-1).astype(keep_b.dtype)

        @pl.when(n >= half)
        def _():
            for bi in range(n_batch):
                kb = keep_b[n - half, bi].astype(f32)
                o_heads = [outs_f[bi][hd] + kb[:, hd * GLA_DV:(hd + 1) * GLA_DV] for hd in heads]
                yhi_ref[bi] = _gla_gate(o_heads, pgf_ref[bi, :, _PG_G:_PG_A].astype(f32), gn_ref[...])
                kf = keep_f[m, bi].astype(f32)
                o_heads = [kf[:, hd * GLA_DV:(hd + 1) * GLA_DV] + outs_b[bi][hd] for hd in heads]
                ylo_ref[bi] = _gla_gate(o_heads, pgb_ref[bi, :, _PG_G:_PG_A].astype(f32), gn_ref[...])
    else:
        @pl.when(n == n_chunks - 1)
        def _():
            sfin_f_ref[...] = st_f[...]
            sfin_b_ref[...] = st_b[...]


def _gla_pass(pg, wa, ba, s0, gn=None):
    n_batch, n_chunks = pg.shape[0], pg.shape[1]
    merge = gn is not None
    half = n_chunks // 2
    const2 = lambda n: (0, 0)
    const3 = lambda n: (0, 0, 0)
    chunk = lambda width, idx: pl.BlockSpec((n_batch, None, GLA_CHUNK, width), lambda n: (0, idx(n), 0, 0))
    w_specs = [pl.BlockSpec((A_LOW_PAD, D_GLA_K), const2), pl.BlockSpec((1, D_GLA_K), const2)]
    s_spec = pl.BlockSpec((n_batch, GLA_DV, D_GLA_K), const3)
    s_shape = jax.ShapeDtypeStruct((n_batch, GLA_DV, D_GLA_K), f32)
    in_specs = [chunk(D_PG, lambda n: n), chunk(D_PG, lambda n: n_chunks - 1 - n)] + w_specs + w_specs + [s_spec,
                                                                                                            s_spec]
    args = [pg, pg, wa[0], ba[0], wa[1], ba[1], s0[0], s0[1]]
    scratch = [pltpu.VMEM((n_batch, GLA_DV, D_GLA_K), f32)] * 2
    if merge:
        assert n_chunks % 2 == 0
        in_specs.append(pl.BlockSpec((1, D_GLA_V), const2))
        args.append(gn)
        out_specs = [chunk(D_GLA_V, lambda n: jnp.minimum(n_chunks - 1 - n, half - 1)),
                     chunk(D_GLA_V, lambda n: jnp.maximum(n - half, 0))]
        y_shape = jax.ShapeDtypeStruct((n_batch, half, GLA_CHUNK, D_GLA_V), f32)
        out_shape = [y_shape, y_shape]
        scratch += [pltpu.VMEM((half, n_batch, GLA_CHUNK, D_GLA_V), bf16)] * 2
    else:
        out_specs = [s_spec, s_spec]
        out_shape = [s_shape, s_shape]
    return pl.pallas_call(
        functools.partial(_gla_kernel, merge=merge),
        grid=(n_chunks,),
        in_specs=in_specs,
        out_specs=out_specs,
        out_shape=out_shape,
        scratch_shapes=scratch,
        compiler_params=_cparams(("arbitrary",)),
        name="gla_merge" if merge else "gla_scan",
    )(*args)


def _pack_rows(v):
    n = v.shape[1] // 2
    w = lax.bitcast_convert_type(v.astype(bf16).astype(f32), jnp.int32)
    return (w[:, :n] & jnp.int32(-65536)) | lax.shift_right_logical(w[:, n:], 16)


def _unpack_rows(w):
    hi = lax.bitcast_convert_type(w & jnp.int32(-65536), f32)
    lo = lax.bitcast_convert_type(lax.shift_left(w, 16), f32)
    return hi, lo


def _outproj_router_kernel(yl_ref, ygl_ref, ygr_ref, x_ref, mod_ref, wo_ref, lg_ref, lb_ref, wrh_ref, wrl_ref,
                           br_ref, x1_ref, hp_ref, ids_ref, rank_ref, wts_ref, cnt_ref, running):
    i = pl.program_id(0)

    @pl.when(i == 0)
    def _():
        running[...] = jnp.zeros_like(running)

    d = D_MODEL
    g1 = mod_ref[0, :, 2 * d:3 * d]
    sh2 = mod_ref[0, :, 3 * d:4 * d]
    sc2 = mod_ref[0, :, 4 * d:5 * d]
    n_parts = x_ref.shape[0] // OUTPROJ_PART
    tm = OUTPROJ_PART
    grid_rows = ygl_ref.shape[2] // n_parts
    tok = [slice(p * tm, (p + 1) * tm) for p in range(n_parts)]
    nt_dims = (((1,), (1,)), ((), ()))
    ne = wrh_ref.shape[0]
    expert = lax.broadcasted_iota(jnp.int32, (ne, tm), 0).astype(f32)
    neg_inf = jnp.float32(-jnp.inf)

    ygs = [jnp.concatenate([ref[0, :, r, :] for r in range(p * grid_rows, (p + 1) * grid_rows)
                            for ref in (ygl_ref, ygr_ref)], axis=0).astype(bf16) for p in range(n_parts)]
    ys = [jnp.dot(yl_ref[tok[p], :], wo_ref[0:D_LRU, :], preferred_element_type=f32)
          + jnp.dot(ygs[p], wo_ref[D_LRU:2 * D_LRU, :], preferred_element_type=f32) for p in range(n_parts)]
    hmods = []
    for p in range(n_parts):
        z = DEEPNORM_ALPHA * x_ref[tok[p], :] + g1 * ys[p]
        mu = jnp.mean(z, axis=-1, keepdims=True)
        zc = z - mu
        var = jnp.mean(zc * zc, axis=-1, keepdims=True)
        x1 = zc * lax.rsqrt(var + LN_EPS) * lg_ref[...] + lb_ref[...]
        x1_ref[tok[p], :] = x1
        hmod = x1 * (1.0 + sc2) + sh2
        hp_ref[tok[p], :] = _pack_rows(hmod)
        hmods.append(hmod)

    logits = []
    for hmod in hmods:
        h_hi = hmod.astype(bf16)
        h_lo = (hmod - h_hi.astype(f32)).astype(bf16)
        logits.append(lax.dot_general(wrh_ref[...], h_hi, nt_dims, preferred_element_type=f32)
                      + lax.dot_general(wrh_ref[...], h_lo, nt_dims, preferred_element_type=f32)
                      + lax.dot_general(wrl_ref[...], h_hi, nt_dims, preferred_element_type=f32) + br_ref[...])
    picks = []
    for live in logits:
        sel = jnp.zeros((ne, tm), f32)
        ids, vals = [], []
        for _ in range(TOP_K):
            m = jnp.max(live, axis=0, keepdims=True)
            j = jnp.min(jnp.where(live == m, expert, float(ne)), axis=0, keepdims=True)
            pick = expert == j
            sel = jnp.where(pick, 1.0, sel)
            live = jnp.where(pick, neg_inf, live)
            ids.append(j)
            vals.append(m)
        picks.append((sel, ids, vals))

    ri = lax.broadcasted_iota(jnp.int32, (tm, tm), 0)
    ci = lax.broadcasted_iota(jnp.int32, (tm, tm), 1)
    earlier = (ri < ci).astype(bf16)
    slot = lax.broadcasted_iota(jnp.int32, (8, tm), 0)
    for p, (sel, ids, vals) in enumerate(picks):
        rank_dense = running[...] + jnp.dot(sel.astype(bf16), earlier, preferred_element_type=f32)
        running[...] = running[...] + jnp.sum(sel, axis=1, keepdims=True)
        exps = [jnp.exp(vk - vals[0]) for vk in vals]
        denom = exps[0] + exps[1] + exps[2] + exps[3]
        ids_out = jnp.zeros((8, tm), f32)
        rank_out = jnp.zeros((8, tm), f32)
        wts_out = jnp.zeros((8, tm), f32)
        for kk in range(TOP_K):
            rk = jnp.sum(jnp.where(expert == ids[kk], rank_dense, 0.0), axis=0, keepdims=True)
            here = slot == kk
            ids_out = jnp.where(here, ids[kk], ids_out)
            rank_out = jnp.where(here, rk, rank_out)
            wts_out = jnp.where(here, exps[kk] / denom, wts_out)
        ids_ref[:, tok[p]] = ids_out.astype(jnp.int32)
        rank_ref[:, tok[p]] = rank_out.astype(jnp.int32)
        wts_ref[tok[p], :] = jnp.concatenate([wts_out, jnp.zeros((LANES - 8, tm), f32)], axis=0).T
    cnt_ref[...] = running[...]


def _outproj_router(yl, yg, xt, mod3, wo, lg, lb, wr_hi, wr_lo, br, tokens_per_batch):
    T = xt.shape[0]
    tm = TOKEN_TILE
    tiles_per_batch = tokens_per_batch // tm
    tokrow = lambda i: (i, 0)
    const = lambda i: (0, 0)
    half_cols = yg[0].shape[1]
    rows_per_tile = tm // (2 * half_cols)
    yg_spec = pl.BlockSpec((1, half_cols, rows_per_tile, D_GLA_V),
                           lambda i: (i // tiles_per_batch, 0, i % tiles_per_batch, 0))
    return pl.pallas_call(
        _outproj_router_kernel,
        grid=(T // tm,),
        in_specs=[pl.BlockSpec((tm, D_LRU), tokrow),
                  yg_spec, yg_spec,
                  pl.BlockSpec((tm, D_MODEL), tokrow),
                  pl.BlockSpec((1, 1, N_MOD * D_MODEL), lambda i: (i // tiles_per_batch, 0, 0)),
                  pl.BlockSpec((D_MODEL, D_MODEL), const),
                  pl.BlockSpec((1, D_MODEL), const),
                  pl.BlockSpec((1, D_MODEL), const),
                  pl.BlockSpec((N_EXPERTS, D_MODEL), const),
                  pl.BlockSpec((N_EXPERTS, D_MODEL), const),
                  pl.BlockSpec((N_EXPERTS, 1), const)],
        out_specs=[pl.BlockSpec((tm, D_MODEL), tokrow),
                   pl.BlockSpec((tm, D_MODEL // 2), tokrow),
                   pl.BlockSpec((8, tm), lambda i: (0, i)),
                   pl.BlockSpec((8, tm), lambda i: (0, i)),
                   pl.BlockSpec((tm, LANES), tokrow),
                   pl.BlockSpec((N_EXPERTS, 1), const)],
        out_shape=[jax.ShapeDtypeStruct((T, D_MODEL), f32),
                   jax.ShapeDtypeStruct((T, D_MODEL // 2), jnp.int32),
                   jax.ShapeDtypeStruct((8, T), jnp.int32),
                   jax.ShapeDtypeStruct((8, T), jnp.int32),
                   jax.ShapeDtypeStruct((T, LANES), f32),
                   jax.ShapeDtypeStruct((N_EXPERTS, 1), f32)],
        scratch_shapes=[pltpu.VMEM((N_EXPERTS, 1), f32)],
        compiler_params=_cparams(("arbitrary",)),
        name="outproj_router",
    )(yl, yg[0], yg[1], xt, mod3, wo, lg, lb, wr_hi, wr_lo, br)


def _route_kernel(ids_ref, rank_ref, cnt_ref, dest_ref, tiles_ref, *, bm):
    ne = cnt_ref.shape[0]
    cnt = cnt_ref[...]
    padded = jnp.floor((cnt + (bm - 1.0)) * (1.0 / bm)) * bm
    sub = lax.broadcasted_iota(jnp.int32, (ne, LANES), 0)
    lane = lax.broadcasted_iota(jnp.int32, (ne, LANES), 1)
    padded_row = jnp.sum(jnp.where(sub == lane, padded, 0.0), axis=0, keepdims=True)
    base = jnp.sum(jnp.where(lane < sub, padded_row, 0.0), axis=1, keepdims=True)
    ends = base + padded

    tc = ids_ref.shape[1]
    expert = lax.broadcasted_iota(jnp.int32, (ne, tc), 0)
    slot = lax.broadcasted_iota(jnp.int32, (8, tc), 0)
    ids = ids_ref[...]
    group_base = jnp.zeros((8, tc), f32)
    for kk in range(TOP_K):
        b_k = jnp.sum(jnp.where(expert == ids[kk:kk + 1], base, 0.0), axis=0, keepdims=True)
        group_base = jnp.where(slot == kk, b_k, group_base)
    dest_ref[...] = group_base.astype(jnp.int32) + rank_ref[...]

    nl = tiles_ref.shape[1]
    start = lax.broadcasted_iota(jnp.int32, (ne, nl), 1).astype(f32) * bm
    te = jnp.minimum(jnp.sum(jnp.where(start >= ends, 1.0, 0.0), axis=0, keepdims=True), ne - 1.0)
    at_te = lax.broadcasted_iota(jnp.int32, (ne, nl), 0).astype(f32) == te
    cnt_te = jnp.sum(jnp.where(at_te, cnt, 0.0), axis=0, keepdims=True)
    base_te = jnp.sum(jnp.where(at_te, base, 0.0), axis=0, keepdims=True)
    valid = jnp.clip(cnt_te - (start[0:1] - base_te), 0.0, float(bm))
    next_group = jnp.sum(jnp.where(at_te, ends, 0.0), axis=0, keepdims=True) * (1.0 / bm)
    last_used = jnp.sum(padded, axis=0, keepdims=True) * (1.0 / bm) - 1.0
    srow = lax.broadcasted_iota(jnp.int32, (8, nl), 0)
    table = jnp.where(srow == 0, te, jnp.where(srow == 1, valid, jnp.where(srow == 2, next_group,
                                                                           jnp.where(srow == 3, last_used, 0.0))))
    tiles_ref[...] = table.astype(jnp.int32)


def _route(ids, rank, cnt, bm, n_tiles):
    T = ids.shape[1]
    tc = 2048
    nl = -(-n_tiles // LANES) * LANES
    tok = lambda i: (0, i)
    const = lambda i: (0, 0)
    return pl.pallas_call(
        functools.partial(_route_kernel, bm=bm),
        grid=(T // tc,),
        in_specs=[pl.BlockSpec((8, tc), tok), pl.BlockSpec((8, tc), tok),
                  pl.BlockSpec((N_EXPERTS, 1), const)],
        out_specs=[pl.BlockSpec((8, tc), tok), pl.BlockSpec((8, nl), const)],
        out_shape=[jax.ShapeDtypeStruct((8, T), jnp.int32), jax.ShapeDtypeStruct((8, nl), jnp.int32)],
        compiler_params=_cparams(("arbitrary",)),
        name="route",
    )(ids, rank, cnt)


def _sc_workers():
    info = plsc.get_sparse_core_info()
    return info.num_cores, info.num_subcores


def _sc_dispatch(rows, dest_flat, n_out):
    T, D = rows.shape
    nc, ns = _sc_workers()
    per_w = T // (nc * ns)
    n_chunks = per_w // SC_CHUNK
    assert n_chunks % 2 == 0
    mesh = plsc.VectorSubcoreMesh(core_axis_name="c", subcore_axis_name="s")

    @functools.partial(
        pl.kernel, mesh=mesh,
        out_type=jax.ShapeDtypeStruct((n_out, D), rows.dtype),
        scratch_types=([pltpu.VMEM((SC_CHUNK,), jnp.int32)] * (2 * TOP_K)
                       + [pltpu.VMEM((SC_CHUNK, D), rows.dtype)] * 2
                       + [pltpu.SemaphoreType.DMA] * (2 * TOP_K)),
    )
    def k(rows_hbm, dest_hbm, out_hbm, *scratch):
        idx_v = (scratch[:TOP_K], scratch[TOP_K:2 * TOP_K])
        rows_v = scratch[2 * TOP_K:2 * TOP_K + 2]
        sems = (scratch[2 * TOP_K + 2:3 * TOP_K + 2], scratch[3 * TOP_K + 2:])
        wid = lax.axis_index("s") * nc + lax.axis_index("c")
        base = wid * per_w

        def load(chunk, b):
            off = base + chunk * SC_CHUNK
            pltpu.sync_copy(rows_hbm.at[pl.ds(off, SC_CHUNK)], rows_v[b])
            for kk in range(TOP_K):
                pltpu.sync_copy(dest_hbm.at[pl.ds(kk * T + off, SC_CHUNK)], idx_v[b][kk])

        load(0, 0)

        @pl.loop(0, n_chunks, step=2)
        def _(j):
            for b in range(2):
                copies = [pltpu.async_copy(rows_v[b], out_hbm.at[idx_v[b][kk]], sems[b][kk])
                          for kk in range(TOP_K)]

                @pl.when(j + b + 1 < n_chunks)
                def _():
                    load(j + b + 1, 1 - b)

                for cp in copies:
                    cp.wait()

    return k(rows, dest_flat)


def _sc_gather(table, idx):
    _, D = table.shape
    N = idx.shape[0]
    nc, ns = _sc_workers()
    per_w = N // (nc * ns)
    n_chunks = per_w // SC_CHUNK
    assert n_chunks % 2 == 0
    mesh = plsc.VectorSubcoreMesh(core_axis_name="c", subcore_axis_name="s")

    @functools.partial(
        pl.kernel, mesh=mesh,
        out_type=jax.ShapeDtypeStruct((N, D), table.dtype),
        scratch_types=([pltpu.VMEM((SC_CHUNK,), jnp.int32)] * 2
                       + [pltpu.VMEM((SC_CHUNK, D), table.dtype)] * 2
                       + [pltpu.SemaphoreType.DMA] * 2),
    )
    def k(table_hbm, idx_hbm, out_hbm, idx0, idx1, buf0, buf1, sem0, sem1):
        idxs, bufs, sems = (idx0, idx1), (buf0, buf1), (sem0, sem1)
        wid = lax.axis_index("s") * nc + lax.axis_index("c")
        base = wid * per_w

        def gather(b):
            return pltpu.make_async_copy(table_hbm.at[idxs[b]], bufs[b], sems[b])

        def start(chunk, b):
            pltpu.sync_copy(idx_hbm.at[pl.ds(base + chunk * SC_CHUNK, SC_CHUNK)], idxs[b])
            gather(b).start()

        def finish(chunk, b):
            gather(b).wait()
            pltpu.sync_copy(bufs[b], out_hbm.at[pl.ds(base + chunk * SC_CHUNK, SC_CHUNK)])

        start(0, 0)

        @pl.loop(0, n_chunks, step=2)
        def _(j):
            start(j + 1, 1)
            finish(j, 0)

            @pl.when(j + 2 < n_chunks)
            def _():
                start(j + 2, 0)

            finish(j + 1, 1)

    return k(table, idx)


def _ffn_kernel(te_ref, tv_ref, tn_ref, tl_ref, xs_ref, wg_hbm, bg_ref, wu_hbm, bu_ref, wd_hbm, bd_ref, eo_ref,
                w_buf, sems, slot_ref):
    i = pl.program_id(0)
    n_tiles = pl.num_programs(0)
    w_hbm = (wg_hbm, wu_hbm, wd_hbm)

    def fetch(e, slot):
        return [pltpu.make_async_copy(w_hbm[m].at[e], w_buf.at[slot, m], sems.at[slot, m]) for m in range(3)]

    @pl.when(i == 0)
    def _():
        slot_ref[0] = 0
        for cp in fetch(te_ref[0], 0):
            cp.start()

    prev = te_ref[jnp.maximum(i - 1, 0)]
    first = (i == 0) | (te_ref[i] != prev)

    @pl.when(first & (i > 0))
    def _():
        slot_ref[0] = 1 - slot_ref[0]

    slot = slot_ref[0]

    @pl.when(first)
    def _():
        for cp in fetch(te_ref[i], slot):
            cp.wait()
        nxt = tn_ref[i]
        e_nxt = te_ref[jnp.minimum(nxt, n_tiles - 1)]

        @pl.when((nxt > i) & (nxt < n_tiles) & (e_nxt != te_ref[i]))
        def _():
            for cp in fetch(e_nxt, 1 - slot):
                cp.start()

    wg_b, wu_b, wd_b = (w_buf.at[slot, m] for m in range(3))
    valid = tv_ref[i]

    def ffn_rows(r0, m):
        blocks = [(r0 + o, min(MOE_BLOCK, m)) for o in range(0, m, MOE_BLOCK)]
        xs, gates, ups, acts = [], [], [], []
        for b0, bm_ in blocks:
            row = lax.broadcasted_iota(jnp.int32, (bm_, 1), 0) + b0
            xw = jnp.where(row < valid, xs_ref[b0:b0 + bm_, :], 0)
            x_hi, x_lo = _unpack_rows(xw)
            xs.append(jnp.concatenate([x_hi, x_lo], axis=1).astype(bf16))
        for x in xs:
            gates.append(jnp.minimum(jnp.dot(x, wg_b[...], preferred_element_type=f32) + bg_ref[...],
                                     SWIGLU_LIMIT))
            ups.append(jnp.clip(jnp.dot(x, wu_b[...], preferred_element_type=f32) + bu_ref[...],
                                -SWIGLU_LIMIT, SWIGLU_LIMIT))
        for gate, up in zip(gates, ups):
            acts.append(((up + 1.0) * gate * jax.nn.sigmoid(SWIGLU_ALPHA * gate)).astype(bf16))
        for (b0, bm_), act in zip(blocks, acts):
            out = jnp.dot(act, wd_b[...], preferred_element_type=f32) + bd_ref[...]
            eo_ref[b0:b0 + bm_, :] = _pack_rows(out)

    def zero_rows(r0, m):
        eo_ref[r0:r0 + m, :] = jnp.zeros((m, eo_ref.shape[1]), eo_ref.dtype)

    for r0 in range(0, xs_ref.shape[0], MOE_PASS):
        lo = 0
        for m in MOE_PASS_SIZES:
            @pl.when((valid > r0 + lo) & ((valid <= r0 + m) | (m == MOE_PASS)))
            def _(r0=r0, m=m):
                ffn_rows(r0, m)
                if m < MOE_PASS:
                    zero_rows(r0 + m, MOE_PASS - m)
            lo = m

        @pl.when(valid <= r0)
        def _(r0=r0):
            zero_rows(r0, MOE_PASS)


def _expert_ffn(tile_expert, tile_valid, tile_next, last_used, xs, w_gate, b_gate, w_up, b_up, w_down, b_down):
    n_rows, dp = xs.shape
    d = 2 * dp
    bm = MOE_TILE
    d_e = w_gate.shape[-1]
    assert d == d_e
    bspec = lambda n_: pl.BlockSpec((None, 1, n_), lambda i, te, tv, tn, tl: (te[i], 0, 0))
    hbm = pl.BlockSpec(memory_space=pl.ANY)
    grid_spec = pltpu.PrefetchScalarGridSpec(
        num_scalar_prefetch=4,
        grid=(n_rows // bm,),
        in_specs=[pl.BlockSpec((bm, dp), lambda i, te, tv, tn, tl: (jnp.minimum(i, tl[0]), 0)),
                  hbm, bspec(d_e), hbm, bspec(d_e), hbm, bspec(d)],
        out_specs=pl.BlockSpec((bm, dp), lambda i, te, tv, tn, tl: (i, 0)),
        scratch_shapes=[pltpu.VMEM((2, 3, d, d_e), f32),
                        pltpu.SemaphoreType.DMA((2, 3)),
                        pltpu.SMEM((1,), jnp.int32)],
    )
    return pl.pallas_call(
        _ffn_kernel,
        grid_spec=grid_spec,
        out_shape=jax.ShapeDtypeStruct((n_rows, dp), jnp.int32),
        compiler_params=_cparams(("arbitrary",)),
        name="expert_ffn",
    )(tile_expert, tile_valid, tile_next, last_used, xs, w_gate, b_gate.reshape(N_EXPERTS, 1, d_e), w_up,
      b_up.reshape(N_EXPERTS, 1, d_e), w_down, b_down.reshape(N_EXPERTS, 1, d))


def _combine_kernel(eg_ref, wts_ref, x1_ref, mod_ref, lg_ref, lb_ref, o_ref):
    w = wts_ref[...]
    y_hi, y_lo = _unpack_rows(eg_ref[0])
    y_hi, y_lo = y_hi * w[:, 0:1], y_lo * w[:, 0:1]
    for kk in range(1, TOP_K):
        e_hi, e_lo = _unpack_rows(eg_ref[kk])
        y_hi = y_hi + e_hi * w[:, kk:kk + 1]
        y_lo = y_lo + e_lo * w[:, kk:kk + 1]
    y = jnp.concatenate([y_hi, y_lo], axis=1)
    z = DEEPNORM_ALPHA * x1_ref[...] + mod_ref[0] * y
    mu = jnp.mean(z, axis=-1, keepdims=True)
    zc = z - mu
    var = jnp.mean(zc * zc, axis=-1, keepdims=True)
    o_ref[...] = zc * lax.rsqrt(var + LN_EPS) * lg_ref[...] + lb_ref[...]


def _combine_ln(eg, wts, x1, mod3, lg, lb, tokens_per_batch):
    T = x1.shape[0]
    tm = TOKEN_TILE
    tiles_per_batch = tokens_per_batch // tm
    const = lambda i: (0, 0)
    return pl.pallas_call(
        _combine_kernel,
        grid=(T // tm,),
        in_specs=[pl.BlockSpec((TOP_K, tm, D_MODEL // 2), lambda i: (0, i, 0)),
                  pl.BlockSpec((tm, LANES), lambda i: (i, 0)),
                  pl.BlockSpec((tm, D_MODEL), lambda i: (i, 0)),
                  pl.BlockSpec((1, 1, D_MODEL), lambda i: (i // tiles_per_batch, 0, N_MOD - 1)),
                  pl.BlockSpec((1, D_MODEL), const),
                  pl.BlockSpec((1, D_MODEL), const)],
        out_specs=pl.BlockSpec((tm, D_MODEL), lambda i: (i, 0)),
        out_shape=jax.ShapeDtypeStruct((T, D_MODEL), f32),
        compiler_params=_cparams(("arbitrary",)),
        name="combine_ln",
    )(eg, wts, x1, mod3, lg, lb)


def _block_diag(w):
    n, c, d = w.shape
    eye = jnp.eye(n, dtype=w.dtype)
    return jnp.einsum('ncd,nm->ncmd', w, eye).reshape(n * c, n * d)


def kernel(x, c, ctx, c_ctx, w_ada, b_ada, w_in, conv_w, conv_b, lru_wa, lru_ba, lru_wx, lru_bx,
           lru_lam, gla_wa, gla_ba, gla_norm_g, w_out, ln1_g, ln1_b, w_router, b_router, w_gate,
           b_gate, w_up, b_up, w_down, b_down, ln2_g, ln2_b):
    B, L, D = x.shape
    Lc = ctx.shape[1]
    T = B * L
    rows = L // GRID_W
    l = 0

    cpad = jnp.zeros((8, D), f32).at[0:B].set(c).at[B].set(c_ctx)
    w_cat = jnp.pad(w_in[l], ((0, 0), (0, D_PROJ - w_in.shape[-1]))).astype(bf16)
    wg = [jnp.concatenate([_block_diag(lru_wa[l, d]), _block_diag(lru_wx[l, d])], axis=1).astype(bf16)
          for d in range(2)]
    bg = [jnp.concatenate([lru_ba[l, d], lru_bx[l, d]])[None] for d in range(2)]
    lam = [lru_lam[l, d][None] for d in range(2)]
    wa = [jnp.pad(gla_wa[l, d], ((0, A_LOW_PAD - GATE_RANK), (0, 0))).astype(bf16) for d in range(2)]
    ba = [gla_ba[l, d][None] for d in range(2)]
    cw, cb = conv_w[l], conv_b[l][None]
    wr_t = w_router[l].T
    wr_hi = wr_t.astype(bf16)
    wr_lo = (wr_t - wr_hi.astype(f32)).astype(bf16)
    br = b_router[l][:, None]
    c_xl, c_gl, c_pg = (0, D_LRU), (D_LRU, 2 * D_LRU), (2 * D_LRU, D_PROJ)

    mod3 = _ada_mod(cpad, w_ada[l], b_ada[l]).reshape(8, 1, N_MOD * D)

    tiles_per_batch = L // INPROJ_TILE
    xl_c, pg_c = _inproj_rows(ctx.reshape(B * Lc, D), mod3, w_cat, lambda i: B, (c_xl, c_pg), (f32, bf16))
    zero_h = jnp.zeros((B, 1, D_LRU), f32)
    zero_s = jnp.zeros((B, GLA_DV, D_GLA_K), f32)
    pg_c = pg_c.reshape(B, Lc // GLA_CHUNK, GLA_CHUNK, D_PG)
    h_ctx = []
    for d in range(2):
        _, _, hf = _lru_pass(xl_c, wg[d], bg[d], lam[d], zero_h, n_batch=B, tile=Lc, reverse=bool(d),
                             conv=(cw, cb, Lc))
        h_ctx.append(hf)
    s_ctx = _gla_pass(pg_c, wa, ba, (zero_s, zero_s))

    xt = x.reshape(T, D)
    xl, gl = _inproj_rows(xt, mod3, w_cat, lambda i: i // tiles_per_batch, (c_xl, c_gl), (f32, bf16))
    hb, u_lru, _ = _lru_pass(xl, wg[1], bg[1], lam[1], h_ctx[1], n_batch=B, tile=LRU_TILE, reverse=True,
                             conv=(cw, cb, GRID_W))
    y_lru, _ = _lru_pass(u_lru, wg[0], bg[0], lam[0], h_ctx[0], n_batch=B, tile=LRU_TILE, reverse=False,
                         merge_with=(hb, gl))
    pg = _inproj_cols(x.reshape(B, rows, GRID_W, D), mod3, w_cat, c_pg)
    y_gla = _gla_pass(pg, wa, ba, s_ctx, gn=gla_norm_g[l][None])

    x1, hp, ids, rank, wts, cnt = _outproj_router(
        y_lru, y_gla, xt, mod3, w_out[l].astype(bf16), ln1_g[l][None], ln1_b[l][None], wr_hi, wr_lo, br, L)

    n_tiles = T * TOP_K // MOE_TILE + N_EXPERTS
    dest, tiles = _route(ids, rank, cnt, MOE_TILE, n_tiles)
    xs = _sc_dispatch(hp, dest.reshape(-1), n_tiles * MOE_TILE)
    eo = _expert_ffn(tiles[0, :n_tiles], tiles[1, :n_tiles], tiles[2, :n_tiles], tiles[3, :1], xs, w_gate[l],
                     b_gate[l], w_up[l], b_up[l], w_down[l], b_down[l])
    eg = _sc_gather(eo, dest[:TOP_K].reshape(-1)).reshape(TOP_K, T, D // 2)
    out = _combine_ln(eg, wts, x1, mod3, ln2_g[l][None], ln2_b[l][None], L)
    return out.reshape(B, L, D)
```
